```python
import jax, jax.numpy as jnp
from jax import lax
import numpy as np

D_MODEL = 1024
BATCH = 8
SEQ = 4096
DEPTH = 1

CHUNK = 64
PLE_DIM = 256
GMLP_BLOCK = 128
GMLP_GROUPS = 12
GMLP_GROUP_DIM = 128
GMLP_WIDTH = GMLP_GROUPS * GMLP_GROUP_DIM
MLA_HEADS = 8
QK_NOPE_DIM = 128
QK_ROPE_DIM = 64
V_HEAD_DIM = 128
Q_LORA = 384
KV_LORA = 256
ROPE_THETA = 10000.0
Q_BLOCK = 128
N_GROUPS = 8
EXPERTS_PER_GROUP = 8
N_EXPERTS = N_GROUPS * EXPERTS_PER_GROUP
TOPK = 2
D_EXPERT = 256
DISPATCH_BLOCK = 128
EPS = 1e-6
IN_COLS = 2 * GMLP_WIDTH + Q_LORA + KV_LORA + QK_ROPE_DIM + 2 * D_MODEL

kernel_name = "hybrid_gmlp_mla_hmoe_ple_block"


def rmsnorm(x, g):
    xf = x.astype(jnp.float32)
    xf = xf * lax.rsqrt(jnp.mean(xf * xf, axis=-1, keepdims=True) + EPS)
    return xf.astype(x.dtype) * g


def layernorm_gain(x, g):
    xf = x.astype(jnp.float32)
    mu = jnp.mean(xf, axis=-1, keepdims=True)
    xc = xf - mu
    xf = xc * lax.rsqrt(jnp.mean(xc * xc, axis=-1, keepdims=True) + EPS)
    return xf.astype(x.dtype) * g


def rope_tables(seq_len, dtype):
    inv_freq = ROPE_THETA ** (-jnp.arange(0, QK_ROPE_DIM, 2, dtype=jnp.float32) / QK_ROPE_DIM)
    ang = jnp.arange(seq_len, dtype=jnp.float32)[:, None] * inv_freq[None, :]
    return jnp.cos(ang).astype(dtype), jnp.sin(ang).astype(dtype)


def apply_rope(x, cos, sin):
    c = cos[None, :, None, :]
    s = sin[None, :, None, :]
    x1, x2 = jnp.split(x, 2, axis=-1)
    return jnp.concatenate([x1 * c - x2 * s, x1 * s + x2 * c], axis=-1)


def gmlp_branch(u, v, g_v, w_spatial, b_spatial, w_proj):
    B, S, _ = u.shape
    nb = S // GMLP_BLOCK
    v = layernorm_gain(v, g_v)
    vb = v.reshape(B, nb, GMLP_BLOCK, GMLP_GROUPS, GMLP_GROUP_DIM)
    t = jnp.arange(GMLP_BLOCK)
    mask = (t[None, :] // CHUNK) <= (t[:, None] // CHUNK)
    w = jnp.where(mask[None], w_spatial, jnp.zeros_like(w_spatial))
    sv = jnp.einsum('gts,bnsgc->bntgc', w, vb) + b_spatial.T[None, None, :, :, None]
    y = u * sv.reshape(B, S, GMLP_WIDTH)
    return y @ w_proj


def mla_branch(c_q, c_kv, k_rope, g_cq, w_uq, g_ckv, w_ukv, w_o):
    B, S, _ = c_q.shape
    H = MLA_HEADS
    cos, sin = rope_tables(S, c_q.dtype)
    q = (rmsnorm(c_q, g_cq) @ w_uq).reshape(B, S, H, QK_NOPE_DIM + QK_ROPE_DIM)
    q_nope, q_pe = jnp.split(q, [QK_NOPE_DIM], axis=-1)
    q = jnp.concatenate([q_nope, apply_rope(q_pe, cos, sin)], axis=-1)
    kv = (rmsnorm(c_kv, g_ckv) @ w_ukv).reshape(B, S, H, QK_NOPE_DIM + V_HEAD_DIM)
    k_nope, v = jnp.split(kv, [QK_NOPE_DIM], axis=-1)
    k_pe = apply_rope(k_rope[:, :, None, :], cos, sin)
    k = jnp.concatenate([k_nope, jnp.broadcast_to(k_pe, (B, S, H, QK_ROPE_DIM))], axis=-1)
    scale = (QK_NOPE_DIM + QK_ROPE_DIM) ** -0.5
    nq = S // Q_BLOCK
    kpos_chunk = jnp.arange(S) // CHUNK
    qb_all = q.reshape(B, nq, Q_BLOCK, H, QK_NOPE_DIM + QK_ROPE_DIM).transpose(1, 0, 2, 3, 4)

    def attend(args):
        qb, i = args
        qpos_chunk = (i * Q_BLOCK + jnp.arange(Q_BLOCK)) // CHUNK
        mask = kpos_chunk[None, :] <= qpos_chunk[:, None]
        s = jnp.einsum('bqhd,bkhd->bhqk', qb, k, preferred_element_type=jnp.float32) * scale
        s = jnp.where(mask[None, None], s, -jnp.inf)
        pr = jax.nn.softmax(s, axis=-1).astype(v.dtype)
        return jnp.einsum('bhqk,bkhv->bqhv', pr, v)

    o = lax.map(attend, (qb_all, jnp.arange(nq)))
    o = o.transpose(1, 0, 2, 3, 4).reshape(B, S, H * V_HEAD_DIM)
    return o @ w_o


def hierarchical_route(xf, w_rg, b_rg, w_re, b_re):
    N = xf.shape[0]
    g_logits = (xf @ w_rg).astype(jnp.float32)
    g_prob = jax.nn.softmax(g_logits, axis=-1)
    _, g_idx = lax.top_k(g_prob + b_rg[None, :].astype(jnp.float32), 1)
    g_w = jnp.take_along_axis(g_prob, g_idx, axis=-1)
    e_logits = (xf @ w_re).astype(jnp.float32).reshape(N, N_GROUPS, EXPERTS_PER_GROUP)
    e_logits = jnp.take_along_axis(e_logits, g_idx[:, :, None], axis=1)[:, 0]
    e_prob = jax.nn.softmax(e_logits, axis=-1)
    e_bias = b_re.astype(jnp.float32)[g_idx[:, 0]]
    _, e_local = lax.top_k(e_prob + e_bias, TOPK)
    e_sel = jnp.take_along_axis(e_prob, e_local, axis=-1)
    e_sel = e_sel / jnp.sum(e_sel, axis=-1, keepdims=True)
    weights = g_w * e_sel
    expert_ids = g_idx * EXPERTS_PER_GROUP + e_local
    return expert_ids, weights


def moe_dispatch(xf, expert_ids, weights, w_gate, w_up, w_down):
    N, D = xf.shape
    A = N * TOPK
    flat_e = expert_ids.reshape(-1)
    flat_t = jnp.repeat(jnp.arange(N), TOPK)
    flat_w = weights.reshape(-1)
    order = jnp.argsort(flat_e)
    se, st, sw = flat_e[order], flat_t[order], flat_w[order]
    counts = jnp.zeros((N_EXPERTS,), jnp.int32).at[flat_e].add(1)
    padded = (counts + DISPATCH_BLOCK - 1) // DISPATCH_BLOCK * DISPATCH_BLOCK
    pad_end = jnp.cumsum(padded)
    pad_start = pad_end - padded
    start = jnp.cumsum(counts) - counts
    dest = pad_start[se] + jnp.arange(A) - start[se]
    P = A + N_EXPERTS * DISPATCH_BLOCK
    nblk = P // DISPATCH_BLOCK
    buf = jnp.zeros((P, D), xf.dtype).at[dest].set(xf[st])
    blk_e = jnp.minimum(jnp.searchsorted(pad_end, jnp.arange(nblk) * DISPATCH_BLOCK, side='right'), N_EXPERTS - 1)

    def expert_block(args):
        xb, e = args
        hdn = jax.nn.silu(xb @ w_gate[e]) * (xb @ w_up[e])
        return hdn @ w_down[e]

    ys = lax.map(expert_block, (buf.reshape(nblk, DISPATCH_BLOCK, D), blk_e)).reshape(P, D)
    return jnp.zeros_like(xf).at[st].add(ys[dest] * sw[:, None].astype(ys.dtype))


def setup_inputs(seed: int = 0) -> dict:
    key = jax.random.key(seed)
    ks = jax.random.split(key, 32)
    f32 = jnp.float32

    def nrm(k, shape, fan_in):
        return jax.random.normal(k, shape, f32) * (fan_in ** -0.5)

    def gain(k, shape):
        return 1.0 + 0.05 * jax.random.normal(k, shape, f32)

    L = DEPTH
    return {
        "x": jax.random.normal(ks[0], (BATCH, SEQ, D_MODEL), f32),
        "p": jax.random.normal(ks[1], (DEPTH, BATCH, SEQ, PLE_DIM), f32),
        "g_mix": gain(ks[2], (L, D_MODEL)),
        "w_in": nrm(ks[3], (L, D_MODEL, IN_COLS), D_MODEL),
        "g_gv": gain(ks[4], (L, GMLP_WIDTH)),
        "w_spatial": nrm(ks[5], (L, GMLP_GROUPS, GMLP_BLOCK, GMLP_BLOCK), GMLP_BLOCK),
        "b_spatial": 1.0 + 0.1 * jax.random.normal(ks[6], (L, GMLP_GROUPS, GMLP_BLOCK), f32),
        "w_gproj": nrm(ks[7], (L, GMLP_WIDTH, D_MODEL), GMLP_WIDTH),
        "g_cq": gain(ks[8], (L, Q_LORA)),
        "w_uq": nrm(ks[9], (L, Q_LORA, MLA_HEADS * (QK_NOPE_DIM + QK_ROPE_DIM)), Q_LORA),
        "g_ckv": gain(ks[10], (L, KV_LORA)),
        "w_ukv": nrm(ks[11], (L, KV_LORA, MLA_HEADS * (QK_NOPE_DIM + V_HEAD_DIM)), KV_LORA),
        "w_mla_o": nrm(ks[12], (L, MLA_HEADS * V_HEAD_DIM, D_MODEL), MLA_HEADS * V_HEAD_DIM),
        "w_out": nrm(ks[13], (L, D_MODEL, D_MODEL), D_MODEL),
        "g_moe": gain(ks[14], (L, D_MODEL)),
        "w_router_g": nrm(ks[15], (L, D_MODEL, N_GROUPS), D_MODEL),
        "b_router_g": 0.01 * jax.random.normal(ks[16], (L, N_GROUPS), f32),
        "w_router_e": nrm(ks[17], (L, D_MODEL, N_EXPERTS), D_MODEL),
        "b_router_e": 0.01 * jax.random.normal(ks[18], (L, N_GROUPS, EXPERTS_PER_GROUP), f32),
        "w_e_gate": nrm(ks[19], (L, N_EXPERTS, D_MODEL, D_EXPERT), D_MODEL),
        "w_e_up": nrm(ks[20], (L, N_EXPERTS, D_MODEL, D_EXPERT), D_MODEL),
        "w_e_down": nrm(ks[21], (L, N_EXPERTS, D_EXPERT, D_MODEL), D_EXPERT),
        "g_ple": gain(ks[22], (L, D_MODEL)),
        "w_ple_gate": nrm(ks[23], (L, D_MODEL, D_MODEL), D_MODEL),
        "w_ple_proj": nrm(ks[24], (L, PLE_DIM, D_MODEL), PLE_DIM),
        "g_final": gain(ks[25], (D_MODEL,)),
    }


def reference(x, p, g_mix, w_in, g_gv, w_spatial, b_spatial, w_gproj, g_cq, w_uq, g_ckv, w_ukv,
              w_mla_o, w_out, g_moe, w_router_g, b_router_g, w_router_e, b_router_e,
              w_e_gate, w_e_up, w_e_down, g_ple, w_ple_gate, w_ple_proj, g_final):
    B, S, D = x.shape
    cuts = [GMLP_WIDTH, 2 * GMLP_WIDTH, 2 * GMLP_WIDTH + Q_LORA, 2 * GMLP_WIDTH + Q_LORA + KV_LORA,
            2 * GMLP_WIDTH + Q_LORA + KV_LORA + QK_ROPE_DIM,
            2 * GMLP_WIDTH + Q_LORA + KV_LORA + QK_ROPE_DIM + D_MODEL]
    h = x
    for i in range(DEPTH):
        a = rmsnorm(h, g_mix[i])
        z = a @ w_in[i]
        z_u, z_v, c_q, c_kv, k_rope, gate_a, gate_b = jnp.split(z, cuts, axis=-1)
        y_a = gmlp_branch(jax.nn.gelu(z_u), jax.nn.gelu(z_v), g_gv[i], w_spatial[i], b_spatial[i], w_gproj[i])
        y_b = mla_branch(c_q, c_kv, k_rope, g_cq[i], w_uq[i], g_ckv[i], w_ukv[i], w_mla_o[i])
        merged = jax.nn.sigmoid(gate_a) * y_a + jax.nn.sigmoid(gate_b) * y_b
        h = h + merged @ w_out[i]
        m = rmsnorm(h, g_moe[i]).reshape(B * S, D)
        ids, wts = hierarchical_route(m, w_router_g[i], b_router_g[i], w_router_e[i], b_router_e[i])
        h = h + moe_dispatch(m, ids, wts, w_e_gate[i], w_e_up[i], w_e_down[i]).reshape(B, S, D)
        n3 = rmsnorm(h, g_ple[i])
        h = h + jax.nn.sigmoid(n3 @ w_ple_gate[i]) * (p[i] @ w_ple_proj[i])
    return rmsnorm(h, g_final)
```

```python
import functools

import jax
import jax.numpy as jnp
from jax import lax
from jax.experimental import pallas as pl
from jax.experimental.pallas import tpu as pltpu

F32 = jnp.float32
BF16 = jnp.bfloat16

D_MODEL = 1024
CHUNK = 64
PLE_DIM = 256
GMLP_BLOCK = 128
GMLP_GROUPS = 12
GMLP_WIDTH = 1536
MLA_HEADS = 8
QK_NOPE_DIM = 128
QK_ROPE_DIM = 64
V_HEAD_DIM = 128
Q_LORA = 384
KV_LORA = 256
ROPE_THETA = 10000.0
N_GROUPS = 8
EXPERTS_PER_GROUP = 8
N_EXPERTS = 64
D_EXPERT = 256
EPS = 1e-6

LANES = 128
QK_PAD = 256
VMEM_LIMIT = 56 * 1024 * 1024

C_U = 0
C_V = C_U + GMLP_WIDTH
C_CQ = C_V + GMLP_WIDTH
C_CKV = C_CQ + Q_LORA
C_GA = C_CKV + KV_LORA
C_GB = C_GA + D_MODEL
C_KR = C_GB + D_MODEL
C_END = C_KR + LANES

TM_IN = 256
TM_GMLP = 512
TQ = 512
TM_OUT = 512
TM_FIN = 512
MOE_BLK = 256
ROW_CH = 512
ROW_UNROLL = 8


def _rms(x):
    return x * lax.rsqrt(jnp.mean(x * x, axis=-1, keepdims=True) + EPS)


def _dot(a, b):
    return jnp.dot(a, b, preferred_element_type=F32)


def _rope128(t, rc, rs1, rs2):
    r1 = pltpu.roll(t, 96, axis=1)
    r2 = pltpu.roll(t, 32, axis=1)
    return t * rc + r1 * rs1 + r2 * rs2


def _inproj_kernel(x_ref, gmix_ref, win_ref, ggv_ref, gcq_ref, gckv_ref, wuq_ref, wukv_ref,
                   rc_ref, rs1_ref, rs2_ref,
                   u_ref, v_ref, sga_ref, sgb_ref, q_ref, k_ref, vv_ref):
    x = x_ref[...]
    ab = (_rms(x) * gmix_ref[...]).astype(BF16)

    def proj(c0, c1):
        return _dot(ab, win_ref[:, c0:c1])

    u_ref[...] = jax.nn.gelu(proj(C_U, C_V)).astype(BF16)

    zv = jax.nn.gelu(proj(C_V, C_CQ))
    xc = zv - jnp.mean(zv, axis=-1, keepdims=True)
    vln = xc * lax.rsqrt(jnp.mean(xc * xc, axis=-1, keepdims=True) + EPS)
    v_ref[...] = (vln * ggv_ref[...]).astype(BF16)

    sga_ref[...] = jax.nn.sigmoid(proj(C_GA, C_GB)).astype(BF16)
    sgb_ref[...] = jax.nn.sigmoid(proj(C_GB, C_KR)).astype(BF16)

    rc = rc_ref[...]
    rs1 = rs1_ref[...]
    rs2 = rs2_ref[...]
    kpe = _rope128(proj(C_KR, C_END), rc, rs1, rs2).astype(BF16)

    cqn = (_rms(proj(C_CQ, C_CKV)) * gcq_ref[...]).astype(BF16)
    ckvn = (_rms(proj(C_CKV, C_GA)) * gckv_ref[...]).astype(BF16)
    scale = (QK_NOPE_DIM + QK_ROPE_DIM) ** -0.5
    for h in range(MLA_HEADS):
        qh = _dot(cqn, wuq_ref[:, h * QK_PAD:(h + 1) * QK_PAD])
        q_ref[0, h, :, 0:LANES] = (qh[:, 0:LANES] * scale).astype(BF16)
        q_ref[0, h, :, LANES:QK_PAD] = (_rope128(qh[:, LANES:QK_PAD], rc, rs1, rs2) * scale).astype(BF16)
        kvh = _dot(ckvn, wukv_ref[:, h * 256:(h + 1) * 256])
        k_ref[0, h, :, 0:LANES] = kvh[:, 0:LANES].astype(BF16)
        k_ref[0, h, :, LANES:QK_PAD] = kpe
        vv_ref[0, h] = kvh[:, LANES:256].astype(BF16)


def _inproj(x2, g_mix, w_in_p, g_gv, g_cq, g_ckv, w_uq_p, w_ukv_b, rc, rs1, rs2, B, S):
    N = x2.shape[0]
    tm = TM_IN
    spt = S // tm
    row = lambda i: (i, 0)
    const = lambda i: (0, 0)
    pos = lambda i: (i % spt, 0)
    head = lambda i: (i // spt, 0, i % spt, 0)
    return pl.pallas_call(
        _inproj_kernel,
        grid=(N // tm,),
        in_specs=[
            pl.BlockSpec((tm, D_MODEL), row),
            pl.BlockSpec((1, D_MODEL), const),
            pl.BlockSpec((D_MODEL, C_END), const),
            pl.BlockSpec((1, GMLP_WIDTH), const),
            pl.BlockSpec((1, Q_LORA), const),
            pl.BlockSpec((1, KV_LORA), const),
            pl.BlockSpec((Q_LORA, MLA_HEADS * QK_PAD), const),
            pl.BlockSpec((KV_LORA, MLA_HEADS * 256), const),
            pl.BlockSpec((tm, LANES), pos),
            pl.BlockSpec((tm, LANES), pos),
            pl.BlockSpec((tm, LANES), pos),
        ],
        out_specs=[
            pl.BlockSpec((tm, GMLP_WIDTH), row),
            pl.BlockSpec((tm, GMLP_WIDTH), row),
            pl.BlockSpec((tm, D_MODEL), row),
            pl.BlockSpec((tm, D_MODEL), row),
            pl.BlockSpec((1, MLA_HEADS, tm, QK_PAD), head),
            pl.BlockSpec((1, MLA_HEADS, tm, QK_PAD), head),
            pl.BlockSpec((1, MLA_HEADS, tm, V_HEAD_DIM), head),
        ],
        out_shape=[
            jax.ShapeDtypeStruct((N, GMLP_WIDTH), BF16),
            jax.ShapeDtypeStruct((N, GMLP_WIDTH), BF16),
            jax.ShapeDtypeStruct((N, D_MODEL), BF16),
            jax.ShapeDtypeStruct((N, D_MODEL), BF16),
            jax.ShapeDtypeStruct((B, MLA_HEADS, S, QK_PAD), BF16),
            jax.ShapeDtypeStruct((B, MLA_HEADS, S, QK_PAD), BF16),
            jax.ShapeDtypeStruct((B, MLA_HEADS, S, V_HEAD_DIM), BF16),
        ],
        compiler_params=pltpu.CompilerParams(
            dimension_semantics=("arbitrary",), vmem_limit_bytes=VMEM_LIMIT),
        name="inproj",
    )(x2, g_mix, w_in_p, g_gv, g_cq, g_ckv, w_uq_p, w_ukv_b, rc, rs1, rs2)


def _gmlp_kernel(u_ref, v_ref, sga_ref, wsp_ref, bsp_ref, wproj_ref, ma_ref, y_sc):
    nb = TM_GMLP // GMLP_BLOCK
    t_out = lax.broadcasted_iota(jnp.int32, (GMLP_BLOCK, GMLP_BLOCK), 0)
    s_in = lax.broadcasted_iota(jnp.int32, (GMLP_BLOCK, GMLP_BLOCK), 1)
    mask = (s_in // CHUNK) <= (t_out // CHUNK)
    for g in range(GMLP_GROUPS):
        c0 = g * LANES
        w = jnp.where(mask, wsp_ref[g], jnp.zeros((), BF16))
        rhs = jnp.concatenate(
            [v_ref[r * GMLP_BLOCK:(r + 1) * GMLP_BLOCK, c0:c0 + LANES] for r in range(nb)], axis=1)
        sv = _dot(w, rhs) + bsp_ref[:, g:g + 1]
        for r in range(nb):
            rows = slice(r * GMLP_BLOCK, (r + 1) * GMLP_BLOCK)
            ub = u_ref[rows, c0:c0 + LANES].astype(F32)
            y_sc[rows, c0:c0 + LANES] = (ub * sv[:, r * LANES:(r + 1) * LANES]).astype(BF16)
    ya = _dot(y_sc[...], wproj_ref[...])
    ma_ref[...] = (sga_ref[...].astype(F32) * ya).astype(BF16)


def _gmlp(u, v, sga, wsp_b, bsp_t, wproj_b):
    N = u.shape[0]
    tm = TM_GMLP
    row = lambda i: (i, 0)
    return pl.pallas_call(
        _gmlp_kernel,
        grid=(N // tm,),
        in_specs=[
            pl.BlockSpec((tm, GMLP_WIDTH), row),
            pl.BlockSpec((tm, GMLP_WIDTH), row),
            pl.BlockSpec((tm, D_MODEL), row),
            pl.BlockSpec((GMLP_GROUPS, GMLP_BLOCK, GMLP_BLOCK), lambda i: (0, 0, 0)),
            pl.BlockSpec((GMLP_BLOCK, GMLP_GROUPS), lambda i: (0, 0)),
            pl.BlockSpec((GMLP_WIDTH, D_MODEL), lambda i: (0, 0)),
        ],
        out_specs=pl.BlockSpec((tm, D_MODEL), row),
        out_shape=jax.ShapeDtypeStruct((N, D_MODEL), BF16),
        scratch_shapes=[pltpu.VMEM((tm, GMLP_WIDTH), BF16)],
        compiler_params=pltpu.CompilerParams(
            dimension_semantics=("arbitrary",), vmem_limit_bytes=VMEM_LIMIT),
        name="gmlp",
    )(u, v, sga, wsp_b, bsp_t, wproj_b)


NEG_BIG = -1e30


def _attn_kernel(q_ref, k_ref, v_ref, o_ref, m_sc, l_sc, acc_sc):
    qi = pl.program_id(2)
    q = q_ref[0, 0]
    m_sc[...] = jnp.full(m_sc.shape, NEG_BIG, F32)
    l_sc[...] = jnp.zeros(l_sc.shape, F32)
    acc_sc[...] = jnp.zeros(acc_sc.shape, F32)

    def step(j, masked):
        start = pl.multiple_of(j * TQ, TQ)
        kb = k_ref[0, 0, pl.ds(start, TQ), :]
        vb = v_ref[0, 0, pl.ds(start, TQ), :]
        s = lax.dot_general(q, kb, (((1,), (1,)), ((), ())), preferred_element_type=F32)
        if masked:
            qc = lax.broadcasted_iota(jnp.int32, (TQ, TQ), 0) // CHUNK
            kc = lax.broadcasted_iota(jnp.int32, (TQ, TQ), 1) // CHUNK
            s = jnp.where(kc <= qc, s, NEG_BIG)
        m_prev = m_sc[...]
        m_new = jnp.maximum(m_prev, jnp.max(s, axis=-1, keepdims=True))
        alpha = jnp.exp(m_prev - m_new)
        p = jnp.exp(s - m_new)
        l_sc[...] = alpha * l_sc[...] + jnp.sum(p, axis=-1, keepdims=True)
        acc_sc[...] = alpha * acc_sc[...] + _dot(p.astype(BF16), vb)
        m_sc[...] = m_new

    def body(j, carry):
        step(j, False)
        return carry

    lax.fori_loop(0, qi, body, 0)
    step(qi, True)
    o_ref[0] = (acc_sc[...] / l_sc[...]).astype(BF16)


def _attention(q, k, v):
    B, H, S, _ = q.shape
    return pl.pallas_call(
        _attn_kernel,
        grid=(B, H, S // TQ),
        in_specs=[
            pl.BlockSpec((1, 1, TQ, QK_PAD), lambda b, h, i: (b, h, i, 0)),
            pl.BlockSpec((1, 1, S, QK_PAD), lambda b, h, i: (b, h, 0, 0)),
            pl.BlockSpec((1, 1, S, V_HEAD_DIM), lambda b, h, i: (b, h, 0, 0)),
        ],
        out_specs=pl.BlockSpec((1, TQ, V_HEAD_DIM), lambda b, h, i: (b, i, h)),
        out_shape=jax.ShapeDtypeStruct((B, S, H * V_HEAD_DIM), BF16),
        scratch_shapes=[
            pltpu.VMEM((TQ, 1), F32),
            pltpu.VMEM((TQ, 1), F32),
            pltpu.VMEM((TQ, V_HEAD_DIM), F32),
        ],
        compiler_params=pltpu.CompilerParams(
            dimension_semantics=("arbitrary", "arbitrary", "arbitrary"),
            vmem_limit_bytes=VMEM_LIMIT),
        name="attention",
    )(q, k, v)


def _lane_sum(x):
    return jnp.sum(x, axis=-1, keepdims=True)


def _first_lane(hit, lane_f):
    return jnp.min(jnp.where(hit, lane_f, float(LANES)), axis=-1, keepdims=True).astype(jnp.int32)


def _out_route_kernel(o_ref, ma_ref, sgb_ref, x_ref, wo_ref, wout_ref, gmoe_ref, wr_ref, br_ref,
                      h1_ref, m_ref, ri_ref, rw_ref, cnt_ref, carry_sc):
    i = pl.program_id(0)
    tm = TM_OUT

    @pl.when(i == 0)
    def _():
        carry_sc[...] = jnp.zeros(carry_sc.shape, F32)

    yb = _dot(o_ref[...], wo_ref[...])
    merged = ma_ref[...].astype(F32) + sgb_ref[...].astype(F32) * yb
    h1 = x_ref[...] + _dot(merged.astype(BF16), wout_ref[...])
    h1_ref[...] = h1
    m = _rms(h1) * gmoe_ref[...]
    m_ref[...] = m

    logits = _dot(m.astype(BF16), wr_ref[...])
    lane = lax.broadcasted_iota(jnp.int32, (tm, LANES), 1)
    lane_f = lane.astype(F32)
    bias = br_ref[...]
    is_g = (lane >= N_EXPERTS) & (lane < N_EXPERTS + N_GROUPS)
    neg = jnp.float32(-jnp.inf)

    gl = jnp.where(is_g, logits, neg)
    ge = jnp.where(is_g, jnp.exp(gl - jnp.max(gl, axis=-1, keepdims=True)), 0.0)
    g_prob = ge / _lane_sum(ge)
    g_score = jnp.where(is_g, g_prob + bias, neg)
    g_best = jnp.max(g_score, axis=-1, keepdims=True)
    g_lane = _first_lane(g_score == g_best, lane_f)
    g_w = _lane_sum(jnp.where(lane == g_lane, g_prob, 0.0))
    g_idx = g_lane - N_EXPERTS

    in_g = (lane // EXPERTS_PER_GROUP) == g_idx
    el = jnp.where(in_g, logits, neg)
    ee = jnp.where(in_g, jnp.exp(el - jnp.max(el, axis=-1, keepdims=True)), 0.0)
    e_prob = ee / _lane_sum(ee)
    e_score = jnp.where(in_g, e_prob + bias, neg)
    best1 = jnp.max(e_score, axis=-1, keepdims=True)
    id1 = _first_lane(e_score == best1, lane_f)
    e_score2 = jnp.where(lane == id1, neg, e_score)
    best2 = jnp.max(e_score2, axis=-1, keepdims=True)
    id2 = _first_lane(e_score2 == best2, lane_f)
    p1 = _lane_sum(jnp.where(lane == id1, e_prob, 0.0))
    p2 = _lane_sum(jnp.where(lane == id2, e_prob, 0.0))
    psum = p1 + p2
    w1 = g_w * (p1 / psum)
    w2 = g_w * (p2 / psum)

    oh = ((lane == id1) | (lane == id2 + N_EXPERTS)).astype(BF16)
    r_out = lax.broadcasted_iota(jnp.int32, (tm, tm), 0)
    r_in = lax.broadcasted_iota(jnp.int32, (tm, tm), 1)
    tri = (r_in < r_out).astype(BF16)
    prefix = _dot(tri, oh)
    tot = jnp.sum(oh.astype(F32), axis=0, keepdims=True)
    tot_sw = pltpu.roll(tot, N_EXPERTS, axis=1)
    lane1 = lax.broadcasted_iota(jnp.int32, (1, LANES), 1)
    carry = carry_sc[...]
    base = carry + jnp.where(lane1 >= N_EXPERTS, tot_sw, 0.0)
    rk = oh.astype(F32) * (base + prefix)
    rank1 = _lane_sum(jnp.where(lane < N_EXPERTS, rk, 0.0))
    rank2 = _lane_sum(jnp.where(lane >= N_EXPERTS, rk, 0.0))
    carry_new = carry + tot + tot_sw
    carry_sc[...] = carry_new
    cnt_ref[...] = carry_new.astype(jnp.int32)

    ri = jnp.where(lane == 0, id1, jnp.where(lane == 1, id2, 0))
    ri = jnp.where(lane == 2, rank1.astype(jnp.int32), ri)
    ri = jnp.where(lane == 3, rank2.astype(jnp.int32), ri)
    ri_ref[...] = ri
    rw_ref[...] = jnp.where(lane == 0, w1, jnp.where(lane == 1, w2, 0.0))


def _out_route(o2, ma, sgb, x2, wo_b, wout_b, g_moe, wr_b, br):
    N = x2.shape[0]
    tm = TM_OUT
    row = lambda i: (i, 0)
    const = lambda i: (0, 0)
    return pl.pallas_call(
        _out_route_kernel,
        grid=(N // tm,),
        in_specs=[
            pl.BlockSpec((tm, D_MODEL), row),
            pl.BlockSpec((tm, D_MODEL), row),
            pl.BlockSpec((tm, D_MODEL), row),
            pl.BlockSpec((tm, D_MODEL), row),
            pl.BlockSpec((D_MODEL, D_MODEL), const),
            pl.BlockSpec((D_MODEL, D_MODEL), const),
            pl.BlockSpec((1, D_MODEL), const),
            pl.BlockSpec((D_MODEL, LANES), const),
            pl.BlockSpec((1, LANES), const),
        ],
        out_specs=[
            pl.BlockSpec((tm, D_MODEL), row),
            pl.BlockSpec((tm, D_MODEL), row),
            pl.BlockSpec((tm, LANES), row),
            pl.BlockSpec((tm, LANES), row),
            pl.BlockSpec((1, LANES), const),
        ],
        out_shape=[
            jax.ShapeDtypeStruct((N, D_MODEL), F32),
            jax.ShapeDtypeStruct((N, D_MODEL), F32),
            jax.ShapeDtypeStruct((N, LANES), jnp.int32),
            jax.ShapeDtypeStruct((N, LANES), F32),
            jax.ShapeDtypeStruct((1, LANES), jnp.int32),
        ],
        scratch_shapes=[pltpu.VMEM((1, LANES), F32)],
        compiler_params=pltpu.CompilerParams(
            dimension_semantics=("arbitrary",), vmem_limit_bytes=VMEM_LIMIT),
        name="out_route",
    )(o2, ma, sgb, x2, wo_b, wout_b, g_moe, wr_b, br)


def _row_copy_kernel(sidx_ref, didx_ref, src_ref, *rest):
    dst_ref, sem = rest[-2:]

    def copy(s, d):
        return pltpu.make_async_copy(src_ref.at[s], dst_ref.at[d], sem)

    def issue(c, carry):
        for u in range(ROW_UNROLL):
            r = c * ROW_UNROLL + u
            copy(sidx_ref[0, 0, r], didx_ref[0, 0, r]).start()
        return carry

    lax.fori_loop(0, ROW_CH // ROW_UNROLL, issue, 0)

    def drain(r, carry):
        copy(0, 0).wait()
        return carry

    lax.fori_loop(0, ROW_CH, drain, 0)


def _row_copy(sidx, didx, src3, n_dst, dst_init=None):
    n = sidx.shape[0]
    nch = n // ROW_CH
    idx_spec = pl.BlockSpec((1, 1, ROW_CH), lambda i: (i, 0, 0), memory_space=pltpu.SMEM)
    any_spec = pl.BlockSpec(memory_space=pl.ANY)
    extra = () if dst_init is None else (dst_init,)
    return pl.pallas_call(
        _row_copy_kernel,
        grid=(nch,),
        in_specs=[idx_spec, idx_spec, any_spec] + [any_spec] * len(extra),
        out_specs=any_spec,
        out_shape=jax.ShapeDtypeStruct((n_dst,) + src3.shape[1:], src3.dtype),
        scratch_shapes=[pltpu.SemaphoreType.DMA(())],
        input_output_aliases={3: 0} if extra else {},
        compiler_params=pltpu.CompilerParams(dimension_semantics=("arbitrary",)),
        name="row_copy",
    )(sidx.reshape(nch, 1, ROW_CH), didx.reshape(nch, 1, ROW_CH), src3, *extra)


def _expert_kernel(blk_e_ref, nused_ref, x_ref, wg_ref, wu_ref, wd_ref, y_ref):
    del blk_e_ref
    i = pl.program_id(0)

    @pl.when(i < nused_ref[0])
    def _():
        xb = x_ref[...].astype(BF16)
        hg = _dot(xb, wg_ref[0])
        hu = _dot(xb, wu_ref[0])
        hdn = (jax.nn.silu(hg) * hu).astype(BF16)
        y_ref[...] = _dot(hdn, wd_ref[0])

    @pl.when(i >= nused_ref[0])
    def _():
        y_ref[...] = jnp.zeros(y_ref.shape, F32)


def _experts(blk_e, nused, buf2, wg_b, wu_b, wd_b):
    P = buf2.shape[0]
    nblk = P // MOE_BLK
    grid_spec = pltpu.PrefetchScalarGridSpec(
        num_scalar_prefetch=2,
        grid=(nblk,),
        in_specs=[
            pl.BlockSpec((MOE_BLK, D_MODEL), lambda i, be, nu: (i, 0)),
            pl.BlockSpec((1, D_MODEL, D_EXPERT), lambda i, be, nu: (be[i], 0, 0)),
            pl.BlockSpec((1, D_MODEL, D_EXPERT), lambda i, be, nu: (be[i], 0, 0)),
            pl.BlockSpec((1, D_EXPERT, D_MODEL), lambda i, be, nu: (be[i], 0, 0)),
        ],
        out_specs=pl.BlockSpec((MOE_BLK, D_MODEL), lambda i, be, nu: (i, 0)),
    )
    return pl.pallas_call(
        _expert_kernel,
        grid_spec=grid_spec,
        out_shape=jax.ShapeDtypeStruct((P, D_MODEL), F32),
        compiler_params=pltpu.CompilerParams(
            dimension_semantics=("arbitrary",), vmem_limit_bytes=VMEM_LIMIT),
        name="experts",
    )(blk_e, nused, buf2, wg_b, wu_b, wd_b)


def _final_kernel(h1_ref, y0_ref, y1_ref, rw_ref, p_ref, gple_ref, wpg_ref, wpp_ref, gfin_ref, out_ref):
    rw = rw_ref[...]
    h2 = h1_ref[...] + (y0_ref[0] * rw[:, 0:1] + y1_ref[0] * rw[:, 1:2])
    n3 = (_rms(h2) * gple_ref[...]).astype(BF16)
    gate = jax.nn.sigmoid(_dot(n3, wpg_ref[...]))
    pp = _dot(p_ref[...].astype(BF16), wpp_ref[...])
    h3 = h2 + gate * pp
    out_ref[...] = _rms(h3) * gfin_ref[...]


def _final(h1, y01, rw, p2, g_ple, wpg_b, wpp_b, g_final):
    N = h1.shape[0]
    tm = TM_FIN
    row = lambda i: (i, 0)
    const = lambda i: (0, 0)
    return pl.pallas_call(
        _final_kernel,
        grid=(N // tm,),
        in_specs=[
            pl.BlockSpec((tm, D_MODEL), row),
            pl.BlockSpec((1, tm, D_MODEL), lambda i: (0, i, 0)),
            pl.BlockSpec((1, tm, D_MODEL), lambda i: (1, i, 0)),
            pl.BlockSpec((tm, LANES), row),
            pl.BlockSpec((tm, PLE_DIM), row),
            pl.BlockSpec((1, D_MODEL), const),
            pl.BlockSpec((D_MODEL, D_MODEL), const),
            pl.BlockSpec((PLE_DIM, D_MODEL), const),
            pl.BlockSpec((1, D_MODEL), const),
        ],
        out_specs=pl.BlockSpec((tm, D_MODEL), row),
        out_shape=jax.ShapeDtypeStruct((N, D_MODEL), F32),
        compiler_params=pltpu.CompilerParams(
            dimension_semantics=("arbitrary",), vmem_limit_bytes=VMEM_LIMIT),
        name="final",
    )(h1, y01, y01, rw, p2, g_ple, wpg_b, wpp_b, g_final)


def _rope_tables(S):
    inv_freq = ROPE_THETA ** (-jnp.arange(0, QK_ROPE_DIM, 2, dtype=F32) / QK_ROPE_DIM)
    ang = jnp.arange(S, dtype=F32)[:, None] * inv_freq[None, :]
    cos, sin = jnp.cos(ang), jnp.sin(ang)
    z = jnp.zeros_like(cos)
    rc = jnp.concatenate([cos, cos, z, z], axis=1)
    rs1 = jnp.concatenate([-sin, z, z, z], axis=1)
    rs2 = jnp.concatenate([z, sin, z, z], axis=1)
    return rc, rs1, rs2


def _layer(h, p_l, g_mix, w_in, g_gv, w_spatial, b_spatial, w_gproj, g_cq, w_uq, g_ckv, w_ukv,
           w_mla_o, w_out, g_moe, w_router_g, b_router_g, w_router_e, b_router_e,
           w_e_gate, w_e_up, w_e_down, g_ple, w_ple_gate, w_ple_proj, g_out):
    B, S, D = h.shape
    N = B * S
    x2 = h.reshape(N, D)

    cu, cv, ccq, cckv, ckr, cga = (GMLP_WIDTH, 2 * GMLP_WIDTH, 2 * GMLP_WIDTH + Q_LORA,
                                   2 * GMLP_WIDTH + Q_LORA + KV_LORA,
                                   2 * GMLP_WIDTH + Q_LORA + KV_LORA + QK_ROPE_DIM,
                                   2 * GMLP_WIDTH + Q_LORA + KV_LORA + QK_ROPE_DIM + D_MODEL)
    w_in_p = jnp.concatenate(
        [w_in[:, :cckv], w_in[:, ckr:], w_in[:, cckv:ckr],
         jnp.zeros((D, LANES - QK_ROPE_DIM), w_in.dtype)], axis=1).astype(BF16)
    w_uq_h = w_uq.reshape(Q_LORA, MLA_HEADS, QK_NOPE_DIM + QK_ROPE_DIM)
    w_uq_p = jnp.concatenate(
        [w_uq_h, jnp.zeros((Q_LORA, MLA_HEADS, QK_PAD - QK_NOPE_DIM - QK_ROPE_DIM), w_uq.dtype)],
        axis=2).reshape(Q_LORA, MLA_HEADS * QK_PAD).astype(BF16)
    rc, rs1, rs2 = _rope_tables(S)

    u, v, sga, sgb, q, k, vv = _inproj(
        x2, g_mix[None], w_in_p, g_gv[None], g_cq[None], g_ckv[None], w_uq_p, w_ukv.astype(BF16),
        rc, rs1, rs2, B, S)
    ma = _gmlp(u, v, sga, w_spatial.astype(BF16), b_spatial.T, w_gproj.astype(BF16))
    o = _attention(q, k, vv)

    wr = jnp.concatenate(
        [w_router_e, w_router_g, jnp.zeros((D, LANES - N_EXPERTS - N_GROUPS), w_router_e.dtype)],
        axis=1).astype(BF16)
    br = jnp.concatenate(
        [b_router_e.reshape(-1), b_router_g, jnp.zeros((LANES - N_EXPERTS - N_GROUPS,), F32)])[None]
    h1, m, ri, rw, cnt = _out_route(
        o.reshape(N, D), ma, sgb, x2, w_mla_o.astype(BF16), w_out.astype(BF16), g_moe[None], wr, br)

    counts = cnt[0, :N_EXPERTS]
    padded = (counts + MOE_BLK - 1) // MOE_BLK * MOE_BLK
    pad_end = jnp.cumsum(padded)
    pad_start = pad_end - padded
    P = 2 * N + N_EXPERTS * MOE_BLK
    nblk = P // MOE_BLK
    blk_start = jnp.arange(nblk, dtype=jnp.int32) * MOE_BLK
    blk_e = jnp.minimum(
        jnp.sum((pad_end[None, :] <= blk_start[:, None]).astype(jnp.int32), axis=1),
        N_EXPERTS - 1).astype(jnp.int32)
    nused = (pad_end[-1:] // MOE_BLK).astype(jnp.int32)
    dest = (pad_start[ri[:, 0:2]] + ri[:, 2:4]).astype(jnp.int32)
    dest_t = dest.T.reshape(-1)
    tok = jnp.arange(N, dtype=jnp.int32)
    tok2 = jnp.concatenate([tok, tok])

    buf = _row_copy(tok2, dest_t, m.reshape(N, 8, LANES), P, jnp.zeros((P, 8, LANES), F32))
    ys = _experts(blk_e, nused, buf.reshape(P, D), w_e_gate.astype(BF16), w_e_up.astype(BF16),
                  w_e_down.astype(BF16))
    y01 = _row_copy(dest_t, jnp.arange(2 * N, dtype=jnp.int32), ys.reshape(P, 8, LANES), 2 * N)
    out = _final(h1, y01.reshape(2, N, D), rw, p_l.reshape(N, PLE_DIM), g_ple[None],
                 w_ple_gate.astype(BF16), w_ple_proj.astype(BF16), g_out[None])
    return out.reshape(B, S, D)


def kernel(x, p, g_mix, w_in, g_gv, w_spatial, b_spatial, w_gproj, g_cq, w_uq, g_ckv, w_ukv, w_mla_o,
           w_out, g_moe, w_router_g, b_router_g, w_router_e, b_router_e, w_e_gate, w_e_up, w_e_down,
           g_ple, w_ple_gate, w_ple_proj, g_final):
    depth = p.shape[0]
    assert depth == 1, "the final rmsnorm is fused into the single layer's last kernel"
    i = 0
    return _layer(x, p[i], g_mix[i], w_in[i], g_gv[i], w_spatial[i], b_spatial[i], w_gproj[i], g_cq[i],
                  w_uq[i], g_ckv[i], w_ukv[i], w_mla_o[i], w_out[i], g_moe[i], w_router_g[i],
                  b_router_g[i], w_router_e[i], b_router_e[i], w_e_gate[i], w_e_up[i], w_e_down[i],
                  g_ple[i], w_ple_gate[i], w_ple_proj[i], g_final)
```

```python
import functools

import jax
import jax.numpy as jnp
from jax import lax
from jax.experimental import pallas as pl
from jax.experimental.pallas import tpu as pltpu

F32 = jnp.float32
BF16 = jnp.bfloat16

D_MODEL = 1024
CHUNK = 64
PLE_DIM = 256
GMLP_BLOCK = 128
GMLP_GROUPS = 12
GMLP_WIDTH = 1536
MLA_HEADS = 8
QK_NOPE_DIM = 128
QK_ROPE_DIM = 64
V_HEAD_DIM = 128
Q_LORA = 384
KV_LORA = 256
ROPE_THETA = 10000.0
N_GROUPS = 8
EXPERTS_PER_GROUP = 8
N_EXPERTS = 64
D_EXPERT = 256
EPS = 1e-6
LOG2E = 1.4426950408889634

LANES = 128
SUBLANES = 8
QK_PAD = 256
VMEM_LIMIT = 56 * 1024 * 1024

C_U = 0
C_V = C_U + GMLP_WIDTH
C_CQ = C_V + GMLP_WIDTH
C_CKV = C_CQ + Q_LORA
C_GA = C_CKV + KV_LORA
C_GB = C_GA + D_MODEL
C_KR = C_GB + D_MODEL
C_END = C_KR + LANES

TM_IN = 256
TM_GMLP = 512
TQ = 512
TM_OUT = 512
TM_ROWS = 512
MOE_BLK = 256
DMA_UNROLL = 8


def _rms(x):
    return x * lax.rsqrt(jnp.mean(x * x, axis=-1, keepdims=True) + EPS)


def _dot(a, b):
    return jnp.dot(a, b, preferred_element_type=F32)


def _store_rows3(ref3, x):
    for j in range(SUBLANES):
        ref3[:, j, :] = x[:, j * LANES:(j + 1) * LANES]


def _load_rows3(ref3):
    return jnp.concatenate([ref3[:, j, :] for j in range(SUBLANES)], axis=1)


def _rope128(t, rc, rs1, rs2):
    r1 = pltpu.roll(t, 96, axis=1)
    r2 = pltpu.roll(t, 32, axis=1)
    return t * rc + r1 * rs1 + r2 * rs2


def _inproj_kernel(x_ref, gmix_ref, win_ref, ggv_ref, gcq_ref, gckv_ref, wuq_ref, wukv_ref,
                   rc_ref, rs1_ref, rs2_ref,
                   u_ref, v_ref, sga_ref, sgb_ref, q_ref, k_ref, vv_ref):
    x = x_ref[...]
    ab = (_rms(x) * gmix_ref[...]).astype(BF16)

    def proj(c0, c1):
        return _dot(ab, win_ref[:, c0:c1])

    u_ref[...] = jax.nn.gelu(proj(C_U, C_V)).astype(BF16)

    zv = jax.nn.gelu(proj(C_V, C_CQ))
    xc = zv - jnp.mean(zv, axis=-1, keepdims=True)
    vln = xc * lax.rsqrt(jnp.mean(xc * xc, axis=-1, keepdims=True) + EPS)
    v_ref[...] = (vln * ggv_ref[...]).astype(BF16)

    sga_ref[...] = jax.nn.sigmoid(proj(C_GA, C_GB)).astype(BF16)
    sgb_ref[...] = jax.nn.sigmoid(proj(C_GB, C_KR)).astype(BF16)

    rc = rc_ref[...]
    rs1 = rs1_ref[...]
    rs2 = rs2_ref[...]
    kpe = _rope128(proj(C_KR, C_END), rc, rs1, rs2).astype(BF16)

    cqn = (_rms(proj(C_CQ, C_CKV)) * gcq_ref[...]).astype(BF16)
    ckvn = (_rms(proj(C_CKV, C_GA)) * gckv_ref[...]).astype(BF16)
    scale = (QK_NOPE_DIM + QK_ROPE_DIM) ** -0.5 * LOG2E
    for h in range(MLA_HEADS):
        qh = _dot(cqn, wuq_ref[:, h * QK_PAD:(h + 1) * QK_PAD])
        q_ref[0, h, :, 0:LANES] = (qh[:, 0:LANES] * scale).astype(BF16)
        q_ref[0, h, :, LANES:QK_PAD] = (_rope128(qh[:, LANES:QK_PAD], rc, rs1, rs2) * scale).astype(BF16)
        kvh = _dot(ckvn, wukv_ref[:, h * 256:(h + 1) * 256])
        k_ref[0, h, :, 0:LANES] = kvh[:, 0:LANES].astype(BF16)
        k_ref[0, h, :, LANES:QK_PAD] = kpe
        vv_ref[0, h] = kvh[:, LANES:256].astype(BF16)


def _inproj(x2, g_mix, w_in_p, g_gv, g_cq, g_ckv, w_uq_p, w_ukv_b, rc, rs1, rs2, B, S):
    N = x2.shape[0]
    tm = TM_IN
    spt = S // tm
    row = lambda i: (i, 0)
    const = lambda i: (0, 0)
    pos = lambda i: (i % spt, 0)
    head = lambda i: (i // spt, 0, i % spt, 0)
    return pl.pallas_call(
        _inproj_kernel,
        grid=(N // tm,),
        in_specs=[
            pl.BlockSpec((tm, D_MODEL), row),
            pl.BlockSpec((1, D_MODEL), const),
            pl.BlockSpec((D_MODEL, C_END), const),
            pl.BlockSpec((1, GMLP_WIDTH), const),
            pl.BlockSpec((1, Q_LORA), const),
            pl.BlockSpec((1, KV_LORA), const),
            pl.BlockSpec((Q_LORA, MLA_HEADS * QK_PAD), const),
            pl.BlockSpec((KV_LORA, MLA_HEADS * 256), const),
            pl.BlockSpec((tm, LANES), pos),
            pl.BlockSpec((tm, LANES), pos),
            pl.BlockSpec((tm, LANES), pos),
        ],
        out_specs=[
            pl.BlockSpec((tm, GMLP_WIDTH), row),
            pl.BlockSpec((tm, GMLP_WIDTH), row),
            pl.BlockSpec((tm, D_MODEL), row),
            pl.BlockSpec((tm, D_MODEL), row),
            pl.BlockSpec((1, MLA_HEADS, tm, QK_PAD), head),
            pl.BlockSpec((1, MLA_HEADS, tm, QK_PAD), head),
            pl.BlockSpec((1, MLA_HEADS, tm, V_HEAD_DIM), head),
        ],
        out_shape=[
            jax.ShapeDtypeStruct((N, GMLP_WIDTH), BF16),
            jax.ShapeDtypeStruct((N, GMLP_WIDTH), BF16),
            jax.ShapeDtypeStruct((N, D_MODEL), BF16),
            jax.ShapeDtypeStruct((N, D_MODEL), BF16),
            jax.ShapeDtypeStruct((B, MLA_HEADS, S, QK_PAD), BF16),
            jax.ShapeDtypeStruct((B, MLA_HEADS, S, QK_PAD), BF16),
            jax.ShapeDtypeStruct((B, MLA_HEADS, S, V_HEAD_DIM), BF16),
        ],
        compiler_params=pltpu.CompilerParams(
            dimension_semantics=("arbitrary",), vmem_limit_bytes=VMEM_LIMIT),
        name="inproj",
    )(x2, g_mix, w_in_p, g_gv, g_cq, g_ckv, w_uq_p, w_ukv_b, rc, rs1, rs2)


def _gmlp_kernel(u_ref, v_ref, sga_ref, wsp_ref, bsp_ref, wproj_ref, ma_ref, y_sc):
    nb = TM_GMLP // GMLP_BLOCK
    t_out = lax.broadcasted_iota(jnp.int32, (GMLP_BLOCK, GMLP_BLOCK), 0)
    s_in = lax.broadcasted_iota(jnp.int32, (GMLP_BLOCK, GMLP_BLOCK), 1)
    mask = (s_in // CHUNK) <= (t_out // CHUNK)
    for g in range(GMLP_GROUPS):
        c0 = g * LANES
        w = jnp.where(mask, wsp_ref[g], jnp.zeros((), BF16))
        rhs = jnp.concatenate(
            [v_ref[r * GMLP_BLOCK:(r + 1) * GMLP_BLOCK, c0:c0 + LANES] for r in range(nb)], axis=1)
        sv = _dot(w, rhs) + bsp_ref[:, g:g + 1]
        for r in range(nb):
            rows = slice(r * GMLP_BLOCK, (r + 1) * GMLP_BLOCK)
            ub = u_ref[rows, c0:c0 + LANES].astype(F32)
            y_sc[rows, c0:c0 + LANES] = (ub * sv[:, r * LANES:(r + 1) * LANES]).astype(BF16)
    ya = _dot(y_sc[...], wproj_ref[...])
    ma_ref[...] = (sga_ref[...].astype(F32) * ya).astype(BF16)


def _gmlp(u, v, sga, wsp_b, bsp_t, wproj_b):
    N = u.shape[0]
    tm = TM_GMLP
    row = lambda i: (i, 0)
    return pl.pallas_call(
        _gmlp_kernel,
        grid=(N // tm,),
        in_specs=[
            pl.BlockSpec((tm, GMLP_WIDTH), row),
            pl.BlockSpec((tm, GMLP_WIDTH), row),
            pl.BlockSpec((tm, D_MODEL), row),
            pl.BlockSpec((GMLP_GROUPS, GMLP_BLOCK, GMLP_BLOCK), lambda i: (0, 0, 0)),
            pl.BlockSpec((GMLP_BLOCK, GMLP_GROUPS), lambda i: (0, 0)),
            pl.BlockSpec((GMLP_WIDTH, D_MODEL), lambda i: (0, 0)),
        ],
        out_specs=pl.BlockSpec((tm, D_MODEL), row),
        out_shape=jax.ShapeDtypeStruct((N, D_MODEL), BF16),
        scratch_shapes=[pltpu.VMEM((tm, GMLP_WIDTH), BF16)],
        compiler_params=pltpu.CompilerParams(
            dimension_semantics=("arbitrary",), vmem_limit_bytes=VMEM_LIMIT),
        name="gmlp",
    )(u, v, sga, wsp_b, bsp_t, wproj_b)


NEG_BIG = -1e30


def _attn_kernel(q_ref, k_ref, v_ref, o_ref, m_sc, l_sc, acc_sc):
    qi = pl.program_id(2)
    q = q_ref[0, 0]
    m_sc[...] = jnp.full(m_sc.shape, NEG_BIG, F32)
    l_sc[...] = jnp.zeros(l_sc.shape, F32)
    acc_sc[...] = jnp.zeros(acc_sc.shape, F32)
    nt = TQ // LANES

    def step(j, masked):
        start = pl.multiple_of(j * TQ, TQ)
        kb = k_ref[0, 0, pl.ds(start, TQ), :]
        vb = v_ref[0, 0, pl.ds(start, TQ), :]
        s = lax.dot_general(q, kb, (((1,), (1,)), ((), ())), preferred_element_type=F32)
        if masked:
            qc = lax.broadcasted_iota(jnp.int32, (TQ, TQ), 0) // CHUNK
            kc = lax.broadcasted_iota(jnp.int32, (TQ, TQ), 1) // CHUNK
            s = jnp.where(kc <= qc, s, NEG_BIG)
        tiles = [s[:, c * LANES:(c + 1) * LANES] for c in range(nt)]
        tile_max = functools.reduce(jnp.maximum, tiles)
        m_prev = m_sc[...]
        m_new = jnp.maximum(m_prev, jnp.max(tile_max, axis=-1, keepdims=True))
        alpha = jnp.exp2(m_prev - m_new)
        ps = [jnp.exp2(t - m_new) for t in tiles]
        l_sc[...] = alpha * l_sc[...] + functools.reduce(jnp.add, ps)
        p = jnp.concatenate([t.astype(BF16) for t in ps], axis=1)
        acc_sc[...] = alpha * acc_sc[...] + _dot(p, vb)
        m_sc[...] = m_new

    def body(j, carry):
        step(j, False)
        return carry

    lax.fori_loop(0, qi, body, 0)
    step(qi, True)
    l = jnp.sum(l_sc[...], axis=-1, keepdims=True)
    o_ref[0] = (acc_sc[...] / l).astype(BF16)


def _attention(q, k, v):
    B, H, S, _ = q.shape
    return pl.pallas_call(
        _attn_kernel,
        grid=(B, H, S // TQ),
        in_specs=[
            pl.BlockSpec((1, 1, TQ, QK_PAD), lambda b, h, i: (b, h, i, 0)),
            pl.BlockSpec((1, 1, S, QK_PAD), lambda b, h, i: (b, h, 0, 0)),
            pl.BlockSpec((1, 1, S, V_HEAD_DIM), lambda b, h, i: (b, h, 0, 0)),
        ],
        out_specs=pl.BlockSpec((1, TQ, V_HEAD_DIM), lambda b, h, i: (b, i, h)),
        out_shape=jax.ShapeDtypeStruct((B, S, H * V_HEAD_DIM), BF16),
        scratch_shapes=[
            pltpu.VMEM((TQ, LANES), F32),
            pltpu.VMEM((TQ, LANES), F32),
            pltpu.VMEM((TQ, V_HEAD_DIM), F32),
        ],
        compiler_params=pltpu.CompilerParams(
            dimension_semantics=("arbitrary", "arbitrary", "arbitrary"),
            vmem_limit_bytes=VMEM_LIMIT),
        name="attention",
    )(q, k, v)


def _lane_sum(x):
    return jnp.sum(x, axis=-1, keepdims=True)


def _first_lane(hit, lane_f):
    return jnp.min(jnp.where(hit, lane_f, float(LANES)), axis=-1, keepdims=True).astype(jnp.int32)


def _out_route_kernel(o_ref, ma_ref, sgb_ref, x_ref, wo_ref, wout_ref, gmoe_ref, wr_ref, br_ref,
                      h1_ref, m_ref, ri_ref, rw_ref, cnt_ref, carry_sc):
    i = pl.program_id(0)
    tm = TM_OUT

    @pl.when(i == 0)
    def _():
        carry_sc[...] = jnp.zeros(carry_sc.shape, F32)

    yb = _dot(o_ref[...], wo_ref[...])
    merged = ma_ref[...].astype(F32) + sgb_ref[...].astype(F32) * yb
    h1 = x_ref[...] + _dot(merged.astype(BF16), wout_ref[...])
    h1_ref[...] = h1
    m = _rms(h1) * gmoe_ref[...]
    _store_rows3(m_ref, m)

    logits = _dot(m.astype(BF16), wr_ref[...])
    lane = lax.broadcasted_iota(jnp.int32, (tm, LANES), 1)
    lane_f = lane.astype(F32)
    bias = br_ref[...]
    is_g = (lane >= N_EXPERTS) & (lane < N_EXPERTS + N_GROUPS)
    neg = jnp.float32(-jnp.inf)

    gl = jnp.where(is_g, logits, neg)
    ge = jnp.where(is_g, jnp.exp(gl - jnp.max(gl, axis=-1, keepdims=True)), 0.0)
    g_prob = ge / _lane_sum(ge)
    g_score = jnp.where(is_g, g_prob + bias, neg)
    g_best = jnp.max(g_score, axis=-1, keepdims=True)
    g_lane = _first_lane(g_score == g_best, lane_f)
    g_w = _lane_sum(jnp.where(lane == g_lane, g_prob, 0.0))
    g_idx = g_lane - N_EXPERTS

    in_g = (lane // EXPERTS_PER_GROUP) == g_idx
    el = jnp.where(in_g, logits, neg)
    ee = jnp.where(in_g, jnp.exp(el - jnp.max(el, axis=-1, keepdims=True)), 0.0)
    e_prob = ee / _lane_sum(ee)
    e_score = jnp.where(in_g, e_prob + bias, neg)
    best1 = jnp.max(e_score, axis=-1, keepdims=True)
    id1 = _first_lane(e_score == best1, lane_f)
    e_score2 = jnp.where(lane == id1, neg, e_score)
    best2 = jnp.max(e_score2, axis=-1, keepdims=True)
    id2 = _first_lane(e_score2 == best2, lane_f)
    p1 = _lane_sum(jnp.where(lane == id1, e_prob, 0.0))
    p2 = _lane_sum(jnp.where(lane == id2, e_prob, 0.0))
    psum = p1 + p2
    w1 = g_w * (p1 / psum)
    w2 = g_w * (p2 / psum)

    oh = ((lane == id1) | (lane == id2 + N_EXPERTS)).astype(BF16)
    r_out = lax.broadcasted_iota(jnp.int32, (tm, tm), 0)
    r_in = lax.broadcasted_iota(jnp.int32, (tm, tm), 1)
    tri = (r_in < r_out).astype(BF16)
    prefix = _dot(tri, oh)
    tot = jnp.sum(oh.astype(F32), axis=0, keepdims=True)
    tot_sw = pltpu.roll(tot, N_EXPERTS, axis=1)
    lane1 = lax.broadcasted_iota(jnp.int32, (1, LANES), 1)
    carry = carry_sc[...]
    base = carry + jnp.where(lane1 >= N_EXPERTS, tot_sw, 0.0)
    rk = oh.astype(F32) * (base + prefix)
    rank1 = _lane_sum(jnp.where(lane < N_EXPERTS, rk, 0.0))
    rank2 = _lane_sum(jnp.where(lane >= N_EXPERTS, rk, 0.0))
    carry_new = carry + tot + tot_sw
    carry_sc[...] = carry_new
    cnt_ref[...] = carry_new.astype(jnp.int32)

    ri = jnp.where(lane == 0, id1, jnp.where(lane == 1, id2, 0))
    ri = jnp.where(lane == 2, rank1.astype(jnp.int32), ri)
    ri = jnp.where(lane == 3, rank2.astype(jnp.int32), ri)
    ri_ref[...] = ri
    rw_ref[...] = jnp.where(lane == 0, w1, jnp.where(lane == 1, w2, 0.0))


def _out_route(o2, ma, sgb, x2, wo_b, wout_b, g_moe, wr_b, br):
    N = x2.shape[0]
    tm = TM_OUT
    row = lambda i: (i, 0)
    const = lambda i: (0, 0)
    return pl.pallas_call(
        _out_route_kernel,
        grid=(N // tm,),
        in_specs=[
            pl.BlockSpec((tm, D_MODEL), row),
            pl.BlockSpec((tm, D_MODEL), row),
            pl.BlockSpec((tm, D_MODEL), row),
            pl.BlockSpec((tm, D_MODEL), row),
            pl.BlockSpec((D_MODEL, D_MODEL), const),
            pl.BlockSpec((D_MODEL, D_MODEL), const),
            pl.BlockSpec((1, D_MODEL), const),
            pl.BlockSpec((D_MODEL, LANES), const),
            pl.BlockSpec((1, LANES), const),
        ],
        out_specs=[
            pl.BlockSpec((tm, D_MODEL), row),
            pl.BlockSpec((tm, SUBLANES, LANES), lambda i: (i, 0, 0)),
            pl.BlockSpec((tm, LANES), row),
            pl.BlockSpec((tm, LANES), row),
            pl.BlockSpec((1, LANES), const),
        ],
        out_shape=[
            jax.ShapeDtypeStruct((N, D_MODEL), F32),
            jax.ShapeDtypeStruct((N, SUBLANES, LANES), F32),
            jax.ShapeDtypeStruct((N, LANES), jnp.int32),
            jax.ShapeDtypeStruct((N, LANES), F32),
            jax.ShapeDtypeStruct((1, LANES), jnp.int32),
        ],
        scratch_shapes=[pltpu.VMEM((1, LANES), F32)],
        compiler_params=pltpu.CompilerParams(
            dimension_semantics=("arbitrary",), vmem_limit_bytes=VMEM_LIMIT),
        name="out_route",
    )(o2, ma, sgb, x2, wo_b, wout_b, g_moe, wr_b, br)


def _dispatch_kernel(dest_ref, m_ref, buf_in_ref, buf_ref, sem):
    del buf_in_ref
    tm = TM_ROWS

    def issue(c, carry):
        for u in range(DMA_UNROLL):
            r = c * DMA_UNROLL + u
            pltpu.make_async_copy(m_ref.at[r], buf_ref.at[dest_ref[0, 0, r]], sem).start()
            pltpu.make_async_copy(m_ref.at[r], buf_ref.at[dest_ref[0, 0, tm + r]], sem).start()
        return carry

    lax.fori_loop(0, tm // DMA_UNROLL, issue, 0)
    for _ in range(2):
        pltpu.make_async_copy(m_ref, buf_ref.at[pl.ds(0, tm)], sem).wait()


def _dispatch(dest_tiles, m3, buf_init):
    N = m3.shape[0]
    tm = TM_ROWS
    return pl.pallas_call(
        _dispatch_kernel,
        grid=(N // tm,),
        in_specs=[
            pl.BlockSpec((1, 1, 2 * tm), lambda i: (i, 0, 0), memory_space=pltpu.SMEM),
            pl.BlockSpec((tm, SUBLANES, LANES), lambda i: (i, 0, 0)),
            pl.BlockSpec(memory_space=pl.ANY),
        ],
        out_specs=pl.BlockSpec(memory_space=pl.ANY),
        out_shape=jax.ShapeDtypeStruct(buf_init.shape, buf_init.dtype),
        scratch_shapes=[pltpu.SemaphoreType.DMA(())],
        input_output_aliases={2: 0},
        compiler_params=pltpu.CompilerParams(dimension_semantics=("arbitrary",)),
        name="dispatch",
    )(dest_tiles, m3, buf_init)


def _expert_kernel(blk_e_ref, nused_ref, x_ref, wg_ref, wu_ref, wd_ref, y_ref):
    del blk_e_ref
    i = pl.program_id(0)

    @pl.when(i < nused_ref[0])
    def _():
        xb = _load_rows3(x_ref).astype(BF16)
        hg = _dot(xb, wg_ref[0])
        hu = _dot(xb, wu_ref[0])
        hdn = (jax.nn.silu(hg) * hu).astype(BF16)
        _store_rows3(y_ref, _dot(hdn, wd_ref[0]))

    @pl.when(i >= nused_ref[0])
    def _():
        y_ref[...] = jnp.zeros(y_ref.shape, F32)


def _experts(blk_e, nused, buf3, wg_b, wu_b, wd_b):
    P = buf3.shape[0]
    nblk = P // MOE_BLK
    rows = pl.BlockSpec((MOE_BLK, SUBLANES, LANES), lambda i, be, nu: (i, 0, 0))
    grid_spec = pltpu.PrefetchScalarGridSpec(
        num_scalar_prefetch=2,
        grid=(nblk,),
        in_specs=[
            rows,
            pl.BlockSpec((1, D_MODEL, D_EXPERT), lambda i, be, nu: (be[i], 0, 0)),
            pl.BlockSpec((1, D_MODEL, D_EXPERT), lambda i, be, nu: (be[i], 0, 0)),
            pl.BlockSpec((1, D_EXPERT, D_MODEL), lambda i, be, nu: (be[i], 0, 0)),
        ],
        out_specs=rows,
    )
    return pl.pallas_call(
        _expert_kernel,
        grid_spec=grid_spec,
        out_shape=jax.ShapeDtypeStruct(buf3.shape, F32),
        compiler_params=pltpu.CompilerParams(
            dimension_semantics=("arbitrary",), vmem_limit_bytes=VMEM_LIMIT),
        name="experts",
    )(blk_e, nused, buf3, wg_b, wu_b, wd_b)


def _final_kernel(dcur_ref, dnext_ref, h1_ref, rw_ref, p_ref, gple_ref, wpg_ref, wpp_ref, gfin_ref,
                  ys_ref, out_ref, ybuf, sem):
    i = pl.program_id(0)
    n = pl.num_programs(0)
    tm = TM_ROWS

    def issue(d_ref, slot):
        def body(c, carry):
            for u in range(DMA_UNROLL):
                r = c * DMA_UNROLL + u
                pltpu.make_async_copy(ys_ref.at[d_ref[0, 0, r]], ybuf.at[slot, r], sem.at[slot]).start()
            return carry

        lax.fori_loop(0, 2 * tm // DMA_UNROLL, body, 0)

    @pl.when(i == 0)
    def _():
        issue(dcur_ref, 0)

    @pl.when(i + 1 < n)
    def _():
        issue(dnext_ref, (i + 1) % 2)

    slot = i % 2
    pltpu.make_async_copy(ys_ref.at[pl.ds(0, 2 * tm)], ybuf.at[slot], sem.at[slot]).wait()
    y0 = _load_rows3(ybuf.at[slot, pl.ds(0, tm)])
    y1 = _load_rows3(ybuf.at[slot, pl.ds(tm, tm)])
    rw = rw_ref[...]
    h2 = h1_ref[...] + (y0 * rw[:, 0:1] + y1 * rw[:, 1:2])
    n3 = (_rms(h2) * gple_ref[...]).astype(BF16)
    gate = jax.nn.sigmoid(_dot(n3, wpg_ref[...]))
    pp = _dot(p_ref[...].astype(BF16), wpp_ref[...])
    h3 = h2 + gate * pp
    out_ref[...] = _rms(h3) * gfin_ref[...]


def _final(dest_tiles, h1, rw, p2, g_ple, wpg_b, wpp_b, g_final, ys3):
    N = h1.shape[0]
    tm = TM_ROWS
    nt = N // tm
    row = lambda i: (i, 0)
    const = lambda i: (0, 0)
    return pl.pallas_call(
        _final_kernel,
        grid=(nt,),
        in_specs=[
            pl.BlockSpec((1, 1, 2 * tm), lambda i: (i, 0, 0), memory_space=pltpu.SMEM),
            pl.BlockSpec((1, 1, 2 * tm), lambda i: (jnp.minimum(i + 1, nt - 1), 0, 0),
                         memory_space=pltpu.SMEM),
            pl.BlockSpec((tm, D_MODEL), row),
            pl.BlockSpec((tm, LANES), row),
            pl.BlockSpec((tm, PLE_DIM), row),
            pl.BlockSpec((1, D_MODEL), const),
            pl.BlockSpec((D_MODEL, D_MODEL), const),
            pl.BlockSpec((PLE_DIM, D_MODEL), const),
            pl.BlockSpec((1, D_MODEL), const),
            pl.BlockSpec(memory_space=pl.ANY),
        ],
        out_specs=pl.BlockSpec((tm, D_MODEL), row),
        out_shape=jax.ShapeDtypeStruct((N, D_MODEL), F32),
        scratch_shapes=[
            pltpu.VMEM((2, 2 * tm, SUBLANES, LANES), F32),
            pltpu.SemaphoreType.DMA((2,)),
        ],
        compiler_params=pltpu.CompilerParams(
            dimension_semantics=("arbitrary",), vmem_limit_bytes=VMEM_LIMIT),
        name="final",
    )(dest_tiles, dest_tiles, h1, rw, p2, g_ple, wpg_b, wpp_b, g_final, ys3)


def _rope_tables(S):
    inv_freq = ROPE_THETA ** (-jnp.arange(0, QK_ROPE_DIM, 2, dtype=F32) / QK_ROPE_DIM)
    ang = jnp.arange(S, dtype=F32)[:, None] * inv_freq[None, :]
    cos, sin = jnp.cos(ang), jnp.sin(ang)
    z = jnp.zeros_like(cos)
    rc = jnp.concatenate([cos, cos, z, z], axis=1)
    rs1 = jnp.concatenate([-sin, z, z, z], axis=1)
    rs2 = jnp.concatenate([z, sin, z, z], axis=1)
    return rc, rs1, rs2


def _layer(h, p_l, g_mix, w_in, g_gv, w_spatial, b_spatial, w_gproj, g_cq, w_uq, g_ckv, w_ukv,
           w_mla_o, w_out, g_moe, w_router_g, b_router_g, w_router_e, b_router_e,
           w_e_gate, w_e_up, w_e_down, g_ple, w_ple_gate, w_ple_proj, g_out):
    B, S, D = h.shape
    N = B * S
    x2 = h.reshape(N, D)

    cu, cv, ccq, cckv, ckr, cga = (GMLP_WIDTH, 2 * GMLP_WIDTH, 2 * GMLP_WIDTH + Q_LORA,
                                   2 * GMLP_WIDTH + Q_LORA + KV_LORA,
                                   2 * GMLP_WIDTH + Q_LORA + KV_LORA + QK_ROPE_DIM,
                                   2 * GMLP_WIDTH + Q_LORA + KV_LORA + QK_ROPE_DIM + D_MODEL)
    w_in_p = jnp.concatenate(
        [w_in[:, :cckv], w_in[:, ckr:], w_in[:, cckv:ckr],
         jnp.zeros((D, LANES - QK_ROPE_DIM), w_in.dtype)], axis=1).astype(BF16)
    w_uq_h = w_uq.reshape(Q_LORA, MLA_HEADS, QK_NOPE_DIM + QK_ROPE_DIM)
    w_uq_p = jnp.concatenate(
        [w_uq_h, jnp.zeros((Q_LORA, MLA_HEADS, QK_PAD - QK_NOPE_DIM - QK_ROPE_DIM), w_uq.dtype)],
        axis=2).reshape(Q_LORA, MLA_HEADS * QK_PAD).astype(BF16)
    rc, rs1, rs2 = _rope_tables(S)

    u, v, sga, sgb, q, k, vv = _inproj(
        x2, g_mix[None], w_in_p, g_gv[None], g_cq[None], g_ckv[None], w_uq_p, w_ukv.astype(BF16),
        rc, rs1, rs2, B, S)
    ma = _gmlp(u, v, sga, w_spatial.astype(BF16), b_spatial.T, w_gproj.astype(BF16))
    o = _attention(q, k, vv)

    wr = jnp.concatenate(
        [w_router_e, w_router_g, jnp.zeros((D, LANES - N_EXPERTS - N_GROUPS), w_router_e.dtype)],
        axis=1).astype(BF16)
    br = jnp.concatenate(
        [b_router_e.reshape(-1), b_router_g, jnp.zeros((LANES - N_EXPERTS - N_GROUPS,), F32)])[None]
    h1, m, ri, rw, cnt = _out_route(
        o.reshape(N, D), ma, sgb, x2, w_mla_o.astype(BF16), w_out.astype(BF16), g_moe[None], wr, br)

    counts = cnt[0, :N_EXPERTS]
    padded = (counts + MOE_BLK - 1) // MOE_BLK * MOE_BLK
    pad_end = jnp.cumsum(padded)
    pad_start = pad_end - padded
    P = 2 * N + N_EXPERTS * MOE_BLK
    nblk = P // MOE_BLK
    blk_start = jnp.arange(nblk, dtype=jnp.int32) * MOE_BLK
    blk_e = jnp.minimum(
        jnp.sum((pad_end[None, :] <= blk_start[:, None]).astype(jnp.int32), axis=1),
        N_EXPERTS - 1).astype(jnp.int32)
    nused = (pad_end[-1:] // MOE_BLK).astype(jnp.int32)
    dest = (pad_start[ri[:, 0:2]] + ri[:, 2:4]).astype(jnp.int32)
    nt = N // TM_ROWS
    dest_tiles = dest.reshape(nt, TM_ROWS, 2).transpose(0, 2, 1).reshape(nt, 1, 2 * TM_ROWS)

    buf = _dispatch(dest_tiles, m, jnp.zeros((P, SUBLANES, LANES), F32))
    ys = _experts(blk_e, nused, buf, w_e_gate.astype(BF16), w_e_up.astype(BF16),
                  w_e_down.astype(BF16))
    out = _final(dest_tiles, h1, rw, p_l.reshape(N, PLE_DIM), g_ple[None],
                 w_ple_gate.astype(BF16), w_ple_proj.astype(BF16), g_out[None], ys)
    return out.reshape(B, S, D)


def kernel(x, p, g_mix, w_in, g_gv, w_spatial, b_spatial, w_gproj, g_cq, w_uq, g_ckv, w_ukv, w_mla_o,
           w_out, g_moe, w_router_g, b_router_g, w_router_e, b_router_e, w_e_gate, w_e_up, w_e_down,
           g_ple, w_ple_gate, w_ple_proj, g_final):
    depth = p.shape[0]
    assert depth == 1, "the final rmsnorm is fused into the single layer's last kernel"
    i = 0
    return _layer(x, p[i], g_mix[i], w_in[i], g_gv[i], w_spatial[i], b_spatial[i], w_gproj[i], g_cq[i],
                  w_uq[i], g_ckv[i], w_ukv[i], w_mla_o[i], w_out[i], g_moe[i], w_router_g[i],
                  b_router_g[i], w_router_e[i], b_router_e[i], w_e_gate[i], w_e_up[i], w_e_down[i],
                  g_ple[i], w_ple_gate[i], w_ple_proj[i], g_final)
```

```python
import functools

import jax
import jax.numpy as jnp
from jax import lax
from jax.experimental import pallas as pl
from jax.experimental.pallas import tpu as pltpu

F32 = jnp.float32
BF16 = jnp.bfloat16

D_MODEL = 1024
CHUNK = 64
PLE_DIM = 256
GMLP_BLOCK = 128
GMLP_GROUPS = 12
GMLP_WIDTH = 1536
MLA_HEADS = 8
QK_NOPE_DIM = 128
QK_ROPE_DIM = 64
V_HEAD_DIM = 128
Q_LORA = 384
KV_LORA = 256
ROPE_THETA = 10000.0
N_GROUPS = 8
EXPERTS_PER_GROUP = 8
N_EXPERTS = 64
D_EXPERT = 256
EPS = 1e-6
LOG2E = 1.4426950408889634

LANES = 128
SUBLANES = 8
QK_PAD = 256
VMEM_LIMIT = 56 * 1024 * 1024

C_U = 0
C_V = C_U + GMLP_WIDTH
C_CQ = C_V + GMLP_WIDTH
C_CKV = C_CQ + Q_LORA
C_GA = C_CKV + KV_LORA
C_GB = C_GA + D_MODEL
C_KR = C_GB + D_MODEL
C_END = C_KR + LANES

TM_IN = 256
TM_GMLP = 512
TQ = 512
HEADS_PER_STEP = 4
TM_OUT = 512
TM_ROWS = 512
MOE_BLK = 256
DMA_UNROLL = 8


def _rms(x):
    return x * lax.rsqrt(jnp.mean(x * x, axis=-1, keepdims=True) + EPS)


def _dot(a, b):
    return jnp.dot(a, b, preferred_element_type=F32)


def _store_rows(ref2, x):
    rows = x.shape[0]
    for j in range(SUBLANES):
        ref2[pl.ds(j, rows, stride=SUBLANES), :] = x[:, j * LANES:(j + 1) * LANES]


def _load_rows(ref2, rows, first_row=0):
    return jnp.concatenate(
        [ref2[pl.ds(first_row * SUBLANES + j, rows, stride=SUBLANES), :] for j in range(SUBLANES)],
        axis=1)


def _rope128(t, rc, rs1, rs2):
    r1 = pltpu.roll(t, 96, axis=1)
    r2 = pltpu.roll(t, 32, axis=1)
    return t * rc + r1 * rs1 + r2 * rs2


def _inproj_kernel(x_ref, gmix_ref, win_ref, ggv_ref, gcq_ref, gckv_ref, wuq_ref, wukv_ref,
                   rc_ref, rs1_ref, rs2_ref,
                   u_ref, v_ref, sga_ref, sgb_ref, q_ref, k_ref, vv_ref):
    x = x_ref[...]
    ab = (_rms(x) * gmix_ref[...]).astype(BF16)

    def proj(c0, c1):
        return _dot(ab, win_ref[:, c0:c1])

    u_ref[...] = jax.nn.gelu(proj(C_U, C_V)).astype(BF16)

    zv = jax.nn.gelu(proj(C_V, C_CQ))
    xc = zv - jnp.mean(zv, axis=-1, keepdims=True)
    vln = xc * lax.rsqrt(jnp.mean(xc * xc, axis=-1, keepdims=True) + EPS)
    v_ref[...] = (vln * ggv_ref[...]).astype(BF16)

    sga_ref[...] = jax.nn.sigmoid(proj(C_GA, C_GB)).astype(BF16)
    sgb_ref[...] = jax.nn.sigmoid(proj(C_GB, C_KR)).astype(BF16)

    rc = rc_ref[...]
    rs1 = rs1_ref[...]
    rs2 = rs2_ref[...]
    kpe = _rope128(proj(C_KR, C_END), rc, rs1, rs2).astype(BF16)

    cqn = (_rms(proj(C_CQ, C_CKV)) * gcq_ref[...]).astype(BF16)
    ckvn = (_rms(proj(C_CKV, C_GA)) * gckv_ref[...]).astype(BF16)
    scale = (QK_NOPE_DIM + QK_ROPE_DIM) ** -0.5 * LOG2E
    for h in range(MLA_HEADS):
        qh = _dot(cqn, wuq_ref[:, h * QK_PAD:(h + 1) * QK_PAD])
        q_ref[0, h, :, 0:LANES] = (qh[:, 0:LANES] * scale).astype(BF16)
        q_ref[0, h, :, LANES:QK_PAD] = (_rope128(qh[:, LANES:QK_PAD], rc, rs1, rs2) * scale).astype(BF16)
        kvh = _dot(ckvn, wukv_ref[:, h * 256:(h + 1) * 256])
        k_ref[0, h, :, 0:LANES] = kvh[:, 0:LANES].astype(BF16)
        k_ref[0, h, :, LANES:QK_PAD] = kpe
        vv_ref[0, h] = kvh[:, LANES:256].astype(BF16)


def _inproj(x2, g_mix, w_in_p, g_gv, g_cq, g_ckv, w_uq_p, w_ukv_b, rc, rs1, rs2, B, S):
    N = x2.shape[0]
    tm = TM_IN
    spt = S // tm
    row = lambda i: (i, 0)
    const = lambda i: (0, 0)
    pos = lambda i: (i % spt, 0)
    head = lambda i: (i // spt, 0, i % spt, 0)
    return pl.pallas_call(
        _inproj_kernel,
        grid=(N // tm,),
        in_specs=[
            pl.BlockSpec((tm, D_MODEL), row),
            pl.BlockSpec((1, D_MODEL), const),
            pl.BlockSpec((D_MODEL, C_END), const),
            pl.BlockSpec((1, GMLP_WIDTH), const),
            pl.BlockSpec((1, Q_LORA), const),
            pl.BlockSpec((1, KV_LORA), const),
            pl.BlockSpec((Q_LORA, MLA_HEADS * QK_PAD), const),
            pl.BlockSpec((KV_LORA, MLA_HEADS * 256), const),
            pl.BlockSpec((tm, LANES), pos),
            pl.BlockSpec((tm, LANES), pos),
            pl.BlockSpec((tm, LANES), pos),
        ],
        out_specs=[
            pl.BlockSpec((tm, GMLP_WIDTH), row),
            pl.BlockSpec((tm, GMLP_WIDTH), row),
            pl.BlockSpec((tm, D_MODEL), row),
            pl.BlockSpec((tm, D_MODEL), row),
            pl.BlockSpec((1, MLA_HEADS, tm, QK_PAD), head),
            pl.BlockSpec((1, MLA_HEADS, tm, QK_PAD), head),
            pl.BlockSpec((1, MLA_HEADS, tm, V_HEAD_DIM), head),
        ],
        out_shape=[
            jax.ShapeDtypeStruct((N, GMLP_WIDTH), BF16),
            jax.ShapeDtypeStruct((N, GMLP_WIDTH), BF16),
            jax.ShapeDtypeStruct((N, D_MODEL), BF16),
            jax.ShapeDtypeStruct((N, D_MODEL), BF16),
            jax.ShapeDtypeStruct((B, MLA_HEADS, S, QK_PAD), BF16),
            jax.ShapeDtypeStruct((B, MLA_HEADS, S, QK_PAD), BF16),
            jax.ShapeDtypeStruct((B, MLA_HEADS, S, V_HEAD_DIM), BF16),
        ],
        compiler_params=pltpu.CompilerParams(
            dimension_semantics=("arbitrary",), vmem_limit_bytes=VMEM_LIMIT),
        name="inproj",
    )(x2, g_mix, w_in_p, g_gv, g_cq, g_ckv, w_uq_p, w_ukv_b, rc, rs1, rs2)


def _gmlp_kernel(u_ref, v_ref, sga_ref, wsp_ref, bsp_ref, wproj_ref, ma_ref, y_sc):
    nb = TM_GMLP // GMLP_BLOCK
    t_out = lax.broadcasted_iota(jnp.int32, (GMLP_BLOCK, GMLP_BLOCK), 0)
    s_in = lax.broadcasted_iota(jnp.int32, (GMLP_BLOCK, GMLP_BLOCK), 1)
    mask = (s_in // CHUNK) <= (t_out // CHUNK)
    for g in range(GMLP_GROUPS):
        c0 = g * LANES
        w = jnp.where(mask, wsp_ref[g], jnp.zeros((), BF16))
        rhs = jnp.concatenate(
            [v_ref[r * GMLP_BLOCK:(r + 1) * GMLP_BLOCK, c0:c0 + LANES] for r in range(nb)], axis=1)
        sv = _dot(w, rhs) + bsp_ref[:, g:g + 1]
        for r in range(nb):
            rows = slice(r * GMLP_BLOCK, (r + 1) * GMLP_BLOCK)
            ub = u_ref[rows, c0:c0 + LANES].astype(F32)
            y_sc[rows, c0:c0 + LANES] = (ub * sv[:, r * LANES:(r + 1) * LANES]).astype(BF16)
    ya = _dot(y_sc[...], wproj_ref[...])
    ma_ref[...] = (sga_ref[...].astype(F32) * ya).astype(BF16)


def _gmlp(u, v, sga, wsp_b, bsp_t, wproj_b):
    N = u.shape[0]
    tm = TM_GMLP
    row = lambda i: (i, 0)
    return pl.pallas_call(
        _gmlp_kernel,
        grid=(N // tm,),
        in_specs=[
            pl.BlockSpec((tm, GMLP_WIDTH), row),
            pl.BlockSpec((tm, GMLP_WIDTH), row),
            pl.BlockSpec((tm, D_MODEL), row),
            pl.BlockSpec((GMLP_GROUPS, GMLP_BLOCK, GMLP_BLOCK), lambda i: (0, 0, 0)),
            pl.BlockSpec((GMLP_BLOCK, GMLP_GROUPS), lambda i: (0, 0)),
            pl.BlockSpec((GMLP_WIDTH, D_MODEL), lambda i: (0, 0)),
        ],
        out_specs=pl.BlockSpec((tm, D_MODEL), row),
        out_shape=jax.ShapeDtypeStruct((N, D_MODEL), BF16),
        scratch_shapes=[pltpu.VMEM((tm, GMLP_WIDTH), BF16)],
        compiler_params=pltpu.CompilerParams(
            dimension_semantics=("arbitrary",), vmem_limit_bytes=VMEM_LIMIT),
        name="gmlp",
    )(u, v, sga, wsp_b, bsp_t, wproj_b)


NEG_BIG = -1e30


def _attn_kernel(q_ref, k_ref, v_ref, o_ref, m_sc, l_sc, acc_sc):
    qi = pl.program_id(2)
    m_sc[...] = jnp.full(m_sc.shape, NEG_BIG, F32)
    l_sc[...] = jnp.zeros(l_sc.shape, F32)
    acc_sc[...] = jnp.zeros(acc_sc.shape, F32)
    nt = TQ // LANES

    def step(j, masked):
        start = pl.multiple_of(j * TQ, TQ)
        for hh in range(HEADS_PER_STEP):
            kb = k_ref[0, hh, pl.ds(start, TQ), :]
            vb = v_ref[0, hh, pl.ds(start, TQ), :]
            s = lax.dot_general(q_ref[0, hh], kb, (((1,), (1,)), ((), ())),
                                preferred_element_type=F32)
            if masked:
                qc = lax.broadcasted_iota(jnp.int32, (TQ, TQ), 0) // CHUNK
                kc = lax.broadcasted_iota(jnp.int32, (TQ, TQ), 1) // CHUNK
                s = jnp.where(kc <= qc, s, NEG_BIG)
            tiles = [s[:, c * LANES:(c + 1) * LANES] for c in range(nt)]
            tile_max = functools.reduce(jnp.maximum, tiles)
            m_prev = m_sc[hh]
            m_new = jnp.maximum(m_prev, jnp.max(tile_max, axis=-1, keepdims=True))
            alpha = jnp.exp2(m_prev - m_new)
            ps = [jnp.exp2(t - m_new) for t in tiles]
            l_sc[hh] = alpha * l_sc[hh] + functools.reduce(jnp.add, ps)
            p = jnp.concatenate([t.astype(BF16) for t in ps], axis=1)
            acc_sc[hh] = alpha * acc_sc[hh] + _dot(p, vb)
            m_sc[hh] = m_new

    def body(j, carry):
        step(j, False)
        return carry

    lax.fori_loop(0, qi, body, 0)
    step(qi, True)
    for hh in range(HEADS_PER_STEP):
        l = jnp.sum(l_sc[hh], axis=-1, keepdims=True)
        o_ref[0, :, hh * V_HEAD_DIM:(hh + 1) * V_HEAD_DIM] = (acc_sc[hh] / l).astype(BF16)


def _attention(q, k, v):
    B, H, S, _ = q.shape
    hps = HEADS_PER_STEP
    return pl.pallas_call(
        _attn_kernel,
        grid=(B, H // hps, S // TQ),
        in_specs=[
            pl.BlockSpec((1, hps, TQ, QK_PAD), lambda b, h, i: (b, h, i, 0)),
            pl.BlockSpec((1, hps, S, QK_PAD), lambda b, h, i: (b, h, 0, 0)),
            pl.BlockSpec((1, hps, S, V_HEAD_DIM), lambda b, h, i: (b, h, 0, 0)),
        ],
        out_specs=pl.BlockSpec((1, TQ, hps * V_HEAD_DIM), lambda b, h, i: (b, i, h)),
        out_shape=jax.ShapeDtypeStruct((B, S, H * V_HEAD_DIM), BF16),
        scratch_shapes=[
            pltpu.VMEM((hps, TQ, LANES), F32),
            pltpu.VMEM((hps, TQ, LANES), F32),
            pltpu.VMEM((hps, TQ, V_HEAD_DIM), F32),
        ],
        compiler_params=pltpu.CompilerParams(
            dimension_semantics=("arbitrary", "arbitrary", "arbitrary"),
            vmem_limit_bytes=VMEM_LIMIT),
        name="attention",
    )(q, k, v)


def _lane_sum(x):
    return jnp.sum(x, axis=-1, keepdims=True)


def _first_lane(hit, lane_f):
    return jnp.min(jnp.where(hit, lane_f, float(LANES)), axis=-1, keepdims=True).astype(jnp.int32)


def _out_route_kernel(o_ref, ma_ref, sgb_ref, x_ref, wo_ref, wout_ref, gmoe_ref, wr_ref, br_ref,
                      h1_ref, m_ref, rt_ref, rw_ref, cnt_ref, carry_sc):
    i = pl.program_id(0)
    tm = TM_OUT

    @pl.when(i == 0)
    def _():
        carry_sc[...] = jnp.zeros(carry_sc.shape, F32)

    yb = _dot(o_ref[...], wo_ref[...])
    merged = ma_ref[...].astype(F32) + sgb_ref[...].astype(F32) * yb
    h1 = x_ref[...] + _dot(merged.astype(BF16), wout_ref[...])
    h1_ref[...] = h1
    m = _rms(h1) * gmoe_ref[...]
    _store_rows(m_ref, m)

    logits = _dot(m.astype(BF16), wr_ref[...])
    lane = lax.broadcasted_iota(jnp.int32, (tm, LANES), 1)
    lane_f = lane.astype(F32)
    bias = br_ref[...]
    is_g = (lane >= N_EXPERTS) & (lane < N_EXPERTS + N_GROUPS)
    neg = jnp.float32(-jnp.inf)

    gl = jnp.where(is_g, logits, neg)
    ge = jnp.where(is_g, jnp.exp(gl - jnp.max(gl, axis=-1, keepdims=True)), 0.0)
    g_prob = ge / _lane_sum(ge)
    g_score = jnp.where(is_g, g_prob + bias, neg)
    g_best = jnp.max(g_score, axis=-1, keepdims=True)
    g_lane = _first_lane(g_score == g_best, lane_f)
    g_w = _lane_sum(jnp.where(lane == g_lane, g_prob, 0.0))
    g_idx = g_lane - N_EXPERTS

    in_g = (lane // EXPERTS_PER_GROUP) == g_idx
    el = jnp.where(in_g, logits, neg)
    ee = jnp.where(in_g, jnp.exp(el - jnp.max(el, axis=-1, keepdims=True)), 0.0)
    e_prob = ee / _lane_sum(ee)
    e_score = jnp.where(in_g, e_prob + bias, neg)
    best1 = jnp.max(e_score, axis=-1, keepdims=True)
    id1 = _first_lane(e_score == best1, lane_f)
    e_score2 = jnp.where(lane == id1, neg, e_score)
    best2 = jnp.max(e_score2, axis=-1, keepdims=True)
    id2 = _first_lane(e_score2 == best2, lane_f)
    p1 = _lane_sum(jnp.where(lane == id1, e_prob, 0.0))
    p2 = _lane_sum(jnp.where(lane == id2, e_prob, 0.0))
    psum = p1 + p2
    w1 = g_w * (p1 / psum)
    w2 = g_w * (p2 / psum)

    oh = ((lane == id1) | (lane == id2 + N_EXPERTS)).astype(BF16)
    r_out = lax.broadcasted_iota(jnp.int32, (tm, tm), 0)
    r_in = lax.broadcasted_iota(jnp.int32, (tm, tm), 1)
    tri = (r_in < r_out).astype(BF16)
    prefix = _dot(tri, oh)
    tot = jnp.sum(oh.astype(F32), axis=0, keepdims=True)
    tot_sw = pltpu.roll(tot, N_EXPERTS, axis=1)
    lane1 = lax.broadcasted_iota(jnp.int32, (1, LANES), 1)
    carry = carry_sc[...]
    base = carry + jnp.where(lane1 >= N_EXPERTS, tot_sw, 0.0)
    rk = oh.astype(F32) * (base + prefix)
    rank1 = _lane_sum(jnp.where(lane < N_EXPERTS, rk, 0.0))
    rank2 = _lane_sum(jnp.where(lane >= N_EXPERTS, rk, 0.0))
    carry_new = carry + tot + tot_sw
    carry_sc[...] = carry_new
    cnt_ref[...] = carry_new.astype(jnp.int32)

    ri = jnp.where(lane == 0, id1.astype(F32), jnp.where(lane == 1, id2.astype(F32), 0.0))
    ri = jnp.where(lane == 2, rank1, ri)
    ri = jnp.where(lane == 3, rank2, ri)
    rt_ref[...] = ri.T[0:SUBLANES, :].astype(jnp.int32)
    rw_ref[...] = jnp.where(lane == 0, w1, jnp.where(lane == 1, w2, 0.0))


def _out_route(o2, ma, sgb, x2, wo_b, wout_b, g_moe, wr_b, br):
    N = x2.shape[0]
    tm = TM_OUT
    row = lambda i: (i, 0)
    const = lambda i: (0, 0)
    return pl.pallas_call(
        _out_route_kernel,
        grid=(N // tm,),
        in_specs=[
            pl.BlockSpec((tm, D_MODEL), row),
            pl.BlockSpec((tm, D_MODEL), row),
            pl.BlockSpec((tm, D_MODEL), row),
            pl.BlockSpec((tm, D_MODEL), row),
            pl.BlockSpec((D_MODEL, D_MODEL), const),
            pl.BlockSpec((D_MODEL, D_MODEL), const),
            pl.BlockSpec((1, D_MODEL), const),
            pl.BlockSpec((D_MODEL, LANES), const),
            pl.BlockSpec((1, LANES), const),
        ],
        out_specs=[
            pl.BlockSpec((tm, D_MODEL), row),
            pl.BlockSpec((tm * SUBLANES, LANES), row),
            pl.BlockSpec((SUBLANES, tm), lambda i: (0, i)),
            pl.BlockSpec((tm, LANES), row),
            pl.BlockSpec((1, LANES), const),
        ],
        out_shape=[
            jax.ShapeDtypeStruct((N, D_MODEL), F32),
            jax.ShapeDtypeStruct((N * SUBLANES, LANES), F32),
            jax.ShapeDtypeStruct((SUBLANES, N), jnp.int32),
            jax.ShapeDtypeStruct((N, LANES), F32),
            jax.ShapeDtypeStruct((1, LANES), jnp.int32),
        ],
        scratch_shapes=[pltpu.VMEM((1, LANES), F32)],
        compiler_params=pltpu.CompilerParams(
            dimension_semantics=("arbitrary",), vmem_limit_bytes=VMEM_LIMIT),
        name="out_route",
    )(o2, ma, sgb, x2, wo_b, wout_b, g_moe, wr_b, br)


def _row_tile(ref2, row8):
    return ref2.at[pl.ds(pl.multiple_of(row8, SUBLANES), SUBLANES)]


def _dispatch_kernel(pad_end_ref, padded_ref, dest_ref, m_ref, buf_ref, zero_sc, sem, zsem):
    i = pl.program_id(0)
    tm = TM_ROWS
    blk8 = MOE_BLK * SUBLANES

    @pl.when(i == 0)
    def _():
        zero_sc[...] = jnp.zeros(zero_sc.shape, F32)

        def zero_copy(e):
            start = pl.multiple_of((pad_end_ref[e] - MOE_BLK) * SUBLANES, blk8)
            return pltpu.make_async_copy(zero_sc, buf_ref.at[pl.ds(start, blk8)], zsem)

        def start(e, carry):
            @pl.when(padded_ref[e] > 0)
            def _():
                zero_copy(e).start()
            return carry

        def wait(e, carry):
            @pl.when(padded_ref[e] > 0)
            def _():
                zero_copy(e).wait()
            return carry

        lax.fori_loop(0, N_EXPERTS, start, 0)
        lax.fori_loop(0, N_EXPERTS, wait, 0)

        def tail_copy(b):
            return pltpu.make_async_copy(
                zero_sc, buf_ref.at[pl.ds(pl.multiple_of(b * blk8, blk8), blk8)], zsem)

        nused = pad_end_ref[N_EXPERTS - 1] // MOE_BLK
        nblk = buf_ref.shape[0] // blk8
        lax.fori_loop(nused, nblk, lambda b, c: (tail_copy(b).start(), c)[1], 0)
        lax.fori_loop(nused, nblk, lambda b, c: (tail_copy(b).wait(), c)[1], 0)

    def issue(c, carry):
        for u in range(DMA_UNROLL):
            r = c * DMA_UNROLL + u
            src = _row_tile(m_ref, r * SUBLANES)
            pltpu.make_async_copy(src, _row_tile(buf_ref, dest_ref[0, 0, r]), sem).start()
            pltpu.make_async_copy(src, _row_tile(buf_ref, dest_ref[0, 0, tm + r]), sem).start()
        return carry

    lax.fori_loop(0, tm // DMA_UNROLL, issue, 0)
    for _ in range(2):
        pltpu.make_async_copy(m_ref, buf_ref.at[pl.ds(0, tm * SUBLANES)], sem).wait()


def _dispatch(pad_end, padded, dest8_tiles, m2, P):
    tm = TM_ROWS
    N = m2.shape[0] // SUBLANES
    grid_spec = pltpu.PrefetchScalarGridSpec(
        num_scalar_prefetch=2,
        grid=(N // tm,),
        in_specs=[
            pl.BlockSpec((1, 1, 2 * tm), lambda i, pe, pd: (i, 0, 0), memory_space=pltpu.SMEM),
            pl.BlockSpec((tm * SUBLANES, LANES), lambda i, pe, pd: (i, 0)),
        ],
        out_specs=pl.BlockSpec(memory_space=pl.ANY),
        scratch_shapes=[
            pltpu.VMEM((MOE_BLK * SUBLANES, LANES), F32),
            pltpu.SemaphoreType.DMA(()),
            pltpu.SemaphoreType.DMA(()),
        ],
    )
    return pl.pallas_call(
        _dispatch_kernel,
        grid_spec=grid_spec,
        out_shape=jax.ShapeDtypeStruct((P * SUBLANES, LANES), F32),
        compiler_params=pltpu.CompilerParams(dimension_semantics=("arbitrary",)),
        name="dispatch",
    )(pad_end, padded, dest8_tiles, m2)


def _expert_kernel(blk_e_ref, nused_ref, x_ref, wg_ref, wu_ref, wd_ref, y_ref):
    del blk_e_ref
    i = pl.program_id(0)

    @pl.when(i < nused_ref[0])
    def _():
        xb = _load_rows(x_ref, MOE_BLK).astype(BF16)
        hg = _dot(xb, wg_ref[0])
        hu = _dot(xb, wu_ref[0])
        hdn = (jax.nn.silu(hg) * hu).astype(BF16)
        _store_rows(y_ref, _dot(hdn, wd_ref[0]))

    @pl.when(i >= nused_ref[0])
    def _():
        y_ref[...] = jnp.zeros(y_ref.shape, F32)


def _experts(blk_e, nused, buf2, wg_b, wu_b, wd_b):
    nblk = buf2.shape[0] // (MOE_BLK * SUBLANES)

    def used(i, nu):
        return jnp.minimum(i, nu[0] - 1)

    blk_rows = MOE_BLK * SUBLANES
    grid_spec = pltpu.PrefetchScalarGridSpec(
        num_scalar_prefetch=2,
        grid=(nblk,),
        in_specs=[
            pl.BlockSpec((blk_rows, LANES), lambda i, be, nu: (used(i, nu), 0)),
            pl.BlockSpec((1, D_MODEL, D_EXPERT), lambda i, be, nu: (be[used(i, nu)], 0, 0)),
            pl.BlockSpec((1, D_MODEL, D_EXPERT), lambda i, be, nu: (be[used(i, nu)], 0, 0)),
            pl.BlockSpec((1, D_EXPERT, D_MODEL), lambda i, be, nu: (be[used(i, nu)], 0, 0)),
        ],
        out_specs=pl.BlockSpec((blk_rows, LANES), lambda i, be, nu: (i, 0)),
    )
    return pl.pallas_call(
        _expert_kernel,
        grid_spec=grid_spec,
        out_shape=jax.ShapeDtypeStruct(buf2.shape, F32),
        compiler_params=pltpu.CompilerParams(
            dimension_semantics=("arbitrary",), vmem_limit_bytes=VMEM_LIMIT),
        name="experts",
    )(blk_e, nused, buf2, wg_b, wu_b, wd_b)


def _final_kernel(dcur_ref, dnext_ref, h1_ref, rw_ref, p_ref, gple_ref, wpg_ref, wpp_ref, gfin_ref,
                  ys_ref, out_ref, ybuf, sem):
    i = pl.program_id(0)
    n = pl.num_programs(0)
    tm = TM_ROWS

    def issue(d_ref, slot):
        def body(c, carry):
            for u in range(DMA_UNROLL):
                r = c * DMA_UNROLL + u
                pltpu.make_async_copy(_row_tile(ys_ref, d_ref[0, 0, r]),
                                      _row_tile(ybuf.at[slot], r * SUBLANES), sem.at[slot]).start()
            return carry

        lax.fori_loop(0, 2 * tm // DMA_UNROLL, body, 0)

    @pl.when(i == 0)
    def _():
        issue(dcur_ref, 0)

    @pl.when(i + 1 < n)
    def _():
        issue(dnext_ref, (i + 1) % 2)

    slot = i % 2
    pltpu.make_async_copy(ys_ref.at[pl.ds(0, 2 * tm * SUBLANES)], ybuf.at[slot], sem.at[slot]).wait()
    y0 = _load_rows(ybuf.at[slot], tm)
    y1 = _load_rows(ybuf.at[slot], tm, first_row=tm)
    rw = rw_ref[...]
    h2 = h1_ref[...] + (y0 * rw[:, 0:1] + y1 * rw[:, 1:2])
    n3 = (_rms(h2) * gple_ref[...]).astype(BF16)
    gate = jax.nn.sigmoid(_dot(n3, wpg_ref[...]))
    pp = _dot(p_ref[...].astype(BF16), wpp_ref[...])
    h3 = h2 + gate * pp
    out_ref[...] = _rms(h3) * gfin_ref[...]


def _final(dest_tiles, h1, rw, p2, g_ple, wpg_b, wpp_b, g_final, ys2):
    N = h1.shape[0]
    tm = TM_ROWS
    nt = N // tm
    row = lambda i: (i, 0)
    const = lambda i: (0, 0)
    return pl.pallas_call(
        _final_kernel,
        grid=(nt,),
        in_specs=[
            pl.BlockSpec((1, 1, 2 * tm), lambda i: (i, 0, 0), memory_space=pltpu.SMEM),
            pl.BlockSpec((1, 1, 2 * tm), lambda i: (jnp.minimum(i + 1, nt - 1), 0, 0),
                         memory_space=pltpu.SMEM),
            pl.BlockSpec((tm, D_MODEL), row),
            pl.BlockSpec((tm, LANES), row),
            pl.BlockSpec((tm, PLE_DIM), row),
            pl.BlockSpec((1, D_MODEL), const),
            pl.BlockSpec((D_MODEL, D_MODEL), const),
            pl.BlockSpec((PLE_DIM, D_MODEL), const),
            pl.BlockSpec((1, D_MODEL), const),
            pl.BlockSpec(memory_space=pl.ANY),
        ],
        out_specs=pl.BlockSpec((tm, D_MODEL), row),
        out_shape=jax.ShapeDtypeStruct((N, D_MODEL), F32),
        scratch_shapes=[
            pltpu.VMEM((2, 2 * tm * SUBLANES, LANES), F32),
            pltpu.SemaphoreType.DMA((2,)),
        ],
        compiler_params=pltpu.CompilerParams(
            dimension_semantics=("arbitrary",), vmem_limit_bytes=VMEM_LIMIT),
        name="final",
    )(dest_tiles, dest_tiles, h1, rw, p2, g_ple, wpg_b, wpp_b, g_final, ys2)


def _rope_tables(S):
    inv_freq = ROPE_THETA ** (-jnp.arange(0, QK_ROPE_DIM, 2, dtype=F32) / QK_ROPE_DIM)
    ang = jnp.arange(S, dtype=F32)[:, None] * inv_freq[None, :]
    cos, sin = jnp.cos(ang), jnp.sin(ang)
    z = jnp.zeros_like(cos)
    rc = jnp.concatenate([cos, cos, z, z], axis=1)
    rs1 = jnp.concatenate([-sin, z, z, z], axis=1)
    rs2 = jnp.concatenate([z, sin, z, z], axis=1)
    return rc, rs1, rs2


def _layer(h, p_l, g_mix, w_in, g_gv, w_spatial, b_spatial, w_gproj, g_cq, w_uq, g_ckv, w_ukv,
           w_mla_o, w_out, g_moe, w_router_g, b_router_g, w_router_e, b_router_e,
           w_e_gate, w_e_up, w_e_down, g_ple, w_ple_gate, w_ple_proj, g_out):
    B, S, D = h.shape
    N = B * S
    x2 = h.reshape(N, D)

    cu, cv, ccq, cckv, ckr, cga = (GMLP_WIDTH, 2 * GMLP_WIDTH, 2 * GMLP_WIDTH + Q_LORA,
                                   2 * GMLP_WIDTH + Q_LORA + KV_LORA,
                                   2 * GMLP_WIDTH + Q_LORA + KV_LORA + QK_ROPE_DIM,
                                   2 * GMLP_WIDTH + Q_LORA + KV_LORA + QK_ROPE_DIM + D_MODEL)
    w_in_p = jnp.concatenate(
        [w_in[:, :cckv], w_in[:, ckr:], w_in[:, cckv:ckr],
         jnp.zeros((D, LANES - QK_ROPE_DIM), w_in.dtype)], axis=1).astype(BF16)
    w_uq_h = w_uq.reshape(Q_LORA, MLA_HEADS, QK_NOPE_DIM + QK_ROPE_DIM)
    w_uq_p = jnp.concatenate(
        [w_uq_h, jnp.zeros((Q_LORA, MLA_HEADS, QK_PAD - QK_NOPE_DIM - QK_ROPE_DIM), w_uq.dtype)],
        axis=2).reshape(Q_LORA, MLA_HEADS * QK_PAD).astype(BF16)
    rc, rs1, rs2 = _rope_tables(S)

    u, v, sga, sgb, q, k, vv = _inproj(
        x2, g_mix[None], w_in_p, g_gv[None], g_cq[None], g_ckv[None], w_uq_p, w_ukv.astype(BF16),
        rc, rs1, rs2, B, S)
    ma = _gmlp(u, v, sga, w_spatial.astype(BF16), b_spatial.T, w_gproj.astype(BF16))
    o = _attention(q, k, vv)

    wr = jnp.concatenate(
        [w_router_e, w_router_g, jnp.zeros((D, LANES - N_EXPERTS - N_GROUPS), w_router_e.dtype)],
        axis=1).astype(BF16)
    br = jnp.concatenate(
        [b_router_e.reshape(-1), b_router_g, jnp.zeros((LANES - N_EXPERTS - N_GROUPS,), F32)])[None]
    h1, m, rt, rw, cnt = _out_route(
        o.reshape(N, D), ma, sgb, x2, w_mla_o.astype(BF16), w_out.astype(BF16), g_moe[None], wr, br)

    counts = cnt[0, :N_EXPERTS]
    padded = (counts + MOE_BLK - 1) // MOE_BLK * MOE_BLK
    pad_end = jnp.cumsum(padded)
    pad_start = pad_end - padded
    P = 2 * N + N_EXPERTS * MOE_BLK
    nblk = P // MOE_BLK
    blk_start = jnp.arange(nblk, dtype=jnp.int32) * MOE_BLK
    blk_e = jnp.minimum(
        jnp.sum((pad_end[None, :] <= blk_start[:, None]).astype(jnp.int32), axis=1),
        N_EXPERTS - 1).astype(jnp.int32)
    nused = (pad_end[-1:] // MOE_BLK).astype(jnp.int32)
    dest8 = (pad_start[rt[0:2]] + rt[2:4]).astype(jnp.int32) * SUBLANES
    nt = N // TM_ROWS
    dest8_tiles = dest8.reshape(2, nt, TM_ROWS).transpose(1, 0, 2).reshape(nt, 1, 2 * TM_ROWS)

    buf = _dispatch(pad_end.astype(jnp.int32), padded.astype(jnp.int32), dest8_tiles, m, P)
    ys = _experts(blk_e, nused, buf, w_e_gate.astype(BF16), w_e_up.astype(BF16),
                  w_e_down.astype(BF16))
    out = _final(dest8_tiles, h1, rw, p_l.reshape(N, PLE_DIM), g_ple[None],
                 w_ple_gate.astype(BF16), w_ple_proj.astype(BF16), g_out[None], ys)
    return out.reshape(B, S, D)


def kernel(x, p, g_mix, w_in, g_gv, w_spatial, b_spatial, w_gproj, g_cq, w_uq, g_ckv, w_ukv, w_mla_o,
           w_out, g_moe, w_router_g, b_router_g, w_router_e, b_router_e, w_e_gate, w_e_up, w_e_down,
           g_ple, w_ple_gate, w_ple_proj, g_final):
    depth = p.shape[0]
    assert depth == 1, "the final rmsnorm is fused into the single layer's last kernel"
    i = 0
    return _layer(x, p[i], g_mix[i], w_in[i], g_gv[i], w_spatial[i], b_spatial[i], w_gproj[i], g_cq[i],
                  w_uq[i], g_ckv[i], w_ukv[i], w_mla_o[i], w_out[i], g_moe[i], w_router_g[i],
                  b_router_g[i], w_router_e[i], b_router_e[i], w_e_gate[i], w_e_up[i], w_e_down[i],
                  g_ple[i], w_ple_gate[i], w_ple_proj[i], g_final)
```

```python
import functools

import jax
import jax.numpy as jnp
from jax import lax
from jax.experimental import pallas as pl
from jax.experimental.pallas import tpu as pltpu

F32 = jnp.float32
BF16 = jnp.bfloat16

D_MODEL = 1024
CHUNK = 64
PLE_DIM = 256
GMLP_BLOCK = 128
GMLP_GROUPS = 12
GMLP_WIDTH = 1536
MLA_HEADS = 8
QK_NOPE_DIM = 128
QK_ROPE_DIM = 64
V_HEAD_DIM = 128
Q_LORA = 384
KV_LORA = 256
ROPE_THETA = 10000.0
N_GROUPS = 8
EXPERTS_PER_GROUP = 8
N_EXPERTS = 64
D_EXPERT = 256
EPS = 1e-6
LOG2E = 1.4426950408889634

LANES = 128
SUBLANES = 8
QK_PAD = 256
VMEM_LIMIT = 56 * 1024 * 1024

C_U = 0
C_V = C_U + GMLP_WIDTH
C_CQ = C_V + GMLP_WIDTH
C_CKV = C_CQ + Q_LORA
C_GA = C_CKV + KV_LORA
C_GB = C_GA + D_MODEL
C_KR = C_GB + D_MODEL
C_END = C_KR + LANES

TM_IN = 256
TM_GMLP = 512
TQ = 512
HEADS_PER_STEP = 4
TM_OUT = 512
TM_ROWS = 512
MOE_BLK = 256
DMA_UNROLL = 8


def _rms(x):
    return x * lax.rsqrt(jnp.mean(x * x, axis=-1, keepdims=True) + EPS)


def _dot(a, b):
    return jnp.dot(a, b, preferred_element_type=F32)


def _store_rows(ref2, x):
    rows = x.shape[0]
    for j in range(SUBLANES):
        ref2[pl.ds(j, rows, stride=SUBLANES), :] = x[:, j * LANES:(j + 1) * LANES]


def _load_rows(ref2, rows, first_row=0):
    return jnp.concatenate(
        [ref2[pl.ds(first_row * SUBLANES + j, rows, stride=SUBLANES), :] for j in range(SUBLANES)],
        axis=1)


def _rope128(t, rc, rs1, rs2):
    r1 = pltpu.roll(t, 96, axis=1)
    r2 = pltpu.roll(t, 32, axis=1)
    return t * rc + r1 * rs1 + r2 * rs2


def _inproj_kernel(x_ref, gmix_ref, win_ref, ggv_ref, gcq_ref, gckv_ref, wuq_ref, wukv_ref,
                   rc_ref, rs1_ref, rs2_ref,
                   u_ref, v_ref, sga_ref, sgb_ref, q_ref, k_ref, vv_ref):
    x = x_ref[...]
    ab = (_rms(x) * gmix_ref[...]).astype(BF16)

    def proj(c0, c1):
        return _dot(ab, win_ref[:, c0:c1])

    u_ref[...] = jax.nn.gelu(proj(C_U, C_V)).astype(BF16)

    zv = jax.nn.gelu(proj(C_V, C_CQ))
    xc = zv - jnp.mean(zv, axis=-1, keepdims=True)
    vln = xc * lax.rsqrt(jnp.mean(xc * xc, axis=-1, keepdims=True) + EPS)
    v_ref[...] = (vln * ggv_ref[...]).astype(BF16)

    sga_ref[...] = jax.nn.sigmoid(proj(C_GA, C_GB)).astype(BF16)
    sgb_ref[...] = jax.nn.sigmoid(proj(C_GB, C_KR)).astype(BF16)

    rc = rc_ref[...]
    rs1 = rs1_ref[...]
    rs2 = rs2_ref[...]
    kpe = _rope128(proj(C_KR, C_END), rc, rs1, rs2).astype(BF16)

    cqn = (_rms(proj(C_CQ, C_CKV)) * gcq_ref[...]).astype(BF16)
    ckvn = (_rms(proj(C_CKV, C_GA)) * gckv_ref[...]).astype(BF16)
    scale = (QK_NOPE_DIM + QK_ROPE_DIM) ** -0.5 * LOG2E
    for h in range(MLA_HEADS):
        qh = _dot(cqn, wuq_ref[:, h * QK_PAD:(h + 1) * QK_PAD])
        q_ref[0, h, :, 0:LANES] = (qh[:, 0:LANES] * scale).astype(BF16)
        q_ref[0, h, :, LANES:QK_PAD] = (_rope128(qh[:, LANES:QK_PAD], rc, rs1, rs2) * scale).astype(BF16)
        kvh = _dot(ckvn, wukv_ref[:, h * 256:(h + 1) * 256])
        k_ref[0, h, :, 0:LANES] = kvh[:, 0:LANES].astype(BF16)
        k_ref[0, h, :, LANES:QK_PAD] = kpe
        vv_ref[0, h] = kvh[:, LANES:256].astype(BF16)


def _inproj(x2, g_mix, w_in_p, g_gv, g_cq, g_ckv, w_uq_p, w_ukv_b, rc, rs1, rs2, B, S):
    N = x2.shape[0]
    tm = TM_IN
    spt = S // tm
    row = lambda i: (i, 0)
    const = lambda i: (0, 0)
    pos = lambda i: (i % spt, 0)
    head = lambda i: (i // spt, 0, i % spt, 0)
    return pl.pallas_call(
        _inproj_kernel,
        grid=(N // tm,),
        in_specs=[
            pl.BlockSpec((tm, D_MODEL), row),
            pl.BlockSpec((1, D_MODEL), const),
            pl.BlockSpec((D_MODEL, C_END), const),
            pl.BlockSpec((1, GMLP_WIDTH), const),
            pl.BlockSpec((1, Q_LORA), const),
            pl.BlockSpec((1, KV_LORA), const),
            pl.BlockSpec((Q_LORA, MLA_HEADS * QK_PAD), const),
            pl.BlockSpec((KV_LORA, MLA_HEADS * 256), const),
            pl.BlockSpec((tm, LANES), pos),
            pl.BlockSpec((tm, LANES), pos),
            pl.BlockSpec((tm, LANES), pos),
        ],
        out_specs=[
            pl.BlockSpec((tm, GMLP_WIDTH), row),
            pl.BlockSpec((tm, GMLP_WIDTH), row),
            pl.BlockSpec((tm, D_MODEL), row),
            pl.BlockSpec((tm, D_MODEL), row),
            pl.BlockSpec((1, MLA_HEADS, tm, QK_PAD), head),
            pl.BlockSpec((1, MLA_HEADS, tm, QK_PAD), head),
            pl.BlockSpec((1, MLA_HEADS, tm, V_HEAD_DIM), head),
        ],
        out_shape=[
            jax.ShapeDtypeStruct((N, GMLP_WIDTH), BF16),
            jax.ShapeDtypeStruct((N, GMLP_WIDTH), BF16),
            jax.ShapeDtypeStruct((N, D_MODEL), BF16),
            jax.ShapeDtypeStruct((N, D_MODEL), BF16),
            jax.ShapeDtypeStruct((B, MLA_HEADS, S, QK_PAD), BF16),
            jax.ShapeDtypeStruct((B, MLA_HEADS, S, QK_PAD), BF16),
            jax.ShapeDtypeStruct((B, MLA_HEADS, S, V_HEAD_DIM), BF16),
        ],
        compiler_params=pltpu.CompilerParams(
            dimension_semantics=("arbitrary",), vmem_limit_bytes=VMEM_LIMIT),
        name="inproj",
    )(x2, g_mix, w_in_p, g_gv, g_cq, g_ckv, w_uq_p, w_ukv_b, rc, rs1, rs2)


def _gmlp_kernel(u_ref, v_ref, sga_ref, wsp_ref, bsp_ref, wproj_ref, ma_ref, y_sc):
    nb = TM_GMLP // GMLP_BLOCK
    t_out = lax.broadcasted_iota(jnp.int32, (GMLP_BLOCK, GMLP_BLOCK), 0)
    s_in = lax.broadcasted_iota(jnp.int32, (GMLP_BLOCK, GMLP_BLOCK), 1)
    mask = (s_in // CHUNK) <= (t_out // CHUNK)
    for g in range(GMLP_GROUPS):
        c0 = g * LANES
        w = jnp.where(mask, wsp_ref[g], jnp.zeros((), BF16))
        rhs = jnp.concatenate(
            [v_ref[r * GMLP_BLOCK:(r + 1) * GMLP_BLOCK, c0:c0 + LANES] for r in range(nb)], axis=1)
        sv = _dot(w, rhs) + bsp_ref[:, g:g + 1]
        for r in range(nb):
            rows = slice(r * GMLP_BLOCK, (r + 1) * GMLP_BLOCK)
            ub = u_ref[rows, c0:c0 + LANES].astype(F32)
            y_sc[rows, c0:c0 + LANES] = (ub * sv[:, r * LANES:(r + 1) * LANES]).astype(BF16)
    ya = _dot(y_sc[...], wproj_ref[...])
    ma_ref[...] = (sga_ref[...].astype(F32) * ya).astype(BF16)


def _gmlp(u, v, sga, wsp_b, bsp_t, wproj_b):
    N = u.shape[0]
    tm = TM_GMLP
    row = lambda i: (i, 0)
    return pl.pallas_call(
        _gmlp_kernel,
        grid=(N // tm,),
        in_specs=[
            pl.BlockSpec((tm, GMLP_WIDTH), row),
            pl.BlockSpec((tm, GMLP_WIDTH), row),
            pl.BlockSpec((tm, D_MODEL), row),
            pl.BlockSpec((GMLP_GROUPS, GMLP_BLOCK, GMLP_BLOCK), lambda i: (0, 0, 0)),
            pl.BlockSpec((GMLP_BLOCK, GMLP_GROUPS), lambda i: (0, 0)),
            pl.BlockSpec((GMLP_WIDTH, D_MODEL), lambda i: (0, 0)),
        ],
        out_specs=pl.BlockSpec((tm, D_MODEL), row),
        out_shape=jax.ShapeDtypeStruct((N, D_MODEL), BF16),
        scratch_shapes=[pltpu.VMEM((tm, GMLP_WIDTH), BF16)],
        compiler_params=pltpu.CompilerParams(
            dimension_semantics=("arbitrary",), vmem_limit_bytes=VMEM_LIMIT),
        name="gmlp",
    )(u, v, sga, wsp_b, bsp_t, wproj_b)


NEG_BIG = -1e30


def _attn_kernel(q_ref, k_ref, v_ref, o_ref, m_sc, l_sc, acc_sc):
    qi = pl.program_id(2)
    m_sc[...] = jnp.full(m_sc.shape, NEG_BIG, F32)
    l_sc[...] = jnp.zeros(l_sc.shape, F32)
    acc_sc[...] = jnp.zeros(acc_sc.shape, F32)
    nt = TQ // LANES

    def step(j, masked):
        start = pl.multiple_of(j * TQ, TQ)
        for hh in range(HEADS_PER_STEP):
            kb = k_ref[0, hh, pl.ds(start, TQ), :]
            vb = v_ref[0, hh, pl.ds(start, TQ), :]
            s = lax.dot_general(q_ref[0, hh], kb, (((1,), (1,)), ((), ())),
                                preferred_element_type=F32)
            if masked:
                qc = lax.broadcasted_iota(jnp.int32, (TQ, TQ), 0) // CHUNK
                kc = lax.broadcasted_iota(jnp.int32, (TQ, TQ), 1) // CHUNK
                s = jnp.where(kc <= qc, s, NEG_BIG)
            tiles = [s[:, c * LANES:(c + 1) * LANES] for c in range(nt)]
            tile_max = functools.reduce(jnp.maximum, tiles)
            m_prev = m_sc[hh]
            m_new = jnp.maximum(m_prev, jnp.max(tile_max, axis=-1, keepdims=True))
            alpha = jnp.exp2(m_prev - m_new)
            ps = [jnp.exp2(t - m_new) for t in tiles]
            l_sc[hh] = alpha * l_sc[hh] + functools.reduce(jnp.add, ps)
            p = jnp.concatenate([t.astype(BF16) for t in ps], axis=1)
            acc_sc[hh] = alpha * acc_sc[hh] + _dot(p, vb)
            m_sc[hh] = m_new

    def body(j, carry):
        step(j, False)
        return carry

    lax.fori_loop(0, qi, body, 0)
    step(qi, True)
    for hh in range(HEADS_PER_STEP):
        l = jnp.sum(l_sc[hh], axis=-1, keepdims=True)
        o_ref[0, :, hh * V_HEAD_DIM:(hh + 1) * V_HEAD_DIM] = (acc_sc[hh] / l).astype(BF16)


def _attention(q, k, v):
    B, H, S, _ = q.shape
    hps = HEADS_PER_STEP
    return pl.pallas_call(
        _attn_kernel,
        grid=(B, H // hps, S // TQ),
        in_specs=[
            pl.BlockSpec((1, hps, TQ, QK_PAD), lambda b, h, i: (b, h, i, 0)),
            pl.BlockSpec((1, hps, S, QK_PAD), lambda b, h, i: (b, h, 0, 0)),
            pl.BlockSpec((1, hps, S, V_HEAD_DIM), lambda b, h, i: (b, h, 0, 0)),
        ],
        out_specs=pl.BlockSpec((1, TQ, hps * V_HEAD_DIM), lambda b, h, i: (b, i, h)),
        out_shape=jax.ShapeDtypeStruct((B, S, H * V_HEAD_DIM), BF16),
        scratch_shapes=[
            pltpu.VMEM((hps, TQ, LANES), F32),
            pltpu.VMEM((hps, TQ, LANES), F32),
            pltpu.VMEM((hps, TQ, V_HEAD_DIM), F32),
        ],
        compiler_params=pltpu.CompilerParams(
            dimension_semantics=("arbitrary", "arbitrary", "arbitrary"),
            vmem_limit_bytes=VMEM_LIMIT),
        name="attention",
    )(q, k, v)


def _lane_sum(x):
    return jnp.sum(x, axis=-1, keepdims=True)


def _first_lane(hit, lane_f):
    return jnp.min(jnp.where(hit, lane_f, float(LANES)), axis=-1, keepdims=True).astype(jnp.int32)


def _out_route_kernel(o_ref, ma_ref, sgb_ref, x_ref, wo_ref, wout_ref, gmoe_ref, wr_ref, br_ref,
                      h1_ref, m_ref, rt_ref, rw_ref, cnt_ref, carry_sc):
    i = pl.program_id(0)
    tm = TM_OUT

    @pl.when(i == 0)
    def _():
        carry_sc[...] = jnp.zeros(carry_sc.shape, F32)

    yb = _dot(o_ref[...], wo_ref[...])
    merged = ma_ref[...].astype(F32) + sgb_ref[...].astype(F32) * yb
    h1 = x_ref[...] + _dot(merged.astype(BF16), wout_ref[...])
    h1_ref[...] = h1
    m = _rms(h1) * gmoe_ref[...]
    _store_rows(m_ref, m)

    logits = _dot(m.astype(BF16), wr_ref[...])
    lane = lax.broadcasted_iota(jnp.int32, (tm, LANES), 1)
    lane_f = lane.astype(F32)
    bias = br_ref[...]
    is_g = (lane >= N_EXPERTS) & (lane < N_EXPERTS + N_GROUPS)
    neg = jnp.float32(-jnp.inf)

    gl = jnp.where(is_g, logits, neg)
    ge = jnp.where(is_g, jnp.exp(gl - jnp.max(gl, axis=-1, keepdims=True)), 0.0)
    g_prob = ge / _lane_sum(ge)
    g_score = jnp.where(is_g, g_prob + bias, neg)
    g_best = jnp.max(g_score, axis=-1, keepdims=True)
    g_lane = _first_lane(g_score == g_best, lane_f)
    g_w = _lane_sum(jnp.where(lane == g_lane, g_prob, 0.0))
    g_idx = g_lane - N_EXPERTS

    in_g = (lane // EXPERTS_PER_GROUP) == g_idx
    el = jnp.where(in_g, logits, neg)
    ee = jnp.where(in_g, jnp.exp(el - jnp.max(el, axis=-1, keepdims=True)), 0.0)
    e_prob = ee / _lane_sum(ee)
    e_score = jnp.where(in_g, e_prob + bias, neg)
    best1 = jnp.max(e_score, axis=-1, keepdims=True)
    id1 = _first_lane(e_score == best1, lane_f)
    e_score2 = jnp.where(lane == id1, neg, e_score)
    best2 = jnp.max(e_score2, axis=-1, keepdims=True)
    id2 = _first_lane(e_score2 == best2, lane_f)
    p1 = _lane_sum(jnp.where(lane == id1, e_prob, 0.0))
    p2 = _lane_sum(jnp.where(lane == id2, e_prob, 0.0))
    psum = p1 + p2
    w1 = g_w * (p1 / psum)
    w2 = g_w * (p2 / psum)

    oh = ((lane == id1) | (lane == id2 + N_EXPERTS)).astype(BF16)
    r_out = lax.broadcasted_iota(jnp.int32, (tm, tm), 0)
    r_in = lax.broadcasted_iota(jnp.int32, (tm, tm), 1)
    tri = (r_in < r_out).astype(BF16)
    prefix = _dot(tri, oh)
    tot = jnp.sum(oh.astype(F32), axis=0, keepdims=True)
    tot_sw = pltpu.roll(tot, N_EXPERTS, axis=1)
    lane1 = lax.broadcasted_iota(jnp.int32, (1, LANES), 1)
    carry = carry_sc[...]
    base = carry + jnp.where(lane1 >= N_EXPERTS, tot_sw, 0.0)
    rk = oh.astype(F32) * (base + prefix)
    rank1 = _lane_sum(jnp.where(lane < N_EXPERTS, rk, 0.0))
    rank2 = _lane_sum(jnp.where(lane >= N_EXPERTS, rk, 0.0))
    carry_new = carry + tot + tot_sw
    carry_sc[...] = carry_new
    cnt_ref[...] = carry_new.astype(jnp.int32)

    ri = jnp.where(lane == 0, id1.astype(F32), jnp.where(lane == 1, id2.astype(F32), 0.0))
    ri = jnp.where(lane == 2, rank1, ri)
    ri = jnp.where(lane == 3, rank2, ri)
    rt_ref[...] = ri.T[0:SUBLANES, :].astype(jnp.int32)
    rw_ref[...] = jnp.where(lane == 0, w1, jnp.where(lane == 1, w2, 0.0))


def _out_route(o2, ma, sgb, x2, wo_b, wout_b, g_moe, wr_b, br):
    N = x2.shape[0]
    tm = TM_OUT
    row = lambda i: (i, 0)
    const = lambda i: (0, 0)
    return pl.pallas_call(
        _out_route_kernel,
        grid=(N // tm,),
        in_specs=[
            pl.BlockSpec((tm, D_MODEL), row),
            pl.BlockSpec((tm, D_MODEL), row),
            pl.BlockSpec((tm, D_MODEL), row),
            pl.BlockSpec((tm, D_MODEL), row),
            pl.BlockSpec((D_MODEL, D_MODEL), const),
            pl.BlockSpec((D_MODEL, D_MODEL), const),
            pl.BlockSpec((1, D_MODEL), const),
            pl.BlockSpec((D_MODEL, LANES), const),
            pl.BlockSpec((1, LANES), const),
        ],
        out_specs=[
            pl.BlockSpec((tm, D_MODEL), row),
            pl.BlockSpec((tm * SUBLANES, LANES), row),
            pl.BlockSpec((SUBLANES, tm), lambda i: (0, i)),
            pl.BlockSpec((tm, LANES), row),
            pl.BlockSpec((1, LANES), const),
        ],
        out_shape=[
            jax.ShapeDtypeStruct((N, D_MODEL), F32),
            jax.ShapeDtypeStruct((N * SUBLANES, LANES), F32),
            jax.ShapeDtypeStruct((SUBLANES, N), jnp.int32),
            jax.ShapeDtypeStruct((N, LANES), F32),
            jax.ShapeDtypeStruct((1, LANES), jnp.int32),
        ],
        scratch_shapes=[pltpu.VMEM((1, LANES), F32)],
        compiler_params=pltpu.CompilerParams(
            dimension_semantics=("arbitrary",), vmem_limit_bytes=VMEM_LIMIT),
        name="out_route",
    )(o2, ma, sgb, x2, wo_b, wout_b, g_moe, wr_b, br)


def _row_tile(ref2, row8):
    return ref2.at[pl.ds(pl.multiple_of(row8, SUBLANES), SUBLANES)]


def _dispatch_kernel(pad_end_ref, padded_ref, dest_ref, m_ref, buf_ref, zero_sc, sem, zsem):
    i = pl.program_id(0)
    tm = TM_ROWS
    blk8 = MOE_BLK * SUBLANES

    @pl.when(i == 0)
    def _():
        zero_sc[...] = jnp.zeros(zero_sc.shape, F32)

        def zero_copy(e):
            start = pl.multiple_of((pad_end_ref[e] - MOE_BLK) * SUBLANES, blk8)
            return pltpu.make_async_copy(zero_sc, buf_ref.at[pl.ds(start, blk8)], zsem)

        def start(e, carry):
            @pl.when(padded_ref[e] > 0)
            def _():
                zero_copy(e).start()
            return carry

        def wait(e, carry):
            @pl.when(padded_ref[e] > 0)
            def _():
                zero_copy(e).wait()
            return carry

        lax.fori_loop(0, N_EXPERTS, start, 0)
        lax.fori_loop(0, N_EXPERTS, wait, 0)

        def tail_copy(b):
            return pltpu.make_async_copy(
                zero_sc, buf_ref.at[pl.ds(pl.multiple_of(b * blk8, blk8), blk8)], zsem)

        nused = pad_end_ref[N_EXPERTS - 1] // MOE_BLK
        nblk = buf_ref.shape[0] // blk8
        lax.fori_loop(nused, nblk, lambda b, c: (tail_copy(b).start(), c)[1], 0)
        lax.fori_loop(nused, nblk, lambda b, c: (tail_copy(b).wait(), c)[1], 0)

    def issue(c, carry):
        for u in range(DMA_UNROLL):
            r = c * DMA_UNROLL + u
            src = _row_tile(m_ref, r * SUBLANES)
            pltpu.make_async_copy(src, _row_tile(buf_ref, dest_ref[0, 0, r]), sem).start(priority=0)
            pltpu.make_async_copy(src, _row_tile(buf_ref, dest_ref[0, 0, tm + r]), sem).start(priority=1)
        return carry

    lax.fori_loop(0, tm // DMA_UNROLL, issue, 0)
    for _ in range(2):
        pltpu.make_async_copy(m_ref, buf_ref.at[pl.ds(0, tm * SUBLANES)], sem).wait()


def _dispatch(pad_end, padded, dest8_tiles, m2, P):
    tm = TM_ROWS
    N = m2.shape[0] // SUBLANES
    grid_spec = pltpu.PrefetchScalarGridSpec(
        num_scalar_prefetch=2,
        grid=(N // tm,),
        in_specs=[
            pl.BlockSpec((1, 1, 2 * tm), lambda i, pe, pd: (i, 0, 0), memory_space=pltpu.SMEM),
            pl.BlockSpec((tm * SUBLANES, LANES), lambda i, pe, pd: (i, 0)),
        ],
        out_specs=pl.BlockSpec(memory_space=pl.ANY),
        scratch_shapes=[
            pltpu.VMEM((MOE_BLK * SUBLANES, LANES), F32),
            pltpu.SemaphoreType.DMA(()),
            pltpu.SemaphoreType.DMA(()),
        ],
    )
    return pl.pallas_call(
        _dispatch_kernel,
        grid_spec=grid_spec,
        out_shape=jax.ShapeDtypeStruct((P * SUBLANES, LANES), F32),
        compiler_params=pltpu.CompilerParams(dimension_semantics=("arbitrary",)),
        name="dispatch",
    )(pad_end, padded, dest8_tiles, m2)


def _expert_kernel(blk_e_ref, nused_ref, x_ref, wg_ref, wu_ref, wd_ref, y_ref):
    del blk_e_ref
    i = pl.program_id(0)

    @pl.when(i < nused_ref[0])
    def _():
        xb = _load_rows(x_ref, MOE_BLK).astype(BF16)
        hg = _dot(xb, wg_ref[0])
        hu = _dot(xb, wu_ref[0])
        hdn = (jax.nn.silu(hg) * hu).astype(BF16)
        _store_rows(y_ref, _dot(hdn, wd_ref[0]))

    @pl.when(i >= nused_ref[0])
    def _():
        y_ref[...] = jnp.zeros(y_ref.shape, F32)


def _experts(blk_e, nused, buf2, wg_b, wu_b, wd_b):
    nblk = buf2.shape[0] // (MOE_BLK * SUBLANES)

    def used(i, nu):
        return jnp.minimum(i, nu[0] - 1)

    blk_rows = MOE_BLK * SUBLANES
    grid_spec = pltpu.PrefetchScalarGridSpec(
        num_scalar_prefetch=2,
        grid=(nblk,),
        in_specs=[
            pl.BlockSpec((blk_rows, LANES), lambda i, be, nu: (used(i, nu), 0)),
            pl.BlockSpec((1, D_MODEL, D_EXPERT), lambda i, be, nu: (be[used(i, nu)], 0, 0)),
            pl.BlockSpec((1, D_MODEL, D_EXPERT), lambda i, be, nu: (be[used(i, nu)], 0, 0)),
            pl.BlockSpec((1, D_EXPERT, D_MODEL), lambda i, be, nu: (be[used(i, nu)], 0, 0)),
        ],
        out_specs=pl.BlockSpec((blk_rows, LANES), lambda i, be, nu: (i, 0)),
    )
    return pl.pallas_call(
        _expert_kernel,
        grid_spec=grid_spec,
        out_shape=jax.ShapeDtypeStruct(buf2.shape, F32),
        compiler_params=pltpu.CompilerParams(
            dimension_semantics=("arbitrary",), vmem_limit_bytes=VMEM_LIMIT),
        name="experts",
    )(blk_e, nused, buf2, wg_b, wu_b, wd_b)


def _final_kernel(dcur_ref, dnext_ref, h1_ref, rw_ref, p_ref, gple_ref, wpg_ref, wpp_ref, gfin_ref,
                  ys_ref, out_ref, ybuf_a, ybuf_b, sem):
    g = pl.program_id(0)
    ng = pl.num_programs(0)
    tm = TM_ROWS
    bufs = (ybuf_a, ybuf_b)

    def row_copy(d_ref, off, r, which):
        return pltpu.make_async_copy(_row_tile(ys_ref, d_ref[0, 0, off + r]),
                                     _row_tile(bufs[which], r * SUBLANES), sem.at[which])

    def issue(d_ref, off, which):
        for r in range(2 * tm):
            row_copy(d_ref, off, r, which).start(priority=r % 2)

    def wait(which):
        pltpu.make_async_copy(ys_ref.at[pl.ds(0, 2 * tm * SUBLANES)], bufs[which],
                              sem.at[which]).wait()

    def compute(which, rows):
        y0 = _load_rows(bufs[which], tm)
        y1 = _load_rows(bufs[which], tm, first_row=tm)
        rw = rw_ref[rows, :]
        h2 = h1_ref[rows, :] + (y0 * rw[:, 0:1] + y1 * rw[:, 1:2])
        n3 = (_rms(h2) * gple_ref[...]).astype(BF16)
        gate = jax.nn.sigmoid(_dot(n3, wpg_ref[...]))
        pp = _dot(p_ref[rows, :].astype(BF16), wpp_ref[...])
        h3 = h2 + gate * pp
        out_ref[rows, :] = _rms(h3) * gfin_ref[...]

    @pl.when(g == 0)
    def _():
        def body(c, carry):
            for u in range(DMA_UNROLL):
                row_copy(dcur_ref, 0, c * DMA_UNROLL + u, 0).start()
            return carry

        lax.fori_loop(0, 2 * tm // DMA_UNROLL, body, 0)

    wait(0)
    issue(dcur_ref, 2 * tm, 1)
    compute(0, slice(0, tm))
    wait(1)
    issue(dnext_ref, 0, 0)
    compute(1, slice(tm, 2 * tm))

    @pl.when(g == ng - 1)
    def _():
        wait(0)


def _final(dest8_pairs, h1, rw, p2, g_ple, wpg_b, wpp_b, g_final, ys2):
    N = h1.shape[0]
    tm = TM_ROWS
    ng = N // (2 * tm)
    row = lambda i: (i, 0)
    const = lambda i: (0, 0)
    return pl.pallas_call(
        _final_kernel,
        grid=(ng,),
        in_specs=[
            pl.BlockSpec((1, 1, 4 * tm), lambda i: (i, 0, 0), memory_space=pltpu.SMEM),
            pl.BlockSpec((1, 1, 4 * tm), lambda i: (jnp.minimum(i + 1, ng - 1), 0, 0),
                         memory_space=pltpu.SMEM),
            pl.BlockSpec((2 * tm, D_MODEL), row),
            pl.BlockSpec((2 * tm, LANES), row),
            pl.BlockSpec((2 * tm, PLE_DIM), row),
            pl.BlockSpec((1, D_MODEL), const),
            pl.BlockSpec((D_MODEL, D_MODEL), const),
            pl.BlockSpec((PLE_DIM, D_MODEL), const),
            pl.BlockSpec((1, D_MODEL), const),
            pl.BlockSpec(memory_space=pl.ANY),
        ],
        out_specs=pl.BlockSpec((2 * tm, D_MODEL), row),
        out_shape=jax.ShapeDtypeStruct((N, D_MODEL), F32),
        scratch_shapes=[
            pltpu.VMEM((2 * tm * SUBLANES, LANES), F32),
            pltpu.VMEM((2 * tm * SUBLANES, LANES), F32),
            pltpu.SemaphoreType.DMA((2,)),
        ],
        compiler_params=pltpu.CompilerParams(
            dimension_semantics=("arbitrary",), vmem_limit_bytes=VMEM_LIMIT),
        name="final",
    )(dest8_pairs, dest8_pairs, h1, rw, p2, g_ple, wpg_b, wpp_b, g_final, ys2)


def _rope_tables(S):
    inv_freq = ROPE_THETA ** (-jnp.arange(0, QK_ROPE_DIM, 2, dtype=F32) / QK_ROPE_DIM)
    ang = jnp.arange(S, dtype=F32)[:, None] * inv_freq[None, :]
    cos, sin = jnp.cos(ang), jnp.sin(ang)
    z = jnp.zeros_like(cos)
    rc = jnp.concatenate([cos, cos, z, z], axis=1)
    rs1 = jnp.concatenate([-sin, z, z, z], axis=1)
    rs2 = jnp.concatenate([z, sin, z, z], axis=1)
    return rc, rs1, rs2


def _layer(h, p_l, g_mix, w_in, g_gv, w_spatial, b_spatial, w_gproj, g_cq, w_uq, g_ckv, w_ukv,
           w_mla_o, w_out, g_moe, w_router_g, b_router_g, w_router_e, b_router_e,
           w_e_gate, w_e_up, w_e_down, g_ple, w_ple_gate, w_ple_proj, g_out):
    B, S, D = h.shape
    N = B * S
    x2 = h.reshape(N, D)

    cu, cv, ccq, cckv, ckr, cga = (GMLP_WIDTH, 2 * GMLP_WIDTH, 2 * GMLP_WIDTH + Q_LORA,
                                   2 * GMLP_WIDTH + Q_LORA + KV_LORA,
                                   2 * GMLP_WIDTH + Q_LORA + KV_LORA + QK_ROPE_DIM,
                                   2 * GMLP_WIDTH + Q_LORA + KV_LORA + QK_ROPE_DIM + D_MODEL)
    w_in_p = jnp.concatenate(
        [w_in[:, :cckv], w_in[:, ckr:], w_in[:, cckv:ckr],
         jnp.zeros((D, LANES - QK_ROPE_DIM), w_in.dtype)], axis=1).astype(BF16)
    w_uq_h = w_uq.reshape(Q_LORA, MLA_HEADS, QK_NOPE_DIM + QK_ROPE_DIM)
    w_uq_p = jnp.concatenate(
        [w_uq_h, jnp.zeros((Q_LORA, MLA_HEADS, QK_PAD - QK_NOPE_DIM - QK_ROPE_DIM), w_uq.dtype)],
        axis=2).reshape(Q_LORA, MLA_HEADS * QK_PAD).astype(BF16)
    rc, rs1, rs2 = _rope_tables(S)

    u, v, sga, sgb, q, k, vv = _inproj(
        x2, g_mix[None], w_in_p, g_gv[None], g_cq[None], g_ckv[None], w_uq_p, w_ukv.astype(BF16),
        rc, rs1, rs2, B, S)
    ma = _gmlp(u, v, sga, w_spatial.astype(BF16), b_spatial.T, w_gproj.astype(BF16))
    o = _attention(q, k, vv)

    wr = jnp.concatenate(
        [w_router_e, w_router_g, jnp.zeros((D, LANES - N_EXPERTS - N_GROUPS), w_router_e.dtype)],
        axis=1).astype(BF16)
    br = jnp.concatenate(
        [b_router_e.reshape(-1), b_router_g, jnp.zeros((LANES - N_EXPERTS - N_GROUPS,), F32)])[None]
    h1, m, rt, rw, cnt = _out_route(
        o.reshape(N, D), ma, sgb, x2, w_mla_o.astype(BF16), w_out.astype(BF16), g_moe[None], wr, br)

    counts = cnt[0, :N_EXPERTS]
    padded = (counts + MOE_BLK - 1) // MOE_BLK * MOE_BLK
    pad_end = jnp.cumsum(padded)
    pad_start = pad_end - padded
    P = 2 * N + N_EXPERTS * MOE_BLK
    nblk = P // MOE_BLK
    blk_start = jnp.arange(nblk, dtype=jnp.int32) * MOE_BLK
    blk_e = jnp.minimum(
        jnp.sum((pad_end[None, :] <= blk_start[:, None]).astype(jnp.int32), axis=1),
        N_EXPERTS - 1).astype(jnp.int32)
    nused = (pad_end[-1:] // MOE_BLK).astype(jnp.int32)
    experts = jnp.arange(N_EXPERTS, dtype=jnp.int32)[:, None, None]
    start_of = jnp.sum(jnp.where(rt[None, 0:2] == experts, pad_start[:, None, None], 0), axis=0)
    dest8 = (start_of + rt[2:4]).astype(jnp.int32) * SUBLANES
    nt = N // TM_ROWS
    dest8_tiles = dest8.reshape(2, nt, TM_ROWS).transpose(1, 0, 2).reshape(nt, 1, 2 * TM_ROWS)

    buf = _dispatch(pad_end.astype(jnp.int32), padded.astype(jnp.int32), dest8_tiles, m, P)
    ys = _experts(blk_e, nused, buf, w_e_gate.astype(BF16), w_e_up.astype(BF16),
                  w_e_down.astype(BF16))
    out = _final(dest8_tiles.reshape(nt // 2, 1, 4 * TM_ROWS), h1, rw, p_l.reshape(N, PLE_DIM), g_ple[None],
                 w_ple_gate.astype(BF16), w_ple_proj.astype(BF16), g_out[None], ys)
    return out.reshape(B, S, D)


def kernel(x, p, g_mix, w_in, g_gv, w_spatial, b_spatial, w_gproj, g_cq, w_uq, g_ckv, w_ukv, w_mla_o,
           w_out, g_moe, w_router_g, b_router_g, w_router_e, b_router_e, w_e_gate, w_e_up, w_e_down,
           g_ple, w_ple_gate, w_ple_proj, g_final):
    depth = p.shape[0]
    assert depth == 1, "the final rmsnorm is fused into the single layer's last kernel"
    i = 0
    return _layer(x, p[i], g_mix[i], w_in[i], g_gv[i], w_spatial[i], b_spatial[i], w_gproj[i], g_cq[i],
                  w_uq[i], g_ckv[i], w_ukv[i], w_mla_o[i], w_out[i], g_moe[i], w_router_g[i],
                  b_router_g[i], w_router_e[i], b_router_e[i], w_e_gate[i], w_e_up[i], w_e_down[i],
                  g_ple[i], w_ple_gate[i], w_ple_proj[i], g_final)
```

```python
import functools

import jax
import jax.numpy as jnp
from jax import lax
from jax.experimental import pallas as pl
from jax.experimental.pallas import tpu as pltpu

F32 = jnp.float32
BF16 = jnp.bfloat16

D_MODEL = 1024
CHUNK = 64
PLE_DIM = 256
GMLP_BLOCK = 128
GMLP_GROUPS = 12
GMLP_WIDTH = 1536
MLA_HEADS = 8
QK_NOPE_DIM = 128
QK_ROPE_DIM = 64
V_HEAD_DIM = 128
Q_LORA = 384
KV_LORA = 256
ROPE_THETA = 10000.0
N_GROUPS = 8
EXPERTS_PER_GROUP = 8
N_EXPERTS = 64
D_EXPERT = 256
EPS = 1e-6
LOG2E = 1.4426950408889634

LANES = 128
SUBLANES = 8
QK_PAD = 256
VMEM_LIMIT = 56 * 1024 * 1024

C_U = 0
C_V = C_U + GMLP_WIDTH
C_CQ = C_V + GMLP_WIDTH
C_CKV = C_CQ + Q_LORA
C_GA = C_CKV + KV_LORA
C_GB = C_GA + D_MODEL
C_KR = C_GB + D_MODEL
C_END = C_KR + LANES

TM_IN = 512
TM_GMLP = 512
TQ = 512
HEADS_PER_STEP = 4
TM_OUT = 512
TM_ROWS = 512
MOE_BLK = 256
DMA_UNROLL = 8


def _rms(x):
    return x * lax.rsqrt(jnp.mean(x * x, axis=-1, keepdims=True) + EPS)


def _dot(a, b):
    return jnp.dot(a, b, preferred_element_type=F32)


def _store_rows(ref2, x):
    rows = x.shape[0]
    for j in range(SUBLANES):
        ref2[pl.ds(j, rows, stride=SUBLANES), :] = x[:, j * LANES:(j + 1) * LANES]


def _load_rows(ref2, rows, first_row=0):
    return jnp.concatenate(
        [ref2[pl.ds(first_row * SUBLANES + j, rows, stride=SUBLANES), :] for j in range(SUBLANES)],
        axis=1)


def _rope128(t, rc, rs1, rs2):
    r1 = pltpu.roll(t, 96, axis=1)
    r2 = pltpu.roll(t, 32, axis=1)
    return t * rc + r1 * rs1 + r2 * rs2


def _inproj_kernel(x_ref, gmix_ref, win_ref, ggv_ref, gcq_ref, gckv_ref, wuq_ref, wukv_ref,
                   rc_ref, rs1_ref, rs2_ref,
                   u_ref, v_ref, sga_ref, sgb_ref, q_ref, k_ref, vv_ref):
    x = x_ref[...]
    ab = (_rms(x) * gmix_ref[...]).astype(BF16)

    def proj(c0, c1):
        return _dot(ab, win_ref[:, c0:c1])

    u_ref[...] = jax.nn.gelu(proj(C_U, C_V)).astype(BF16)

    zv = jax.nn.gelu(proj(C_V, C_CQ))
    xc = zv - jnp.mean(zv, axis=-1, keepdims=True)
    vln = xc * lax.rsqrt(jnp.mean(xc * xc, axis=-1, keepdims=True) + EPS)
    v_ref[...] = (vln * ggv_ref[...]).astype(BF16)

    sga_ref[...] = jax.nn.sigmoid(proj(C_GA, C_GB)).astype(BF16)
    sgb_ref[...] = jax.nn.sigmoid(proj(C_GB, C_KR)).astype(BF16)

    rc = rc_ref[...]
    rs1 = rs1_ref[...]
    rs2 = rs2_ref[...]
    kpe = _rope128(proj(C_KR, C_END), rc, rs1, rs2).astype(BF16)

    cqn = (_rms(proj(C_CQ, C_CKV)) * gcq_ref[...]).astype(BF16)
    ckvn = (_rms(proj(C_CKV, C_GA)) * gckv_ref[...]).astype(BF16)
    scale = (QK_NOPE_DIM + QK_ROPE_DIM) ** -0.5 * LOG2E
    for h in range(MLA_HEADS):
        qh = _dot(cqn, wuq_ref[:, h * QK_PAD:(h + 1) * QK_PAD])
        q_ref[0, h, :, 0:LANES] = (qh[:, 0:LANES] * scale).astype(BF16)
        q_ref[0, h, :, LANES:QK_PAD] = (_rope128(qh[:, LANES:QK_PAD], rc, rs1, rs2) * scale).astype(BF16)
        kvh = _dot(ckvn, wukv_ref[:, h * 256:(h + 1) * 256])
        k_ref[0, h, :, 0:LANES] = kvh[:, 0:LANES].astype(BF16)
        k_ref[0, h, :, LANES:QK_PAD] = kpe
        vv_ref[0, h] = kvh[:, LANES:256].astype(BF16)


def _inproj(x2, g_mix, w_in_p, g_gv, g_cq, g_ckv, w_uq_p, w_ukv_b, rc, rs1, rs2, B, S):
    N = x2.shape[0]
    tm = TM_IN
    spt = S // tm
    row = lambda i: (i, 0)
    const = lambda i: (0, 0)
    pos = lambda i: (i % spt, 0)
    head = lambda i: (i // spt, 0, i % spt, 0)
    return pl.pallas_call(
        _inproj_kernel,
        grid=(N // tm,),
        in_specs=[
            pl.BlockSpec((tm, D_MODEL), row),
            pl.BlockSpec((1, D_MODEL), const),
            pl.BlockSpec((D_MODEL, C_END), const, pipeline_mode=pl.Buffered(1)),
            pl.BlockSpec((1, GMLP_WIDTH), const),
            pl.BlockSpec((1, Q_LORA), const),
            pl.BlockSpec((1, KV_LORA), const),
            pl.BlockSpec((Q_LORA, MLA_HEADS * QK_PAD), const),
            pl.BlockSpec((KV_LORA, MLA_HEADS * 256), const),
            pl.BlockSpec((tm, LANES), pos),
            pl.BlockSpec((tm, LANES), pos),
            pl.BlockSpec((tm, LANES), pos),
        ],
        out_specs=[
            pl.BlockSpec((tm, GMLP_WIDTH), row),
            pl.BlockSpec((tm, GMLP_WIDTH), row),
            pl.BlockSpec((tm, D_MODEL), row),
            pl.BlockSpec((tm, D_MODEL), row),
            pl.BlockSpec((1, MLA_HEADS, tm, QK_PAD), head),
            pl.BlockSpec((1, MLA_HEADS, tm, QK_PAD), head),
            pl.BlockSpec((1, MLA_HEADS, tm, V_HEAD_DIM), head),
        ],
        out_shape=[
            jax.ShapeDtypeStruct((N, GMLP_WIDTH), BF16),
            jax.ShapeDtypeStruct((N, GMLP_WIDTH), BF16),
            jax.ShapeDtypeStruct((N, D_MODEL), BF16),
            jax.ShapeDtypeStruct((N, D_MODEL), BF16),
            jax.ShapeDtypeStruct((B, MLA_HEADS, S, QK_PAD), BF16),
            jax.ShapeDtypeStruct((B, MLA_HEADS, S, QK_PAD), BF16),
            jax.ShapeDtypeStruct((B, MLA_HEADS, S, V_HEAD_DIM), BF16),
        ],
        compiler_params=pltpu.CompilerParams(
            dimension_semantics=("arbitrary",), vmem_limit_bytes=VMEM_LIMIT),
        name="inproj",
    )(x2, g_mix, w_in_p, g_gv, g_cq, g_ckv, w_uq_p, w_ukv_b, rc, rs1, rs2)


def _gmlp_kernel(u_ref, v_ref, sga_ref, wsp_ref, bsp_ref, wproj_ref, ma_ref, y_sc):
    nb = TM_GMLP // GMLP_BLOCK
    t_out = lax.broadcasted_iota(jnp.int32, (GMLP_BLOCK, GMLP_BLOCK), 0)
    s_in = lax.broadcasted_iota(jnp.int32, (GMLP_BLOCK, GMLP_BLOCK), 1)
    mask = (s_in // CHUNK) <= (t_out // CHUNK)
    for g in range(GMLP_GROUPS):
        c0 = g * LANES
        w = jnp.where(mask, wsp_ref[g], jnp.zeros((), BF16))
        rhs = jnp.concatenate(
            [v_ref[r * GMLP_BLOCK:(r + 1) * GMLP_BLOCK, c0:c0 + LANES] for r in range(nb)], axis=1)
        sv = _dot(w, rhs) + bsp_ref[:, g:g + 1]
        for r in range(nb):
            rows = slice(r * GMLP_BLOCK, (r + 1) * GMLP_BLOCK)
            ub = u_ref[rows, c0:c0 + LANES].astype(F32)
            y_sc[rows, c0:c0 + LANES] = (ub * sv[:, r * LANES:(r + 1) * LANES]).astype(BF16)
    ya = _dot(y_sc[...], wproj_ref[...])
    ma_ref[...] = (sga_ref[...].astype(F32) * ya).astype(BF16)


def _gmlp(u, v, sga, wsp_b, bsp_t, wproj_b):
    N = u.shape[0]
    tm = TM_GMLP
    row = lambda i: (i, 0)
    return pl.pallas_call(
        _gmlp_kernel,
        grid=(N // tm,),
        in_specs=[
            pl.BlockSpec((tm, GMLP_WIDTH), row),
            pl.BlockSpec((tm, GMLP_WIDTH), row),
            pl.BlockSpec((tm, D_MODEL), row),
            pl.BlockSpec((GMLP_GROUPS, GMLP_BLOCK, GMLP_BLOCK), lambda i: (0, 0, 0)),
            pl.BlockSpec((GMLP_BLOCK, GMLP_GROUPS), lambda i: (0, 0)),
            pl.BlockSpec((GMLP_WIDTH, D_MODEL), lambda i: (0, 0)),
        ],
        out_specs=pl.BlockSpec((tm, D_MODEL), row),
        out_shape=jax.ShapeDtypeStruct((N, D_MODEL), BF16),
        scratch_shapes=[pltpu.VMEM((tm, GMLP_WIDTH), BF16)],
        compiler_params=pltpu.CompilerParams(
            dimension_semantics=("arbitrary",), vmem_limit_bytes=VMEM_LIMIT),
        name="gmlp",
    )(u, v, sga, wsp_b, bsp_t, wproj_b)


NEG_BIG = -1e30


def _attn_kernel(q_ref, k_ref, v_ref, o_ref, m_sc, l_sc, acc_sc):
    qi = pl.program_id(2)
    m_sc[...] = jnp.full(m_sc.shape, NEG_BIG, F32)
    l_sc[...] = jnp.zeros(l_sc.shape, F32)
    acc_sc[...] = jnp.zeros(acc_sc.shape, F32)
    nt = TQ // LANES

    def step(j, masked):
        start = pl.multiple_of(j * TQ, TQ)
        for hh in range(HEADS_PER_STEP):
            kb = k_ref[0, hh, pl.ds(start, TQ), :]
            vb = v_ref[0, hh, pl.ds(start, TQ), :]
            s = lax.dot_general(q_ref[0, hh], kb, (((1,), (1,)), ((), ())),
                                preferred_element_type=F32)
            if masked:
                qc = lax.broadcasted_iota(jnp.int32, (TQ, TQ), 0) // CHUNK
                kc = lax.broadcasted_iota(jnp.int32, (TQ, TQ), 1) // CHUNK
                s = jnp.where(kc <= qc, s, NEG_BIG)
            tiles = [s[:, c * LANES:(c + 1) * LANES] for c in range(nt)]
            tile_max = functools.reduce(jnp.maximum, tiles)
            m_prev = m_sc[hh]
            m_new = jnp.maximum(m_prev, jnp.max(tile_max, axis=-1, keepdims=True))
            alpha = jnp.exp2(m_prev - m_new)
            ps = [jnp.exp2(t - m_new) for t in tiles]
            l_sc[hh] = alpha * l_sc[hh] + functools.reduce(jnp.add, ps)
            p = jnp.concatenate([t.astype(BF16) for t in ps], axis=1)
            acc_sc[hh] = alpha * acc_sc[hh] + _dot(p, vb)
            m_sc[hh] = m_new

    def body(j, carry):
        step(j, False)
        return carry

    lax.fori_loop(0, qi, body, 0)
    step(qi, True)
    for hh in range(HEADS_PER_STEP):
        l = jnp.sum(l_sc[hh], axis=-1, keepdims=True)
        o_ref[0, :, hh * V_HEAD_DIM:(hh + 1) * V_HEAD_DIM] = (acc_sc[hh] / l).astype(BF16)


def _attention(q, k, v):
    B, H, S, _ = q.shape
    hps = HEADS_PER_STEP
    return pl.pallas_call(
        _attn_kernel,
        grid=(B, H // hps, S // TQ),
        in_specs=[
            pl.BlockSpec((1, hps, TQ, QK_PAD), lambda b, h, i: (b, h, i, 0)),
            pl.BlockSpec((1, hps, S, QK_PAD), lambda b, h, i: (b, h, 0, 0)),
            pl.BlockSpec((1, hps, S, V_HEAD_DIM), lambda b, h, i: (b, h, 0, 0)),
        ],
        out_specs=pl.BlockSpec((1, TQ, hps * V_HEAD_DIM), lambda b, h, i: (b, i, h)),
        out_shape=jax.ShapeDtypeStruct((B, S, H * V_HEAD_DIM), BF16),
        scratch_shapes=[
            pltpu.VMEM((hps, TQ, LANES), F32),
            pltpu.VMEM((hps, TQ, LANES), F32),
            pltpu.VMEM((hps, TQ, V_HEAD_DIM), F32),
        ],
        compiler_params=pltpu.CompilerParams(
            dimension_semantics=("arbitrary", "arbitrary", "arbitrary"),
            vmem_limit_bytes=VMEM_LIMIT),
        name="attention",
    )(q, k, v)


def _lane_sum(x):
    return jnp.sum(x, axis=-1, keepdims=True)


def _first_lane(hit, lane_f):
    return jnp.min(jnp.where(hit, lane_f, float(LANES)), axis=-1, keepdims=True).astype(jnp.int32)


def _out_route_kernel(o_ref, ma_ref, sgb_ref, x_ref, wo_ref, wout_ref, gmoe_ref, wr_ref, br_ref,
                      h1_ref, m_ref, rt_ref, rw_ref, cnt_ref, carry_sc):
    i = pl.program_id(0)
    tm = TM_OUT

    @pl.when(i == 0)
    def _():
        carry_sc[...] = jnp.zeros(carry_sc.shape, F32)

    yb = _dot(o_ref[...], wo_ref[...])
    merged = ma_ref[...].astype(F32) + sgb_ref[...].astype(F32) * yb
    h1 = x_ref[...] + _dot(merged.astype(BF16), wout_ref[...])
    h1_ref[...] = h1
    m = _rms(h1) * gmoe_ref[...]
    _store_rows(m_ref, m)

    logits = _dot(m.astype(BF16), wr_ref[...])
    lane = lax.broadcasted_iota(jnp.int32, (tm, LANES), 1)
    lane_f = lane.astype(F32)
    bias = br_ref[...]
    is_g = (lane >= N_EXPERTS) & (lane < N_EXPERTS + N_GROUPS)
    neg = jnp.float32(-jnp.inf)

    gl = jnp.where(is_g, logits, neg)
    ge = jnp.where(is_g, jnp.exp(gl - jnp.max(gl, axis=-1, keepdims=True)), 0.0)
    g_prob = ge / _lane_sum(ge)
    g_score = jnp.where(is_g, g_prob + bias, neg)
    g_best = jnp.max(g_score, axis=-1, keepdims=True)
    g_lane = _first_lane(g_score == g_best, lane_f)
    g_w = _lane_sum(jnp.where(lane == g_lane, g_prob, 0.0))
    g_idx = g_lane - N_EXPERTS

    in_g = (lane // EXPERTS_PER_GROUP) == g_idx
    el = jnp.where(in_g, logits, neg)
    ee = jnp.where(in_g, jnp.exp(el - jnp.max(el, axis=-1, keepdims=True)), 0.0)
    e_prob = ee / _lane_sum(ee)
    e_score = jnp.where(in_g, e_prob + bias, neg)
    best1 = jnp.max(e_score, axis=-1, keepdims=True)
    id1 = _first_lane(e_score == best1, lane_f)
    e_score2 = jnp.where(lane == id1, neg, e_score)
    best2 = jnp.max(e_score2, axis=-1, keepdims=True)
    id2 = _first_lane(e_score2 == best2, lane_f)
    p1 = _lane_sum(jnp.where(lane == id1, e_prob, 0.0))
    p2 = _lane_sum(jnp.where(lane == id2, e_prob, 0.0))
    psum = p1 + p2
    w1 = g_w * (p1 / psum)
    w2 = g_w * (p2 / psum)

    oh = ((lane == id1) | (lane == id2 + N_EXPERTS)).astype(BF16)
    r_out = lax.broadcasted_iota(jnp.int32, (tm, tm), 0)
    r_in = lax.broadcasted_iota(jnp.int32, (tm, tm), 1)
    tri = (r_in < r_out).astype(BF16)
    prefix = _dot(tri, oh)
    tot = jnp.sum(oh.astype(F32), axis=0, keepdims=True)
    tot_sw = pltpu.roll(tot, N_EXPERTS, axis=1)
    lane1 = lax.broadcasted_iota(jnp.int32, (1, LANES), 1)
    carry = carry_sc[...]
    base = carry + jnp.where(lane1 >= N_EXPERTS, tot_sw, 0.0)
    rk = oh.astype(F32) * (base + prefix)
    rank1 = _lane_sum(jnp.where(lane < N_EXPERTS, rk, 0.0))
    rank2 = _lane_sum(jnp.where(lane >= N_EXPERTS, rk, 0.0))
    carry_new = carry + tot + tot_sw
    carry_sc[...] = carry_new
    cnt_ref[...] = carry_new.astype(jnp.int32)

    ri = jnp.where(lane == 0, id1.astype(F32), jnp.where(lane == 1, id2.astype(F32), 0.0))
    ri = jnp.where(lane == 2, rank1, ri)
    ri = jnp.where(lane == 3, rank2, ri)
    rt_ref[...] = ri.T[0:SUBLANES, :].astype(jnp.int32)
    rw_ref[...] = jnp.where(lane == 0, w1, jnp.where(lane == 1, w2, 0.0))


def _out_route(o2, ma, sgb, x2, wo_b, wout_b, g_moe, wr_b, br):
    N = x2.shape[0]
    tm = TM_OUT
    row = lambda i: (i, 0)
    const = lambda i: (0, 0)
    return pl.pallas_call(
        _out_route_kernel,
        grid=(N // tm,),
        in_specs=[
            pl.BlockSpec((tm, D_MODEL), row),
            pl.BlockSpec((tm, D_MODEL), row),
            pl.BlockSpec((tm, D_MODEL), row),
            pl.BlockSpec((tm, D_MODEL), row),
            pl.BlockSpec((D_MODEL, D_MODEL), const),
            pl.BlockSpec((D_MODEL, D_MODEL), const),
            pl.BlockSpec((1, D_MODEL), const),
            pl.BlockSpec((D_MODEL, LANES), const),
            pl.BlockSpec((1, LANES), const),
        ],
        out_specs=[
            pl.BlockSpec((tm, D_MODEL), row),
            pl.BlockSpec((tm * SUBLANES, LANES), row),
            pl.BlockSpec((SUBLANES, tm), lambda i: (0, i)),
            pl.BlockSpec((tm, LANES), row),
            pl.BlockSpec((1, LANES), const),
        ],
        out_shape=[
            jax.ShapeDtypeStruct((N, D_MODEL), F32),
            jax.ShapeDtypeStruct((N * SUBLANES, LANES), F32),
            jax.ShapeDtypeStruct((SUBLANES, N), jnp.int32),
            jax.ShapeDtypeStruct((N, LANES), F32),
            jax.ShapeDtypeStruct((1, LANES), jnp.int32),
        ],
        scratch_shapes=[pltpu.VMEM((1, LANES), F32)],
        compiler_params=pltpu.CompilerParams(
            dimension_semantics=("arbitrary",), vmem_limit_bytes=VMEM_LIMIT),
        name="out_route",
    )(o2, ma, sgb, x2, wo_b, wout_b, g_moe, wr_b, br)


def _row_tile(ref2, row8):
    return ref2.at[pl.ds(pl.multiple_of(row8, SUBLANES), SUBLANES)]


def _dispatch_kernel(pad_end_ref, padded_ref, dest_ref, m_ref, buf_ref, zero_sc, sem, zsem):
    i = pl.program_id(0)
    tm = TM_ROWS
    blk8 = MOE_BLK * SUBLANES

    @pl.when(i == 0)
    def _():
        zero_sc[...] = jnp.zeros(zero_sc.shape, F32)

        def zero_copy(e):
            start = pl.multiple_of((pad_end_ref[e] - MOE_BLK) * SUBLANES, blk8)
            return pltpu.make_async_copy(zero_sc, buf_ref.at[pl.ds(start, blk8)], zsem)

        def start(e, carry):
            @pl.when(padded_ref[e] > 0)
            def _():
                zero_copy(e).start()
            return carry

        def wait(e, carry):
            @pl.when(padded_ref[e] > 0)
            def _():
                zero_copy(e).wait()
            return carry

        lax.fori_loop(0, N_EXPERTS, start, 0)
        lax.fori_loop(0, N_EXPERTS, wait, 0)

        def tail_copy(b):
            return pltpu.make_async_copy(
                zero_sc, buf_ref.at[pl.ds(pl.multiple_of(b * blk8, blk8), blk8)], zsem)

        nused = pad_end_ref[N_EXPERTS - 1] // MOE_BLK
        nblk = buf_ref.shape[0] // blk8
        lax.fori_loop(nused, nblk, lambda b, c: (tail_copy(b).start(), c)[1], 0)
        lax.fori_loop(nused, nblk, lambda b, c: (tail_copy(b).wait(), c)[1], 0)

    def issue(c, carry):
        for u in range(DMA_UNROLL):
            r = c * DMA_UNROLL + u
            src = _row_tile(m_ref, r * SUBLANES)
            pltpu.make_async_copy(src, _row_tile(buf_ref, dest_ref[0, 0, r]), sem).start(priority=0)
            pltpu.make_async_copy(src, _row_tile(buf_ref, dest_ref[0, 0, tm + r]), sem).start(priority=1)
        return carry

    lax.fori_loop(0, tm // DMA_UNROLL, issue, 0)
    for _ in range(2):
        pltpu.make_async_copy(m_ref, buf_ref.at[pl.ds(0, tm * SUBLANES)], sem).wait()


def _dispatch(pad_end, padded, dest8_tiles, m2, P):
    tm = TM_ROWS
    N = m2.shape[0] // SUBLANES
    grid_spec = pltpu.PrefetchScalarGridSpec(
        num_scalar_prefetch=2,
        grid=(N // tm,),
        in_specs=[
            pl.BlockSpec((1, 1, 2 * tm), lambda i, pe, pd: (i, 0, 0), memory_space=pltpu.SMEM),
            pl.BlockSpec((tm * SUBLANES, LANES), lambda i, pe, pd: (i, 0)),
        ],
        out_specs=pl.BlockSpec(memory_space=pl.ANY),
        scratch_shapes=[
            pltpu.VMEM((MOE_BLK * SUBLANES, LANES), F32),
            pltpu.SemaphoreType.DMA(()),
            pltpu.SemaphoreType.DMA(()),
        ],
    )
    return pl.pallas_call(
        _dispatch_kernel,
        grid_spec=grid_spec,
        out_shape=jax.ShapeDtypeStruct((P * SUBLANES, LANES), F32),
        compiler_params=pltpu.CompilerParams(dimension_semantics=("arbitrary",)),
        name="dispatch",
    )(pad_end, padded, dest8_tiles, m2)


def _expert_kernel(blk_e_ref, nused_ref, x_ref, wg_ref, wu_ref, wd_ref, y_ref, wgu_sc, wd_sc):
    i = pl.program_id(0)
    live = i < nused_ref[0]
    new_expert = (i == 0) | (blk_e_ref[i] != blk_e_ref[jnp.maximum(i - 1, 0)])

    @pl.when(live & new_expert)
    def _():
        wgu_sc[:, 0:D_EXPERT] = wg_ref[0].astype(BF16)
        wgu_sc[:, D_EXPERT:2 * D_EXPERT] = wu_ref[0].astype(BF16)
        wd_sc[...] = wd_ref[0].astype(BF16)

    @pl.when(live)
    def _():
        xb = _load_rows(x_ref, MOE_BLK).astype(BF16)
        h = _dot(xb, wgu_sc[...])
        hdn = (jax.nn.silu(h[:, 0:D_EXPERT]) * h[:, D_EXPERT:2 * D_EXPERT]).astype(BF16)
        _store_rows(y_ref, _dot(hdn, wd_sc[...]))

    @pl.when(i >= nused_ref[0])
    def _():
        y_ref[...] = jnp.zeros(y_ref.shape, F32)


def _experts(blk_e, nused, buf2, wg, wu, wd):
    nblk = buf2.shape[0] // (MOE_BLK * SUBLANES)

    def used(i, nu):
        return jnp.minimum(i, nu[0] - 1)

    blk_rows = MOE_BLK * SUBLANES
    grid_spec = pltpu.PrefetchScalarGridSpec(
        num_scalar_prefetch=2,
        grid=(nblk,),
        in_specs=[
            pl.BlockSpec((blk_rows, LANES), lambda i, be, nu: (used(i, nu), 0)),
            pl.BlockSpec((1, D_MODEL, D_EXPERT), lambda i, be, nu: (be[used(i, nu)], 0, 0)),
            pl.BlockSpec((1, D_MODEL, D_EXPERT), lambda i, be, nu: (be[used(i, nu)], 0, 0)),
            pl.BlockSpec((1, D_EXPERT, D_MODEL), lambda i, be, nu: (be[used(i, nu)], 0, 0)),
        ],
        out_specs=pl.BlockSpec((blk_rows, LANES), lambda i, be, nu: (i, 0)),
        scratch_shapes=[
            pltpu.VMEM((D_MODEL, 2 * D_EXPERT), BF16),
            pltpu.VMEM((D_EXPERT, D_MODEL), BF16),
        ],
    )
    return pl.pallas_call(
        _expert_kernel,
        grid_spec=grid_spec,
        out_shape=jax.ShapeDtypeStruct(buf2.shape, F32),
        compiler_params=pltpu.CompilerParams(
            dimension_semantics=("arbitrary",), vmem_limit_bytes=VMEM_LIMIT),
        name="experts",
    )(blk_e, nused, buf2, wg, wu, wd)


def _final_kernel(dcur_ref, dnext_ref, h1_ref, rw_ref, p_ref, gple_ref, wpg_ref, wpp_ref, gfin_ref,
                  ys_ref, out_ref, ybuf_a, ybuf_b, sem):
    g = pl.program_id(0)
    ng = pl.num_programs(0)
    tm = TM_ROWS
    bufs = (ybuf_a, ybuf_b)

    def row_copy(d_ref, off, r, which):
        return pltpu.make_async_copy(_row_tile(ys_ref, d_ref[0, 0, off + r]),
                                     _row_tile(bufs[which], r * SUBLANES), sem.at[which])

    def issue(d_ref, off, which):
        for r in range(2 * tm):
            row_copy(d_ref, off, r, which).start(priority=r % 2)

    def wait(which):
        pltpu.make_async_copy(ys_ref.at[pl.ds(0, 2 * tm * SUBLANES)], bufs[which],
                              sem.at[which]).wait()

    def compute(which, rows):
        y0 = _load_rows(bufs[which], tm)
        y1 = _load_rows(bufs[which], tm, first_row=tm)
        rw = rw_ref[rows, :]
        h2 = h1_ref[rows, :] + (y0 * rw[:, 0:1] + y1 * rw[:, 1:2])
        n3 = (_rms(h2) * gple_ref[...]).astype(BF16)
        gate = jax.nn.sigmoid(_dot(n3, wpg_ref[...]))
        pp = _dot(p_ref[rows, :].astype(BF16), wpp_ref[...])
        h3 = h2 + gate * pp
        out_ref[rows, :] = _rms(h3) * gfin_ref[...]

    @pl.when(g == 0)
    def _():
        def body(c, carry):
            for u in range(DMA_UNROLL):
                row_copy(dcur_ref, 0, c * DMA_UNROLL + u, 0).start()
            return carry

        lax.fori_loop(0, 2 * tm // DMA_UNROLL, body, 0)

    wait(0)
    issue(dcur_ref, 2 * tm, 1)
    compute(0, slice(0, tm))
    wait(1)
    issue(dnext_ref, 0, 0)
    compute(1, slice(tm, 2 * tm))

    @pl.when(g == ng - 1)
    def _():
        wait(0)


def _final(dest8_pairs, h1, rw, p2, g_ple, wpg_b, wpp_b, g_final, ys2):
    N = h1.shape[0]
    tm = TM_ROWS
    ng = N // (2 * tm)
    row = lambda i: (i, 0)
    const = lambda i: (0, 0)
    return pl.pallas_call(
        _final_kernel,
        grid=(ng,),
        in_specs=[
            pl.BlockSpec((1, 1, 4 * tm), lambda i: (i, 0, 0), memory_space=pltpu.SMEM),
            pl.BlockSpec((1, 1, 4 * tm), lambda i: (jnp.minimum(i + 1, ng - 1), 0, 0),
                         memory_space=pltpu.SMEM),
            pl.BlockSpec((2 * tm, D_MODEL), row),
            pl.BlockSpec((2 * tm, LANES), row),
            pl.BlockSpec((2 * tm, PLE_DIM), row),
            pl.BlockSpec((1, D_MODEL), const),
            pl.BlockSpec((D_MODEL, D_MODEL), const),
            pl.BlockSpec((PLE_DIM, D_MODEL), const),
            pl.BlockSpec((1, D_MODEL), const),
            pl.BlockSpec(memory_space=pl.ANY),
        ],
        out_specs=pl.BlockSpec((2 * tm, D_MODEL), row),
        out_shape=jax.ShapeDtypeStruct((N, D_MODEL), F32),
        scratch_shapes=[
            pltpu.VMEM((2 * tm * SUBLANES, LANES), F32),
            pltpu.VMEM((2 * tm * SUBLANES, LANES), F32),
            pltpu.SemaphoreType.DMA((2,)),
        ],
        compiler_params=pltpu.CompilerParams(
            dimension_semantics=("arbitrary",), vmem_limit_bytes=VMEM_LIMIT),
        name="final",
    )(dest8_pairs, dest8_pairs, h1, rw, p2, g_ple, wpg_b, wpp_b, g_final, ys2)


def _rope_tables(S):
    inv_freq = ROPE_THETA ** (-jnp.arange(0, QK_ROPE_DIM, 2, dtype=F32) / QK_ROPE_DIM)
    ang = jnp.arange(S, dtype=F32)[:, None] * inv_freq[None, :]
    cos, sin = jnp.cos(ang), jnp.sin(ang)
    z = jnp.zeros_like(cos)
    rc = jnp.concatenate([cos, cos, z, z], axis=1)
    rs1 = jnp.concatenate([-sin, z, z, z], axis=1)
    rs2 = jnp.concatenate([z, sin, z, z], axis=1)
    return rc, rs1, rs2


def _layer(h, p_l, g_mix, w_in, g_gv, w_spatial, b_spatial, w_gproj, g_cq, w_uq, g_ckv, w_ukv,
           w_mla_o, w_out, g_moe, w_router_g, b_router_g, w_router_e, b_router_e,
           w_e_gate, w_e_up, w_e_down, g_ple, w_ple_gate, w_ple_proj, g_out):
    B, S, D = h.shape
    N = B * S
    x2 = h.reshape(N, D)

    cu, cv, ccq, cckv, ckr, cga = (GMLP_WIDTH, 2 * GMLP_WIDTH, 2 * GMLP_WIDTH + Q_LORA,
                                   2 * GMLP_WIDTH + Q_LORA + KV_LORA,
                                   2 * GMLP_WIDTH + Q_LORA + KV_LORA + QK_ROPE_DIM,
                                   2 * GMLP_WIDTH + Q_LORA + KV_LORA + QK_ROPE_DIM + D_MODEL)
    w_in_p = jnp.concatenate(
        [w_in[:, :cckv], w_in[:, ckr:], w_in[:, cckv:ckr],
         jnp.zeros((D, LANES - QK_ROPE_DIM), w_in.dtype)], axis=1).astype(BF16)
    w_uq_h = w_uq.reshape(Q_LORA, MLA_HEADS, QK_NOPE_DIM + QK_ROPE_DIM)
    w_uq_p = jnp.concatenate(
        [w_uq_h, jnp.zeros((Q_LORA, MLA_HEADS, QK_PAD - QK_NOPE_DIM - QK_ROPE_DIM), w_uq.dtype)],
        axis=2).reshape(Q_LORA, MLA_HEADS * QK_PAD).astype(BF16)
    rc, rs1, rs2 = _rope_tables(S)

    u, v, sga, sgb, q, k, vv = _inproj(
        x2, g_mix[None], w_in_p, g_gv[None], g_cq[None], g_ckv[None], w_uq_p, w_ukv.astype(BF16),
        rc, rs1, rs2, B, S)
    ma = _gmlp(u, v, sga, w_spatial.astype(BF16), b_spatial.T, w_gproj.astype(BF16))
    o = _attention(q, k, vv)

    wr = jnp.concatenate(
        [w_router_e, w_router_g, jnp.zeros((D, LANES - N_EXPERTS - N_GROUPS), w_router_e.dtype)],
        axis=1).astype(BF16)
    br = jnp.concatenate(
        [b_router_e.reshape(-1), b_router_g, jnp.zeros((LANES - N_EXPERTS - N_GROUPS,), F32)])[None]
    h1, m, rt, rw, cnt = _out_route(
        o.reshape(N, D), ma, sgb, x2, w_mla_o.astype(BF16), w_out.astype(BF16), g_moe[None], wr, br)

    counts = cnt[0, :N_EXPERTS]
    padded = (counts + MOE_BLK - 1) // MOE_BLK * MOE_BLK
    pad_end = jnp.cumsum(padded)
    pad_start = pad_end - padded
    P = 2 * N + N_EXPERTS * MOE_BLK
    nblk = P // MOE_BLK
    blk_start = jnp.arange(nblk, dtype=jnp.int32) * MOE_BLK
    blk_e = jnp.minimum(
        jnp.sum((pad_end[None, :] <= blk_start[:, None]).astype(jnp.int32), axis=1),
        N_EXPERTS - 1).astype(jnp.int32)
    nused = (pad_end[-1:] // MOE_BLK).astype(jnp.int32)
    experts = jnp.arange(N_EXPERTS, dtype=jnp.int32)[:, None, None]
    start_of = jnp.sum(jnp.where(rt[None, 0:2] == experts, pad_start[:, None, None], 0), axis=0)
    dest8 = (start_of + rt[2:4]).astype(jnp.int32) * SUBLANES
    nt = N // TM_ROWS
    dest8_tiles = dest8.reshape(2, nt, TM_ROWS).transpose(1, 0, 2).reshape(nt, 1, 2 * TM_ROWS)

    buf = _dispatch(pad_end.astype(jnp.int32), padded.astype(jnp.int32), dest8_tiles, m, P)
    ys = _experts(blk_e, nused, buf, w_e_gate, w_e_up, w_e_down)
    out = _final(dest8_tiles.reshape(nt // 2, 1, 4 * TM_ROWS), h1, rw, p_l.reshape(N, PLE_DIM), g_ple[None],
                 w_ple_gate.astype(BF16), w_ple_proj.astype(BF16), g_out[None], ys)
    return out.reshape(B, S, D)


def kernel(x, p, g_mix, w_in, g_gv, w_spatial, b_spatial, w_gproj, g_cq, w_uq, g_ckv, w_ukv, w_mla_o,
           w_out, g_moe, w_router_g, b_router_g, w_router_e, b_router_e, w_e_gate, w_e_up, w_e_down,
           g_ple, w_ple_gate, w_ple_proj, g_final):
    depth = p.shape[0]
    assert depth == 1, "the final rmsnorm is fused into the single layer's last kernel"
    i = 0
    return _layer(x, p[i], g_mix[i], w_in[i], g_gv[i], w_spatial[i], b_spatial[i], w_gproj[i], g_cq[i],
                  w_uq[i], g_ckv[i], w_ukv[i], w_mla_o[i], w_out[i], g_moe[i], w_router_g[i],
                  b_router_g[i], w_router_e[i], b_router_e[i], w_e_gate[i], w_e_up[i], w_e_down[i],
                  g_ple[i], w_ple_gate[i], w_ple_proj[i], g_final)
```

```python
import functools

import jax
import jax.numpy as jnp
from jax import lax
from jax.experimental import pallas as pl
from jax.experimental.pallas import tpu as pltpu

F32 = jnp.float32
BF16 = jnp.bfloat16

D_MODEL = 1024
CHUNK = 64
PLE_DIM = 256
GMLP_BLOCK = 128
GMLP_GROUPS = 12
GMLP_WIDTH = 1536
MLA_HEADS = 8
QK_NOPE_DIM = 128
QK_ROPE_DIM = 64
V_HEAD_DIM = 128
Q_LORA = 384
KV_LORA = 256
ROPE_THETA = 10000.0
N_GROUPS = 8
EXPERTS_PER_GROUP = 8
N_EXPERTS = 64
D_EXPERT = 256
EPS = 1e-6
LOG2E = 1.4426950408889634

LANES = 128
SUBLANES = 8
QK_PAD = 256
VMEM_LIMIT = 56 * 1024 * 1024

C_U = 0
C_V = C_U + GMLP_WIDTH
C_CQ = C_V + GMLP_WIDTH
C_CKV = C_CQ + Q_LORA
C_GA = C_CKV + KV_LORA
C_GB = C_GA + D_MODEL
C_KR = C_GB + D_MODEL
C_END = C_KR + LANES

TM_IN = 512
TM_GMLP = 512
TQ = 512
HEADS_PER_STEP = 4
TM_OUT = 512
TM_ROWS = 512
MOE_BLK = 256
X_AHEAD = 2
DMA_UNROLL = 8


def _rms(x):
    return x * lax.rsqrt(jnp.mean(x * x, axis=-1, keepdims=True) + EPS)


def _dot(a, b):
    return jnp.dot(a, b, preferred_element_type=F32)


def _store_rows(ref2, x):
    rows = x.shape[0]
    for j in range(SUBLANES):
        ref2[pl.ds(j, rows, stride=SUBLANES), :] = x[:, j * LANES:(j + 1) * LANES]


def _load_rows(ref2, rows, first_row=0):
    return jnp.concatenate(
        [ref2[pl.ds(first_row * SUBLANES + j, rows, stride=SUBLANES), :] for j in range(SUBLANES)],
        axis=1)


def _rope128(t, rc, rs1, rs2):
    r1 = pltpu.roll(t, 96, axis=1)
    r2 = pltpu.roll(t, 32, axis=1)
    return t * rc + r1 * rs1 + r2 * rs2


def _inproj_kernel(x_ref, gmix_ref, win_ref, ggv_ref, gcq_ref, gckv_ref, wuq_ref, wukv_ref,
                   rc_ref, rs1_ref, rs2_ref,
                   u_ref, v_ref, sga_ref, sgb_ref, q_ref, k_ref, vv_ref):
    x = x_ref[...]
    ab = (_rms(x) * gmix_ref[...]).astype(BF16)

    def proj(c0, c1):
        return _dot(ab, win_ref[:, c0:c1])

    u_ref[...] = jax.nn.gelu(proj(C_U, C_V)).astype(BF16)

    zv = jax.nn.gelu(proj(C_V, C_CQ))
    xc = zv - jnp.mean(zv, axis=-1, keepdims=True)
    vln = xc * lax.rsqrt(jnp.mean(xc * xc, axis=-1, keepdims=True) + EPS)
    v_ref[...] = (vln * ggv_ref[...]).astype(BF16)

    sga_ref[...] = jax.nn.sigmoid(proj(C_GA, C_GB)).astype(BF16)
    sgb_ref[...] = jax.nn.sigmoid(proj(C_GB, C_KR)).astype(BF16)

    rc = rc_ref[...]
    rs1 = rs1_ref[...]
    rs2 = rs2_ref[...]
    kpe = _rope128(proj(C_KR, C_END), rc, rs1, rs2).astype(BF16)

    cqn = (_rms(proj(C_CQ, C_CKV)) * gcq_ref[...]).astype(BF16)
    ckvn = (_rms(proj(C_CKV, C_GA)) * gckv_ref[...]).astype(BF16)
    scale = (QK_NOPE_DIM + QK_ROPE_DIM) ** -0.5 * LOG2E
    for h in range(MLA_HEADS):
        qh = _dot(cqn, wuq_ref[:, h * QK_PAD:(h + 1) * QK_PAD])
        q_ref[0, h, :, 0:LANES] = (qh[:, 0:LANES] * scale).astype(BF16)
        q_ref[0, h, :, LANES:QK_PAD] = (_rope128(qh[:, LANES:QK_PAD], rc, rs1, rs2) * scale).astype(BF16)
        kvh = _dot(ckvn, wukv_ref[:, h * 256:(h + 1) * 256])
        k_ref[0, h, :, 0:LANES] = kvh[:, 0:LANES].astype(BF16)
        k_ref[0, h, :, LANES:QK_PAD] = kpe
        vv_ref[0, h] = kvh[:, LANES:256].astype(BF16)


def _inproj(x2, g_mix, w_in_p, g_gv, g_cq, g_ckv, w_uq_p, w_ukv_b, rc, rs1, rs2, B, S):
    N = x2.shape[0]
    tm = TM_IN
    spt = S // tm
    row = lambda i: (i, 0)
    const = lambda i: (0, 0)
    pos = lambda i: (i % spt, 0)
    head = lambda i: (i // spt, 0, i % spt, 0)
    return pl.pallas_call(
        _inproj_kernel,
        grid=(N // tm,),
        in_specs=[
            pl.BlockSpec((tm, D_MODEL), row),
            pl.BlockSpec((1, D_MODEL), const),
            pl.BlockSpec((D_MODEL, C_END), const, pipeline_mode=pl.Buffered(1)),
            pl.BlockSpec((1, GMLP_WIDTH), const),
            pl.BlockSpec((1, Q_LORA), const),
            pl.BlockSpec((1, KV_LORA), const),
            pl.BlockSpec((Q_LORA, MLA_HEADS * QK_PAD), const),
            pl.BlockSpec((KV_LORA, MLA_HEADS * 256), const),
            pl.BlockSpec((tm, LANES), pos),
            pl.BlockSpec((tm, LANES), pos),
            pl.BlockSpec((tm, LANES), pos),
        ],
        out_specs=[
            pl.BlockSpec((tm, GMLP_WIDTH), row),
            pl.BlockSpec((tm, GMLP_WIDTH), row),
            pl.BlockSpec((tm, D_MODEL), row),
            pl.BlockSpec((tm, D_MODEL), row),
            pl.BlockSpec((1, MLA_HEADS, tm, QK_PAD), head),
            pl.BlockSpec((1, MLA_HEADS, tm, QK_PAD), head),
            pl.BlockSpec((1, MLA_HEADS, tm, V_HEAD_DIM), head),
        ],
        out_shape=[
            jax.ShapeDtypeStruct((N, GMLP_WIDTH), BF16),
            jax.ShapeDtypeStruct((N, GMLP_WIDTH), BF16),
            jax.ShapeDtypeStruct((N, D_MODEL), BF16),
            jax.ShapeDtypeStruct((N, D_MODEL), BF16),
            jax.ShapeDtypeStruct((B, MLA_HEADS, S, QK_PAD), BF16),
            jax.ShapeDtypeStruct((B, MLA_HEADS, S, QK_PAD), BF16),
            jax.ShapeDtypeStruct((B, MLA_HEADS, S, V_HEAD_DIM), BF16),
        ],
        compiler_params=pltpu.CompilerParams(
            dimension_semantics=("arbitrary",), vmem_limit_bytes=VMEM_LIMIT),
        name="inproj",
    )(x2, g_mix, w_in_p, g_gv, g_cq, g_ckv, w_uq_p, w_ukv_b, rc, rs1, rs2)


def _gmlp_kernel(u_ref, v_ref, sga_ref, wsp_ref, bsp_ref, wproj_ref, ma_ref, y_sc):
    nb = TM_GMLP // GMLP_BLOCK
    t_out = lax.broadcasted_iota(jnp.int32, (GMLP_BLOCK, GMLP_BLOCK), 0)
    s_in = lax.broadcasted_iota(jnp.int32, (GMLP_BLOCK, GMLP_BLOCK), 1)
    mask = (s_in // CHUNK) <= (t_out // CHUNK)
    for g in range(GMLP_GROUPS):
        c0 = g * LANES
        w = jnp.where(mask, wsp_ref[g], jnp.zeros((), BF16))
        rhs = jnp.concatenate(
            [v_ref[r * GMLP_BLOCK:(r + 1) * GMLP_BLOCK, c0:c0 + LANES] for r in range(nb)], axis=1)
        sv = _dot(w, rhs) + bsp_ref[:, g:g + 1]
        for r in range(nb):
            rows = slice(r * GMLP_BLOCK, (r + 1) * GMLP_BLOCK)
            ub = u_ref[rows, c0:c0 + LANES].astype(F32)
            y_sc[rows, c0:c0 + LANES] = (ub * sv[:, r * LANES:(r + 1) * LANES]).astype(BF16)
    ya = _dot(y_sc[...], wproj_ref[...])
    ma_ref[...] = (sga_ref[...].astype(F32) * ya).astype(BF16)


def _gmlp(u, v, sga, wsp_b, bsp_t, wproj_b):
    N = u.shape[0]
    tm = TM_GMLP
    row = lambda i: (i, 0)
    return pl.pallas_call(
        _gmlp_kernel,
        grid=(N // tm,),
        in_specs=[
            pl.BlockSpec((tm, GMLP_WIDTH), row),
            pl.BlockSpec((tm, GMLP_WIDTH), row),
            pl.BlockSpec((tm, D_MODEL), row),
            pl.BlockSpec((GMLP_GROUPS, GMLP_BLOCK, GMLP_BLOCK), lambda i: (0, 0, 0)),
            pl.BlockSpec((GMLP_BLOCK, GMLP_GROUPS), lambda i: (0, 0)),
            pl.BlockSpec((GMLP_WIDTH, D_MODEL), lambda i: (0, 0)),
        ],
        out_specs=pl.BlockSpec((tm, D_MODEL), row),
        out_shape=jax.ShapeDtypeStruct((N, D_MODEL), BF16),
        scratch_shapes=[pltpu.VMEM((tm, GMLP_WIDTH), BF16)],
        compiler_params=pltpu.CompilerParams(
            dimension_semantics=("arbitrary",), vmem_limit_bytes=VMEM_LIMIT),
        name="gmlp",
    )(u, v, sga, wsp_b, bsp_t, wproj_b)


NEG_BIG = -1e30


def _attn_kernel(q_ref, k_ref, v_ref, o_ref, m_sc, l_sc, acc_sc):
    qi = pl.program_id(2)
    m_sc[...] = jnp.full(m_sc.shape, NEG_BIG, F32)
    l_sc[...] = jnp.zeros(l_sc.shape, F32)
    acc_sc[...] = jnp.zeros(acc_sc.shape, F32)
    def chunk_mask(r0, nq, nk):
        qc = (lax.broadcasted_iota(jnp.int32, (nq, nk), 0) + r0) // CHUNK
        kc = lax.broadcasted_iota(jnp.int32, (nq, nk), 1) // CHUNK
        return kc <= qc

    def update(hh, r0, nq, start, nk, mask):
        rows = pl.ds(r0, nq)
        kb = k_ref[0, hh, pl.ds(start, nk), :]
        vb = v_ref[0, hh, pl.ds(start, nk), :]
        s = lax.dot_general(q_ref[0, hh, rows, :], kb, (((1,), (1,)), ((), ())),
                            preferred_element_type=F32)
        if mask is not None:
            s = jnp.where(mask, s, NEG_BIG)
        tiles = [s[:, c * LANES:(c + 1) * LANES] for c in range(nk // LANES)]
        tile_max = functools.reduce(jnp.maximum, tiles)
        m_prev = m_sc[hh, rows, :]
        m_new = jnp.maximum(m_prev, jnp.max(tile_max, axis=-1, keepdims=True))
        alpha = jnp.exp2(m_prev - m_new)
        ps = [jnp.exp2(t - m_new) for t in tiles]
        l_sc[hh, rows, :] = alpha * l_sc[hh, rows, :] + functools.reduce(jnp.add, ps)
        p = jnp.concatenate([t.astype(BF16) for t in ps], axis=1)
        acc_sc[hh, rows, :] = alpha * acc_sc[hh, rows, :] + _dot(p, vb)
        m_sc[hh, rows, :] = m_new

    def body(j, carry):
        start = pl.multiple_of(j * TQ, TQ)
        for hh in range(HEADS_PER_STEP):
            update(hh, 0, TQ, start, TQ, None)
        return carry

    lax.fori_loop(0, qi, body, 0)
    start = pl.multiple_of(qi * TQ, TQ)
    mask = chunk_mask(0, TQ, TQ)
    for hh in range(HEADS_PER_STEP):
        update(hh, 0, TQ, start, TQ, mask)
    for hh in range(HEADS_PER_STEP):
        l = jnp.sum(l_sc[hh], axis=-1, keepdims=True)
        o_ref[0, :, hh * V_HEAD_DIM:(hh + 1) * V_HEAD_DIM] = (acc_sc[hh] / l).astype(BF16)


def _attention(q, k, v):
    B, H, S, _ = q.shape
    hps = HEADS_PER_STEP
    return pl.pallas_call(
        _attn_kernel,
        grid=(B, H // hps, S // TQ),
        in_specs=[
            pl.BlockSpec((1, hps, TQ, QK_PAD), lambda b, h, i: (b, h, i, 0)),
            pl.BlockSpec((1, hps, S, QK_PAD), lambda b, h, i: (b, h, 0, 0)),
            pl.BlockSpec((1, hps, S, V_HEAD_DIM), lambda b, h, i: (b, h, 0, 0)),
        ],
        out_specs=pl.BlockSpec((1, TQ, hps * V_HEAD_DIM), lambda b, h, i: (b, i, h)),
        out_shape=jax.ShapeDtypeStruct((B, S, H * V_HEAD_DIM), BF16),
        scratch_shapes=[
            pltpu.VMEM((hps, TQ, LANES), F32),
            pltpu.VMEM((hps, TQ, LANES), F32),
            pltpu.VMEM((hps, TQ, V_HEAD_DIM), F32),
        ],
        compiler_params=pltpu.CompilerParams(
            dimension_semantics=("arbitrary", "arbitrary", "arbitrary"),
            vmem_limit_bytes=VMEM_LIMIT),
        name="attention",
    )(q, k, v)


def _lane_sum(x):
    return jnp.sum(x, axis=-1, keepdims=True)


def _first_lane(hit, lane_f):
    return jnp.min(jnp.where(hit, lane_f, float(LANES)), axis=-1, keepdims=True).astype(jnp.int32)


def _out_route_kernel(o_ref, ma_ref, sgb_ref, x_ref, wo_ref, wout_ref, gmoe_ref, wr_ref, br_ref,
                      h1_ref, m_ref, rt_ref, rw_ref, cnt_ref, carry_sc):
    i = pl.program_id(0)
    tm = TM_OUT

    @pl.when(i == 0)
    def _():
        carry_sc[...] = jnp.zeros(carry_sc.shape, F32)

    yb = _dot(o_ref[...], wo_ref[...])
    merged = ma_ref[...].astype(F32) + sgb_ref[...].astype(F32) * yb
    h1 = x_ref[...] + _dot(merged.astype(BF16), wout_ref[...])
    h1_ref[...] = h1
    m = _rms(h1) * gmoe_ref[...]
    _store_rows(m_ref, m)

    logits = _dot(m.astype(BF16), wr_ref[...])
    lane = lax.broadcasted_iota(jnp.int32, (tm, LANES), 1)
    lane_f = lane.astype(F32)
    bias = br_ref[...]
    is_g = (lane >= N_EXPERTS) & (lane < N_EXPERTS + N_GROUPS)
    neg = jnp.float32(-jnp.inf)

    gl = jnp.where(is_g, logits, neg)
    ge = jnp.where(is_g, jnp.exp(gl - jnp.max(gl, axis=-1, keepdims=True)), 0.0)
    g_prob = ge / _lane_sum(ge)
    g_score = jnp.where(is_g, g_prob + bias, neg)
    g_best = jnp.max(g_score, axis=-1, keepdims=True)
    g_lane = _first_lane(g_score == g_best, lane_f)
    g_w = _lane_sum(jnp.where(lane == g_lane, g_prob, 0.0))
    g_idx = g_lane - N_EXPERTS

    in_g = (lane // EXPERTS_PER_GROUP) == g_idx
    el = jnp.where(in_g, logits, neg)
    ee = jnp.where(in_g, jnp.exp(el - jnp.max(el, axis=-1, keepdims=True)), 0.0)
    e_prob = ee / _lane_sum(ee)
    e_score = jnp.where(in_g, e_prob + bias, neg)
    best1 = jnp.max(e_score, axis=-1, keepdims=True)
    id1 = _first_lane(e_score == best1, lane_f)
    e_score2 = jnp.where(lane == id1, neg, e_score)
    best2 = jnp.max(e_score2, axis=-1, keepdims=True)
    id2 = _first_lane(e_score2 == best2, lane_f)
    p1 = _lane_sum(jnp.where(lane == id1, e_prob, 0.0))
    p2 = _lane_sum(jnp.where(lane == id2, e_prob, 0.0))
    psum = p1 + p2
    w1 = g_w * (p1 / psum)
    w2 = g_w * (p2 / psum)

    oh = ((lane == id1) | (lane == id2 + N_EXPERTS)).astype(BF16)
    r_out = lax.broadcasted_iota(jnp.int32, (tm, tm), 0)
    r_in = lax.broadcasted_iota(jnp.int32, (tm, tm), 1)
    tri = (r_in < r_out).astype(BF16)
    prefix = _dot(tri, oh)
    tot = jnp.sum(oh.astype(F32), axis=0, keepdims=True)
    tot_sw = pltpu.roll(tot, N_EXPERTS, axis=1)
    lane1 = lax.broadcasted_iota(jnp.int32, (1, LANES), 1)
    carry = carry_sc[...]
    base = carry + jnp.where(lane1 >= N_EXPERTS, tot_sw, 0.0)
    rk = oh.astype(F32) * (base + prefix)
    rank1 = _lane_sum(jnp.where(lane < N_EXPERTS, rk, 0.0))
    rank2 = _lane_sum(jnp.where(lane >= N_EXPERTS, rk, 0.0))
    carry_new = carry + tot + tot_sw
    carry_sc[...] = carry_new
    cnt_ref[...] = carry_new.astype(jnp.int32)

    ri = jnp.where(lane == 0, id1.astype(F32), jnp.where(lane == 1, id2.astype(F32), 0.0))
    ri = jnp.where(lane == 2, rank1, ri)
    ri = jnp.where(lane == 3, rank2, ri)
    rt_ref[...] = ri.T[0:SUBLANES, :].astype(jnp.int32)
    rw_ref[...] = jnp.where(lane == 0, w1, jnp.where(lane == 1, w2, 0.0))


def _out_route(o2, ma, sgb, x2, wo_b, wout_b, g_moe, wr_b, br):
    N = x2.shape[0]
    tm = TM_OUT
    row = lambda i: (i, 0)
    const = lambda i: (0, 0)
    return pl.pallas_call(
        _out_route_kernel,
        grid=(N // tm,),
        in_specs=[
            pl.BlockSpec((tm, D_MODEL), row),
            pl.BlockSpec((tm, D_MODEL), row),
            pl.BlockSpec((tm, D_MODEL), row),
            pl.BlockSpec((tm, D_MODEL), row),
            pl.BlockSpec((D_MODEL, D_MODEL), const),
            pl.BlockSpec((D_MODEL, D_MODEL), const),
            pl.BlockSpec((1, D_MODEL), const),
            pl.BlockSpec((D_MODEL, LANES), const),
            pl.BlockSpec((1, LANES), const),
        ],
        out_specs=[
            pl.BlockSpec((tm, D_MODEL), row),
            pl.BlockSpec((tm * SUBLANES, LANES), row),
            pl.BlockSpec((SUBLANES, tm), lambda i: (0, i)),
            pl.BlockSpec((tm, LANES), row),
            pl.BlockSpec((1, LANES), const),
        ],
        out_shape=[
            jax.ShapeDtypeStruct((N, D_MODEL), F32),
            jax.ShapeDtypeStruct((N * SUBLANES, LANES), F32),
            jax.ShapeDtypeStruct((SUBLANES, N), jnp.int32),
            jax.ShapeDtypeStruct((N, LANES), F32),
            jax.ShapeDtypeStruct((1, LANES), jnp.int32),
        ],
        scratch_shapes=[pltpu.VMEM((1, LANES), F32)],
        compiler_params=pltpu.CompilerParams(
            dimension_semantics=("arbitrary",), vmem_limit_bytes=VMEM_LIMIT),
        name="out_route",
    )(o2, ma, sgb, x2, wo_b, wout_b, g_moe, wr_b, br)


def _row_tile(ref2, row8):
    return ref2.at[pl.ds(pl.multiple_of(row8, SUBLANES), SUBLANES)]


def _dispatch_kernel(pad_end_ref, padded_ref, dest_ref, m_ref, buf_ref, zero_sc, sem, zsem):
    i = pl.program_id(0)
    tm = TM_ROWS
    blk8 = MOE_BLK * SUBLANES

    @pl.when(i == 0)
    def _():
        zero_sc[...] = jnp.zeros(zero_sc.shape, F32)

        def zero_copy(e):
            start = pl.multiple_of((pad_end_ref[e] - MOE_BLK) * SUBLANES, blk8)
            return pltpu.make_async_copy(zero_sc, buf_ref.at[pl.ds(start, blk8)], zsem)

        def start(e, carry):
            @pl.when(padded_ref[e] > 0)
            def _():
                zero_copy(e).start()
            return carry

        def wait(e, carry):
            @pl.when(padded_ref[e] > 0)
            def _():
                zero_copy(e).wait()
            return carry

        lax.fori_loop(0, N_EXPERTS, start, 0)
        lax.fori_loop(0, N_EXPERTS, wait, 0)

        def tail_copy(b):
            return pltpu.make_async_copy(
                zero_sc, buf_ref.at[pl.ds(pl.multiple_of(b * blk8, blk8), blk8)], zsem)

        nused = pad_end_ref[N_EXPERTS - 1] // MOE_BLK
        nblk = buf_ref.shape[0] // blk8
        lax.fori_loop(nused, nblk, lambda b, c: (tail_copy(b).start(), c)[1], 0)
        lax.fori_loop(nused, nblk, lambda b, c: (tail_copy(b).wait(), c)[1], 0)

    def issue(c, carry):
        for u in range(DMA_UNROLL):
            r = c * DMA_UNROLL + u
            src = _row_tile(m_ref, r * SUBLANES)
            pltpu.make_async_copy(src, _row_tile(buf_ref, dest_ref[0, 0, r]), sem).start(priority=0)
            pltpu.make_async_copy(src, _row_tile(buf_ref, dest_ref[0, 0, tm + r]), sem).start(priority=1)
        return carry

    lax.fori_loop(0, tm // DMA_UNROLL, issue, 0)
    for _ in range(2):
        pltpu.make_async_copy(m_ref, buf_ref.at[pl.ds(0, tm * SUBLANES)], sem).wait()


def _dispatch(pad_end, padded, dest8_tiles, m2, P):
    tm = TM_ROWS
    N = m2.shape[0] // SUBLANES
    grid_spec = pltpu.PrefetchScalarGridSpec(
        num_scalar_prefetch=2,
        grid=(N // tm,),
        in_specs=[
            pl.BlockSpec((1, 1, 2 * tm), lambda i, pe, pd: (i, 0, 0), memory_space=pltpu.SMEM),
            pl.BlockSpec((tm * SUBLANES, LANES), lambda i, pe, pd: (i, 0)),
        ],
        out_specs=pl.BlockSpec(memory_space=pl.ANY),
        scratch_shapes=[
            pltpu.VMEM((MOE_BLK * SUBLANES, LANES), F32),
            pltpu.SemaphoreType.DMA(()),
            pltpu.SemaphoreType.DMA(()),
        ],
    )
    return pl.pallas_call(
        _dispatch_kernel,
        grid_spec=grid_spec,
        out_shape=jax.ShapeDtypeStruct((P * SUBLANES, LANES), F32),
        compiler_params=pltpu.CompilerParams(dimension_semantics=("arbitrary",)),
        name="dispatch",
    )(pad_end, padded, dest8_tiles, m2)


def _expert_kernel(blk_e_ref, nused_ref, x_hbm, wg_ref, wu_ref, wd_ref, y_ref,
                   wgu_sc, wd_sc, xbuf, xsem):
    i = pl.program_id(0)
    nused = nused_ref[0]
    live = i < nused
    new_expert = (i == 0) | (blk_e_ref[i] != blk_e_ref[jnp.maximum(i - 1, 0)])
    blk8 = MOE_BLK * SUBLANES
    nbuf = X_AHEAD + 1

    def fetch(b):
        slot = b % nbuf
        return pltpu.make_async_copy(
            x_hbm.at[pl.ds(pl.multiple_of(b * blk8, blk8), blk8)], xbuf.at[slot], xsem.at[slot])

    @pl.when(i == 0)
    def _():
        for b in range(X_AHEAD):
            @pl.when(b < nused)
            def _():
                fetch(b).start()

    @pl.when(i + X_AHEAD < nused)
    def _():
        fetch(i + X_AHEAD).start()

    @pl.when(live & new_expert)
    def _():
        wgu_sc[:, 0:D_EXPERT] = wg_ref[0].astype(BF16)
        wgu_sc[:, D_EXPERT:2 * D_EXPERT] = wu_ref[0].astype(BF16)
        wd_sc[...] = wd_ref[0].astype(BF16)

    @pl.when(live)
    def _():
        fetch(i).wait()
        xb = _load_rows(xbuf.at[i % nbuf], MOE_BLK).astype(BF16)
        h = _dot(xb, wgu_sc[...])
        hdn = (jax.nn.silu(h[:, 0:D_EXPERT]) * h[:, D_EXPERT:2 * D_EXPERT]).astype(BF16)
        _store_rows(y_ref, _dot(hdn, wd_sc[...]))

    @pl.when(i >= nused)
    def _():
        y_ref[...] = jnp.zeros(y_ref.shape, F32)


def _experts(blk_e, nused, buf2, wg, wu, wd):
    nblk = buf2.shape[0] // (MOE_BLK * SUBLANES)

    def used(i, nu):
        return jnp.minimum(i, nu[0] - 1)

    blk_rows = MOE_BLK * SUBLANES
    grid_spec = pltpu.PrefetchScalarGridSpec(
        num_scalar_prefetch=2,
        grid=(nblk,),
        in_specs=[
            pl.BlockSpec(memory_space=pl.ANY),
            pl.BlockSpec((1, D_MODEL, D_EXPERT), lambda i, be, nu: (be[used(i, nu)], 0, 0)),
            pl.BlockSpec((1, D_MODEL, D_EXPERT), lambda i, be, nu: (be[used(i, nu)], 0, 0)),
            pl.BlockSpec((1, D_EXPERT, D_MODEL), lambda i, be, nu: (be[used(i, nu)], 0, 0)),
        ],
        out_specs=pl.BlockSpec((blk_rows, LANES), lambda i, be, nu: (i, 0)),
        scratch_shapes=[
            pltpu.VMEM((D_MODEL, 2 * D_EXPERT), BF16),
            pltpu.VMEM((D_EXPERT, D_MODEL), BF16),
            pltpu.VMEM((X_AHEAD + 1, blk_rows, LANES), F32),
            pltpu.SemaphoreType.DMA((X_AHEAD + 1,)),
        ],
    )
    return pl.pallas_call(
        _expert_kernel,
        grid_spec=grid_spec,
        out_shape=jax.ShapeDtypeStruct(buf2.shape, F32),
        compiler_params=pltpu.CompilerParams(
            dimension_semantics=("arbitrary",), vmem_limit_bytes=VMEM_LIMIT),
        name="experts",
    )(blk_e, nused, buf2, wg, wu, wd)


def _final_kernel(dcur_ref, dnext_ref, h1_ref, rw_ref, p_ref, gple_ref, wpg_ref, wpp_ref, gfin_ref,
                  ys_ref, out_ref, ybuf_a, ybuf_b, sem):
    g = pl.program_id(0)
    ng = pl.num_programs(0)
    tm = TM_ROWS
    bufs = (ybuf_a, ybuf_b)

    def row_copy(d_ref, off, r, which):
        return pltpu.make_async_copy(_row_tile(ys_ref, d_ref[0, 0, off + r]),
                                     _row_tile(bufs[which], r * SUBLANES), sem.at[which])

    def issue(d_ref, off, which):
        for r in range(2 * tm):
            row_copy(d_ref, off, r, which).start(priority=r % 2)

    def wait(which):
        pltpu.make_async_copy(ys_ref.at[pl.ds(0, 2 * tm * SUBLANES)], bufs[which],
                              sem.at[which]).wait()

    def compute(which, rows):
        y0 = _load_rows(bufs[which], tm)
        y1 = _load_rows(bufs[which], tm, first_row=tm)
        rw = rw_ref[rows, :]
        h2 = h1_ref[rows, :] + (y0 * rw[:, 0:1] + y1 * rw[:, 1:2])
        n3 = (_rms(h2) * gple_ref[...]).astype(BF16)
        gate = jax.nn.sigmoid(_dot(n3, wpg_ref[...]))
        pp = _dot(p_ref[rows, :].astype(BF16), wpp_ref[...])
        h3 = h2 + gate * pp
        out_ref[rows, :] = _rms(h3) * gfin_ref[...]

    @pl.when(g == 0)
    def _():
        def body(c, carry):
            for u in range(DMA_UNROLL):
                row_copy(dcur_ref, 0, c * DMA_UNROLL + u, 0).start()
            return carry

        lax.fori_loop(0, 2 * tm // DMA_UNROLL, body, 0)

    wait(0)
    issue(dcur_ref, 2 * tm, 1)
    compute(0, slice(0, tm))
    wait(1)
    issue(dnext_ref, 0, 0)
    compute(1, slice(tm, 2 * tm))

    @pl.when(g == ng - 1)
    def _():
        wait(0)


def _final(dest8_pairs, h1, rw, p2, g_ple, wpg_b, wpp_b, g_final, ys2):
    N = h1.shape[0]
    tm = TM_ROWS
    ng = N // (2 * tm)
    row = lambda i: (i, 0)
    const = lambda i: (0, 0)
    return pl.pallas_call(
        _final_kernel,
        grid=(ng,),
        in_specs=[
            pl.BlockSpec((1, 1, 4 * tm), lambda i: (i, 0, 0), memory_space=pltpu.SMEM),
            pl.BlockSpec((1, 1, 4 * tm), lambda i: (jnp.minimum(i + 1, ng - 1), 0, 0),
                         memory_space=pltpu.SMEM),
            pl.BlockSpec((2 * tm, D_MODEL), row),
            pl.BlockSpec((2 * tm, LANES), row),
            pl.BlockSpec((2 * tm, PLE_DIM), row),
            pl.BlockSpec((1, D_MODEL), const),
            pl.BlockSpec((D_MODEL, D_MODEL), const),
            pl.BlockSpec((PLE_DIM, D_MODEL), const),
            pl.BlockSpec((1, D_MODEL), const),
            pl.BlockSpec(memory_space=pl.ANY),
        ],
        out_specs=pl.BlockSpec((2 * tm, D_MODEL), row),
        out_shape=jax.ShapeDtypeStruct((N, D_MODEL), F32),
        scratch_shapes=[
            pltpu.VMEM((2 * tm * SUBLANES, LANES), F32),
            pltpu.VMEM((2 * tm * SUBLANES, LANES), F32),
            pltpu.SemaphoreType.DMA((2,)),
        ],
        compiler_params=pltpu.CompilerParams(
            dimension_semantics=("arbitrary",), vmem_limit_bytes=VMEM_LIMIT),
        name="final",
    )(dest8_pairs, dest8_pairs, h1, rw, p2, g_ple, wpg_b, wpp_b, g_final, ys2)


def _rope_tables(S):
    inv_freq = ROPE_THETA ** (-jnp.arange(0, QK_ROPE_DIM, 2, dtype=F32) / QK_ROPE_DIM)
    ang = jnp.arange(S, dtype=F32)[:, None] * inv_freq[None, :]
    cos, sin = jnp.cos(ang), jnp.sin(ang)
    z = jnp.zeros_like(cos)
    rc = jnp.concatenate([cos, cos, z, z], axis=1)
    rs1 = jnp.concatenate([-sin, z, z, z], axis=1)
    rs2 = jnp.concatenate([z, sin, z, z], axis=1)
    return rc, rs1, rs2


def _layer(h, p_l, g_mix, w_in, g_gv, w_spatial, b_spatial, w_gproj, g_cq, w_uq, g_ckv, w_ukv,
           w_mla_o, w_out, g_moe, w_router_g, b_router_g, w_router_e, b_router_e,
           w_e_gate, w_e_up, w_e_down, g_ple, w_ple_gate, w_ple_proj, g_out):
    B, S, D = h.shape
    N = B * S
    x2 = h.reshape(N, D)

    cu, cv, ccq, cckv, ckr, cga = (GMLP_WIDTH, 2 * GMLP_WIDTH, 2 * GMLP_WIDTH + Q_LORA,
                                   2 * GMLP_WIDTH + Q_LORA + KV_LORA,
                                   2 * GMLP_WIDTH + Q_LORA + KV_LORA + QK_ROPE_DIM,
                                   2 * GMLP_WIDTH + Q_LORA + KV_LORA + QK_ROPE_DIM + D_MODEL)
    w_in_p = jnp.concatenate(
        [w_in[:, :cckv], w_in[:, ckr:], w_in[:, cckv:ckr],
         jnp.zeros((D, LANES - QK_ROPE_DIM), w_in.dtype)], axis=1).astype(BF16)
    w_uq_h = w_uq.reshape(Q_LORA, MLA_HEADS, QK_NOPE_DIM + QK_ROPE_DIM)
    w_uq_p = jnp.concatenate(
        [w_uq_h, jnp.zeros((Q_LORA, MLA_HEADS, QK_PAD - QK_NOPE_DIM - QK_ROPE_DIM), w_uq.dtype)],
        axis=2).reshape(Q_LORA, MLA_HEADS * QK_PAD).astype(BF16)
    rc, rs1, rs2 = _rope_tables(S)

    u, v, sga, sgb, q, k, vv = _inproj(
        x2, g_mix[None], w_in_p, g_gv[None], g_cq[None], g_ckv[None], w_uq_p, w_ukv.astype(BF16),
        rc, rs1, rs2, B, S)
    ma = _gmlp(u, v, sga, w_spatial.astype(BF16), b_spatial.T, w_gproj.astype(BF16))
    o = _attention(q, k, vv)

    wr = jnp.concatenate(
        [w_router_e, w_router_g, jnp.zeros((D, LANES - N_EXPERTS - N_GROUPS), w_router_e.dtype)],
        axis=1).astype(BF16)
    br = jnp.concatenate(
        [b_router_e.reshape(-1), b_router_g, jnp.zeros((LANES - N_EXPERTS - N_GROUPS,), F32)])[None]
    h1, m, rt, rw, cnt = _out_route(
        o.reshape(N, D), ma, sgb, x2, w_mla_o.astype(BF16), w_out.astype(BF16), g_moe[None], wr, br)

    counts = cnt[0, :N_EXPERTS]
    padded = (counts + MOE_BLK - 1) // MOE_BLK * MOE_BLK
    pad_end = jnp.cumsum(padded)
    pad_start = pad_end - padded
    P = 2 * N + N_EXPERTS * MOE_BLK
    nblk = P // MOE_BLK
    blk_start = jnp.arange(nblk, dtype=jnp.int32) * MOE_BLK
    blk_e = jnp.minimum(
        jnp.sum((pad_end[None, :] <= blk_start[:, None]).astype(jnp.int32), axis=1),
        N_EXPERTS - 1).astype(jnp.int32)
    nused = (pad_end[-1:] // MOE_BLK).astype(jnp.int32)
    experts = jnp.arange(N_EXPERTS, dtype=jnp.int32)[:, None, None]
    start_of = jnp.sum(jnp.where(rt[None, 0:2] == experts, pad_start[:, None, None], 0), axis=0)
    dest8 = (start_of + rt[2:4]).astype(jnp.int32) * SUBLANES
    nt = N // TM_ROWS
    dest8_tiles = dest8.reshape(2, nt, TM_ROWS).transpose(1, 0, 2).reshape(nt, 1, 2 * TM_ROWS)

    buf = _dispatch(pad_end.astype(jnp.int32), padded.astype(jnp.int32), dest8_tiles, m, P)
    ys = _experts(blk_e, nused, buf, w_e_gate, w_e_up, w_e_down)
    out = _final(dest8_tiles.reshape(nt // 2, 1, 4 * TM_ROWS), h1, rw, p_l.reshape(N, PLE_DIM), g_ple[None],
                 w_ple_gate.astype(BF16), w_ple_proj.astype(BF16), g_out[None], ys)
    return out.reshape(B, S, D)


def kernel(x, p, g_mix, w_in, g_gv, w_spatial, b_spatial, w_gproj, g_cq, w_uq, g_ckv, w_ukv, w_mla_o,
           w_out, g_moe, w_router_g, b_router_g, w_router_e, b_router_e, w_e_gate, w_e_up, w_e_down,
           g_ple, w_ple_gate, w_ple_proj, g_final):
    depth = p.shape[0]
    assert depth == 1, "the final rmsnorm is fused into the single layer's last kernel"
    i = 0
    return _layer(x, p[i], g_mix[i], w_in[i], g_gv[i], w_spatial[i], b_spatial[i], w_gproj[i], g_cq[i],
                  w_uq[i], g_ckv[i], w_ukv[i], w_mla_o[i], w_out[i], g_moe[i], w_router_g[i],
                  b_router_g[i], w_router_e[i], b_router_e[i], w_e_gate[i], w_e_up[i], w_e_down[i],
                  g_ple[i], w_ple_gate[i], w_ple_proj[i], g_final)
```

```python
import functools

import jax
import jax.numpy as jnp
from jax import lax
from jax.experimental import pallas as pl
from jax.experimental.pallas import tpu as pltpu

F32 = jnp.float32
BF16 = jnp.bfloat16

D_MODEL = 1024
CHUNK = 64
PLE_DIM = 256
GMLP_BLOCK = 128
GMLP_GROUPS = 12
GMLP_WIDTH = 1536
MLA_HEADS = 8
QK_NOPE_DIM = 128
QK_ROPE_DIM = 64
V_HEAD_DIM = 128
Q_LORA = 384
KV_LORA = 256
ROPE_THETA = 10000.0
N_GROUPS = 8
EXPERTS_PER_GROUP = 8
N_EXPERTS = 64
D_EXPERT = 256
EPS = 1e-6
LOG2E = 1.4426950408889634

LANES = 128
SUBLANES = 8
QK_PAD = 256
VMEM_LIMIT = 56 * 1024 * 1024

C_U = 0
C_V = C_U + GMLP_WIDTH
C_CQ = C_V + GMLP_WIDTH
C_CKV = C_CQ + Q_LORA
C_GA = C_CKV + KV_LORA
C_GB = C_GA + D_MODEL
C_KR = C_GB + D_MODEL
C_END = C_KR + LANES

TM_IN = 512
TM_GMLP = 512
TQ = 512
HEADS_PER_STEP = 4
TM_OUT = 512
TM_ROWS = 512
MOE_BLK = 256
X_AHEAD = 2
DMA_UNROLL = 8


def _rms(x):
    return x * lax.rsqrt(jnp.mean(x * x, axis=-1, keepdims=True) + EPS)


def _dot(a, b):
    return jnp.dot(a, b, preferred_element_type=F32)


def _store_rows(ref2, x):
    rows = x.shape[0]
    for j in range(SUBLANES):
        ref2[pl.ds(j, rows, stride=SUBLANES), :] = x[:, j * LANES:(j + 1) * LANES]


def _load_rows(ref2, rows, first_row=0):
    return jnp.concatenate(
        [ref2[pl.ds(first_row * SUBLANES + j, rows, stride=SUBLANES), :] for j in range(SUBLANES)],
        axis=1)


def _rope128(t, rc, rs1, rs2):
    r1 = pltpu.roll(t, 96, axis=1)
    r2 = pltpu.roll(t, 32, axis=1)
    return t * rc + r1 * rs1 + r2 * rs2


def _inproj_kernel(x_ref, gmix_ref, win_ref, ggv_ref, gcq_ref, gckv_ref, wuq_ref, wukv_ref,
                   rc_ref, rs1_ref, rs2_ref,
                   u_ref, v_ref, sga_ref, sgb_ref, q_ref, k_ref, vv_ref):
    x = x_ref[...]
    ab = (_rms(x) * gmix_ref[...]).astype(BF16)

    def proj(c0, c1):
        return _dot(ab, win_ref[:, c0:c1])

    u_ref[...] = jax.nn.gelu(proj(C_U, C_V)).astype(BF16)

    zv = jax.nn.gelu(proj(C_V, C_CQ))
    xc = zv - jnp.mean(zv, axis=-1, keepdims=True)
    vln = xc * lax.rsqrt(jnp.mean(xc * xc, axis=-1, keepdims=True) + EPS)
    v_ref[...] = (vln * ggv_ref[...]).astype(BF16)

    sga_ref[...] = jax.nn.sigmoid(proj(C_GA, C_GB)).astype(BF16)
    sgb_ref[...] = jax.nn.sigmoid(proj(C_GB, C_KR)).astype(BF16)

    rc = rc_ref[...]
    rs1 = rs1_ref[...]
    rs2 = rs2_ref[...]
    kpe = _rope128(proj(C_KR, C_END), rc, rs1, rs2).astype(BF16)

    cqn = (_rms(proj(C_CQ, C_CKV)) * gcq_ref[...]).astype(BF16)
    ckvn = (_rms(proj(C_CKV, C_GA)) * gckv_ref[...]).astype(BF16)
    scale = (QK_NOPE_DIM + QK_ROPE_DIM) ** -0.5 * LOG2E
    for h in range(MLA_HEADS):
        qh = _dot(cqn, wuq_ref[:, h * QK_PAD:(h + 1) * QK_PAD])
        q_ref[0, h, :, 0:LANES] = (qh[:, 0:LANES] * scale).astype(BF16)
        q_ref[0, h, :, LANES:QK_PAD] = (_rope128(qh[:, LANES:QK_PAD], rc, rs1, rs2) * scale).astype(BF16)
        kvh = _dot(ckvn, wukv_ref[:, h * 256:(h + 1) * 256])
        k_ref[0, h, :, 0:LANES] = kvh[:, 0:LANES].astype(BF16)
        k_ref[0, h, :, LANES:QK_PAD] = kpe
        vv_ref[0, h] = kvh[:, LANES:256].astype(BF16)


def _inproj(x2, g_mix, w_in_p, g_gv, g_cq, g_ckv, w_uq_p, w_ukv_b, rc, rs1, rs2, B, S):
    N = x2.shape[0]
    tm = TM_IN
    spt = S // tm
    row = lambda i: (i, 0)
    const = lambda i: (0, 0)
    pos = lambda i: (i % spt, 0)
    head = lambda i: (i // spt, 0, i % spt, 0)
    return pl.pallas_call(
        _inproj_kernel,
        grid=(N // tm,),
        in_specs=[
            pl.BlockSpec((tm, D_MODEL), row),
            pl.BlockSpec((1, D_MODEL), const),
            pl.BlockSpec((D_MODEL, C_END), const, pipeline_mode=pl.Buffered(1)),
            pl.BlockSpec((1, GMLP_WIDTH), const),
            pl.BlockSpec((1, Q_LORA), const),
            pl.BlockSpec((1, KV_LORA), const),
            pl.BlockSpec((Q_LORA, MLA_HEADS * QK_PAD), const),
            pl.BlockSpec((KV_LORA, MLA_HEADS * 256), const),
            pl.BlockSpec((tm, LANES), pos),
            pl.BlockSpec((tm, LANES), pos),
            pl.BlockSpec((tm, LANES), pos),
        ],
        out_specs=[
            pl.BlockSpec((tm, GMLP_WIDTH), row),
            pl.BlockSpec((tm, GMLP_WIDTH), row),
            pl.BlockSpec((tm, D_MODEL), row),
            pl.BlockSpec((tm, D_MODEL), row),
            pl.BlockSpec((1, MLA_HEADS, tm, QK_PAD), head),
            pl.BlockSpec((1, MLA_HEADS, tm, QK_PAD), head),
            pl.BlockSpec((1, MLA_HEADS, tm, V_HEAD_DIM), head),
        ],
        out_shape=[
            jax.ShapeDtypeStruct((N, GMLP_WIDTH), BF16),
            jax.ShapeDtypeStruct((N, GMLP_WIDTH), BF16),
            jax.ShapeDtypeStruct((N, D_MODEL), BF16),
            jax.ShapeDtypeStruct((N, D_MODEL), BF16),
            jax.ShapeDtypeStruct((B, MLA_HEADS, S, QK_PAD), BF16),
            jax.ShapeDtypeStruct((B, MLA_HEADS, S, QK_PAD), BF16),
            jax.ShapeDtypeStruct((B, MLA_HEADS, S, V_HEAD_DIM), BF16),
        ],
        compiler_params=pltpu.CompilerParams(
            dimension_semantics=("arbitrary",), vmem_limit_bytes=VMEM_LIMIT),
        name="inproj",
    )(x2, g_mix, w_in_p, g_gv, g_cq, g_ckv, w_uq_p, w_ukv_b, rc, rs1, rs2)


def _gmlp_kernel(u_ref, v_ref, sga_ref, wsp_ref, bsp_ref, wproj_ref, ma_ref, y_sc):
    nb = TM_GMLP // GMLP_BLOCK
    t_out = lax.broadcasted_iota(jnp.int32, (GMLP_BLOCK, GMLP_BLOCK), 0)
    s_in = lax.broadcasted_iota(jnp.int32, (GMLP_BLOCK, GMLP_BLOCK), 1)
    mask = (s_in // CHUNK) <= (t_out // CHUNK)
    for g in range(GMLP_GROUPS):
        c0 = g * LANES
        w = jnp.where(mask, wsp_ref[g], jnp.zeros((), BF16))
        rhs = jnp.concatenate(
            [v_ref[r * GMLP_BLOCK:(r + 1) * GMLP_BLOCK, c0:c0 + LANES] for r in range(nb)], axis=1)
        sv = _dot(w, rhs) + bsp_ref[:, g:g + 1]
        for r in range(nb):
            rows = slice(r * GMLP_BLOCK, (r + 1) * GMLP_BLOCK)
            ub = u_ref[rows, c0:c0 + LANES].astype(F32)
            y_sc[rows, c0:c0 + LANES] = (ub * sv[:, r * LANES:(r + 1) * LANES]).astype(BF16)
    ya = _dot(y_sc[...], wproj_ref[...])
    ma_ref[...] = (sga_ref[...].astype(F32) * ya).astype(BF16)


def _gmlp(u, v, sga, wsp_b, bsp_t, wproj_b):
    N = u.shape[0]
    tm = TM_GMLP
    row = lambda i: (i, 0)
    return pl.pallas_call(
        _gmlp_kernel,
        grid=(N // tm,),
        in_specs=[
            pl.BlockSpec((tm, GMLP_WIDTH), row),
            pl.BlockSpec((tm, GMLP_WIDTH), row),
            pl.BlockSpec((tm, D_MODEL), row),
            pl.BlockSpec((GMLP_GROUPS, GMLP_BLOCK, GMLP_BLOCK), lambda i: (0, 0, 0)),
            pl.BlockSpec((GMLP_BLOCK, GMLP_GROUPS), lambda i: (0, 0)),
            pl.BlockSpec((GMLP_WIDTH, D_MODEL), lambda i: (0, 0)),
        ],
        out_specs=pl.BlockSpec((tm, D_MODEL), row),
        out_shape=jax.ShapeDtypeStruct((N, D_MODEL), BF16),
        scratch_shapes=[pltpu.VMEM((tm, GMLP_WIDTH), BF16)],
        compiler_params=pltpu.CompilerParams(
            dimension_semantics=("arbitrary",), vmem_limit_bytes=VMEM_LIMIT),
        name="gmlp",
    )(u, v, sga, wsp_b, bsp_t, wproj_b)


NEG_BIG = -1e30


def _attn_kernel(q_ref, k_ref, v_ref, o_ref, m_sc, l_sc, acc_sc):
    qi = pl.program_id(2)
    m_sc[...] = jnp.full(m_sc.shape, NEG_BIG, F32)
    l_sc[...] = jnp.zeros(l_sc.shape, F32)
    acc_sc[...] = jnp.zeros(acc_sc.shape, F32)
    def chunk_mask(r0, nq, nk):
        qc = (lax.broadcasted_iota(jnp.int32, (nq, nk), 0) + r0) // CHUNK
        kc = lax.broadcasted_iota(jnp.int32, (nq, nk), 1) // CHUNK
        return kc <= qc

    def update(hh, r0, nq, start, nk, mask):
        rows = pl.ds(r0, nq)
        kb = k_ref[0, hh, pl.ds(start, nk), :]
        vb = v_ref[0, hh, pl.ds(start, nk), :]
        s = lax.dot_general(q_ref[0, hh, rows, :], kb, (((1,), (1,)), ((), ())),
                            preferred_element_type=F32)
        if mask is not None:
            s = jnp.where(mask, s, NEG_BIG)
        tiles = [s[:, c * LANES:(c + 1) * LANES] for c in range(nk // LANES)]
        tile_max = functools.reduce(jnp.maximum, tiles)
        m_prev = m_sc[hh, rows, :]
        m_new = jnp.maximum(m_prev, jnp.max(tile_max, axis=-1, keepdims=True))
        alpha = jnp.exp2(m_prev - m_new)
        ps = [jnp.exp2(t - m_new) for t in tiles]
        l_sc[hh, rows, :] = alpha * l_sc[hh, rows, :] + functools.reduce(jnp.add, ps)
        p = jnp.concatenate([t.astype(BF16) for t in ps], axis=1)
        acc_sc[hh, rows, :] = alpha * acc_sc[hh, rows, :] + _dot(p, vb)
        m_sc[hh, rows, :] = m_new

    def body(j, carry):
        start = pl.multiple_of(j * TQ, TQ)
        for hh in range(HEADS_PER_STEP):
            update(hh, 0, TQ, start, TQ, None)
        return carry

    lax.fori_loop(0, qi, body, 0)
    start = pl.multiple_of(qi * TQ, TQ)
    mask = chunk_mask(0, TQ, TQ)
    for hh in range(HEADS_PER_STEP):
        update(hh, 0, TQ, start, TQ, mask)
    for hh in range(HEADS_PER_STEP):
        l = jnp.sum(l_sc[hh], axis=-1, keepdims=True)
        o_ref[0, :, hh * V_HEAD_DIM:(hh + 1) * V_HEAD_DIM] = (acc_sc[hh] / l).astype(BF16)


def _attention(q, k, v):
    B, H, S, _ = q.shape
    hps = HEADS_PER_STEP
    return pl.pallas_call(
        _attn_kernel,
        grid=(B, H // hps, S // TQ),
        in_specs=[
            pl.BlockSpec((1, hps, TQ, QK_PAD), lambda b, h, i: (b, h, i, 0)),
            pl.BlockSpec((1, hps, S, QK_PAD), lambda b, h, i: (b, h, 0, 0)),
            pl.BlockSpec((1, hps, S, V_HEAD_DIM), lambda b, h, i: (b, h, 0, 0)),
        ],
        out_specs=pl.BlockSpec((1, TQ, hps * V_HEAD_DIM), lambda b, h, i: (b, i, h)),
        out_shape=jax.ShapeDtypeStruct((B, S, H * V_HEAD_DIM), BF16),
        scratch_shapes=[
            pltpu.VMEM((hps, TQ, LANES), F32),
            pltpu.VMEM((hps, TQ, LANES), F32),
            pltpu.VMEM((hps, TQ, V_HEAD_DIM), F32),
        ],
        compiler_params=pltpu.CompilerParams(
            dimension_semantics=("arbitrary", "arbitrary", "arbitrary"),
            vmem_limit_bytes=VMEM_LIMIT),
        name="attention",
    )(q, k, v)


def _col_sum(x):
    return jnp.sum(x, axis=0, keepdims=True)


def _col_max(x):
    return jnp.max(x, axis=0, keepdims=True)


def _first_row(hit, row_f):
    return jnp.min(jnp.where(hit, row_f, float(LANES)), axis=0, keepdims=True).astype(jnp.int32)


def _out_route_kernel(o_ref, ma_ref, sgb_ref, x_ref, wo_ref, wout_ref, gmoe_ref, wr_ref, br_ref,
                      h1_ref, m_ref, rt_ref, rw_ref, cnt_ref, carry_sc):
    i = pl.program_id(0)
    tm = TM_OUT

    @pl.when(i == 0)
    def _():
        carry_sc[...] = jnp.zeros(carry_sc.shape, F32)

    yb = _dot(o_ref[...], wo_ref[...])
    merged = ma_ref[...].astype(F32) + sgb_ref[...].astype(F32) * yb
    h1 = x_ref[...] + _dot(merged.astype(BF16), wout_ref[...])
    h1_ref[...] = h1
    m = _rms(h1) * gmoe_ref[...]
    _store_rows(m_ref, m)

    logits = lax.dot_general(wr_ref[...], m.astype(BF16), (((1,), (1,)), ((), ())),
                             preferred_element_type=F32)
    row = lax.broadcasted_iota(jnp.int32, (LANES, tm), 0)
    row_f = row.astype(F32)
    bias = br_ref[...]
    is_g = (row >= N_EXPERTS) & (row < N_EXPERTS + N_GROUPS)
    neg = jnp.float32(-jnp.inf)

    gl = jnp.where(is_g, logits, neg)
    ge = jnp.where(is_g, jnp.exp(gl - _col_max(gl)), 0.0)
    g_prob = ge / _col_sum(ge)
    g_score = jnp.where(is_g, g_prob + bias, neg)
    g_row = _first_row(g_score == _col_max(g_score), row_f)
    g_w = _col_sum(jnp.where(row == g_row, g_prob, 0.0))
    g_idx = g_row - N_EXPERTS

    in_g = (row // EXPERTS_PER_GROUP) == g_idx
    el = jnp.where(in_g, logits, neg)
    ee = jnp.where(in_g, jnp.exp(el - _col_max(el)), 0.0)
    e_prob = ee / _col_sum(ee)
    e_score = jnp.where(in_g, e_prob + bias, neg)
    id1 = _first_row(e_score == _col_max(e_score), row_f)
    e_score2 = jnp.where(row == id1, neg, e_score)
    id2 = _first_row(e_score2 == _col_max(e_score2), row_f)
    p1 = _col_sum(jnp.where(row == id1, e_prob, 0.0))
    p2 = _col_sum(jnp.where(row == id2, e_prob, 0.0))
    psum = p1 + p2
    w1 = g_w * (p1 / psum)
    w2 = g_w * (p2 / psum)

    oh = ((row == id1) | (row == id2 + N_EXPERTS)).astype(BF16)
    t_in = lax.broadcasted_iota(jnp.int32, (tm, tm), 0)
    t_out = lax.broadcasted_iota(jnp.int32, (tm, tm), 1)
    tri = (t_in < t_out).astype(BF16)
    prefix = _dot(oh, tri)
    tot = jnp.sum(oh.astype(F32), axis=1, keepdims=True)
    tot_sw = jnp.concatenate([tot[N_EXPERTS:], tot[:N_EXPERTS]], axis=0)
    row1 = lax.broadcasted_iota(jnp.int32, (LANES, 1), 0)
    carry = carry_sc[...]
    base = carry + jnp.where(row1 >= N_EXPERTS, tot_sw, 0.0)
    rk = oh.astype(F32) * (base + prefix)
    rank1 = _col_sum(jnp.where(row < N_EXPERTS, rk, 0.0))
    rank2 = _col_sum(jnp.where(row >= N_EXPERTS, rk, 0.0))
    carry_new = carry + tot + tot_sw
    carry_sc[...] = carry_new
    cnt_ref[...] = carry_new.astype(jnp.int32)

    row8 = lax.broadcasted_iota(jnp.int32, (SUBLANES, tm), 0)
    rt = jnp.where(row8 == 0, id1, jnp.where(row8 == 1, id2, 0))
    rt = jnp.where(row8 == 2, rank1.astype(jnp.int32), rt)
    rt_ref[...] = jnp.where(row8 == 3, rank2.astype(jnp.int32), rt)
    wt = jnp.where(row == 0, w1, jnp.where(row == 1, w2, 0.0))
    rw_ref[...] = wt.T


def _out_route(o2, ma, sgb, x2, wo_b, wout_b, g_moe, wr_b, br):
    N = x2.shape[0]
    tm = TM_OUT
    row = lambda i: (i, 0)
    const = lambda i: (0, 0)
    return pl.pallas_call(
        _out_route_kernel,
        grid=(N // tm,),
        in_specs=[
            pl.BlockSpec((tm, D_MODEL), row),
            pl.BlockSpec((tm, D_MODEL), row),
            pl.BlockSpec((tm, D_MODEL), row),
            pl.BlockSpec((tm, D_MODEL), row),
            pl.BlockSpec((D_MODEL, D_MODEL), const),
            pl.BlockSpec((D_MODEL, D_MODEL), const),
            pl.BlockSpec((1, D_MODEL), const),
            pl.BlockSpec((LANES, D_MODEL), const),
            pl.BlockSpec((LANES, tm), const),
        ],
        out_specs=[
            pl.BlockSpec((tm, D_MODEL), row),
            pl.BlockSpec((tm * SUBLANES, LANES), row),
            pl.BlockSpec((SUBLANES, tm), lambda i: (0, i)),
            pl.BlockSpec((tm, LANES), row),
            pl.BlockSpec((LANES, 1), const),
        ],
        out_shape=[
            jax.ShapeDtypeStruct((N, D_MODEL), F32),
            jax.ShapeDtypeStruct((N * SUBLANES, LANES), F32),
            jax.ShapeDtypeStruct((SUBLANES, N), jnp.int32),
            jax.ShapeDtypeStruct((N, LANES), F32),
            jax.ShapeDtypeStruct((LANES, 1), jnp.int32),
        ],
        scratch_shapes=[pltpu.VMEM((LANES, 1), F32)],
        compiler_params=pltpu.CompilerParams(
            dimension_semantics=("arbitrary",), vmem_limit_bytes=VMEM_LIMIT),
        name="out_route",
    )(o2, ma, sgb, x2, wo_b, wout_b, g_moe, wr_b, br)


def _row_tile(ref2, row8):
    return ref2.at[pl.ds(pl.multiple_of(row8, SUBLANES), SUBLANES)]


def _dispatch_kernel(pad_end_ref, padded_ref, dest_ref, m_ref, buf_ref, zero_sc, sem, zsem):
    i = pl.program_id(0)
    tm = TM_ROWS
    blk8 = MOE_BLK * SUBLANES

    @pl.when(i == 0)
    def _():
        zero_sc[...] = jnp.zeros(zero_sc.shape, F32)

        def zero_copy(e):
            start = pl.multiple_of((pad_end_ref[e] - MOE_BLK) * SUBLANES, blk8)
            return pltpu.make_async_copy(zero_sc, buf_ref.at[pl.ds(start, blk8)], zsem)

        def start(e, carry):
            @pl.when(padded_ref[e] > 0)
            def _():
                zero_copy(e).start()
            return carry

        def wait(e, carry):
            @pl.when(padded_ref[e] > 0)
            def _():
                zero_copy(e).wait()
            return carry

        lax.fori_loop(0, N_EXPERTS, start, 0)
        lax.fori_loop(0, N_EXPERTS, wait, 0)

        def tail_copy(b):
            return pltpu.make_async_copy(
                zero_sc, buf_ref.at[pl.ds(pl.multiple_of(b * blk8, blk8), blk8)], zsem)

        nused = pad_end_ref[N_EXPERTS - 1] // MOE_BLK
        nblk = buf_ref.shape[0] // blk8
        lax.fori_loop(nused, nblk, lambda b, c: (tail_copy(b).start(), c)[1], 0)
        lax.fori_loop(nused, nblk, lambda b, c: (tail_copy(b).wait(), c)[1], 0)

    def issue(c, carry):
        for u in range(DMA_UNROLL):
            r = c * DMA_UNROLL + u
            src = _row_tile(m_ref, r * SUBLANES)
            pltpu.make_async_copy(src, _row_tile(buf_ref, dest_ref[0, 0, r]), sem).start(priority=0)
            pltpu.make_async_copy(src, _row_tile(buf_ref, dest_ref[0, 0, tm + r]), sem).start(priority=1)
        return carry

    lax.fori_loop(0, tm // DMA_UNROLL, issue, 0)
    for _ in range(2):
        pltpu.make_async_copy(m_ref, buf_ref.at[pl.ds(0, tm * SUBLANES)], sem).wait()


def _dispatch(pad_end, padded, dest8_tiles, m2, P):
    tm = TM_ROWS
    N = m2.shape[0] // SUBLANES
    grid_spec = pltpu.PrefetchScalarGridSpec(
        num_scalar_prefetch=2,
        grid=(N // tm,),
        in_specs=[
            pl.BlockSpec((1, 1, 2 * tm), lambda i, pe, pd: (i, 0, 0), memory_space=pltpu.SMEM),
            pl.BlockSpec((tm * SUBLANES, LANES), lambda i, pe, pd: (i, 0)),
        ],
        out_specs=pl.BlockSpec(memory_space=pl.ANY),
        scratch_shapes=[
            pltpu.VMEM((MOE_BLK * SUBLANES, LANES), F32),
            pltpu.SemaphoreType.DMA(()),
            pltpu.SemaphoreType.DMA(()),
        ],
    )
    return pl.pallas_call(
        _dispatch_kernel,
        grid_spec=grid_spec,
        out_shape=jax.ShapeDtypeStruct((P * SUBLANES, LANES), F32),
        compiler_params=pltpu.CompilerParams(dimension_semantics=("arbitrary",)),
        name="dispatch",
    )(pad_end, padded, dest8_tiles, m2)


def _expert_kernel(blk_e_ref, nused_ref, x_hbm, wg_ref, wu_ref, wd_ref, y_ref,
                   wgu_sc, wd_sc, xbuf, xsem):
    i = pl.program_id(0)
    nused = nused_ref[0]
    live = i < nused
    new_expert = (i == 0) | (blk_e_ref[i] != blk_e_ref[jnp.maximum(i - 1, 0)])
    blk8 = MOE_BLK * SUBLANES
    nbuf = X_AHEAD + 1

    def fetch(b):
        slot = b % nbuf
        return pltpu.make_async_copy(
            x_hbm.at[pl.ds(pl.multiple_of(b * blk8, blk8), blk8)], xbuf.at[slot], xsem.at[slot])

    @pl.when(i == 0)
    def _():
        for b in range(X_AHEAD):
            @pl.when(b < nused)
            def _():
                fetch(b).start()

    @pl.when(i + X_AHEAD < nused)
    def _():
        fetch(i + X_AHEAD).start()

    @pl.when(live & new_expert)
    def _():
        wgu_sc[:, 0:D_EXPERT] = wg_ref[0].astype(BF16)
        wgu_sc[:, D_EXPERT:2 * D_EXPERT] = wu_ref[0].astype(BF16)
        wd_sc[...] = wd_ref[0].astype(BF16)

    @pl.when(live)
    def _():
        fetch(i).wait()
        xb = _load_rows(xbuf.at[i % nbuf], MOE_BLK).astype(BF16)
        h = _dot(xb, wgu_sc[...])
        hdn = (jax.nn.silu(h[:, 0:D_EXPERT]) * h[:, D_EXPERT:2 * D_EXPERT]).astype(BF16)
        _store_rows(y_ref, _dot(hdn, wd_sc[...]))

    @pl.when(i >= nused)
    def _():
        y_ref[...] = jnp.zeros(y_ref.shape, F32)


def _experts(blk_e, nused, buf2, wg, wu, wd):
    nblk = buf2.shape[0] // (MOE_BLK * SUBLANES)

    def used(i, nu):
        return jnp.minimum(i, nu[0] - 1)

    blk_rows = MOE_BLK * SUBLANES
    grid_spec = pltpu.PrefetchScalarGridSpec(
        num_scalar_prefetch=2,
        grid=(nblk,),
        in_specs=[
            pl.BlockSpec(memory_space=pl.ANY),
            pl.BlockSpec((1, D_MODEL, D_EXPERT), lambda i, be, nu: (be[used(i, nu)], 0, 0)),
            pl.BlockSpec((1, D_MODEL, D_EXPERT), lambda i, be, nu: (be[used(i, nu)], 0, 0)),
            pl.BlockSpec((1, D_EXPERT, D_MODEL), lambda i, be, nu: (be[used(i, nu)], 0, 0)),
        ],
        out_specs=pl.BlockSpec((blk_rows, LANES), lambda i, be, nu: (i, 0)),
        scratch_shapes=[
            pltpu.VMEM((D_MODEL, 2 * D_EXPERT), BF16),
            pltpu.VMEM((D_EXPERT, D_MODEL), BF16),
            pltpu.VMEM((X_AHEAD + 1, blk_rows, LANES), F32),
            pltpu.SemaphoreType.DMA((X_AHEAD + 1,)),
        ],
    )
    return pl.pallas_call(
        _expert_kernel,
        grid_spec=grid_spec,
        out_shape=jax.ShapeDtypeStruct(buf2.shape, F32),
        compiler_params=pltpu.CompilerParams(
            dimension_semantics=("arbitrary",), vmem_limit_bytes=VMEM_LIMIT),
        name="experts",
    )(blk_e, nused, buf2, wg, wu, wd)


def _final_kernel(dcur_ref, dnext_ref, h1_ref, rw_ref, p_ref, gple_ref, wpg_ref, wpp_ref, gfin_ref,
                  ys_ref, out_ref, ybuf_a, ybuf_b, sem):
    g = pl.program_id(0)
    ng = pl.num_programs(0)
    tm = TM_ROWS
    bufs = (ybuf_a, ybuf_b)

    def row_copy(d_ref, off, r, which):
        return pltpu.make_async_copy(_row_tile(ys_ref, d_ref[0, 0, off + r]),
                                     _row_tile(bufs[which], r * SUBLANES), sem.at[which])

    def issue(d_ref, off, which):
        for r in range(2 * tm):
            row_copy(d_ref, off, r, which).start(priority=r % 2)

    def wait(which):
        pltpu.make_async_copy(ys_ref.at[pl.ds(0, 2 * tm * SUBLANES)], bufs[which],
                              sem.at[which]).wait()

    def compute(which, rows):
        y0 = _load_rows(bufs[which], tm)
        y1 = _load_rows(bufs[which], tm, first_row=tm)
        rw = rw_ref[rows, :]
        h2 = h1_ref[rows, :] + (y0 * rw[:, 0:1] + y1 * rw[:, 1:2])
        n3 = (_rms(h2) * gple_ref[...]).astype(BF16)
        gate = jax.nn.sigmoid(_dot(n3, wpg_ref[...]))
        pp = _dot(p_ref[rows, :].astype(BF16), wpp_ref[...])
        h3 = h2 + gate * pp
        out_ref[rows, :] = _rms(h3) * gfin_ref[...]

    @pl.when(g == 0)
    def _():
        def body(c, carry):
            for u in range(DMA_UNROLL):
                row_copy(dcur_ref, 0, c * DMA_UNROLL + u, 0).start()
            return carry

        lax.fori_loop(0, 2 * tm // DMA_UNROLL, body, 0)

    wait(0)
    issue(dcur_ref, 2 * tm, 1)
    compute(0, slice(0, tm))
    wait(1)
    issue(dnext_ref, 0, 0)
    compute(1, slice(tm, 2 * tm))

    @pl.when(g == ng - 1)
    def _():
        wait(0)


def _final(dest8_pairs, h1, rw, p2, g_ple, wpg_b, wpp_b, g_final, ys2):
    N = h1.shape[0]
    tm = TM_ROWS
    ng = N // (2 * tm)
    row = lambda i: (i, 0)
    const = lambda i: (0, 0)
    return pl.pallas_call(
        _final_kernel,
        grid=(ng,),
        in_specs=[
            pl.BlockSpec((1, 1, 4 * tm), lambda i: (i, 0, 0), memory_space=pltpu.SMEM),
            pl.BlockSpec((1, 1, 4 * tm), lambda i: (jnp.minimum(i + 1, ng - 1), 0, 0),
                         memory_space=pltpu.SMEM),
            pl.BlockSpec((2 * tm, D_MODEL), row),
            pl.BlockSpec((2 * tm, LANES), row),
            pl.BlockSpec((2 * tm, PLE_DIM), row),
            pl.BlockSpec((1, D_MODEL), const),
            pl.BlockSpec((D_MODEL, D_MODEL), const),
            pl.BlockSpec((PLE_DIM, D_MODEL), const),
            pl.BlockSpec((1, D_MODEL), const),
            pl.BlockSpec(memory_space=pl.ANY),
        ],
        out_specs=pl.BlockSpec((2 * tm, D_MODEL), row),
        out_shape=jax.ShapeDtypeStruct((N, D_MODEL), F32),
        scratch_shapes=[
            pltpu.VMEM((2 * tm * SUBLANES, LANES), F32),
            pltpu.VMEM((2 * tm * SUBLANES, LANES), F32),
            pltpu.SemaphoreType.DMA((2,)),
        ],
        compiler_params=pltpu.CompilerParams(
            dimension_semantics=("arbitrary",), vmem_limit_bytes=VMEM_LIMIT),
        name="final",
    )(dest8_pairs, dest8_pairs, h1, rw, p2, g_ple, wpg_b, wpp_b, g_final, ys2)


def _rope_tables(S):
    inv_freq = ROPE_THETA ** (-jnp.arange(0, QK_ROPE_DIM, 2, dtype=F32) / QK_ROPE_DIM)
    ang = jnp.arange(S, dtype=F32)[:, None] * inv_freq[None, :]
    cos, sin = jnp.cos(ang), jnp.sin(ang)
    z = jnp.zeros_like(cos)
    rc = jnp.concatenate([cos, cos, z, z], axis=1)
    rs1 = jnp.concatenate([-sin, z, z, z], axis=1)
    rs2 = jnp.concatenate([z, sin, z, z], axis=1)
    return rc, rs1, rs2


def _layer(h, p_l, g_mix, w_in, g_gv, w_spatial, b_spatial, w_gproj, g_cq, w_uq, g_ckv, w_ukv,
           w_mla_o, w_out, g_moe, w_router_g, b_router_g, w_router_e, b_router_e,
           w_e_gate, w_e_up, w_e_down, g_ple, w_ple_gate, w_ple_proj, g_out):
    B, S, D = h.shape
    N = B * S
    x2 = h.reshape(N, D)

    cu, cv, ccq, cckv, ckr, cga = (GMLP_WIDTH, 2 * GMLP_WIDTH, 2 * GMLP_WIDTH + Q_LORA,
                                   2 * GMLP_WIDTH + Q_LORA + KV_LORA,
                                   2 * GMLP_WIDTH + Q_LORA + KV_LORA + QK_ROPE_DIM,
                                   2 * GMLP_WIDTH + Q_LORA + KV_LORA + QK_ROPE_DIM + D_MODEL)
    w_in_p = jnp.concatenate(
        [w_in[:, :cckv], w_in[:, ckr:], w_in[:, cckv:ckr],
         jnp.zeros((D, LANES - QK_ROPE_DIM), w_in.dtype)], axis=1).astype(BF16)
    w_uq_h = w_uq.reshape(Q_LORA, MLA_HEADS, QK_NOPE_DIM + QK_ROPE_DIM)
    w_uq_p = jnp.concatenate(
        [w_uq_h, jnp.zeros((Q_LORA, MLA_HEADS, QK_PAD - QK_NOPE_DIM - QK_ROPE_DIM), w_uq.dtype)],
        axis=2).reshape(Q_LORA, MLA_HEADS * QK_PAD).astype(BF16)
    rc, rs1, rs2 = _rope_tables(S)

    u, v, sga, sgb, q, k, vv = _inproj(
        x2, g_mix[None], w_in_p, g_gv[None], g_cq[None], g_ckv[None], w_uq_p, w_ukv.astype(BF16),
        rc, rs1, rs2, B, S)
    ma = _gmlp(u, v, sga, w_spatial.astype(BF16), b_spatial.T, w_gproj.astype(BF16))
    o = _attention(q, k, vv)

    wr = jnp.concatenate(
        [w_router_e.T, w_router_g.T, jnp.zeros((LANES - N_EXPERTS - N_GROUPS, D), w_router_e.dtype)],
        axis=0).astype(BF16)
    br = jnp.concatenate(
        [b_router_e.reshape(-1), b_router_g, jnp.zeros((LANES - N_EXPERTS - N_GROUPS,), F32)])
    br = jnp.broadcast_to(br[:, None], (LANES, TM_OUT))
    h1, m, rt, rw, cnt = _out_route(
        o.reshape(N, D), ma, sgb, x2, w_mla_o.astype(BF16), w_out.astype(BF16), g_moe[None], wr, br)

    counts = cnt[:N_EXPERTS, 0]
    padded = (counts + MOE_BLK - 1) // MOE_BLK * MOE_BLK
    pad_end = jnp.cumsum(padded)
    pad_start = pad_end - padded
    P = 2 * N + N_EXPERTS * MOE_BLK
    nblk = P // MOE_BLK
    blk_start = jnp.arange(nblk, dtype=jnp.int32) * MOE_BLK
    blk_e = jnp.minimum(
        jnp.sum((pad_end[None, :] <= blk_start[:, None]).astype(jnp.int32), axis=1),
        N_EXPERTS - 1).astype(jnp.int32)
    nused = (pad_end[-1:] // MOE_BLK).astype(jnp.int32)
    experts = jnp.arange(N_EXPERTS, dtype=jnp.int32)[:, None, None]
    start_of = jnp.sum(jnp.where(rt[None, 0:2] == experts, pad_start[:, None, None], 0), axis=0)
    dest8 = (start_of + rt[2:4]).astype(jnp.int32) * SUBLANES
    nt = N // TM_ROWS
    dest8_tiles = dest8.reshape(2, nt, TM_ROWS).transpose(1, 0, 2).reshape(nt, 1, 2 * TM_ROWS)

    buf = _dispatch(pad_end.astype(jnp.int32), padded.astype(jnp.int32), dest8_tiles, m, P)
    ys = _experts(blk_e, nused, buf, w_e_gate, w_e_up, w_e_down)
    out = _final(dest8_tiles.reshape(nt // 2, 1, 4 * TM_ROWS), h1, rw, p_l.reshape(N, PLE_DIM), g_ple[None],
                 w_ple_gate.astype(BF16), w_ple_proj.astype(BF16), g_out[None], ys)
    return out.reshape(B, S, D)


def kernel(x, p, g_mix, w_in, g_gv, w_spatial, b_spatial, w_gproj, g_cq, w_uq, g_ckv, w_ukv, w_mla_o,
           w_out, g_moe, w_router_g, b_router_g, w_router_e, b_router_e, w_e_gate, w_e_up, w_e_down,
           g_ple, w_ple_gate, w_ple_proj, g_final):
    depth = p.shape[0]
    assert depth == 1, "the final rmsnorm is fused into the single layer's last kernel"
    i = 0
    return _layer(x, p[i], g_mix[i], w_in[i], g_gv[i], w_spatial[i], b_spatial[i], w_gproj[i], g_cq[i],
                  w_uq[i], g_ckv[i], w_ukv[i], w_mla_o[i], w_out[i], g_moe[i], w_router_g[i],
                  b_router_g[i], w_router_e[i], b_router_e[i], w_e_gate[i], w_e_up[i], w_e_down[i],
                  g_ple[i], w_ple_gate[i], w_ple_proj[i], g_final)
```

```python
import functools

import jax
import jax.numpy as jnp
from jax import lax
from jax.experimental import pallas as pl
from jax.experimental.pallas import tpu as pltpu

F32 = jnp.float32
BF16 = jnp.bfloat16

D_MODEL = 1024
CHUNK = 64
PLE_DIM = 256
GMLP_BLOCK = 128
GMLP_GROUPS = 12
GMLP_WIDTH = 1536
MLA_HEADS = 8
QK_NOPE_DIM = 128
QK_ROPE_DIM = 64
V_HEAD_DIM = 128
Q_LORA = 384
KV_LORA = 256
ROPE_THETA = 10000.0
N_GROUPS = 8
EXPERTS_PER_GROUP = 8
N_EXPERTS = 64
D_EXPERT = 256
EPS = 1e-6
LOG2E = 1.4426950408889634

LANES = 128
SUBLANES = 8
QK_PAD = 256
V_PAD = 256
VMEM_LIMIT = 56 * 1024 * 1024

C_U = 0
C_V = C_U + GMLP_WIDTH
C_CQ = C_V + GMLP_WIDTH
C_CKV = C_CQ + Q_LORA
C_GA = C_CKV + KV_LORA
C_GB = C_GA + D_MODEL
C_KR = C_GB + D_MODEL
C_END = C_KR + LANES

TM_IN = 512
TM_GMLP = 512
TQ = 512
HEADS_PER_STEP = 4
TM_OUT = 512
TM_ROWS = 512
MOE_BLK = 256
X_AHEAD = 2
DMA_UNROLL = 8


def _rms(x):
    return x * lax.rsqrt(jnp.mean(x * x, axis=-1, keepdims=True) + EPS)


def _dot(a, b):
    return jnp.dot(a, b, preferred_element_type=F32)


def _store_rows(ref2, x):
    rows = x.shape[0]
    for j in range(SUBLANES):
        ref2[pl.ds(j, rows, stride=SUBLANES), :] = x[:, j * LANES:(j + 1) * LANES]


def _load_rows(ref2, rows, first_row=0):
    return jnp.concatenate(
        [ref2[pl.ds(first_row * SUBLANES + j, rows, stride=SUBLANES), :] for j in range(SUBLANES)],
        axis=1)


def _rope128(t, rc, rs1, rs2):
    r1 = pltpu.roll(t, 96, axis=1)
    r2 = pltpu.roll(t, 32, axis=1)
    return t * rc + r1 * rs1 + r2 * rs2


def _inproj_kernel(x_ref, gmix_ref, win_ref, ggv_ref, gcq_ref, gckv_ref, wuq_ref, wukv_ref,
                   rc_ref, rs1_ref, rs2_ref,
                   u_ref, v_ref, sga_ref, sgb_ref, q_ref, k_ref, vv_ref):
    x = x_ref[...]
    ab = (_rms(x) * gmix_ref[...]).astype(BF16)

    def proj(c0, c1):
        return _dot(ab, win_ref[:, c0:c1])

    u_ref[...] = jax.nn.gelu(proj(C_U, C_V)).astype(BF16)

    zv = jax.nn.gelu(proj(C_V, C_CQ))
    xc = zv - jnp.mean(zv, axis=-1, keepdims=True)
    vln = xc * lax.rsqrt(jnp.mean(xc * xc, axis=-1, keepdims=True) + EPS)
    v_ref[...] = (vln * ggv_ref[...]).astype(BF16)

    sga_ref[...] = jax.nn.sigmoid(proj(C_GA, C_GB)).astype(BF16)
    sgb_ref[...] = jax.nn.sigmoid(proj(C_GB, C_KR)).astype(BF16)

    rc = rc_ref[...]
    rs1 = rs1_ref[...]
    rs2 = rs2_ref[...]
    kpe = _rope128(proj(C_KR, C_END), rc, rs1, rs2).astype(BF16)

    cqn = (_rms(proj(C_CQ, C_CKV)) * gcq_ref[...]).astype(BF16)
    ckvn = (_rms(proj(C_CKV, C_GA)) * gckv_ref[...]).astype(BF16)
    scale = (QK_NOPE_DIM + QK_ROPE_DIM) ** -0.5 * LOG2E
    ones_col = (lax.broadcasted_iota(jnp.int32, (x.shape[0], V_PAD - LANES), 1) == 0).astype(BF16)
    for h in range(MLA_HEADS):
        qh = _dot(cqn, wuq_ref[:, h * QK_PAD:(h + 1) * QK_PAD])
        q_ref[0, h, :, 0:LANES] = (qh[:, 0:LANES] * scale).astype(BF16)
        q_ref[0, h, :, LANES:QK_PAD] = (_rope128(qh[:, LANES:QK_PAD], rc, rs1, rs2) * scale).astype(BF16)
        kvh = _dot(ckvn, wukv_ref[:, h * 256:(h + 1) * 256])
        k_ref[0, h, :, 0:LANES] = kvh[:, 0:LANES].astype(BF16)
        k_ref[0, h, :, LANES:QK_PAD] = kpe
        vv_ref[0, h, :, 0:LANES] = kvh[:, LANES:256].astype(BF16)
        vv_ref[0, h, :, LANES:V_PAD] = ones_col


def _inproj(x2, g_mix, w_in_p, g_gv, g_cq, g_ckv, w_uq_p, w_ukv_b, rc, rs1, rs2, B, S):
    N = x2.shape[0]
    tm = TM_IN
    spt = S // tm
    row = lambda i: (i, 0)
    const = lambda i: (0, 0)
    pos = lambda i: (i % spt, 0)
    head = lambda i: (i // spt, 0, i % spt, 0)
    return pl.pallas_call(
        _inproj_kernel,
        grid=(N // tm,),
        in_specs=[
            pl.BlockSpec((tm, D_MODEL), row),
            pl.BlockSpec((1, D_MODEL), const),
            pl.BlockSpec((D_MODEL, C_END), const, pipeline_mode=pl.Buffered(1)),
            pl.BlockSpec((1, GMLP_WIDTH), const),
            pl.BlockSpec((1, Q_LORA), const),
            pl.BlockSpec((1, KV_LORA), const),
            pl.BlockSpec((Q_LORA, MLA_HEADS * QK_PAD), const),
            pl.BlockSpec((KV_LORA, MLA_HEADS * 256), const),
            pl.BlockSpec((tm, LANES), pos),
            pl.BlockSpec((tm, LANES), pos),
            pl.BlockSpec((tm, LANES), pos),
        ],
        out_specs=[
            pl.BlockSpec((tm, GMLP_WIDTH), row),
            pl.BlockSpec((tm, GMLP_WIDTH), row),
            pl.BlockSpec((tm, D_MODEL), row),
            pl.BlockSpec((tm, D_MODEL), row),
            pl.BlockSpec((1, MLA_HEADS, tm, QK_PAD), head),
            pl.BlockSpec((1, MLA_HEADS, tm, QK_PAD), head),
            pl.BlockSpec((1, MLA_HEADS, tm, V_PAD), head),
        ],
        out_shape=[
            jax.ShapeDtypeStruct((N, GMLP_WIDTH), BF16),
            jax.ShapeDtypeStruct((N, GMLP_WIDTH), BF16),
            jax.ShapeDtypeStruct((N, D_MODEL), BF16),
            jax.ShapeDtypeStruct((N, D_MODEL), BF16),
            jax.ShapeDtypeStruct((B, MLA_HEADS, S, QK_PAD), BF16),
            jax.ShapeDtypeStruct((B, MLA_HEADS, S, QK_PAD), BF16),
            jax.ShapeDtypeStruct((B, MLA_HEADS, S, V_PAD), BF16),
        ],
        compiler_params=pltpu.CompilerParams(
            dimension_semantics=("arbitrary",), vmem_limit_bytes=VMEM_LIMIT),
        name="inproj",
    )(x2, g_mix, w_in_p, g_gv, g_cq, g_ckv, w_uq_p, w_ukv_b, rc, rs1, rs2)


def _gmlp_kernel(u_ref, v_ref, sga_ref, wsp_ref, bsp_ref, wproj_ref, ma_ref, y_sc):
    nb = TM_GMLP // GMLP_BLOCK
    t_out = lax.broadcasted_iota(jnp.int32, (GMLP_BLOCK, GMLP_BLOCK), 0)
    s_in = lax.broadcasted_iota(jnp.int32, (GMLP_BLOCK, GMLP_BLOCK), 1)
    mask = (s_in // CHUNK) <= (t_out // CHUNK)
    for g in range(GMLP_GROUPS):
        c0 = g * LANES
        w = jnp.where(mask, wsp_ref[g], jnp.zeros((), BF16))
        rhs = jnp.concatenate(
            [v_ref[r * GMLP_BLOCK:(r + 1) * GMLP_BLOCK, c0:c0 + LANES] for r in range(nb)], axis=1)
        sv = _dot(w, rhs) + bsp_ref[:, g:g + 1]
        for r in range(nb):
            rows = slice(r * GMLP_BLOCK, (r + 1) * GMLP_BLOCK)
            ub = u_ref[rows, c0:c0 + LANES].astype(F32)
            y_sc[rows, c0:c0 + LANES] = (ub * sv[:, r * LANES:(r + 1) * LANES]).astype(BF16)
    ya = _dot(y_sc[...], wproj_ref[...])
    ma_ref[...] = (sga_ref[...].astype(F32) * ya).astype(BF16)


def _gmlp(u, v, sga, wsp_b, bsp_t, wproj_b):
    N = u.shape[0]
    tm = TM_GMLP
    row = lambda i: (i, 0)
    return pl.pallas_call(
        _gmlp_kernel,
        grid=(N // tm,),
        in_specs=[
            pl.BlockSpec((tm, GMLP_WIDTH), row),
            pl.BlockSpec((tm, GMLP_WIDTH), row),
            pl.BlockSpec((tm, D_MODEL), row),
            pl.BlockSpec((GMLP_GROUPS, GMLP_BLOCK, GMLP_BLOCK), lambda i: (0, 0, 0)),
            pl.BlockSpec((GMLP_BLOCK, GMLP_GROUPS), lambda i: (0, 0)),
            pl.BlockSpec((GMLP_WIDTH, D_MODEL), lambda i: (0, 0)),
        ],
        out_specs=pl.BlockSpec((tm, D_MODEL), row),
        out_shape=jax.ShapeDtypeStruct((N, D_MODEL), BF16),
        scratch_shapes=[pltpu.VMEM((tm, GMLP_WIDTH), BF16)],
        compiler_params=pltpu.CompilerParams(
            dimension_semantics=("arbitrary",), vmem_limit_bytes=VMEM_LIMIT),
        name="gmlp",
    )(u, v, sga, wsp_b, bsp_t, wproj_b)


NEG_BIG = -1e30


def _attn_kernel(q_ref, k_ref, v_ref, o_ref, m_sc, acc_sc):
    qi = pl.program_id(2)
    m_sc[...] = jnp.full(m_sc.shape, NEG_BIG, F32)
    acc_sc[...] = jnp.zeros(acc_sc.shape, F32)

    def chunk_mask(r0, nq, nk):
        qc = (lax.broadcasted_iota(jnp.int32, (nq, nk), 0) + r0) // CHUNK
        kc = lax.broadcasted_iota(jnp.int32, (nq, nk), 1) // CHUNK
        return kc <= qc

    def update(hh, r0, nq, start, nk, mask):
        rows = pl.ds(r0, nq)
        kb = k_ref[0, hh, pl.ds(start, nk), :]
        vb = v_ref[0, hh, pl.ds(start, nk), :]
        s = lax.dot_general(q_ref[0, hh, rows, :], kb, (((1,), (1,)), ((), ())),
                            preferred_element_type=F32)
        if mask is not None:
            s = jnp.where(mask, s, NEG_BIG)
        tiles = [s[:, c * LANES:(c + 1) * LANES] for c in range(nk // LANES)]
        tile_max = functools.reduce(jnp.maximum, tiles)
        m_prev = m_sc[hh, rows, :]
        m_new = jnp.maximum(m_prev, jnp.max(tile_max, axis=-1, keepdims=True))
        alpha = jnp.exp2(m_prev - m_new)
        p = jnp.concatenate([jnp.exp2(t - m_new).astype(BF16) for t in tiles], axis=1)
        alpha2 = jnp.concatenate([alpha] * (V_PAD // LANES), axis=1)
        acc_sc[hh, rows, :] = alpha2 * acc_sc[hh, rows, :] + _dot(p, vb)
        m_sc[hh, rows, :] = m_new

    def body(j, carry):
        start = pl.multiple_of(j * TQ, TQ)
        for hh in range(HEADS_PER_STEP):
            update(hh, 0, TQ, start, TQ, None)
        return carry

    lax.fori_loop(0, qi, body, 0)
    start = pl.multiple_of(qi * TQ, TQ)
    mask = chunk_mask(0, TQ, TQ)
    for hh in range(HEADS_PER_STEP):
        update(hh, 0, TQ, start, TQ, mask)
    for hh in range(HEADS_PER_STEP):
        l = acc_sc[hh, :, V_HEAD_DIM:V_HEAD_DIM + 1]
        o_ref[0, :, hh * V_HEAD_DIM:(hh + 1) * V_HEAD_DIM] = (
            acc_sc[hh, :, 0:V_HEAD_DIM] / l).astype(BF16)


def _attention(q, k, v):
    B, H, S, _ = q.shape
    hps = HEADS_PER_STEP
    return pl.pallas_call(
        _attn_kernel,
        grid=(B, H // hps, S // TQ),
        in_specs=[
            pl.BlockSpec((1, hps, TQ, QK_PAD), lambda b, h, i: (b, h, i, 0)),
            pl.BlockSpec((1, hps, S, QK_PAD), lambda b, h, i: (b, h, 0, 0)),
            pl.BlockSpec((1, hps, S, V_PAD), lambda b, h, i: (b, h, 0, 0)),
        ],
        out_specs=pl.BlockSpec((1, TQ, hps * V_HEAD_DIM), lambda b, h, i: (b, i, h)),
        out_shape=jax.ShapeDtypeStruct((B, S, H * V_HEAD_DIM), BF16),
        scratch_shapes=[
            pltpu.VMEM((hps, TQ, LANES), F32),
            pltpu.VMEM((hps, TQ, V_PAD), F32),
        ],
        compiler_params=pltpu.CompilerParams(
            dimension_semantics=("arbitrary", "arbitrary", "arbitrary"),
            vmem_limit_bytes=VMEM_LIMIT),
        name="attention",
    )(q, k, v)


def _col_sum(x):
    return jnp.sum(x, axis=0, keepdims=True)


def _col_max(x):
    return jnp.max(x, axis=0, keepdims=True)


def _first_row(hit, row_f):
    return jnp.min(jnp.where(hit, row_f, float(LANES)), axis=0, keepdims=True).astype(jnp.int32)


def _out_route_kernel(o_ref, ma_ref, sgb_ref, x_ref, wo_ref, wout_ref, gmoe_ref, wr_ref, br_ref,
                      h1_ref, m_ref, rt_ref, rw_ref, cnt_ref, carry_sc):
    i = pl.program_id(0)
    tm = TM_OUT

    @pl.when(i == 0)
    def _():
        carry_sc[...] = jnp.zeros(carry_sc.shape, F32)

    yb = _dot(o_ref[...], wo_ref[...])
    merged = ma_ref[...].astype(F32) + sgb_ref[...].astype(F32) * yb
    h1 = x_ref[...] + _dot(merged.astype(BF16), wout_ref[...])
    h1_ref[...] = h1
    m = _rms(h1) * gmoe_ref[...]
    _store_rows(m_ref, m)

    logits = lax.dot_general(wr_ref[...], m.astype(BF16), (((1,), (1,)), ((), ())),
                             preferred_element_type=F32)
    row = lax.broadcasted_iota(jnp.int32, (LANES, tm), 0)
    row_f = row.astype(F32)
    bias = br_ref[...]
    is_g = (row >= N_EXPERTS) & (row < N_EXPERTS + N_GROUPS)
    neg = jnp.float32(-jnp.inf)

    gl = jnp.where(is_g, logits, neg)
    ge = jnp.where(is_g, jnp.exp(gl - _col_max(gl)), 0.0)
    g_prob = ge / _col_sum(ge)
    g_score = jnp.where(is_g, g_prob + bias, neg)
    g_row = _first_row(g_score == _col_max(g_score), row_f)
    g_w = _col_sum(jnp.where(row == g_row, g_prob, 0.0))
    g_idx = g_row - N_EXPERTS

    in_g = (row // EXPERTS_PER_GROUP) == g_idx
    el = jnp.where(in_g, logits, neg)
    ee = jnp.where(in_g, jnp.exp(el - _col_max(el)), 0.0)
    e_prob = ee / _col_sum(ee)
    e_score = jnp.where(in_g, e_prob + bias, neg)
    id1 = _first_row(e_score == _col_max(e_score), row_f)
    e_score2 = jnp.where(row == id1, neg, e_score)
    id2 = _first_row(e_score2 == _col_max(e_score2), row_f)
    p1 = _col_sum(jnp.where(row == id1, e_prob, 0.0))
    p2 = _col_sum(jnp.where(row == id2, e_prob, 0.0))
    psum = p1 + p2
    w1 = g_w * (p1 / psum)
    w2 = g_w * (p2 / psum)

    oh = ((row == id1) | (row == id2 + N_EXPERTS)).astype(BF16)
    t_in = lax.broadcasted_iota(jnp.int32, (tm, tm), 0)
    t_out = lax.broadcasted_iota(jnp.int32, (tm, tm), 1)
    tri = (t_in < t_out).astype(BF16)
    prefix = _dot(oh, tri)
    tot = jnp.sum(oh.astype(F32), axis=1, keepdims=True)
    tot_sw = jnp.concatenate([tot[N_EXPERTS:], tot[:N_EXPERTS]], axis=0)
    row1 = lax.broadcasted_iota(jnp.int32, (LANES, 1), 0)
    carry = carry_sc[...]
    base = carry + jnp.where(row1 >= N_EXPERTS, tot_sw, 0.0)
    rk = oh.astype(F32) * (base + prefix)
    rank1 = _col_sum(jnp.where(row < N_EXPERTS, rk, 0.0))
    rank2 = _col_sum(jnp.where(row >= N_EXPERTS, rk, 0.0))
    carry_new = carry + tot + tot_sw
    carry_sc[...] = carry_new
    cnt_ref[...] = carry_new.astype(jnp.int32)

    row8 = lax.broadcasted_iota(jnp.int32, (SUBLANES, tm), 0)
    rt = jnp.where(row8 == 0, id1, jnp.where(row8 == 1, id2, 0))
    rt = jnp.where(row8 == 2, rank1.astype(jnp.int32), rt)
    rt_ref[...] = jnp.where(row8 == 3, rank2.astype(jnp.int32), rt)
    wt = jnp.where(row == 0, w1, jnp.where(row == 1, w2, 0.0))
    rw_ref[...] = wt.T


def _out_route(o2, ma, sgb, x2, wo_b, wout_b, g_moe, wr_b, br):
    N = x2.shape[0]
    tm = TM_OUT
    row = lambda i: (i, 0)
    const = lambda i: (0, 0)
    return pl.pallas_call(
        _out_route_kernel,
        grid=(N // tm,),
        in_specs=[
            pl.BlockSpec((tm, D_MODEL), row),
            pl.BlockSpec((tm, D_MODEL), row),
            pl.BlockSpec((tm, D_MODEL), row),
            pl.BlockSpec((tm, D_MODEL), row),
            pl.BlockSpec((D_MODEL, D_MODEL), const),
            pl.BlockSpec((D_MODEL, D_MODEL), const),
            pl.BlockSpec((1, D_MODEL), const),
            pl.BlockSpec((LANES, D_MODEL), const),
            pl.BlockSpec((LANES, tm), const),
        ],
        out_specs=[
            pl.BlockSpec((tm, D_MODEL), row),
            pl.BlockSpec((tm * SUBLANES, LANES), row),
            pl.BlockSpec((SUBLANES, tm), lambda i: (0, i)),
            pl.BlockSpec((tm, LANES), row),
            pl.BlockSpec((LANES, 1), const),
        ],
        out_shape=[
            jax.ShapeDtypeStruct((N, D_MODEL), F32),
            jax.ShapeDtypeStruct((N * SUBLANES, LANES), F32),
            jax.ShapeDtypeStruct((SUBLANES, N), jnp.int32),
            jax.ShapeDtypeStruct((N, LANES), F32),
            jax.ShapeDtypeStruct((LANES, 1), jnp.int32),
        ],
        scratch_shapes=[pltpu.VMEM((LANES, 1), F32)],
        compiler_params=pltpu.CompilerParams(
            dimension_semantics=("arbitrary",), vmem_limit_bytes=VMEM_LIMIT),
        name="out_route",
    )(o2, ma, sgb, x2, wo_b, wout_b, g_moe, wr_b, br)


def _row_tile(ref2, row8):
    return ref2.at[pl.ds(pl.multiple_of(row8, SUBLANES), SUBLANES)]


def _dispatch_kernel(pad_end_ref, padded_ref, dest_ref, m_ref, buf_ref, zero_sc, sem, zsem):
    i = pl.program_id(0)
    tm = TM_ROWS
    blk8 = MOE_BLK * SUBLANES

    @pl.when(i == 0)
    def _():
        zero_sc[...] = jnp.zeros(zero_sc.shape, F32)

        def zero_copy(e):
            start = pl.multiple_of((pad_end_ref[e] - MOE_BLK) * SUBLANES, blk8)
            return pltpu.make_async_copy(zero_sc, buf_ref.at[pl.ds(start, blk8)], zsem)

        def start(e, carry):
            @pl.when(padded_ref[e] > 0)
            def _():
                zero_copy(e).start()
            return carry

        def wait(e, carry):
            @pl.when(padded_ref[e] > 0)
            def _():
                zero_copy(e).wait()
            return carry

        lax.fori_loop(0, N_EXPERTS, start, 0)
        lax.fori_loop(0, N_EXPERTS, wait, 0)

        def tail_copy(b):
            return pltpu.make_async_copy(
                zero_sc, buf_ref.at[pl.ds(pl.multiple_of(b * blk8, blk8), blk8)], zsem)

        nused = pad_end_ref[N_EXPERTS - 1] // MOE_BLK
        nblk = buf_ref.shape[0] // blk8
        lax.fori_loop(nused, nblk, lambda b, c: (tail_copy(b).start(), c)[1], 0)
        lax.fori_loop(nused, nblk, lambda b, c: (tail_copy(b).wait(), c)[1], 0)

    def issue(c, carry):
        for u in range(DMA_UNROLL):
            r = c * DMA_UNROLL + u
            src = _row_tile(m_ref, r * SUBLANES)
            pltpu.make_async_copy(src, _row_tile(buf_ref, dest_ref[0, 0, r]), sem).start(priority=0)
            pltpu.make_async_copy(src, _row_tile(buf_ref, dest_ref[0, 0, tm + r]), sem).start(priority=1)
        return carry

    lax.fori_loop(0, tm // DMA_UNROLL, issue, 0)
    for _ in range(2):
        pltpu.make_async_copy(m_ref, buf_ref.at[pl.ds(0, tm * SUBLANES)], sem).wait()


def _dispatch(pad_end, padded, dest8_tiles, m2, P):
    tm = TM_ROWS
    N = m2.shape[0] // SUBLANES
    grid_spec = pltpu.PrefetchScalarGridSpec(
        num_scalar_prefetch=2,
        grid=(N // tm,),
        in_specs=[
            pl.BlockSpec((1, 1, 2 * tm), lambda i, pe, pd: (i, 0, 0), memory_space=pltpu.SMEM),
            pl.BlockSpec((tm * SUBLANES, LANES), lambda i, pe, pd: (i, 0)),
        ],
        out_specs=pl.BlockSpec(memory_space=pl.ANY),
        scratch_shapes=[
            pltpu.VMEM((MOE_BLK * SUBLANES, LANES), F32),
            pltpu.SemaphoreType.DMA(()),
            pltpu.SemaphoreType.DMA(()),
        ],
    )
    return pl.pallas_call(
        _dispatch_kernel,
        grid_spec=grid_spec,
        out_shape=jax.ShapeDtypeStruct((P * SUBLANES, LANES), F32),
        compiler_params=pltpu.CompilerParams(dimension_semantics=("arbitrary",)),
        name="dispatch",
    )(pad_end, padded, dest8_tiles, m2)


def _expert_kernel(blk_e_ref, run_blocks_ref, nused_ref, x_hbm, wg_hbm, wu_hbm, wd_hbm, y_hbm,
                   xbuf, ybuf, wg_raw, wu_raw, wd_raw, wgu_sc, wd_sc, xsem, ysem, wsem):
    nused = nused_ref[0]
    blk8 = MOE_BLK * SUBLANES
    nblk = y_hbm.shape[0] // blk8
    nxbuf = X_AHEAD + 1

    def block_rows(ref, b):
        return ref.at[pl.ds(pl.multiple_of(b * blk8, blk8), blk8)]

    def x_copy(b):
        slot = b % nxbuf
        return pltpu.make_async_copy(block_rows(x_hbm, b), xbuf.at[slot], xsem.at[slot])

    def y_copy(b, slot):
        return pltpu.make_async_copy(ybuf.at[slot], block_rows(y_hbm, b), ysem.at[slot])

    def w_copies(e, slot):
        return (pltpu.make_async_copy(wg_hbm.at[e], wg_raw.at[slot], wsem.at[slot]),
                pltpu.make_async_copy(wu_hbm.at[e], wu_raw.at[slot], wsem.at[slot]),
                pltpu.make_async_copy(wd_hbm.at[e], wd_raw.at[slot], wsem.at[slot]))

    for b in range(X_AHEAD):
        @pl.when(b < nused)
        def _():
            x_copy(b).start()
    for c in w_copies(blk_e_ref[0], 0):
        c.start()

    def body(i, run):
        e = blk_e_ref[i]
        new_expert = (i == 0) | (e != blk_e_ref[jnp.maximum(i - 1, 0)])
        run = run + new_expert.astype(jnp.int32)
        wslot = run % 2

        @pl.when(new_expert)
        def _():
            for c in w_copies(e, wslot):
                c.wait()
            wgu_sc[:, 0:D_EXPERT] = wg_raw[wslot].astype(BF16)
            wgu_sc[:, D_EXPERT:2 * D_EXPERT] = wu_raw[wslot].astype(BF16)
            wd_sc[...] = wd_raw[wslot].astype(BF16)
            nxt = i + run_blocks_ref[e]

            @pl.when(nxt < nused)
            def _():
                for c in w_copies(blk_e_ref[nxt], 1 - wslot):
                    c.start()

        @pl.when(i + X_AHEAD < nused)
        def _():
            x_copy(i + X_AHEAD).start()

        yslot = i % 2
        x_copy(i).wait()

        @pl.when(i >= 2)
        def _():
            y_copy(i - 2, yslot).wait()

        xb = _load_rows(xbuf.at[i % nxbuf], MOE_BLK).astype(BF16)
        h = _dot(xb, wgu_sc[...])
        hdn = (jax.nn.silu(h[:, 0:D_EXPERT]) * h[:, D_EXPERT:2 * D_EXPERT]).astype(BF16)
        _store_rows(ybuf.at[yslot], _dot(hdn, wd_sc[...]))
        y_copy(i, yslot).start()
        return run

    lax.fori_loop(0, nused, body, jnp.int32(-1))

    @pl.when(nused >= 2)
    def _():
        y_copy(nused - 2, nused % 2).wait()

    y_copy(nused - 1, (nused - 1) % 2).wait()

    ybuf[0] = jnp.zeros(ybuf.shape[1:], F32)
    lax.fori_loop(nused, nblk, lambda b, c: (y_copy(b, 0).start(), c)[1], 0)
    lax.fori_loop(nused, nblk, lambda b, c: (y_copy(b, 0).wait(), c)[1], 0)


def _experts(blk_e, run_blocks, nused, buf2, wg, wu, wd):
    blk_rows = MOE_BLK * SUBLANES
    any_spec = pl.BlockSpec(memory_space=pl.ANY)
    grid_spec = pltpu.PrefetchScalarGridSpec(
        num_scalar_prefetch=3,
        grid=(1,),
        in_specs=[any_spec, any_spec, any_spec, any_spec],
        out_specs=any_spec,
        scratch_shapes=[
            pltpu.VMEM((X_AHEAD + 1, blk_rows, LANES), F32),
            pltpu.VMEM((2, blk_rows, LANES), F32),
            pltpu.VMEM((2, D_MODEL, D_EXPERT), F32),
            pltpu.VMEM((2, D_MODEL, D_EXPERT), F32),
            pltpu.VMEM((2, D_EXPERT, D_MODEL), F32),
            pltpu.VMEM((D_MODEL, 2 * D_EXPERT), BF16),
            pltpu.VMEM((D_EXPERT, D_MODEL), BF16),
            pltpu.SemaphoreType.DMA((X_AHEAD + 1,)),
            pltpu.SemaphoreType.DMA((2,)),
            pltpu.SemaphoreType.DMA((2,)),
        ],
    )
    return pl.pallas_call(
        _expert_kernel,
        grid_spec=grid_spec,
        out_shape=jax.ShapeDtypeStruct(buf2.shape, F32),
        compiler_params=pltpu.CompilerParams(
            dimension_semantics=("arbitrary",), vmem_limit_bytes=VMEM_LIMIT),
        name="experts",
    )(blk_e, run_blocks, nused, buf2, wg, wu, wd)


def _final_kernel(dcur_ref, dnext_ref, h1_ref, rw_ref, p_ref, gple_ref, wpg_ref, wpp_ref, gfin_ref,
                  ys_ref, out_ref, ybuf_a, ybuf_b, sem):
    g = pl.program_id(0)
    ng = pl.num_programs(0)
    tm = TM_ROWS
    bufs = (ybuf_a, ybuf_b)

    def row_copy(d_ref, off, r, which):
        return pltpu.make_async_copy(_row_tile(ys_ref, d_ref[0, 0, off + r]),
                                     _row_tile(bufs[which], r * SUBLANES), sem.at[which])

    def issue(d_ref, off, which):
        for r in range(2 * tm):
            row_copy(d_ref, off, r, which).start(priority=r % 2)

    def wait(which):
        pltpu.make_async_copy(ys_ref.at[pl.ds(0, 2 * tm * SUBLANES)], bufs[which],
                              sem.at[which]).wait()

    def compute(which, rows):
        y0 = _load_rows(bufs[which], tm)
        y1 = _load_rows(bufs[which], tm, first_row=tm)
        rw = rw_ref[rows, :]
        h2 = h1_ref[rows, :] + (y0 * rw[:, 0:1] + y1 * rw[:, 1:2])
        n3 = (_rms(h2) * gple_ref[...]).astype(BF16)
        gate = jax.nn.sigmoid(_dot(n3, wpg_ref[...]))
        pp = _dot(p_ref[rows, :].astype(BF16), wpp_ref[...])
        h3 = h2 + gate * pp
        out_ref[rows, :] = _rms(h3) * gfin_ref[...]

    @pl.when(g == 0)
    def _():
        def body(c, carry):
            for u in range(DMA_UNROLL):
                row_copy(dcur_ref, 0, c * DMA_UNROLL + u, 0).start()
            return carry

        lax.fori_loop(0, 2 * tm // DMA_UNROLL, body, 0)

    wait(0)
    issue(dcur_ref, 2 * tm, 1)
    compute(0, slice(0, tm))
    wait(1)
    issue(dnext_ref, 0, 0)
    compute(1, slice(tm, 2 * tm))

    @pl.when(g == ng - 1)
    def _():
        wait(0)


def _final(dest8_pairs, h1, rw, p2, g_ple, wpg_b, wpp_b, g_final, ys2):
    N = h1.shape[0]
    tm = TM_ROWS
    ng = N // (2 * tm)
    row = lambda i: (i, 0)
    const = lambda i: (0, 0)
    return pl.pallas_call(
        _final_kernel,
        grid=(ng,),
        in_specs=[
            pl.BlockSpec((1, 1, 4 * tm), lambda i: (i, 0, 0), memory_space=pltpu.SMEM),
            pl.BlockSpec((1, 1, 4 * tm), lambda i: (jnp.minimum(i + 1, ng - 1), 0, 0),
                         memory_space=pltpu.SMEM),
            pl.BlockSpec((2 * tm, D_MODEL), row),
            pl.BlockSpec((2 * tm, LANES), row),
            pl.BlockSpec((2 * tm, PLE_DIM), row),
            pl.BlockSpec((1, D_MODEL), const),
            pl.BlockSpec((D_MODEL, D_MODEL), const),
            pl.BlockSpec((PLE_DIM, D_MODEL), const),
            pl.BlockSpec((1, D_MODEL), const),
            pl.BlockSpec(memory_space=pl.ANY),
        ],
        out_specs=pl.BlockSpec((2 * tm, D_MODEL), row),
        out_shape=jax.ShapeDtypeStruct((N, D_MODEL), F32),
        scratch_shapes=[
            pltpu.VMEM((2 * tm * SUBLANES, LANES), F32),
            pltpu.VMEM((2 * tm * SUBLANES, LANES), F32),
            pltpu.SemaphoreType.DMA((2,)),
        ],
        compiler_params=pltpu.CompilerParams(
            dimension_semantics=("arbitrary",), vmem_limit_bytes=VMEM_LIMIT),
        name="final",
    )(dest8_pairs, dest8_pairs, h1, rw, p2, g_ple, wpg_b, wpp_b, g_final, ys2)


def _rope_tables(S):
    inv_freq = ROPE_THETA ** (-jnp.arange(0, QK_ROPE_DIM, 2, dtype=F32) / QK_ROPE_DIM)
    ang = jnp.arange(S, dtype=F32)[:, None] * inv_freq[None, :]
    cos, sin = jnp.cos(ang), jnp.sin(ang)
    z = jnp.zeros_like(cos)
    rc = jnp.concatenate([cos, cos, z, z], axis=1)
    rs1 = jnp.concatenate([-sin, z, z, z], axis=1)
    rs2 = jnp.concatenate([z, sin, z, z], axis=1)
    return rc, rs1, rs2


def _layer(h, p_l, g_mix, w_in, g_gv, w_spatial, b_spatial, w_gproj, g_cq, w_uq, g_ckv, w_ukv,
           w_mla_o, w_out, g_moe, w_router_g, b_router_g, w_router_e, b_router_e,
           w_e_gate, w_e_up, w_e_down, g_ple, w_ple_gate, w_ple_proj, g_out):
    B, S, D = h.shape
    N = B * S
    x2 = h.reshape(N, D)

    cu, cv, ccq, cckv, ckr, cga = (GMLP_WIDTH, 2 * GMLP_WIDTH, 2 * GMLP_WIDTH + Q_LORA,
                                   2 * GMLP_WIDTH + Q_LORA + KV_LORA,
                                   2 * GMLP_WIDTH + Q_LORA + KV_LORA + QK_ROPE_DIM,
                                   2 * GMLP_WIDTH + Q_LORA + KV_LORA + QK_ROPE_DIM + D_MODEL)
    w_in_p = jnp.concatenate(
        [w_in[:, :cckv], w_in[:, ckr:], w_in[:, cckv:ckr],
         jnp.zeros((D, LANES - QK_ROPE_DIM), w_in.dtype)], axis=1).astype(BF16)
    w_uq_h = w_uq.reshape(Q_LORA, MLA_HEADS, QK_NOPE_DIM + QK_ROPE_DIM)
    w_uq_p = jnp.concatenate(
        [w_uq_h, jnp.zeros((Q_LORA, MLA_HEADS, QK_PAD - QK_NOPE_DIM - QK_ROPE_DIM), w_uq.dtype)],
        axis=2).reshape(Q_LORA, MLA_HEADS * QK_PAD).astype(BF16)
    rc, rs1, rs2 = _rope_tables(S)

    u, v, sga, sgb, q, k, vv = _inproj(
        x2, g_mix[None], w_in_p, g_gv[None], g_cq[None], g_ckv[None], w_uq_p, w_ukv.astype(BF16),
        rc, rs1, rs2, B, S)
    ma = _gmlp(u, v, sga, w_spatial.astype(BF16), b_spatial.T, w_gproj.astype(BF16))
    o = _attention(q, k, vv)

    wr = jnp.concatenate(
        [w_router_e.T, w_router_g.T, jnp.zeros((LANES - N_EXPERTS - N_GROUPS, D), w_router_e.dtype)],
        axis=0).astype(BF16)
    br = jnp.concatenate(
        [b_router_e.reshape(-1), b_router_g, jnp.zeros((LANES - N_EXPERTS - N_GROUPS,), F32)])
    br = jnp.broadcast_to(br[:, None], (LANES, TM_OUT))
    h1, m, rt, rw, cnt = _out_route(
        o.reshape(N, D), ma, sgb, x2, w_mla_o.astype(BF16), w_out.astype(BF16), g_moe[None], wr, br)

    counts = cnt[:N_EXPERTS, 0]
    padded = (counts + MOE_BLK - 1) // MOE_BLK * MOE_BLK
    pad_end = jnp.cumsum(padded)
    pad_start = pad_end - padded
    P = 2 * N + N_EXPERTS * MOE_BLK
    nblk = P // MOE_BLK
    blk_start = jnp.arange(nblk, dtype=jnp.int32) * MOE_BLK
    blk_e = jnp.minimum(
        jnp.sum((pad_end[None, :] <= blk_start[:, None]).astype(jnp.int32), axis=1),
        N_EXPERTS - 1).astype(jnp.int32)
    nused = (pad_end[-1:] // MOE_BLK).astype(jnp.int32)
    experts = jnp.arange(N_EXPERTS, dtype=jnp.int32)[:, None, None]
    start_of = jnp.sum(jnp.where(rt[None, 0:2] == experts, pad_start[:, None, None], 0), axis=0)
    dest8 = (start_of + rt[2:4]).astype(jnp.int32) * SUBLANES
    nt = N // TM_ROWS
    dest8_tiles = dest8.reshape(2, nt, TM_ROWS).transpose(1, 0, 2).reshape(nt, 1, 2 * TM_ROWS)

    buf = _dispatch(pad_end.astype(jnp.int32), padded.astype(jnp.int32), dest8_tiles, m, P)
    ys = _experts(blk_e, (padded // MOE_BLK).astype(jnp.int32), nused, buf, w_e_gate, w_e_up, w_e_down)
    out = _final(dest8_tiles.reshape(nt // 2, 1, 4 * TM_ROWS), h1, rw, p_l.reshape(N, PLE_DIM), g_ple[None],
                 w_ple_gate.astype(BF16), w_ple_proj.astype(BF16), g_out[None], ys)
    return out.reshape(B, S, D)


def kernel(x, p, g_mix, w_in, g_gv, w_spatial, b_spatial, w_gproj, g_cq, w_uq, g_ckv, w_ukv, w_mla_o,
           w_out, g_moe, w_router_g, b_router_g, w_router_e, b_router_e, w_e_gate, w_e_up, w_e_down,
           g_ple, w_ple_gate, w_ple_proj, g_final):
    depth = p.shape[0]
    assert depth == 1, "the final rmsnorm is fused into the single layer's last kernel"
    i = 0
    return _layer(x, p[i], g_mix[i], w_in[i], g_gv[i], w_spatial[i], b_spatial[i], w_gproj[i], g_cq[i],
                  w_uq[i], g_ckv[i], w_ukv[i], w_mla_o[i], w_out[i], g_moe[i], w_router_g[i],
                  b_router_g[i], w_router_e[i], b_router_e[i], w_e_gate[i], w_e_up[i], w_e_down[i],
                  g_ple[i], w_ple_gate[i], w_ple_proj[i], g_final)
```

```python
import functools

import jax
import jax.numpy as jnp
from jax import lax
from jax.experimental import pallas as pl
from jax.experimental.pallas import tpu as pltpu

F32 = jnp.float32
BF16 = jnp.bfloat16

D_MODEL = 1024
CHUNK = 64
PLE_DIM = 256
GMLP_BLOCK = 128
GMLP_GROUPS = 12
GMLP_WIDTH = 1536
MLA_HEADS = 8
QK_NOPE_DIM = 128
QK_ROPE_DIM = 64
V_HEAD_DIM = 128
Q_LORA = 384
KV_LORA = 256
ROPE_THETA = 10000.0
N_GROUPS = 8
EXPERTS_PER_GROUP = 8
N_EXPERTS = 64
D_EXPERT = 256
EPS = 1e-6
LOG2E = 1.4426950408889634

LANES = 128
SUBLANES = 8
QK_PAD = 256
V_PAD = 256
VMEM_LIMIT = 56 * 1024 * 1024

C_U = 0
C_V = C_U + GMLP_WIDTH
C_CQ = C_V + GMLP_WIDTH
C_CKV = C_CQ + Q_LORA
C_GA = C_CKV + KV_LORA
C_GB = C_GA + D_MODEL
C_KR = C_GB + D_MODEL
C_END = C_KR + LANES

TM_IN = 512
TM_GMLP = 512
TQ = 512
HEADS_PER_STEP = 4
TM_OUT = 512
TM_ROWS = 512
MOE_BLK = 256
X_AHEAD = 2
DMA_UNROLL = 8


def _rms(x):
    return x * lax.rsqrt(jnp.mean(x * x, axis=-1, keepdims=True) + EPS)


def _dot(a, b):
    return jnp.dot(a, b, preferred_element_type=F32)


def _store_rows(ref2, x):
    rows = x.shape[0]
    for j in range(SUBLANES):
        ref2[pl.ds(j, rows, stride=SUBLANES), :] = x[:, j * LANES:(j + 1) * LANES]


def _load_rows(ref2, rows, first_row=0):
    return jnp.concatenate(
        [ref2[pl.ds(first_row * SUBLANES + j, rows, stride=SUBLANES), :] for j in range(SUBLANES)],
        axis=1)


def _rope128(t, rc, rs1, rs2):
    r1 = pltpu.roll(t, 96, axis=1)
    r2 = pltpu.roll(t, 32, axis=1)
    return t * rc + r1 * rs1 + r2 * rs2


def _inproj_kernel(x_ref, gmix_ref, win_ref, ggv_ref, gcq_ref, gckv_ref, wuq_ref, wukv_ref,
                   rc_ref, rs1_ref, rs2_ref,
                   u_ref, v_ref, sga_ref, sgb_ref, q_ref, k_ref, vv_ref):
    x = x_ref[...]
    ab = (_rms(x) * gmix_ref[...]).astype(BF16)

    def proj(c0, c1):
        return _dot(ab, win_ref[:, c0:c1])

    u_ref[...] = jax.nn.gelu(proj(C_U, C_V)).astype(BF16)

    zv = jax.nn.gelu(proj(C_V, C_CQ))
    xc = zv - jnp.mean(zv, axis=-1, keepdims=True)
    vln = xc * lax.rsqrt(jnp.mean(xc * xc, axis=-1, keepdims=True) + EPS)
    v_ref[...] = (vln * ggv_ref[...]).astype(BF16)

    sga_ref[...] = jax.nn.sigmoid(proj(C_GA, C_GB)).astype(BF16)
    sgb_ref[...] = jax.nn.sigmoid(proj(C_GB, C_KR)).astype(BF16)

    rc = rc_ref[...]
    rs1 = rs1_ref[...]
    rs2 = rs2_ref[...]
    kpe = _rope128(proj(C_KR, C_END), rc, rs1, rs2).astype(BF16)

    cqn = (_rms(proj(C_CQ, C_CKV)) * gcq_ref[...]).astype(BF16)
    ckvn = (_rms(proj(C_CKV, C_GA)) * gckv_ref[...]).astype(BF16)
    scale = (QK_NOPE_DIM + QK_ROPE_DIM) ** -0.5 * LOG2E
    ones_col = (lax.broadcasted_iota(jnp.int32, (x.shape[0], V_PAD - LANES), 1) == 0).astype(BF16)
    for h in range(MLA_HEADS):
        qh = _dot(cqn, wuq_ref[:, h * QK_PAD:(h + 1) * QK_PAD])
        q_ref[0, h, :, 0:LANES] = (qh[:, 0:LANES] * scale).astype(BF16)
        q_ref[0, h, :, LANES:QK_PAD] = (_rope128(qh[:, LANES:QK_PAD], rc, rs1, rs2) * scale).astype(BF16)
        kvh = _dot(ckvn, wukv_ref[:, h * 256:(h + 1) * 256])
        k_ref[0, h, :, 0:LANES] = kvh[:, 0:LANES].astype(BF16)
        k_ref[0, h, :, LANES:QK_PAD] = kpe
        vv_ref[0, h, :, 0:LANES] = kvh[:, LANES:256].astype(BF16)
        vv_ref[0, h, :, LANES:V_PAD] = ones_col


def _inproj(x2, g_mix, w_in_p, g_gv, g_cq, g_ckv, w_uq_p, w_ukv_b, rc, rs1, rs2, B, S):
    N = x2.shape[0]
    tm = TM_IN
    spt = S // tm
    row = lambda i: (i, 0)
    const = lambda i: (0, 0)
    pos = lambda i: (i % spt, 0)
    head = lambda i: (i // spt, 0, i % spt, 0)
    return pl.pallas_call(
        _inproj_kernel,
        grid=(N // tm,),
        in_specs=[
            pl.BlockSpec((tm, D_MODEL), row),
            pl.BlockSpec((1, D_MODEL), const),
            pl.BlockSpec((D_MODEL, C_END), const, pipeline_mode=pl.Buffered(1)),
            pl.BlockSpec((1, GMLP_WIDTH), const),
            pl.BlockSpec((1, Q_LORA), const),
            pl.BlockSpec((1, KV_LORA), const),
            pl.BlockSpec((Q_LORA, MLA_HEADS * QK_PAD), const),
            pl.BlockSpec((KV_LORA, MLA_HEADS * 256), const),
            pl.BlockSpec((tm, LANES), pos),
            pl.BlockSpec((tm, LANES), pos),
            pl.BlockSpec((tm, LANES), pos),
        ],
        out_specs=[
            pl.BlockSpec((tm, GMLP_WIDTH), row),
            pl.BlockSpec((tm, GMLP_WIDTH), row),
            pl.BlockSpec((tm, D_MODEL), row),
            pl.BlockSpec((tm, D_MODEL), row),
            pl.BlockSpec((1, MLA_HEADS, tm, QK_PAD), head),
            pl.BlockSpec((1, MLA_HEADS, tm, QK_PAD), head),
            pl.BlockSpec((1, MLA_HEADS, tm, V_PAD), head),
        ],
        out_shape=[
            jax.ShapeDtypeStruct((N, GMLP_WIDTH), BF16),
            jax.ShapeDtypeStruct((N, GMLP_WIDTH), BF16),
            jax.ShapeDtypeStruct((N, D_MODEL), BF16),
            jax.ShapeDtypeStruct((N, D_MODEL), BF16),
            jax.ShapeDtypeStruct((B, MLA_HEADS, S, QK_PAD), BF16),
            jax.ShapeDtypeStruct((B, MLA_HEADS, S, QK_PAD), BF16),
            jax.ShapeDtypeStruct((B, MLA_HEADS, S, V_PAD), BF16),
        ],
        compiler_params=pltpu.CompilerParams(
            dimension_semantics=("arbitrary",), vmem_limit_bytes=VMEM_LIMIT),
        name="inproj",
    )(x2, g_mix, w_in_p, g_gv, g_cq, g_ckv, w_uq_p, w_ukv_b, rc, rs1, rs2)


def _gmlp_kernel(u_ref, v_ref, sga_ref, wsp_ref, bsp_ref, wproj_ref, ma_ref, y_sc):
    nb = TM_GMLP // GMLP_BLOCK
    t_out = lax.broadcasted_iota(jnp.int32, (GMLP_BLOCK, GMLP_BLOCK), 0)
    s_in = lax.broadcasted_iota(jnp.int32, (GMLP_BLOCK, GMLP_BLOCK), 1)
    mask = (s_in // CHUNK) <= (t_out // CHUNK)
    for g in range(GMLP_GROUPS):
        c0 = g * LANES
        w = jnp.where(mask, wsp_ref[g], jnp.zeros((), BF16))
        rhs = jnp.concatenate(
            [v_ref[r * GMLP_BLOCK:(r + 1) * GMLP_BLOCK, c0:c0 + LANES] for r in range(nb)], axis=1)
        sv = _dot(w, rhs) + bsp_ref[:, g:g + 1]
        for r in range(nb):
            rows = slice(r * GMLP_BLOCK, (r + 1) * GMLP_BLOCK)
            ub = u_ref[rows, c0:c0 + LANES].astype(F32)
            y_sc[rows, c0:c0 + LANES] = (ub * sv[:, r * LANES:(r + 1) * LANES]).astype(BF16)
    ya = _dot(y_sc[...], wproj_ref[...])
    ma_ref[...] = (sga_ref[...].astype(F32) * ya).astype(BF16)


def _gmlp(u, v, sga, wsp_b, bsp_t, wproj_b):
    N = u.shape[0]
    tm = TM_GMLP
    row = lambda i: (i, 0)
    return pl.pallas_call(
        _gmlp_kernel,
        grid=(N // tm,),
        in_specs=[
            pl.BlockSpec((tm, GMLP_WIDTH), row),
            pl.BlockSpec((tm, GMLP_WIDTH), row),
            pl.BlockSpec((tm, D_MODEL), row),
            pl.BlockSpec((GMLP_GROUPS, GMLP_BLOCK, GMLP_BLOCK), lambda i: (0, 0, 0)),
            pl.BlockSpec((GMLP_BLOCK, GMLP_GROUPS), lambda i: (0, 0)),
            pl.BlockSpec((GMLP_WIDTH, D_MODEL), lambda i: (0, 0)),
        ],
        out_specs=pl.BlockSpec((tm, D_MODEL), row),
        out_shape=jax.ShapeDtypeStruct((N, D_MODEL), BF16),
        scratch_shapes=[pltpu.VMEM((tm, GMLP_WIDTH), BF16)],
        compiler_params=pltpu.CompilerParams(
            dimension_semantics=("arbitrary",), vmem_limit_bytes=VMEM_LIMIT),
        name="gmlp",
    )(u, v, sga, wsp_b, bsp_t, wproj_b)


NEG_BIG = -1e30


def _attn_kernel(q_ref, k_ref, v_ref, o_ref, m_sc, acc_sc):
    qi = pl.program_id(2)
    m_sc[...] = jnp.full(m_sc.shape, NEG_BIG, F32)
    acc_sc[...] = jnp.zeros(acc_sc.shape, F32)

    def chunk_mask(r0, nq, nk):
        qc = (lax.broadcasted_iota(jnp.int32, (nq, nk), 0) + r0) // CHUNK
        kc = lax.broadcasted_iota(jnp.int32, (nq, nk), 1) // CHUNK
        return kc <= qc

    def update(hh, r0, nq, start, nk, mask):
        rows = pl.ds(r0, nq)
        kb = k_ref[0, hh, pl.ds(start, nk), :]
        vb = v_ref[0, hh, pl.ds(start, nk), :]
        s = lax.dot_general(q_ref[0, hh, rows, :], kb, (((1,), (1,)), ((), ())),
                            preferred_element_type=F32)
        if mask is not None:
            s = jnp.where(mask, s, NEG_BIG)
        tiles = [s[:, c * LANES:(c + 1) * LANES] for c in range(nk // LANES)]
        tile_max = functools.reduce(jnp.maximum, tiles)
        m_prev = m_sc[hh, rows, :]
        m_new = jnp.maximum(m_prev, jnp.max(tile_max, axis=-1, keepdims=True))
        alpha = jnp.exp2(m_prev - m_new)
        p = jnp.concatenate([jnp.exp2(t - m_new).astype(BF16) for t in tiles], axis=1)
        alpha2 = jnp.concatenate([alpha] * (V_PAD // LANES), axis=1)
        acc_sc[hh, rows, :] = alpha2 * acc_sc[hh, rows, :] + _dot(p, vb)
        m_sc[hh, rows, :] = m_new

    def body(j, carry):
        start = pl.multiple_of(j * TQ, TQ)
        for hh in range(HEADS_PER_STEP):
            update(hh, 0, TQ, start, TQ, None)
        return carry

    lax.fori_loop(0, qi, body, 0)
    start = pl.multiple_of(qi * TQ, TQ)
    mask = chunk_mask(0, TQ, TQ)
    for hh in range(HEADS_PER_STEP):
        update(hh, 0, TQ, start, TQ, mask)
    for hh in range(HEADS_PER_STEP):
        l = acc_sc[hh, :, V_HEAD_DIM:V_HEAD_DIM + 1]
        o_ref[0, :, hh * V_HEAD_DIM:(hh + 1) * V_HEAD_DIM] = (
            acc_sc[hh, :, 0:V_HEAD_DIM] / l).astype(BF16)


def _attention(q, k, v):
    B, H, S, _ = q.shape
    hps = HEADS_PER_STEP
    return pl.pallas_call(
        _attn_kernel,
        grid=(B, H // hps, S // TQ),
        in_specs=[
            pl.BlockSpec((1, hps, TQ, QK_PAD), lambda b, h, i: (b, h, i, 0)),
            pl.BlockSpec((1, hps, S, QK_PAD), lambda b, h, i: (b, h, 0, 0)),
            pl.BlockSpec((1, hps, S, V_PAD), lambda b, h, i: (b, h, 0, 0)),
        ],
        out_specs=pl.BlockSpec((1, TQ, hps * V_HEAD_DIM), lambda b, h, i: (b, i, h)),
        out_shape=jax.ShapeDtypeStruct((B, S, H * V_HEAD_DIM), BF16),
        scratch_shapes=[
            pltpu.VMEM((hps, TQ, LANES), F32),
            pltpu.VMEM((hps, TQ, V_PAD), F32),
        ],
        compiler_params=pltpu.CompilerParams(
            dimension_semantics=("arbitrary", "arbitrary", "arbitrary"),
            vmem_limit_bytes=VMEM_LIMIT),
        name="attention",
    )(q, k, v)


def _col_sum(x):
    return jnp.sum(x, axis=0, keepdims=True)


def _col_max(x):
    return jnp.max(x, axis=0, keepdims=True)


def _first_row(hit, row_f):
    return jnp.min(jnp.where(hit, row_f, float(LANES)), axis=0, keepdims=True).astype(jnp.int32)


def _out_route_kernel(o_ref, ma_ref, sgb_ref, x_ref, wo_ref, wout_ref, gmoe_ref, wr_ref, br_ref,
                      h1_ref, m_ref, rt_ref, rw_ref, cnt_ref, carry_sc):
    i = pl.program_id(0)
    tm = TM_OUT

    @pl.when(i == 0)
    def _():
        carry_sc[...] = jnp.zeros(carry_sc.shape, F32)

    yb = _dot(o_ref[...], wo_ref[...])
    merged = ma_ref[...].astype(F32) + sgb_ref[...].astype(F32) * yb
    h1 = x_ref[...] + _dot(merged.astype(BF16), wout_ref[...])
    h1_ref[...] = h1
    m = _rms(h1) * gmoe_ref[...]
    _store_rows(m_ref, m)

    logits = lax.dot_general(wr_ref[...], m.astype(BF16), (((1,), (1,)), ((), ())),
                             preferred_element_type=F32)
    row = lax.broadcasted_iota(jnp.int32, (LANES, tm), 0)
    row_f = row.astype(F32)
    bias = br_ref[...]
    is_g = (row >= N_EXPERTS) & (row < N_EXPERTS + N_GROUPS)
    neg = jnp.float32(-jnp.inf)

    gl = jnp.where(is_g, logits, neg)
    ge = jnp.where(is_g, jnp.exp(gl - _col_max(gl)), 0.0)
    g_prob = ge / _col_sum(ge)
    g_score = jnp.where(is_g, g_prob + bias, neg)
    g_row = _first_row(g_score == _col_max(g_score), row_f)
    g_w = _col_sum(jnp.where(row == g_row, g_prob, 0.0))
    g_idx = g_row - N_EXPERTS

    in_g = (row // EXPERTS_PER_GROUP) == g_idx
    el = jnp.where(in_g, logits, neg)
    ee = jnp.where(in_g, jnp.exp(el - _col_max(el)), 0.0)
    e_prob = ee / _col_sum(ee)
    e_score = jnp.where(in_g, e_prob + bias, neg)
    id1 = _first_row(e_score == _col_max(e_score), row_f)
    e_score2 = jnp.where(row == id1, neg, e_score)
    id2 = _first_row(e_score2 == _col_max(e_score2), row_f)
    p1 = _col_sum(jnp.where(row == id1, e_prob, 0.0))
    p2 = _col_sum(jnp.where(row == id2, e_prob, 0.0))
    psum = p1 + p2
    w1 = g_w * (p1 / psum)
    w2 = g_w * (p2 / psum)

    oh = ((row == id1) | (row == id2 + N_EXPERTS)).astype(BF16)
    t_in = lax.broadcasted_iota(jnp.int32, (tm, tm), 0)
    t_out = lax.broadcasted_iota(jnp.int32, (tm, tm), 1)
    tri = (t_in < t_out).astype(BF16)
    prefix = _dot(oh, tri)
    tot = jnp.sum(oh.astype(F32), axis=1, keepdims=True)
    tot_sw = jnp.concatenate([tot[N_EXPERTS:], tot[:N_EXPERTS]], axis=0)
    row1 = lax.broadcasted_iota(jnp.int32, (LANES, 1), 0)
    carry = carry_sc[...]
    base = carry + jnp.where(row1 >= N_EXPERTS, tot_sw, 0.0)
    rk = oh.astype(F32) * (base + prefix)
    rank1 = _col_sum(jnp.where(row < N_EXPERTS, rk, 0.0))
    rank2 = _col_sum(jnp.where(row >= N_EXPERTS, rk, 0.0))
    carry_new = carry + tot + tot_sw
    carry_sc[...] = carry_new
    cnt_ref[...] = carry_new.astype(jnp.int32)

    row8 = lax.broadcasted_iota(jnp.int32, (SUBLANES, tm), 0)
    rt = jnp.where(row8 == 0, id1, jnp.where(row8 == 1, id2, 0))
    rt = jnp.where(row8 == 2, rank1.astype(jnp.int32), rt)
    rt_ref[...] = jnp.where(row8 == 3, rank2.astype(jnp.int32), rt)
    wt = jnp.where(row == 0, w1, jnp.where(row == 1, w2, 0.0))
    rw_ref[...] = wt.T


def _out_route(o2, ma, sgb, x2, wo_b, wout_b, g_moe, wr_b, br):
    N = x2.shape[0]
    tm = TM_OUT
    row = lambda i: (i, 0)
    const = lambda i: (0, 0)
    return pl.pallas_call(
        _out_route_kernel,
        grid=(N // tm,),
        in_specs=[
            pl.BlockSpec((tm, D_MODEL), row),
            pl.BlockSpec((tm, D_MODEL), row),
            pl.BlockSpec((tm, D_MODEL), row),
            pl.BlockSpec((tm, D_MODEL), row),
            pl.BlockSpec((D_MODEL, D_MODEL), const),
            pl.BlockSpec((D_MODEL, D_MODEL), const),
            pl.BlockSpec((1, D_MODEL), const),
            pl.BlockSpec((LANES, D_MODEL), const),
            pl.BlockSpec((LANES, tm), const),
        ],
        out_specs=[
            pl.BlockSpec((tm, D_MODEL), row),
            pl.BlockSpec((tm * SUBLANES, LANES), row),
            pl.BlockSpec((SUBLANES, tm), lambda i: (0, i)),
            pl.BlockSpec((tm, LANES), row),
            pl.BlockSpec((LANES, 1), const),
        ],
        out_shape=[
            jax.ShapeDtypeStruct((N, D_MODEL), F32),
            jax.ShapeDtypeStruct((N * SUBLANES, LANES), F32),
            jax.ShapeDtypeStruct((SUBLANES, N), jnp.int32),
            jax.ShapeDtypeStruct((N, LANES), F32),
            jax.ShapeDtypeStruct((LANES, 1), jnp.int32),
        ],
        scratch_shapes=[pltpu.VMEM((LANES, 1), F32)],
        compiler_params=pltpu.CompilerParams(
            dimension_semantics=("arbitrary",), vmem_limit_bytes=VMEM_LIMIT),
        name="out_route",
    )(o2, ma, sgb, x2, wo_b, wout_b, g_moe, wr_b, br)


def _row_tile(ref2, row8):
    return ref2.at[pl.ds(pl.multiple_of(row8, SUBLANES), SUBLANES)]


def _dispatch_kernel(pad_end_ref, padded_ref, dest_ref, m_ref, inv0_ref, buf_ref, inv_ref,
                     zero_sc, inv_sm, sem, zsem, isem):
    i = pl.program_id(0)
    tm = TM_ROWS
    n_tok = pl.num_programs(0) * tm
    blk8 = MOE_BLK * SUBLANES

    @pl.when(i == 0)
    def _():
        init = pltpu.make_async_copy(inv0_ref, inv_sm, isem)
        init.start()
        init.wait()
        zero_sc[...] = jnp.zeros(zero_sc.shape, F32)

        def zero_copy(e):
            start = pl.multiple_of((pad_end_ref[e] - MOE_BLK) * SUBLANES, blk8)
            return pltpu.make_async_copy(zero_sc, buf_ref.at[pl.ds(start, blk8)], zsem)

        def start(e, carry):
            @pl.when(padded_ref[e] > 0)
            def _():
                zero_copy(e).start()
            return carry

        def wait(e, carry):
            @pl.when(padded_ref[e] > 0)
            def _():
                zero_copy(e).wait()
            return carry

        lax.fori_loop(0, N_EXPERTS, start, 0)
        lax.fori_loop(0, N_EXPERTS, wait, 0)

        def tail_copy(b):
            return pltpu.make_async_copy(
                zero_sc, buf_ref.at[pl.ds(pl.multiple_of(b * blk8, blk8), blk8)], zsem)

        nused = pad_end_ref[N_EXPERTS - 1] // MOE_BLK
        nblk = buf_ref.shape[0] // blk8
        lax.fori_loop(nused, nblk, lambda b, c: (tail_copy(b).start(), c)[1], 0)
        lax.fori_loop(nused, nblk, lambda b, c: (tail_copy(b).wait(), c)[1], 0)

    def issue(c, carry):
        for u in range(DMA_UNROLL):
            r = c * DMA_UNROLL + u
            src = _row_tile(m_ref, r * SUBLANES)
            d0 = dest_ref[0, 0, r]
            d1 = dest_ref[0, 0, tm + r]
            pltpu.make_async_copy(src, _row_tile(buf_ref, d0), sem).start(priority=0)
            pltpu.make_async_copy(src, _row_tile(buf_ref, d1), sem).start(priority=1)
            tok8 = (i * tm + r) * SUBLANES
            inv_sm[lax.shift_right_logical(d0, 3)] = tok8
            inv_sm[lax.shift_right_logical(d1, 3)] = n_tok * SUBLANES + tok8
        return carry

    lax.fori_loop(0, tm // DMA_UNROLL, issue, 0)
    for _ in range(2):
        pltpu.make_async_copy(m_ref, buf_ref.at[pl.ds(0, tm * SUBLANES)], sem).wait()

    @pl.when(i == pl.num_programs(0) - 1)
    def _():
        out = pltpu.make_async_copy(inv_sm, inv_ref, isem)
        out.start()
        out.wait()


def _dispatch(pad_end, padded, dest8_tiles, m2, inv0):
    P = inv0.shape[0]
    tm = TM_ROWS
    N = m2.shape[0] // SUBLANES
    any_spec = pl.BlockSpec(memory_space=pl.ANY)
    grid_spec = pltpu.PrefetchScalarGridSpec(
        num_scalar_prefetch=2,
        grid=(N // tm,),
        in_specs=[
            pl.BlockSpec((1, 1, 2 * tm), lambda i, pe, pd: (i, 0, 0), memory_space=pltpu.SMEM),
            pl.BlockSpec((tm * SUBLANES, LANES), lambda i, pe, pd: (i, 0)),
            any_spec,
        ],
        out_specs=[any_spec, any_spec],
        scratch_shapes=[
            pltpu.VMEM((MOE_BLK * SUBLANES, LANES), F32),
            pltpu.SMEM((P,), jnp.int32),
            pltpu.SemaphoreType.DMA(()),
            pltpu.SemaphoreType.DMA(()),
            pltpu.SemaphoreType.DMA(()),
        ],
    )
    return pl.pallas_call(
        _dispatch_kernel,
        grid_spec=grid_spec,
        out_shape=[jax.ShapeDtypeStruct((P * SUBLANES, LANES), F32),
                   jax.ShapeDtypeStruct((P,), jnp.int32)],
        compiler_params=pltpu.CompilerParams(dimension_semantics=("arbitrary",)),
        name="dispatch",
    )(pad_end, padded, dest8_tiles, m2, inv0)


def _expert_kernel(blk_e_ref, run_blocks_ref, nused_ref, x_hbm, inv_hbm, wg_hbm, wu_hbm, wd_hbm,
                   y_hbm, xbuf, ybuf, inv_sm, wg_raw, wu_raw, wd_raw, wgu_sc, wd_sc,
                   xsem, ysem, isem, wsem, zsem):
    nused = nused_ref[0]
    blk8 = MOE_BLK * SUBLANES
    nblk = y_hbm.shape[0] // blk8
    nxbuf = X_AHEAD + 1
    n_inv = inv_sm.shape[0]

    def block_rows(ref, b):
        return ref.at[pl.ds(pl.multiple_of(b * blk8, blk8), blk8)]

    def x_copy(b):
        slot = b % nxbuf
        return pltpu.make_async_copy(block_rows(x_hbm, b), xbuf.at[slot], xsem.at[slot])

    def inv_copy(b):
        slot = b % n_inv
        return pltpu.make_async_copy(inv_hbm.at[b], inv_sm.at[pl.ds(slot, 1)], isem.at[slot])

    def scatter_rows(b):
        yslot = b % 2
        islot = b % n_inv
        for r in range(MOE_BLK):
            pltpu.make_async_copy(_row_tile(ybuf.at[yslot], r * SUBLANES),
                                  _row_tile(y_hbm, inv_sm[islot, r]),
                                  ysem.at[yslot]).start(priority=r % 2)

    def scatter_wait(b):
        yslot = b % 2
        pltpu.make_async_copy(ybuf.at[yslot], y_hbm.at[pl.ds(0, blk8)], ysem.at[yslot]).wait()

    def w_copies(e, slot):
        return (pltpu.make_async_copy(wg_hbm.at[e], wg_raw.at[slot], wsem.at[slot]),
                pltpu.make_async_copy(wu_hbm.at[e], wu_raw.at[slot], wsem.at[slot]),
                pltpu.make_async_copy(wd_hbm.at[e], wd_raw.at[slot], wsem.at[slot]))

    def prepare(i, run):
        e = blk_e_ref[i]
        new_expert = (i == 0) | (e != blk_e_ref[jnp.maximum(i - 1, 0)])
        run = run + new_expert.astype(jnp.int32)
        wslot = run % 2

        @pl.when(new_expert)
        def _():
            for c in w_copies(e, wslot):
                c.wait()
            wgu_sc[:, 0:D_EXPERT] = wg_raw[wslot].astype(BF16)
            wgu_sc[:, D_EXPERT:2 * D_EXPERT] = wu_raw[wslot].astype(BF16)
            wd_sc[...] = wd_raw[wslot].astype(BF16)
            nxt = i + run_blocks_ref[e]

            @pl.when(nxt < nused)
            def _():
                for c in w_copies(blk_e_ref[nxt], 1 - wslot):
                    c.start()

        @pl.when(i + X_AHEAD < nused)
        def _():
            x_copy(i + X_AHEAD).start()

        x_copy(i).wait()

        @pl.when(i >= 2)
        def _():
            scatter_wait(i - 2)

        return run

    def compute(i):
        xb = _load_rows(xbuf.at[i % nxbuf], MOE_BLK).astype(BF16)
        h = _dot(xb, wgu_sc[...])
        hdn = (jax.nn.silu(h[:, 0:D_EXPERT]) * h[:, D_EXPERT:2 * D_EXPERT]).astype(BF16)
        _store_rows(ybuf.at[i % 2], _dot(hdn, wd_sc[...]))

    for b in range(X_AHEAD):
        @pl.when(b < nused)
        def _():
            x_copy(b).start()
    for c in w_copies(blk_e_ref[0], 0):
        c.start()
    inv_copy(0).start()

    ybuf[1] = jnp.zeros(ybuf.shape[1:], F32)

    def zero_copy(b):
        return pltpu.make_async_copy(ybuf.at[1], block_rows(y_hbm, b), zsem)

    lax.fori_loop(nblk - N_EXPERTS, nblk, lambda b, c: (zero_copy(b).start(), c)[1], 0)
    run0 = prepare(0, jnp.int32(-1))
    compute(0)
    lax.fori_loop(nblk - N_EXPERTS, nblk, lambda b, c: (zero_copy(b).wait(), c)[1], 0)

    def body(i, run):
        run = prepare(i, run)
        inv_copy(i).start()
        inv_copy(i - 1).wait()
        scatter_rows(i - 1)
        compute(i)
        return run

    lax.fori_loop(1, nused, body, run0)
    inv_copy(nused - 1).wait()
    scatter_rows(nused - 1)

    @pl.when(nused >= 2)
    def _():
        scatter_wait(nused - 2)

    scatter_wait(nused - 1)


def _experts(blk_e, run_blocks, nused, buf2, inv3, wg, wu, wd):
    blk_rows = MOE_BLK * SUBLANES
    any_spec = pl.BlockSpec(memory_space=pl.ANY)
    grid_spec = pltpu.PrefetchScalarGridSpec(
        num_scalar_prefetch=3,
        grid=(1,),
        in_specs=[any_spec] * 5,
        out_specs=any_spec,
        scratch_shapes=[
            pltpu.VMEM((X_AHEAD + 1, blk_rows, LANES), F32),
            pltpu.VMEM((2, blk_rows, LANES), F32),
            pltpu.SMEM((3, MOE_BLK), jnp.int32),
            pltpu.VMEM((2, D_MODEL, D_EXPERT), F32),
            pltpu.VMEM((2, D_MODEL, D_EXPERT), F32),
            pltpu.VMEM((2, D_EXPERT, D_MODEL), F32),
            pltpu.VMEM((D_MODEL, 2 * D_EXPERT), BF16),
            pltpu.VMEM((D_EXPERT, D_MODEL), BF16),
            pltpu.SemaphoreType.DMA((X_AHEAD + 1,)),
            pltpu.SemaphoreType.DMA((2,)),
            pltpu.SemaphoreType.DMA((3,)),
            pltpu.SemaphoreType.DMA((2,)),
            pltpu.SemaphoreType.DMA(()),
        ],
    )
    return pl.pallas_call(
        _expert_kernel,
        grid_spec=grid_spec,
        out_shape=jax.ShapeDtypeStruct(buf2.shape, F32),
        compiler_params=pltpu.CompilerParams(
            dimension_semantics=("arbitrary",), vmem_limit_bytes=VMEM_LIMIT),
        name="experts",
    )(blk_e, run_blocks, nused, buf2, inv3, wg, wu, wd)


def _final_kernel(h1_ref, y0_ref, y1_ref, rw_ref, p_ref, gple_ref, wpg_ref, wpp_ref, gfin_ref,
                  out_ref):
    tm = TM_ROWS
    rw = rw_ref[...]
    y0 = _load_rows(y0_ref, tm)
    y1 = _load_rows(y1_ref, tm)
    h2 = h1_ref[...] + (y0 * rw[:, 0:1] + y1 * rw[:, 1:2])
    n3 = (_rms(h2) * gple_ref[...]).astype(BF16)
    gate = jax.nn.sigmoid(_dot(n3, wpg_ref[...]))
    pp = _dot(p_ref[...].astype(BF16), wpp_ref[...])
    h3 = h2 + gate * pp
    out_ref[...] = _rms(h3) * gfin_ref[...]


def _final(h1, y01, rw, p2, g_ple, wpg_b, wpp_b, g_final):
    N = h1.shape[0]
    tm = TM_ROWS
    nt = N // tm
    row = lambda i: (i, 0)
    const = lambda i: (0, 0)
    return pl.pallas_call(
        _final_kernel,
        grid=(nt,),
        in_specs=[
            pl.BlockSpec((tm, D_MODEL), row),
            pl.BlockSpec((tm * SUBLANES, LANES), row),
            pl.BlockSpec((tm * SUBLANES, LANES), lambda i: (i + nt, 0)),
            pl.BlockSpec((tm, LANES), row),
            pl.BlockSpec((tm, PLE_DIM), row),
            pl.BlockSpec((1, D_MODEL), const),
            pl.BlockSpec((D_MODEL, D_MODEL), const),
            pl.BlockSpec((PLE_DIM, D_MODEL), const),
            pl.BlockSpec((1, D_MODEL), const),
        ],
        out_specs=pl.BlockSpec((tm, D_MODEL), row),
        out_shape=jax.ShapeDtypeStruct((N, D_MODEL), F32),
        compiler_params=pltpu.CompilerParams(
            dimension_semantics=("arbitrary",), vmem_limit_bytes=VMEM_LIMIT),
        name="final",
    )(h1, y01, y01, rw, p2, g_ple, wpg_b, wpp_b, g_final)


def _rope_tables(S):
    inv_freq = ROPE_THETA ** (-jnp.arange(0, QK_ROPE_DIM, 2, dtype=F32) / QK_ROPE_DIM)
    ang = jnp.arange(S, dtype=F32)[:, None] * inv_freq[None, :]
    cos, sin = jnp.cos(ang), jnp.sin(ang)
    z = jnp.zeros_like(cos)
    rc = jnp.concatenate([cos, cos, z, z], axis=1)
    rs1 = jnp.concatenate([-sin, z, z, z], axis=1)
    rs2 = jnp.concatenate([z, sin, z, z], axis=1)
    return rc, rs1, rs2


def _layer(h, p_l, g_mix, w_in, g_gv, w_spatial, b_spatial, w_gproj, g_cq, w_uq, g_ckv, w_ukv,
           w_mla_o, w_out, g_moe, w_router_g, b_router_g, w_router_e, b_router_e,
           w_e_gate, w_e_up, w_e_down, g_ple, w_ple_gate, w_ple_proj, g_out):
    B, S, D = h.shape
    N = B * S
    x2 = h.reshape(N, D)

    cu, cv, ccq, cckv, ckr, cga = (GMLP_WIDTH, 2 * GMLP_WIDTH, 2 * GMLP_WIDTH + Q_LORA,
                                   2 * GMLP_WIDTH + Q_LORA + KV_LORA,
                                   2 * GMLP_WIDTH + Q_LORA + KV_LORA + QK_ROPE_DIM,
                                   2 * GMLP_WIDTH + Q_LORA + KV_LORA + QK_ROPE_DIM + D_MODEL)
    w_in_p = jnp.concatenate(
        [w_in[:, :cckv], w_in[:, ckr:], w_in[:, cckv:ckr],
         jnp.zeros((D, LANES - QK_ROPE_DIM), w_in.dtype)], axis=1).astype(BF16)
    w_uq_h = w_uq.reshape(Q_LORA, MLA_HEADS, QK_NOPE_DIM + QK_ROPE_DIM)
    w_uq_p = jnp.concatenate(
        [w_uq_h, jnp.zeros((Q_LORA, MLA_HEADS, QK_PAD - QK_NOPE_DIM - QK_ROPE_DIM), w_uq.dtype)],
        axis=2).reshape(Q_LORA, MLA_HEADS * QK_PAD).astype(BF16)
    rc, rs1, rs2 = _rope_tables(S)

    u, v, sga, sgb, q, k, vv = _inproj(
        x2, g_mix[None], w_in_p, g_gv[None], g_cq[None], g_ckv[None], w_uq_p, w_ukv.astype(BF16),
        rc, rs1, rs2, B, S)
    ma = _gmlp(u, v, sga, w_spatial.astype(BF16), b_spatial.T, w_gproj.astype(BF16))
    o = _attention(q, k, vv)

    wr = jnp.concatenate(
        [w_router_e.T, w_router_g.T, jnp.zeros((LANES - N_EXPERTS - N_GROUPS, D), w_router_e.dtype)],
        axis=0).astype(BF16)
    br = jnp.concatenate(
        [b_router_e.reshape(-1), b_router_g, jnp.zeros((LANES - N_EXPERTS - N_GROUPS,), F32)])
    br = jnp.broadcast_to(br[:, None], (LANES, TM_OUT))
    h1, m, rt, rw, cnt = _out_route(
        o.reshape(N, D), ma, sgb, x2, w_mla_o.astype(BF16), w_out.astype(BF16), g_moe[None], wr, br)

    counts = cnt[:N_EXPERTS, 0]
    padded = (counts + MOE_BLK - 1) // MOE_BLK * MOE_BLK
    pad_end = jnp.cumsum(padded)
    pad_start = pad_end - padded
    P = 2 * N + N_EXPERTS * MOE_BLK
    nblk = P // MOE_BLK
    blk_start = jnp.arange(nblk, dtype=jnp.int32) * MOE_BLK
    blk_e = jnp.minimum(
        jnp.sum((pad_end[None, :] <= blk_start[:, None]).astype(jnp.int32), axis=1),
        N_EXPERTS - 1).astype(jnp.int32)
    nused = (pad_end[-1:] // MOE_BLK).astype(jnp.int32)
    experts = jnp.arange(N_EXPERTS, dtype=jnp.int32)[:, None, None]
    start_of = jnp.sum(jnp.where(rt[None, 0:2] == experts, pad_start[:, None, None], 0), axis=0)
    dest8 = (start_of + rt[2:4]).astype(jnp.int32) * SUBLANES
    nt = N // TM_ROWS
    dest8_tiles = dest8.reshape(2, nt, TM_ROWS).transpose(1, 0, 2).reshape(nt, 1, 2 * TM_ROWS)

    buf_row = jnp.arange(P, dtype=jnp.int32)
    inv0 = (2 * N + blk_e[buf_row // MOE_BLK] * MOE_BLK + buf_row % MOE_BLK) * SUBLANES
    buf, inv = _dispatch(pad_end.astype(jnp.int32), padded.astype(jnp.int32), dest8_tiles, m, inv0)
    ys = _experts(blk_e, (padded // MOE_BLK).astype(jnp.int32), nused, buf,
                  inv.reshape(nblk, 1, MOE_BLK), w_e_gate, w_e_up, w_e_down)
    out = _final(h1, ys, rw, p_l.reshape(N, PLE_DIM), g_ple[None],
                 w_ple_gate.astype(BF16), w_ple_proj.astype(BF16), g_out[None])
    return out.reshape(B, S, D)


def kernel(x, p, g_mix, w_in, g_gv, w_spatial, b_spatial, w_gproj, g_cq, w_uq, g_ckv, w_ukv, w_mla_o,
           w_out, g_moe, w_router_g, b_router_g, w_router_e, b_router_e, w_e_gate, w_e_up, w_e_down,
           g_ple, w_ple_gate, w_ple_proj, g_final):
    depth = p.shape[0]
    assert depth == 1, "the final rmsnorm is fused into the single layer's last kernel"
    i = 0
    return _layer(x, p[i], g_mix[i], w_in[i], g_gv[i], w_spatial[i], b_spatial[i], w_gproj[i], g_cq[i],
                  w_uq[i], g_ckv[i], w_ukv[i], w_mla_o[i], w_out[i], g_moe[i], w_router_g[i],
                  b_router_g[i], w_router_e[i], b_router_e[i], w_e_gate[i], w_e_up[i], w_e_down[i],
                  g_ple[i], w_ple_gate[i], w_ple_proj[i], g_final)
```

```python
import functools

import jax
import jax.numpy as jnp
from jax import lax
from jax.experimental import pallas as pl
from jax.experimental.pallas import tpu as pltpu

F32 = jnp.float32
BF16 = jnp.bfloat16

D_MODEL = 1024
CHUNK = 64
PLE_DIM = 256
GMLP_BLOCK = 128
GMLP_GROUPS = 12
GMLP_WIDTH = 1536
MLA_HEADS = 8
QK_NOPE_DIM = 128
QK_ROPE_DIM = 64
V_HEAD_DIM = 128
Q_LORA = 384
KV_LORA = 256
ROPE_THETA = 10000.0
N_GROUPS = 8
EXPERTS_PER_GROUP = 8
N_EXPERTS = 64
D_EXPERT = 256
EPS = 1e-6
LOG2E = 1.4426950408889634

LANES = 128
SUBLANES = 8
QK_PAD = 256
V_PAD = 256
VMEM_LIMIT = 56 * 1024 * 1024

C_U = 0
C_V = C_U + GMLP_WIDTH
C_CQ = C_V + GMLP_WIDTH
C_CKV = C_CQ + Q_LORA
C_GA = C_CKV + KV_LORA
C_GB = C_GA + D_MODEL
C_KR = C_GB + D_MODEL
C_END = C_KR + LANES

TM_IN = 512
TM_GMLP = 512
TQ = 512
HEADS_PER_STEP = 4
TM_OUT = 512
TM_ROWS = 512
MOE_BLK = 256
X_AHEAD = 3
DMA_UNROLL = 8


def _rms(x):
    return x * lax.rsqrt(jnp.mean(x * x, axis=-1, keepdims=True) + EPS)


def _dot(a, b):
    return jnp.dot(a, b, preferred_element_type=F32)


def _store_rows(ref2, x):
    rows = x.shape[0]
    for j in range(SUBLANES):
        ref2[pl.ds(j, rows, stride=SUBLANES), :] = x[:, j * LANES:(j + 1) * LANES]


def _load_rows(ref2, rows, first_row=0):
    return jnp.concatenate(
        [ref2[pl.ds(first_row * SUBLANES + j, rows, stride=SUBLANES), :] for j in range(SUBLANES)],
        axis=1)


def _rope128(t, rc, rs1, rs2):
    r1 = pltpu.roll(t, 96, axis=1)
    r2 = pltpu.roll(t, 32, axis=1)
    return t * rc + r1 * rs1 + r2 * rs2


def _inproj_kernel(x_ref, gmix_ref, win_ref, ggv_ref, gcq_ref, gckv_ref, wuq_ref, wukv_ref,
                   rc_ref, rs1_ref, rs2_ref,
                   u_ref, v_ref, sga_ref, sgb_ref, q_ref, k_ref, vv_ref):
    x = x_ref[...]
    ab = (_rms(x) * gmix_ref[...]).astype(BF16)

    def proj(c0, c1):
        return _dot(ab, win_ref[:, c0:c1])

    u_ref[...] = jax.nn.gelu(proj(C_U, C_V)).astype(BF16)

    zv = jax.nn.gelu(proj(C_V, C_CQ))
    xc = zv - jnp.mean(zv, axis=-1, keepdims=True)
    vln = xc * lax.rsqrt(jnp.mean(xc * xc, axis=-1, keepdims=True) + EPS)
    v_ref[...] = (vln * ggv_ref[...]).astype(BF16)

    sga_ref[...] = jax.nn.sigmoid(proj(C_GA, C_GB)).astype(BF16)
    sgb_ref[...] = jax.nn.sigmoid(proj(C_GB, C_KR)).astype(BF16)

    rc = rc_ref[...]
    rs1 = rs1_ref[...]
    rs2 = rs2_ref[...]
    kpe = _rope128(proj(C_KR, C_END), rc, rs1, rs2).astype(BF16)

    cqn = (_rms(proj(C_CQ, C_CKV)) * gcq_ref[...]).astype(BF16)
    ckvn = (_rms(proj(C_CKV, C_GA)) * gckv_ref[...]).astype(BF16)
    scale = (QK_NOPE_DIM + QK_ROPE_DIM) ** -0.5 * LOG2E
    for h in range(MLA_HEADS):
        qh = _dot(cqn, wuq_ref[:, h * QK_PAD:(h + 1) * QK_PAD])
        q_ref[0, h, :, 0:LANES] = (qh[:, 0:LANES] * scale).astype(BF16)
        q_ref[0, h, :, LANES:QK_PAD] = (_rope128(qh[:, LANES:QK_PAD], rc, rs1, rs2) * scale).astype(BF16)
        kvh = _dot(ckvn, wukv_ref[:, h * 256:(h + 1) * 256])
        k_ref[0, h, :, 0:LANES] = kvh[:, 0:LANES].astype(BF16)
        k_ref[0, h, :, LANES:QK_PAD] = kpe
        vv_ref[0, h] = kvh[:, LANES:256].astype(BF16)


def _inproj(x2, g_mix, w_in_p, g_gv, g_cq, g_ckv, w_uq_p, w_ukv_b, rc, rs1, rs2, B, S):
    N = x2.shape[0]
    tm = TM_IN
    spt = S // tm
    row = lambda i: (i, 0)
    const = lambda i: (0, 0)
    pos = lambda i: (i % spt, 0)
    head = lambda i: (i // spt, 0, i % spt, 0)
    return pl.pallas_call(
        _inproj_kernel,
        grid=(N // tm,),
        in_specs=[
            pl.BlockSpec((tm, D_MODEL), row),
            pl.BlockSpec((1, D_MODEL), const),
            pl.BlockSpec((D_MODEL, C_END), const, pipeline_mode=pl.Buffered(1)),
            pl.BlockSpec((1, GMLP_WIDTH), const),
            pl.BlockSpec((1, Q_LORA), const),
            pl.BlockSpec((1, KV_LORA), const),
            pl.BlockSpec((Q_LORA, MLA_HEADS * QK_PAD), const),
            pl.BlockSpec((KV_LORA, MLA_HEADS * 256), const),
            pl.BlockSpec((tm, LANES), pos),
            pl.BlockSpec((tm, LANES), pos),
            pl.BlockSpec((tm, LANES), pos),
        ],
        out_specs=[
            pl.BlockSpec((tm, GMLP_WIDTH), row),
            pl.BlockSpec((tm, GMLP_WIDTH), row),
            pl.BlockSpec((tm, D_MODEL), row),
            pl.BlockSpec((tm, D_MODEL), row),
            pl.BlockSpec((1, MLA_HEADS, tm, QK_PAD), head),
            pl.BlockSpec((1, MLA_HEADS, tm, QK_PAD), head),
            pl.BlockSpec((1, MLA_HEADS, tm, V_HEAD_DIM), head),
        ],
        out_shape=[
            jax.ShapeDtypeStruct((N, GMLP_WIDTH), BF16),
            jax.ShapeDtypeStruct((N, GMLP_WIDTH), BF16),
            jax.ShapeDtypeStruct((N, D_MODEL), BF16),
            jax.ShapeDtypeStruct((N, D_MODEL), BF16),
            jax.ShapeDtypeStruct((B, MLA_HEADS, S, QK_PAD), BF16),
            jax.ShapeDtypeStruct((B, MLA_HEADS, S, QK_PAD), BF16),
            jax.ShapeDtypeStruct((B, MLA_HEADS, S, V_HEAD_DIM), BF16),
        ],
        compiler_params=pltpu.CompilerParams(
            dimension_semantics=("arbitrary",), vmem_limit_bytes=VMEM_LIMIT),
        name="inproj",
    )(x2, g_mix, w_in_p, g_gv, g_cq, g_ckv, w_uq_p, w_ukv_b, rc, rs1, rs2)


def _gmlp_kernel(u_ref, v_ref, sga_ref, wsp_ref, bsp_ref, wproj_ref, ma_ref, y_sc):
    nb = TM_GMLP // GMLP_BLOCK
    t_out = lax.broadcasted_iota(jnp.int32, (GMLP_BLOCK, GMLP_BLOCK), 0)
    s_in = lax.broadcasted_iota(jnp.int32, (GMLP_BLOCK, GMLP_BLOCK), 1)
    mask = (s_in // CHUNK) <= (t_out // CHUNK)
    for g in range(GMLP_GROUPS):
        c0 = g * LANES
        w = jnp.where(mask, wsp_ref[g], jnp.zeros((), BF16))
        rhs = jnp.concatenate(
            [v_ref[r * GMLP_BLOCK:(r + 1) * GMLP_BLOCK, c0:c0 + LANES] for r in range(nb)], axis=1)
        sv = _dot(w, rhs) + bsp_ref[:, g:g + 1]
        for r in range(nb):
            rows = slice(r * GMLP_BLOCK, (r + 1) * GMLP_BLOCK)
            ub = u_ref[rows, c0:c0 + LANES].astype(F32)
            y_sc[rows, c0:c0 + LANES] = (ub * sv[:, r * LANES:(r + 1) * LANES]).astype(BF16)
    ya = _dot(y_sc[...], wproj_ref[...])
    ma_ref[...] = (sga_ref[...].astype(F32) * ya).astype(BF16)


def _gmlp(u, v, sga, wsp_b, bsp_t, wproj_b):
    N = u.shape[0]
    tm = TM_GMLP
    row = lambda i: (i, 0)
    return pl.pallas_call(
        _gmlp_kernel,
        grid=(N // tm,),
        in_specs=[
            pl.BlockSpec((tm, GMLP_WIDTH), row),
            pl.BlockSpec((tm, GMLP_WIDTH), row),
            pl.BlockSpec((tm, D_MODEL), row),
            pl.BlockSpec((GMLP_GROUPS, GMLP_BLOCK, GMLP_BLOCK), lambda i: (0, 0, 0)),
            pl.BlockSpec((GMLP_BLOCK, GMLP_GROUPS), lambda i: (0, 0)),
            pl.BlockSpec((GMLP_WIDTH, D_MODEL), lambda i: (0, 0)),
        ],
        out_specs=pl.BlockSpec((tm, D_MODEL), row),
        out_shape=jax.ShapeDtypeStruct((N, D_MODEL), BF16),
        scratch_shapes=[pltpu.VMEM((tm, GMLP_WIDTH), BF16)],
        compiler_params=pltpu.CompilerParams(
            dimension_semantics=("arbitrary",), vmem_limit_bytes=VMEM_LIMIT),
        name="gmlp",
    )(u, v, sga, wsp_b, bsp_t, wproj_b)


NEG_BIG = -1e30


def _attn_kernel(q_ref, k_ref, v_ref, o_ref, m_sc, acc_sc):
    qi = pl.program_id(2)
    m_sc[...] = jnp.full(m_sc.shape, NEG_BIG, F32)
    acc_sc[...] = jnp.zeros(acc_sc.shape, F32)
    ones_col = (lax.broadcasted_iota(jnp.int32, (TQ, V_PAD - V_HEAD_DIM), 1) == 0).astype(BF16)

    def chunk_mask(r0, nq, nk):
        qc = (lax.broadcasted_iota(jnp.int32, (nq, nk), 0) + r0) // CHUNK
        kc = lax.broadcasted_iota(jnp.int32, (nq, nk), 1) // CHUNK
        return kc <= qc

    def update(hh, r0, nq, start, nk, mask):
        rows = pl.ds(r0, nq)
        kb = k_ref[0, hh, pl.ds(start, nk), :]
        vb = jnp.concatenate([v_ref[0, hh, pl.ds(start, nk), :], ones_col[0:nk]], axis=1)
        s = lax.dot_general(q_ref[0, hh, rows, :], kb, (((1,), (1,)), ((), ())),
                            preferred_element_type=F32)
        if mask is not None:
            s = jnp.where(mask, s, NEG_BIG)
        tiles = [s[:, c * LANES:(c + 1) * LANES] for c in range(nk // LANES)]
        tile_max = functools.reduce(jnp.maximum, tiles)
        m_prev = m_sc[hh, rows, :]
        m_new = jnp.maximum(m_prev, jnp.max(tile_max, axis=-1, keepdims=True))
        alpha = jnp.exp2(m_prev - m_new)
        p = jnp.concatenate([jnp.exp2((t - m_new).astype(BF16)) for t in tiles], axis=1)
        alpha2 = jnp.concatenate([alpha] * (V_PAD // LANES), axis=1)
        acc_sc[hh, rows, :] = alpha2 * acc_sc[hh, rows, :] + _dot(p, vb)
        m_sc[hh, rows, :] = m_new

    def body(j, carry):
        start = pl.multiple_of(j * TQ, TQ)
        for hh in range(HEADS_PER_STEP):
            update(hh, 0, TQ, start, TQ, None)
        return carry

    lax.fori_loop(0, qi, body, 0)
    start = pl.multiple_of(qi * TQ, TQ)
    mask = chunk_mask(0, TQ, TQ)
    for hh in range(HEADS_PER_STEP):
        update(hh, 0, TQ, start, TQ, mask)
    for hh in range(HEADS_PER_STEP):
        l = acc_sc[hh, :, V_HEAD_DIM:V_HEAD_DIM + 1]
        o_ref[0, :, hh * V_HEAD_DIM:(hh + 1) * V_HEAD_DIM] = (
            acc_sc[hh, :, 0:V_HEAD_DIM] / l).astype(BF16)


def _attention(q, k, v):
    B, H, S, _ = q.shape
    hps = HEADS_PER_STEP
    return pl.pallas_call(
        _attn_kernel,
        grid=(B, H // hps, S // TQ),
        in_specs=[
            pl.BlockSpec((1, hps, TQ, QK_PAD), lambda b, h, i: (b, h, i, 0)),
            pl.BlockSpec((1, hps, S, QK_PAD), lambda b, h, i: (b, h, 0, 0)),
            pl.BlockSpec((1, hps, S, V_HEAD_DIM), lambda b, h, i: (b, h, 0, 0)),
        ],
        out_specs=pl.BlockSpec((1, TQ, hps * V_HEAD_DIM), lambda b, h, i: (b, i, h)),
        out_shape=jax.ShapeDtypeStruct((B, S, H * V_HEAD_DIM), BF16),
        scratch_shapes=[
            pltpu.VMEM((hps, TQ, LANES), F32),
            pltpu.VMEM((hps, TQ, V_PAD), F32),
        ],
        compiler_params=pltpu.CompilerParams(
            dimension_semantics=("arbitrary", "arbitrary", "arbitrary"),
            vmem_limit_bytes=VMEM_LIMIT),
        name="attention",
    )(q, k, v)


def _col_sum(x):
    return jnp.sum(x, axis=0, keepdims=True)


def _col_max(x):
    return jnp.max(x, axis=0, keepdims=True)


def _first_row(hit, row_f):
    return jnp.min(jnp.where(hit, row_f, float(LANES)), axis=0, keepdims=True).astype(jnp.int32)


def _out_route_kernel(o_ref, ma_ref, sgb_ref, x_ref, wo_ref, wout_ref, gmoe_ref, wr_ref, br_ref,
                      h1_ref, m_ref, rt_ref, rw_ref, cnt_ref, carry_sc):
    i = pl.program_id(0)
    tm = TM_OUT

    @pl.when(i == 0)
    def _():
        carry_sc[...] = jnp.zeros(carry_sc.shape, F32)

    yb = _dot(o_ref[...], wo_ref[...])
    merged = ma_ref[...].astype(F32) + sgb_ref[...].astype(F32) * yb
    h1 = x_ref[...] + _dot(merged.astype(BF16), wout_ref[...])
    h1_ref[...] = h1
    m = _rms(h1) * gmoe_ref[...]
    _store_rows(m_ref, m)

    logits = lax.dot_general(wr_ref[...], m.astype(BF16), (((1,), (1,)), ((), ())),
                             preferred_element_type=F32)
    row = lax.broadcasted_iota(jnp.int32, (LANES, tm), 0)
    row_f = row.astype(F32)
    bias = br_ref[...]
    is_g = (row >= N_EXPERTS) & (row < N_EXPERTS + N_GROUPS)
    neg = jnp.float32(-jnp.inf)

    gl = jnp.where(is_g, logits, neg)
    ge = jnp.where(is_g, jnp.exp(gl - _col_max(gl)), 0.0)
    g_prob = ge / _col_sum(ge)
    g_score = jnp.where(is_g, g_prob + bias, neg)
    g_row = _first_row(g_score == _col_max(g_score), row_f)
    g_w = _col_sum(jnp.where(row == g_row, g_prob, 0.0))
    g_idx = g_row - N_EXPERTS

    in_g = (row // EXPERTS_PER_GROUP) == g_idx
    el = jnp.where(in_g, logits, neg)
    ee = jnp.where(in_g, jnp.exp(el - _col_max(el)), 0.0)
    e_prob = ee / _col_sum(ee)
    e_score = jnp.where(in_g, e_prob + bias, neg)
    id1 = _first_row(e_score == _col_max(e_score), row_f)
    e_score2 = jnp.where(row == id1, neg, e_score)
    id2 = _first_row(e_score2 == _col_max(e_score2), row_f)
    p1 = _col_sum(jnp.where(row == id1, e_prob, 0.0))
    p2 = _col_sum(jnp.where(row == id2, e_prob, 0.0))
    psum = p1 + p2
    w1 = g_w * (p1 / psum)
    w2 = g_w * (p2 / psum)

    oh = ((row == id1) | (row == id2 + N_EXPERTS)).astype(BF16)
    t_in = lax.broadcasted_iota(jnp.int32, (tm, tm), 0)
    t_out = lax.broadcasted_iota(jnp.int32, (tm, tm), 1)
    tri = (t_in < t_out).astype(BF16)
    prefix = _dot(oh, tri)
    tot = jnp.sum(oh.astype(F32), axis=1, keepdims=True)
    tot_sw = jnp.concatenate([tot[N_EXPERTS:], tot[:N_EXPERTS]], axis=0)
    row1 = lax.broadcasted_iota(jnp.int32, (LANES, 1), 0)
    carry = carry_sc[...]
    base = carry + jnp.where(row1 >= N_EXPERTS, tot_sw, 0.0)
    rk = oh.astype(F32) * (base + prefix)
    rank1 = _col_sum(jnp.where(row < N_EXPERTS, rk, 0.0))
    rank2 = _col_sum(jnp.where(row >= N_EXPERTS, rk, 0.0))
    carry_new = carry + tot + tot_sw
    carry_sc[...] = carry_new
    cnt_ref[...] = carry_new.astype(jnp.int32)

    row8 = lax.broadcasted_iota(jnp.int32, (SUBLANES, tm), 0)
    rt = jnp.where(row8 == 0, id1, jnp.where(row8 == 1, id2, 0))
    rt = jnp.where(row8 == 2, rank1.astype(jnp.int32), rt)
    rt_ref[...] = jnp.where(row8 == 3, rank2.astype(jnp.int32), rt)
    wt = jnp.where(row == 0, w1, jnp.where(row == 1, w2, 0.0))
    rw_ref[...] = wt.T


def _out_route(o2, ma, sgb, x2, wo_b, wout_b, g_moe, wr_b, br):
    N = x2.shape[0]
    tm = TM_OUT
    row = lambda i: (i, 0)
    const = lambda i: (0, 0)
    return pl.pallas_call(
        _out_route_kernel,
        grid=(N // tm,),
        in_specs=[
            pl.BlockSpec((tm, D_MODEL), row),
            pl.BlockSpec((tm, D_MODEL), row),
            pl.BlockSpec((tm, D_MODEL), row),
            pl.BlockSpec((tm, D_MODEL), row),
            pl.BlockSpec((D_MODEL, D_MODEL), const),
            pl.BlockSpec((D_MODEL, D_MODEL), const),
            pl.BlockSpec((1, D_MODEL), const),
            pl.BlockSpec((LANES, D_MODEL), const),
            pl.BlockSpec((LANES, tm), const),
        ],
        out_specs=[
            pl.BlockSpec((tm, D_MODEL), row),
            pl.BlockSpec((tm * SUBLANES, LANES), row),
            pl.BlockSpec((SUBLANES, tm), lambda i: (0, i)),
            pl.BlockSpec((tm, LANES), row),
            pl.BlockSpec((LANES, 1), const),
        ],
        out_shape=[
            jax.ShapeDtypeStruct((N, D_MODEL), F32),
            jax.ShapeDtypeStruct((N * SUBLANES, LANES), F32),
            jax.ShapeDtypeStruct((SUBLANES, N), jnp.int32),
            jax.ShapeDtypeStruct((N, LANES), F32),
            jax.ShapeDtypeStruct((LANES, 1), jnp.int32),
        ],
        scratch_shapes=[pltpu.VMEM((LANES, 1), F32)],
        compiler_params=pltpu.CompilerParams(
            dimension_semantics=("arbitrary",), vmem_limit_bytes=VMEM_LIMIT),
        name="out_route",
    )(o2, ma, sgb, x2, wo_b, wout_b, g_moe, wr_b, br)


def _row_tile(ref2, row8):
    return ref2.at[pl.ds(pl.multiple_of(row8, SUBLANES), SUBLANES)]


def _dispatch_kernel(pad_end_ref, padded_ref, dest_ref, m_ref, buf_ref, zero_sc, sem, zsem):
    i = pl.program_id(0)
    tm = TM_ROWS
    blk8 = MOE_BLK * SUBLANES

    @pl.when(i == 0)
    def _():
        zero_sc[...] = jnp.zeros(zero_sc.shape, F32)

        def zero_copy(e):
            start = pl.multiple_of((pad_end_ref[e] - MOE_BLK) * SUBLANES, blk8)
            return pltpu.make_async_copy(zero_sc, buf_ref.at[pl.ds(start, blk8)], zsem)

        def start(e, carry):
            @pl.when(padded_ref[e] > 0)
            def _():
                zero_copy(e).start()
            return carry

        def wait(e, carry):
            @pl.when(padded_ref[e] > 0)
            def _():
                zero_copy(e).wait()
            return carry

        lax.fori_loop(0, N_EXPERTS, start, 0)
        lax.fori_loop(0, N_EXPERTS, wait, 0)

        def tail_copy(b):
            return pltpu.make_async_copy(
                zero_sc, buf_ref.at[pl.ds(pl.multiple_of(b * blk8, blk8), blk8)], zsem)

        nused = pad_end_ref[N_EXPERTS - 1] // MOE_BLK
        nblk = buf_ref.shape[0] // blk8
        lax.fori_loop(nused, nblk, lambda b, c: (tail_copy(b).start(), c)[1], 0)
        lax.fori_loop(nused, nblk, lambda b, c: (tail_copy(b).wait(), c)[1], 0)

    def issue(c, carry):
        for u in range(DMA_UNROLL):
            r = c * DMA_UNROLL + u
            src = _row_tile(m_ref, r * SUBLANES)
            pltpu.make_async_copy(src, _row_tile(buf_ref, dest_ref[0, 0, r]), sem).start(priority=0)
            pltpu.make_async_copy(src, _row_tile(buf_ref, dest_ref[0, 0, tm + r]), sem).start(priority=1)
        return carry

    lax.fori_loop(0, tm // DMA_UNROLL, issue, 0)
    for _ in range(2):
        pltpu.make_async_copy(m_ref, buf_ref.at[pl.ds(0, tm * SUBLANES)], sem).wait()


def _dispatch(pad_end, padded, dest8_tiles, m2, P):
    tm = TM_ROWS
    N = m2.shape[0] // SUBLANES
    grid_spec = pltpu.PrefetchScalarGridSpec(
        num_scalar_prefetch=2,
        grid=(N // tm,),
        in_specs=[
            pl.BlockSpec((1, 1, 2 * tm), lambda i, pe, pd: (i, 0, 0), memory_space=pltpu.SMEM),
            pl.BlockSpec((tm * SUBLANES, LANES), lambda i, pe, pd: (i, 0)),
        ],
        out_specs=pl.BlockSpec(memory_space=pl.ANY),
        scratch_shapes=[
            pltpu.VMEM((MOE_BLK * SUBLANES, LANES), F32),
            pltpu.SemaphoreType.DMA(()),
            pltpu.SemaphoreType.DMA(()),
        ],
    )
    return pl.pallas_call(
        _dispatch_kernel,
        grid_spec=grid_spec,
        out_shape=jax.ShapeDtypeStruct((P * SUBLANES, LANES), F32),
        compiler_params=pltpu.CompilerParams(dimension_semantics=("arbitrary",)),
        name="dispatch",
    )(pad_end, padded, dest8_tiles, m2)


def _expert_kernel(blk_e_ref, run_blocks_ref, nused_ref, x_hbm, wg_hbm, wu_hbm, wd_hbm, y_hbm,
                   xbuf, ybuf, wg_raw, wu_raw, wd_raw, wgu_sc, wd_sc, xsem, ysem, wsem):
    nused = nused_ref[0]
    blk8 = MOE_BLK * SUBLANES
    nblk = y_hbm.shape[0] // blk8
    nxbuf = X_AHEAD + 1

    def block_rows(ref, b):
        return ref.at[pl.ds(pl.multiple_of(b * blk8, blk8), blk8)]

    def x_copy(b):
        slot = b % nxbuf
        return pltpu.make_async_copy(block_rows(x_hbm, b), xbuf.at[slot], xsem.at[slot])

    def y_copy(b, slot):
        return pltpu.make_async_copy(ybuf.at[slot], block_rows(y_hbm, b), ysem.at[slot])

    def w_copies(e, slot):
        return (pltpu.make_async_copy(wg_hbm.at[e], wg_raw.at[slot], wsem.at[slot]),
                pltpu.make_async_copy(wu_hbm.at[e], wu_raw.at[slot], wsem.at[slot]),
                pltpu.make_async_copy(wd_hbm.at[e], wd_raw.at[slot], wsem.at[slot]))

    for b in range(X_AHEAD):
        @pl.when(b < nused)
        def _():
            x_copy(b).start()
    for c in w_copies(blk_e_ref[0], 0):
        c.start()

    def body(i, run):
        e = blk_e_ref[i]
        new_expert = (i == 0) | (e != blk_e_ref[jnp.maximum(i - 1, 0)])
        run = run + new_expert.astype(jnp.int32)
        wslot = run % 2

        @pl.when(new_expert)
        def _():
            for c in w_copies(e, wslot):
                c.wait()
            wgu_sc[:, 0:D_EXPERT] = wg_raw[wslot].astype(BF16)
            wgu_sc[:, D_EXPERT:2 * D_EXPERT] = wu_raw[wslot].astype(BF16)
            wd_sc[...] = wd_raw[wslot].astype(BF16)
            nxt = i + run_blocks_ref[e]

            @pl.when(nxt < nused)
            def _():
                for c in w_copies(blk_e_ref[nxt], 1 - wslot):
                    c.start()

        @pl.when(i + X_AHEAD < nused)
        def _():
            x_copy(i + X_AHEAD).start()

        yslot = i % 2
        x_copy(i).wait()

        @pl.when(i >= 2)
        def _():
            y_copy(i - 2, yslot).wait()

        xb = _load_rows(xbuf.at[i % nxbuf], MOE_BLK).astype(BF16)
        h = _dot(xb, wgu_sc[...])
        hdn = (jax.nn.silu(h[:, 0:D_EXPERT]) * h[:, D_EXPERT:2 * D_EXPERT]).astype(BF16)
        _store_rows(ybuf.at[yslot], _dot(hdn, wd_sc[...]))
        y_copy(i, yslot).start()
        return run

    lax.fori_loop(0, nused, body, jnp.int32(-1))

    @pl.when(nused >= 2)
    def _():
        y_copy(nused - 2, nused % 2).wait()

    y_copy(nused - 1, (nused - 1) % 2).wait()

    ybuf[0] = jnp.zeros(ybuf.shape[1:], F32)
    lax.fori_loop(nused, nblk, lambda b, c: (y_copy(b, 0).start(), c)[1], 0)
    lax.fori_loop(nused, nblk, lambda b, c: (y_copy(b, 0).wait(), c)[1], 0)


def _experts(blk_e, run_blocks, nused, buf2, wg, wu, wd):
    blk_rows = MOE_BLK * SUBLANES
    any_spec = pl.BlockSpec(memory_space=pl.ANY)
    grid_spec = pltpu.PrefetchScalarGridSpec(
        num_scalar_prefetch=3,
        grid=(1,),
        in_specs=[any_spec, any_spec, any_spec, any_spec],
        out_specs=any_spec,
        scratch_shapes=[
            pltpu.VMEM((X_AHEAD + 1, blk_rows, LANES), F32),
            pltpu.VMEM((2, blk_rows, LANES), F32),
            pltpu.VMEM((2, D_MODEL, D_EXPERT), F32),
            pltpu.VMEM((2, D_MODEL, D_EXPERT), F32),
            pltpu.VMEM((2, D_EXPERT, D_MODEL), F32),
            pltpu.VMEM((D_MODEL, 2 * D_EXPERT), BF16),
            pltpu.VMEM((D_EXPERT, D_MODEL), BF16),
            pltpu.SemaphoreType.DMA((X_AHEAD + 1,)),
            pltpu.SemaphoreType.DMA((2,)),
            pltpu.SemaphoreType.DMA((2,)),
        ],
    )
    return pl.pallas_call(
        _expert_kernel,
        grid_spec=grid_spec,
        out_shape=jax.ShapeDtypeStruct(buf2.shape, F32),
        compiler_params=pltpu.CompilerParams(
            dimension_semantics=("arbitrary",), vmem_limit_bytes=VMEM_LIMIT),
        name="experts",
    )(blk_e, run_blocks, nused, buf2, wg, wu, wd)


def _final_kernel(dcur_ref, dnext_ref, h1_ref, rw_ref, p_ref, gple_ref, wpg_ref, wpp_ref, gfin_ref,
                  ys_ref, out_ref, ybuf_a, ybuf_b, sem):
    g = pl.program_id(0)
    ng = pl.num_programs(0)
    tm = TM_ROWS
    bufs = (ybuf_a, ybuf_b)

    def row_copy(d_ref, off, r, which):
        return pltpu.make_async_copy(_row_tile(ys_ref, d_ref[0, 0, off + r]),
                                     _row_tile(bufs[which], r * SUBLANES), sem.at[which])

    def issue(d_ref, off, which):
        for r in range(2 * tm):
            row_copy(d_ref, off, r, which).start(priority=r % 2)

    def wait(which):
        pltpu.make_async_copy(ys_ref.at[pl.ds(0, 2 * tm * SUBLANES)], bufs[which],
                              sem.at[which]).wait()

    def compute(which, rows):
        y0 = _load_rows(bufs[which], tm)
        y1 = _load_rows(bufs[which], tm, first_row=tm)
        rw = rw_ref[rows, :]
        h2 = h1_ref[rows, :] + (y0 * rw[:, 0:1] + y1 * rw[:, 1:2])
        n3 = (_rms(h2) * gple_ref[...]).astype(BF16)
        gate = jax.nn.sigmoid(_dot(n3, wpg_ref[...]))
        pp = _dot(p_ref[rows, :].astype(BF16), wpp_ref[...])
        h3 = h2 + gate * pp
        out_ref[rows, :] = _rms(h3) * gfin_ref[...]

    @pl.when(g == 0)
    def _():
        def body(c, carry):
            for u in range(DMA_UNROLL):
                row_copy(dcur_ref, 0, c * DMA_UNROLL + u, 0).start()
            return carry

        lax.fori_loop(0, 2 * tm // DMA_UNROLL, body, 0)

    wait(0)
    issue(dcur_ref, 2 * tm, 1)
    compute(0, slice(0, tm))
    wait(1)
    issue(dnext_ref, 0, 0)
    compute(1, slice(tm, 2 * tm))

    @pl.when(g == ng - 1)
    def _():
        wait(0)


def _final(dest8_pairs, h1, rw, p2, g_ple, wpg_b, wpp_b, g_final, ys2):
    N = h1.shape[0]
    tm = TM_ROWS
    ng = N // (2 * tm)
    row = lambda i: (i, 0)
    const = lambda i: (0, 0)
    return pl.pallas_call(
        _final_kernel,
        grid=(ng,),
        in_specs=[
            pl.BlockSpec((1, 1, 4 * tm), lambda i: (i, 0, 0), memory_space=pltpu.SMEM),
            pl.BlockSpec((1, 1, 4 * tm), lambda i: (jnp.minimum(i + 1, ng - 1), 0, 0),
                         memory_space=pltpu.SMEM),
            pl.BlockSpec((2 * tm, D_MODEL), row),
            pl.BlockSpec((2 * tm, LANES), row),
            pl.BlockSpec((2 * tm, PLE_DIM), row),
            pl.BlockSpec((1, D_MODEL), const),
            pl.BlockSpec((D_MODEL, D_MODEL), const),
            pl.BlockSpec((PLE_DIM, D_MODEL), const),
            pl.BlockSpec((1, D_MODEL), const),
            pl.BlockSpec(memory_space=pl.ANY),
        ],
        out_specs=pl.BlockSpec((2 * tm, D_MODEL), row),
        out_shape=jax.ShapeDtypeStruct((N, D_MODEL), F32),
        scratch_shapes=[
            pltpu.VMEM((2 * tm * SUBLANES, LANES), F32),
            pltpu.VMEM((2 * tm * SUBLANES, LANES), F32),
            pltpu.SemaphoreType.DMA((2,)),
        ],
        compiler_params=pltpu.CompilerParams(
            dimension_semantics=("arbitrary",), vmem_limit_bytes=VMEM_LIMIT),
        name="final",
    )(dest8_pairs, dest8_pairs, h1, rw, p2, g_ple, wpg_b, wpp_b, g_final, ys2)


def _rope_tables(S):
    inv_freq = ROPE_THETA ** (-jnp.arange(0, QK_ROPE_DIM, 2, dtype=F32) / QK_ROPE_DIM)
    ang = jnp.arange(S, dtype=F32)[:, None] * inv_freq[None, :]
    cos, sin = jnp.cos(ang), jnp.sin(ang)
    z = jnp.zeros_like(cos)
    rc = jnp.concatenate([cos, cos, z, z], axis=1)
    rs1 = jnp.concatenate([-sin, z, z, z], axis=1)
    rs2 = jnp.concatenate([z, sin, z, z], axis=1)
    return rc, rs1, rs2


def _layer(h, p_l, g_mix, w_in, g_gv, w_spatial, b_spatial, w_gproj, g_cq, w_uq, g_ckv, w_ukv,
           w_mla_o, w_out, g_moe, w_router_g, b_router_g, w_router_e, b_router_e,
           w_e_gate, w_e_up, w_e_down, g_ple, w_ple_gate, w_ple_proj, g_out):
    B, S, D = h.shape
    N = B * S
    x2 = h.reshape(N, D)

    cu, cv, ccq, cckv, ckr, cga = (GMLP_WIDTH, 2 * GMLP_WIDTH, 2 * GMLP_WIDTH + Q_LORA,
                                   2 * GMLP_WIDTH + Q_LORA + KV_LORA,
                                   2 * GMLP_WIDTH + Q_LORA + KV_LORA + QK_ROPE_DIM,
                                   2 * GMLP_WIDTH + Q_LORA + KV_LORA + QK_ROPE_DIM + D_MODEL)
    w_in_p = jnp.concatenate(
        [w_in[:, :cckv], w_in[:, ckr:], w_in[:, cckv:ckr],
         jnp.zeros((D, LANES - QK_ROPE_DIM), w_in.dtype)], axis=1).astype(BF16)
    w_uq_h = w_uq.reshape(Q_LORA, MLA_HEADS, QK_NOPE_DIM + QK_ROPE_DIM)
    w_uq_p = jnp.concatenate(
        [w_uq_h, jnp.zeros((Q_LORA, MLA_HEADS, QK_PAD - QK_NOPE_DIM - QK_ROPE_DIM), w_uq.dtype)],
        axis=2).reshape(Q_LORA, MLA_HEADS * QK_PAD).astype(BF16)
    rc, rs1, rs2 = _rope_tables(S)

    u, v, sga, sgb, q, k, vv = _inproj(
        x2, g_mix[None], w_in_p, g_gv[None], g_cq[None], g_ckv[None], w_uq_p, w_ukv.astype(BF16),
        rc, rs1, rs2, B, S)
    ma = _gmlp(u, v, sga, w_spatial.astype(BF16), b_spatial.T, w_gproj.astype(BF16))
    o = _attention(q, k, vv)

    wr = jnp.concatenate(
        [w_router_e.T, w_router_g.T, jnp.zeros((LANES - N_EXPERTS - N_GROUPS, D), w_router_e.dtype)],
        axis=0).astype(BF16)
    br = jnp.concatenate(
        [b_router_e.reshape(-1), b_router_g, jnp.zeros((LANES - N_EXPERTS - N_GROUPS,), F32)])
    br = jnp.broadcast_to(br[:, None], (LANES, TM_OUT))
    h1, m, rt, rw, cnt = _out_route(
        o.reshape(N, D), ma, sgb, x2, w_mla_o.astype(BF16), w_out.astype(BF16), g_moe[None], wr, br)

    counts = cnt[:N_EXPERTS, 0]
    padded = (counts + MOE_BLK - 1) // MOE_BLK * MOE_BLK
    pad_end = jnp.cumsum(padded)
    pad_start = pad_end - padded
    P = 2 * N + N_EXPERTS * MOE_BLK
    nblk = P // MOE_BLK
    blk_start = jnp.arange(nblk, dtype=jnp.int32) * MOE_BLK
    blk_e = jnp.minimum(
        jnp.sum((pad_end[None, :] <= blk_start[:, None]).astype(jnp.int32), axis=1),
        N_EXPERTS - 1).astype(jnp.int32)
    nused = (pad_end[-1:] // MOE_BLK).astype(jnp.int32)
    experts = jnp.arange(N_EXPERTS, dtype=jnp.int32)[:, None, None]
    start_of = jnp.sum(jnp.where(rt[None, 0:2] == experts, pad_start[:, None, None], 0), axis=0)
    dest8 = (start_of + rt[2:4]).astype(jnp.int32) * SUBLANES
    nt = N // TM_ROWS
    dest8_tiles = dest8.reshape(2, nt, TM_ROWS).transpose(1, 0, 2).reshape(nt, 1, 2 * TM_ROWS)

    buf = _dispatch(pad_end.astype(jnp.int32), padded.astype(jnp.int32), dest8_tiles, m, P)
    ys = _experts(blk_e, (padded // MOE_BLK).astype(jnp.int32), nused, buf, w_e_gate, w_e_up, w_e_down)
    out = _final(dest8_tiles.reshape(nt // 2, 1, 4 * TM_ROWS), h1, rw, p_l.reshape(N, PLE_DIM), g_ple[None],
                 w_ple_gate.astype(BF16), w_ple_proj.astype(BF16), g_out[None], ys)
    return out.reshape(B, S, D)


def kernel(x, p, g_mix, w_in, g_gv, w_spatial, b_spatial, w_gproj, g_cq, w_uq, g_ckv, w_ukv, w_mla_o,
           w_out, g_moe, w_router_g, b_router_g, w_router_e, b_router_e, w_e_gate, w_e_up, w_e_down,
           g_ple, w_ple_gate, w_ple_proj, g_final):
    depth = p.shape[0]
    assert depth == 1, "the final rmsnorm is fused into the single layer's last kernel"
    i = 0
    return _layer(x, p[i], g_mix[i], w_in[i], g_gv[i], w_spatial[i], b_spatial[i], w_gproj[i], g_cq[i],
                  w_uq[i], g_ckv[i], w_ukv[i], w_mla_o[i], w_out[i], g_moe[i], w_router_g[i],
                  b_router_g[i], w_router_e[i], b_router_e[i], w_e_gate[i], w_e_up[i], w_e_down[i],
                  g_ple[i], w_ple_gate[i], w_ple_proj[i], g_final)
```

```python
import functools

import jax
import jax.numpy as jnp
from jax import lax
from jax.experimental import pallas as pl
from jax.experimental.pallas import tpu as pltpu

F32 = jnp.float32
BF16 = jnp.bfloat16

D_MODEL = 1024
CHUNK = 64
PLE_DIM = 256
GMLP_BLOCK = 128
GMLP_GROUPS = 12
GMLP_WIDTH = 1536
MLA_HEADS = 8
QK_NOPE_DIM = 128
QK_ROPE_DIM = 64
V_HEAD_DIM = 128
Q_LORA = 384
KV_LORA = 256
ROPE_THETA = 10000.0
N_GROUPS = 8
EXPERTS_PER_GROUP = 8
N_EXPERTS = 64
D_EXPERT = 256
EPS = 1e-6
LOG2E = 1.4426950408889634

LANES = 128
SUBLANES = 8
QK_PAD = 256
V_PAD = 256
VMEM_LIMIT = 56 * 1024 * 1024

C_U = 0
C_V = C_U + GMLP_WIDTH
C_CQ = C_V + GMLP_WIDTH
C_CKV = C_CQ + Q_LORA
C_GA = C_CKV + KV_LORA
C_GB = C_GA + D_MODEL
C_KR = C_GB + D_MODEL
C_END = C_KR + LANES

TM_IN = 512
TM_GMLP = 512
TQ = 512
HEADS_PER_STEP = 4
TM_OUT = 512
TM_ROWS = 512
MOE_BLK = 256
X_AHEAD = 6
Y_BUFS = 4
DMA_UNROLL = 8


def _rms(x):
    return x * lax.rsqrt(jnp.mean(x * x, axis=-1, keepdims=True) + EPS)


def _dot(a, b):
    return jnp.dot(a, b, preferred_element_type=F32)


def _store_rows(ref2, x):
    rows = x.shape[0]
    for j in range(SUBLANES):
        ref2[pl.ds(j, rows, stride=SUBLANES), :] = x[:, j * LANES:(j + 1) * LANES]


def _load_rows(ref2, rows, first_row=0):
    return jnp.concatenate(
        [ref2[pl.ds(first_row * SUBLANES + j, rows, stride=SUBLANES), :] for j in range(SUBLANES)],
        axis=1)


def _rope128(t, rc, rs1, rs2):
    r1 = pltpu.roll(t, 96, axis=1)
    r2 = pltpu.roll(t, 32, axis=1)
    return t * rc + r1 * rs1 + r2 * rs2


def _inproj_kernel(x_ref, gmix_ref, win_ref, ggv_ref, gcq_ref, gckv_ref, wuq_ref, wukv_ref,
                   rc_ref, rs1_ref, rs2_ref,
                   u_ref, v_ref, sga_ref, sgb_ref, q_ref, k_ref, vv_ref):
    x = x_ref[...]
    ab = (_rms(x) * gmix_ref[...]).astype(BF16)

    def proj(c0, c1):
        return _dot(ab, win_ref[:, c0:c1])

    u_ref[...] = jax.nn.gelu(proj(C_U, C_V)).astype(BF16)

    zv = jax.nn.gelu(proj(C_V, C_CQ))
    xc = zv - jnp.mean(zv, axis=-1, keepdims=True)
    vln = xc * lax.rsqrt(jnp.mean(xc * xc, axis=-1, keepdims=True) + EPS)
    v_ref[...] = (vln * ggv_ref[...]).astype(BF16)

    sga_ref[...] = jax.nn.sigmoid(proj(C_GA, C_GB)).astype(BF16)
    sgb_ref[...] = jax.nn.sigmoid(proj(C_GB, C_KR)).astype(BF16)

    rc = rc_ref[...]
    rs1 = rs1_ref[...]
    rs2 = rs2_ref[...]
    kpe = _rope128(proj(C_KR, C_END), rc, rs1, rs2).astype(BF16)

    cqn = (_rms(proj(C_CQ, C_CKV)) * gcq_ref[...]).astype(BF16)
    ckvn = (_rms(proj(C_CKV, C_GA)) * gckv_ref[...]).astype(BF16)
    scale = (QK_NOPE_DIM + QK_ROPE_DIM) ** -0.5 * LOG2E
    for h in range(MLA_HEADS):
        qh = _dot(cqn, wuq_ref[:, h * QK_PAD:(h + 1) * QK_PAD])
        q_ref[0, h, :, 0:LANES] = (qh[:, 0:LANES] * scale).astype(BF16)
        q_ref[0, h, :, LANES:QK_PAD] = (_rope128(qh[:, LANES:QK_PAD], rc, rs1, rs2) * scale).astype(BF16)
        kvh = _dot(ckvn, wukv_ref[:, h * 256:(h + 1) * 256])
        k_ref[0, h, :, 0:LANES] = kvh[:, 0:LANES].astype(BF16)
        k_ref[0, h, :, LANES:QK_PAD] = kpe
        vv_ref[0, h] = kvh[:, LANES:256].astype(BF16)


def _inproj(x2, g_mix, w_in_p, g_gv, g_cq, g_ckv, w_uq_p, w_ukv_b, rc, rs1, rs2, B, S):
    N = x2.shape[0]
    tm = TM_IN
    spt = S // tm
    row = lambda i: (i, 0)
    const = lambda i: (0, 0)
    pos = lambda i: (i % spt, 0)
    head = lambda i: (i // spt, 0, i % spt, 0)
    return pl.pallas_call(
        _inproj_kernel,
        grid=(N // tm,),
        in_specs=[
            pl.BlockSpec((tm, D_MODEL), row),
            pl.BlockSpec((1, D_MODEL), const),
            pl.BlockSpec((D_MODEL, C_END), const, pipeline_mode=pl.Buffered(1)),
            pl.BlockSpec((1, GMLP_WIDTH), const),
            pl.BlockSpec((1, Q_LORA), const),
            pl.BlockSpec((1, KV_LORA), const),
            pl.BlockSpec((Q_LORA, MLA_HEADS * QK_PAD), const),
            pl.BlockSpec((KV_LORA, MLA_HEADS * 256), const),
            pl.BlockSpec((tm, LANES), pos),
            pl.BlockSpec((tm, LANES), pos),
            pl.BlockSpec((tm, LANES), pos),
        ],
        out_specs=[
            pl.BlockSpec((tm, GMLP_WIDTH), row),
            pl.BlockSpec((tm, GMLP_WIDTH), row),
            pl.BlockSpec((tm, D_MODEL), row),
            pl.BlockSpec((tm, D_MODEL), row),
            pl.BlockSpec((1, MLA_HEADS, tm, QK_PAD), head),
            pl.BlockSpec((1, MLA_HEADS, tm, QK_PAD), head),
            pl.BlockSpec((1, MLA_HEADS, tm, V_HEAD_DIM), head),
        ],
        out_shape=[
            jax.ShapeDtypeStruct((N, GMLP_WIDTH), BF16),
            jax.ShapeDtypeStruct((N, GMLP_WIDTH), BF16),
            jax.ShapeDtypeStruct((N, D_MODEL), BF16),
            jax.ShapeDtypeStruct((N, D_MODEL), BF16),
            jax.ShapeDtypeStruct((B, MLA_HEADS, S, QK_PAD), BF16),
            jax.ShapeDtypeStruct((B, MLA_HEADS, S, QK_PAD), BF16),
            jax.ShapeDtypeStruct((B, MLA_HEADS, S, V_HEAD_DIM), BF16),
        ],
        compiler_params=pltpu.CompilerParams(
            dimension_semantics=("arbitrary",), vmem_limit_bytes=VMEM_LIMIT),
        name="inproj",
    )(x2, g_mix, w_in_p, g_gv, g_cq, g_ckv, w_uq_p, w_ukv_b, rc, rs1, rs2)


def _gmlp_kernel(u_ref, v_ref, sga_ref, wsp_ref, bsp_ref, wproj_ref, ma_ref, y_sc):
    nb = TM_GMLP // GMLP_BLOCK
    t_out = lax.broadcasted_iota(jnp.int32, (GMLP_BLOCK, GMLP_BLOCK), 0)
    s_in = lax.broadcasted_iota(jnp.int32, (GMLP_BLOCK, GMLP_BLOCK), 1)
    mask = (s_in // CHUNK) <= (t_out // CHUNK)
    for g in range(GMLP_GROUPS):
        c0 = g * LANES
        w = jnp.where(mask, wsp_ref[g], jnp.zeros((), BF16))
        rhs = jnp.concatenate(
            [v_ref[r * GMLP_BLOCK:(r + 1) * GMLP_BLOCK, c0:c0 + LANES] for r in range(nb)], axis=1)
        sv = _dot(w, rhs) + bsp_ref[:, g:g + 1]
        for r in range(nb):
            rows = slice(r * GMLP_BLOCK, (r + 1) * GMLP_BLOCK)
            ub = u_ref[rows, c0:c0 + LANES].astype(F32)
            y_sc[rows, c0:c0 + LANES] = (ub * sv[:, r * LANES:(r + 1) * LANES]).astype(BF16)
    ya = _dot(y_sc[...], wproj_ref[...])
    ma_ref[...] = (sga_ref[...].astype(F32) * ya).astype(BF16)


def _gmlp(u, v, sga, wsp_b, bsp_t, wproj_b):
    N = u.shape[0]
    tm = TM_GMLP
    row = lambda i: (i, 0)
    return pl.pallas_call(
        _gmlp_kernel,
        grid=(N // tm,),
        in_specs=[
            pl.BlockSpec((tm, GMLP_WIDTH), row),
            pl.BlockSpec((tm, GMLP_WIDTH), row),
            pl.BlockSpec((tm, D_MODEL), row),
            pl.BlockSpec((GMLP_GROUPS, GMLP_BLOCK, GMLP_BLOCK), lambda i: (0, 0, 0)),
            pl.BlockSpec((GMLP_BLOCK, GMLP_GROUPS), lambda i: (0, 0)),
            pl.BlockSpec((GMLP_WIDTH, D_MODEL), lambda i: (0, 0)),
        ],
        out_specs=pl.BlockSpec((tm, D_MODEL), row),
        out_shape=jax.ShapeDtypeStruct((N, D_MODEL), BF16),
        scratch_shapes=[pltpu.VMEM((tm, GMLP_WIDTH), BF16)],
        compiler_params=pltpu.CompilerParams(
            dimension_semantics=("arbitrary",), vmem_limit_bytes=VMEM_LIMIT),
        name="gmlp",
    )(u, v, sga, wsp_b, bsp_t, wproj_b)


NEG_BIG = -1e30


def _attn_kernel(q_ref, k_ref, v_ref, o_ref, m_sc, acc_sc):
    qi = pl.program_id(2)
    m_sc[...] = jnp.full(m_sc.shape, NEG_BIG, F32)
    acc_sc[...] = jnp.zeros(acc_sc.shape, F32)
    ones_col = (lax.broadcasted_iota(jnp.int32, (TQ, V_PAD - V_HEAD_DIM), 1) == 0).astype(BF16)

    def chunk_mask(r0, nq, nk):
        qc = (lax.broadcasted_iota(jnp.int32, (nq, nk), 0) + r0) // CHUNK
        kc = lax.broadcasted_iota(jnp.int32, (nq, nk), 1) // CHUNK
        return kc <= qc

    def update(hh, r0, nq, start, nk, mask):
        rows = pl.ds(r0, nq)
        kb = k_ref[0, hh, pl.ds(start, nk), :]
        vb = jnp.concatenate([v_ref[0, hh, pl.ds(start, nk), :], ones_col[0:nk]], axis=1)
        s = lax.dot_general(q_ref[0, hh, rows, :], kb, (((1,), (1,)), ((), ())),
                            preferred_element_type=F32)
        if mask is not None:
            s = jnp.where(mask, s, NEG_BIG)
        tiles = [s[:, c * LANES:(c + 1) * LANES] for c in range(nk // LANES)]
        tile_max = functools.reduce(jnp.maximum, tiles)
        m_prev = m_sc[hh, rows, :]
        m_new = jnp.maximum(m_prev, jnp.max(tile_max, axis=-1, keepdims=True))
        alpha = jnp.exp2(m_prev - m_new)
        p = jnp.concatenate([jnp.exp2(t - m_new).astype(BF16) for t in tiles], axis=1)
        alpha2 = jnp.concatenate([alpha] * (V_PAD // LANES), axis=1)
        acc_sc[hh, rows, :] = alpha2 * acc_sc[hh, rows, :] + _dot(p, vb)
        m_sc[hh, rows, :] = m_new

    def body(j, carry):
        start = pl.multiple_of(j * TQ, TQ)
        for hh in range(HEADS_PER_STEP):
            update(hh, 0, TQ, start, TQ, None)
        return carry

    lax.fori_loop(0, qi, body, 0)
    start = pl.multiple_of(qi * TQ, TQ)
    mask = chunk_mask(0, TQ, TQ)
    for hh in range(HEADS_PER_STEP):
        update(hh, 0, TQ, start, TQ, mask)
    for hh in range(HEADS_PER_STEP):
        l = acc_sc[hh, :, V_HEAD_DIM:V_HEAD_DIM + 1]
        o_ref[0, :, hh * V_HEAD_DIM:(hh + 1) * V_HEAD_DIM] = (
            acc_sc[hh, :, 0:V_HEAD_DIM] / l).astype(BF16)


def _attention(q, k, v):
    B, H, S, _ = q.shape
    hps = HEADS_PER_STEP
    return pl.pallas_call(
        _attn_kernel,
        grid=(B, H // hps, S // TQ),
        in_specs=[
            pl.BlockSpec((1, hps, TQ, QK_PAD), lambda b, h, i: (b, h, i, 0)),
            pl.BlockSpec((1, hps, S, QK_PAD), lambda b, h, i: (b, h, 0, 0)),
            pl.BlockSpec((1, hps, S, V_HEAD_DIM), lambda b, h, i: (b, h, 0, 0)),
        ],
        out_specs=pl.BlockSpec((1, TQ, hps * V_HEAD_DIM), lambda b, h, i: (b, i, h)),
        out_shape=jax.ShapeDtypeStruct((B, S, H * V_HEAD_DIM), BF16),
        scratch_shapes=[
            pltpu.VMEM((hps, TQ, LANES), F32),
            pltpu.VMEM((hps, TQ, V_PAD), F32),
        ],
        compiler_params=pltpu.CompilerParams(
            dimension_semantics=("arbitrary", "arbitrary", "arbitrary"),
            vmem_limit_bytes=VMEM_LIMIT),
        name="attention",
    )(q, k, v)


def _col_sum(x):
    return jnp.sum(x, axis=0, keepdims=True)


def _col_max(x):
    return jnp.max(x, axis=0, keepdims=True)


def _first_row(hit, row_f):
    return jnp.min(jnp.where(hit, row_f, float(LANES)), axis=0, keepdims=True).astype(jnp.int32)


def _out_route_kernel(o_ref, ma_ref, sgb_ref, x_ref, wo_ref, wout_ref, gmoe_ref, wr_ref, br_ref,
                      h1_ref, m_ref, rt_ref, rw_ref, cnt_ref, carry_sc):
    i = pl.program_id(0)
    tm = TM_OUT

    @pl.when(i == 0)
    def _():
        carry_sc[...] = jnp.zeros(carry_sc.shape, F32)

    yb = _dot(o_ref[...], wo_ref[...])
    merged = ma_ref[...].astype(F32) + sgb_ref[...].astype(F32) * yb
    h1 = x_ref[...] + _dot(merged.astype(BF16), wout_ref[...])
    h1_ref[...] = h1
    m = _rms(h1) * gmoe_ref[...]
    _store_rows(m_ref, m)

    logits = lax.dot_general(wr_ref[...], m.astype(BF16), (((1,), (1,)), ((), ())),
                             preferred_element_type=F32)
    row = lax.broadcasted_iota(jnp.int32, (LANES, tm), 0)
    row_f = row.astype(F32)
    bias = br_ref[...]
    is_g = (row >= N_EXPERTS) & (row < N_EXPERTS + N_GROUPS)
    neg = jnp.float32(-jnp.inf)

    gl = jnp.where(is_g, logits, neg)
    ge = jnp.where(is_g, jnp.exp(gl - _col_max(gl)), 0.0)
    g_prob = ge / _col_sum(ge)
    g_score = jnp.where(is_g, g_prob + bias, neg)
    g_row = _first_row(g_score == _col_max(g_score), row_f)
    g_w = _col_sum(jnp.where(row == g_row, g_prob, 0.0))
    g_idx = g_row - N_EXPERTS

    in_g = (row // EXPERTS_PER_GROUP) == g_idx
    el = jnp.where(in_g, logits, neg)
    ee = jnp.where(in_g, jnp.exp(el - _col_max(el)), 0.0)
    e_prob = ee / _col_sum(ee)
    e_score = jnp.where(in_g, e_prob + bias, neg)
    id1 = _first_row(e_score == _col_max(e_score), row_f)
    e_score2 = jnp.where(row == id1, neg, e_score)
    id2 = _first_row(e_score2 == _col_max(e_score2), row_f)
    p1 = _col_sum(jnp.where(row == id1, e_prob, 0.0))
    p2 = _col_sum(jnp.where(row == id2, e_prob, 0.0))
    psum = p1 + p2
    w1 = g_w * (p1 / psum)
    w2 = g_w * (p2 / psum)

    oh = ((row == id1) | (row == id2 + N_EXPERTS)).astype(BF16)
    t_in = lax.broadcasted_iota(jnp.int32, (tm, tm), 0)
    t_out = lax.broadcasted_iota(jnp.int32, (tm, tm), 1)
    tri = (t_in < t_out).astype(BF16)
    prefix = _dot(oh, tri)
    tot = jnp.sum(oh.astype(F32), axis=1, keepdims=True)
    tot_sw = jnp.concatenate([tot[N_EXPERTS:], tot[:N_EXPERTS]], axis=0)
    row1 = lax.broadcasted_iota(jnp.int32, (LANES, 1), 0)
    carry = carry_sc[...]
    base = carry + jnp.where(row1 >= N_EXPERTS, tot_sw, 0.0)
    rk = oh.astype(F32) * (base + prefix)
    rank1 = _col_sum(jnp.where(row < N_EXPERTS, rk, 0.0))
    rank2 = _col_sum(jnp.where(row >= N_EXPERTS, rk, 0.0))
    carry_new = carry + tot + tot_sw
    carry_sc[...] = carry_new
    cnt_ref[...] = carry_new.astype(jnp.int32)

    row8 = lax.broadcasted_iota(jnp.int32, (SUBLANES, tm), 0)
    rt = jnp.where(row8 == 0, id1, jnp.where(row8 == 1, id2, 0))
    rt = jnp.where(row8 == 2, rank1.astype(jnp.int32), rt)
    rt_ref[...] = jnp.where(row8 == 3, rank2.astype(jnp.int32), rt)
    wt = jnp.where(row == 0, w1, jnp.where(row == 1, w2, 0.0))
    rw_ref[...] = wt.T


def _out_route(o2, ma, sgb, x2, wo_b, wout_b, g_moe, wr_b, br):
    N = x2.shape[0]
    tm = TM_OUT
    row = lambda i: (i, 0)
    const = lambda i: (0, 0)
    return pl.pallas_call(
        _out_route_kernel,
        grid=(N // tm,),
        in_specs=[
            pl.BlockSpec((tm, D_MODEL), row),
            pl.BlockSpec((tm, D_MODEL), row),
            pl.BlockSpec((tm, D_MODEL), row),
            pl.BlockSpec((tm, D_MODEL), row),
            pl.BlockSpec((D_MODEL, D_MODEL), const),
            pl.BlockSpec((D_MODEL, D_MODEL), const),
            pl.BlockSpec((1, D_MODEL), const),
            pl.BlockSpec((LANES, D_MODEL), const),
            pl.BlockSpec((LANES, tm), const),
        ],
        out_specs=[
            pl.BlockSpec((tm, D_MODEL), row),
            pl.BlockSpec((tm * SUBLANES, LANES), row),
            pl.BlockSpec((SUBLANES, tm), lambda i: (0, i)),
            pl.BlockSpec((tm, LANES), row),
            pl.BlockSpec((LANES, 1), const),
        ],
        out_shape=[
            jax.ShapeDtypeStruct((N, D_MODEL), F32),
            jax.ShapeDtypeStruct((N * SUBLANES, LANES), F32),
            jax.ShapeDtypeStruct((SUBLANES, N), jnp.int32),
            jax.ShapeDtypeStruct((N, LANES), F32),
            jax.ShapeDtypeStruct((LANES, 1), jnp.int32),
        ],
        scratch_shapes=[pltpu.VMEM((LANES, 1), F32)],
        compiler_params=pltpu.CompilerParams(
            dimension_semantics=("arbitrary",), vmem_limit_bytes=VMEM_LIMIT),
        name="out_route",
    )(o2, ma, sgb, x2, wo_b, wout_b, g_moe, wr_b, br)


def _row_tile(ref2, row8):
    return ref2.at[pl.ds(pl.multiple_of(row8, SUBLANES), SUBLANES)]


def _dispatch_kernel(pad_end_ref, padded_ref, dest_ref, m_ref, buf_ref, zero_sc, sem, zsem):
    i = pl.program_id(0)
    tm = TM_ROWS
    blk8 = MOE_BLK * SUBLANES

    @pl.when(i == 0)
    def _():
        zero_sc[...] = jnp.zeros(zero_sc.shape, F32)

        def zero_copy(e):
            start = pl.multiple_of((pad_end_ref[e] - MOE_BLK) * SUBLANES, blk8)
            return pltpu.make_async_copy(zero_sc, buf_ref.at[pl.ds(start, blk8)], zsem)

        def start(e, carry):
            @pl.when(padded_ref[e] > 0)
            def _():
                zero_copy(e).start()
            return carry

        def wait(e, carry):
            @pl.when(padded_ref[e] > 0)
            def _():
                zero_copy(e).wait()
            return carry

        lax.fori_loop(0, N_EXPERTS, start, 0)
        lax.fori_loop(0, N_EXPERTS, wait, 0)

        def tail_copy(b):
            return pltpu.make_async_copy(
                zero_sc, buf_ref.at[pl.ds(pl.multiple_of(b * blk8, blk8), blk8)], zsem)

        nused = pad_end_ref[N_EXPERTS - 1] // MOE_BLK
        nblk = buf_ref.shape[0] // blk8
        lax.fori_loop(nused, nblk, lambda b, c: (tail_copy(b).start(), c)[1], 0)
        lax.fori_loop(nused, nblk, lambda b, c: (tail_copy(b).wait(), c)[1], 0)

    def issue(c, carry):
        for u in range(DMA_UNROLL):
            r = c * DMA_UNROLL + u
            src = _row_tile(m_ref, r * SUBLANES)
            pltpu.make_async_copy(src, _row_tile(buf_ref, dest_ref[0, 0, r]), sem).start(priority=0)
            pltpu.make_async_copy(src, _row_tile(buf_ref, dest_ref[0, 0, tm + r]), sem).start(priority=1)
        return carry

    lax.fori_loop(0, tm // DMA_UNROLL, issue, 0)
    for _ in range(2):
        pltpu.make_async_copy(m_ref, buf_ref.at[pl.ds(0, tm * SUBLANES)], sem).wait()


def _dispatch(pad_end, padded, dest8_tiles, m2, P):
    tm = TM_ROWS
    N = m2.shape[0] // SUBLANES
    grid_spec = pltpu.PrefetchScalarGridSpec(
        num_scalar_prefetch=2,
        grid=(N // tm,),
        in_specs=[
            pl.BlockSpec((1, 1, 2 * tm), lambda i, pe, pd: (i, 0, 0), memory_space=pltpu.SMEM),
            pl.BlockSpec((tm * SUBLANES, LANES), lambda i, pe, pd: (i, 0)),
        ],
        out_specs=pl.BlockSpec(memory_space=pl.ANY),
        scratch_shapes=[
            pltpu.VMEM((MOE_BLK * SUBLANES, LANES), F32),
            pltpu.SemaphoreType.DMA(()),
            pltpu.SemaphoreType.DMA(()),
        ],
    )
    return pl.pallas_call(
        _dispatch_kernel,
        grid_spec=grid_spec,
        out_shape=jax.ShapeDtypeStruct((P * SUBLANES, LANES), F32),
        compiler_params=pltpu.CompilerParams(dimension_semantics=("arbitrary",)),
        name="dispatch",
    )(pad_end, padded, dest8_tiles, m2)


def _expert_kernel(blk_e_ref, run_blocks_ref, nused_ref, x_hbm, wg_hbm, wu_hbm, wd_hbm, y_hbm,
                   xbuf, ybuf, wg_raw, wu_raw, wd_raw, wgu_sc, wd_sc, xsem, ysem, wsem):
    nused = nused_ref[0]
    blk8 = MOE_BLK * SUBLANES
    nblk = y_hbm.shape[0] // blk8
    nxbuf = X_AHEAD + 1

    def block_rows(ref, b):
        return ref.at[pl.ds(pl.multiple_of(b * blk8, blk8), blk8)]

    def x_copy(b):
        slot = b % nxbuf
        return pltpu.make_async_copy(block_rows(x_hbm, b), xbuf.at[slot], xsem.at[slot])

    def y_copy(b, slot):
        return pltpu.make_async_copy(ybuf.at[slot], block_rows(y_hbm, b), ysem.at[slot])

    def w_copies(e, slot):
        return (pltpu.make_async_copy(wg_hbm.at[e], wg_raw.at[slot], wsem.at[slot]),
                pltpu.make_async_copy(wu_hbm.at[e], wu_raw.at[slot], wsem.at[slot]),
                pltpu.make_async_copy(wd_hbm.at[e], wd_raw.at[slot], wsem.at[slot]))

    for b in range(X_AHEAD):
        @pl.when(b < nused)
        def _():
            x_copy(b).start()
    for c in w_copies(blk_e_ref[0], 0):
        c.start()

    def body(i, run):
        e = blk_e_ref[i]
        new_expert = (i == 0) | (e != blk_e_ref[jnp.maximum(i - 1, 0)])
        run = run + new_expert.astype(jnp.int32)
        wslot = run % 2

        @pl.when(new_expert)
        def _():
            for c in w_copies(e, wslot):
                c.wait()
            wgu_sc[:, 0:D_EXPERT] = wg_raw[wslot].astype(BF16)
            wgu_sc[:, D_EXPERT:2 * D_EXPERT] = wu_raw[wslot].astype(BF16)
            wd_sc[...] = wd_raw[wslot].astype(BF16)
            nxt = i + run_blocks_ref[e]

            @pl.when(nxt < nused)
            def _():
                for c in w_copies(blk_e_ref[nxt], 1 - wslot):
                    c.start()

        @pl.when(i + X_AHEAD < nused)
        def _():
            x_copy(i + X_AHEAD).start()

        yslot = i % Y_BUFS
        x_copy(i).wait()

        @pl.when(i >= Y_BUFS)
        def _():
            y_copy(i - Y_BUFS, yslot).wait()

        xb = _load_rows(xbuf.at[i % nxbuf], MOE_BLK).astype(BF16)
        h = _dot(xb, wgu_sc[...])
        hdn = (jax.nn.silu(h[:, 0:D_EXPERT]) * h[:, D_EXPERT:2 * D_EXPERT]).astype(BF16)
        _store_rows(ybuf.at[yslot], _dot(hdn, wd_sc[...]))
        y_copy(i, yslot).start()
        return run

    lax.fori_loop(0, nused, body, jnp.int32(-1))

    for back in range(1, Y_BUFS + 1):
        @pl.when(nused >= back)
        def _():
            y_copy(nused - back, (nused - back) % Y_BUFS).wait()

    ybuf[0] = jnp.zeros(ybuf.shape[1:], F32)
    lax.fori_loop(nused, nblk, lambda b, c: (y_copy(b, 0).start(), c)[1], 0)
    lax.fori_loop(nused, nblk, lambda b, c: (y_copy(b, 0).wait(), c)[1], 0)


def _experts(blk_e, run_blocks, nused, buf2, wg, wu, wd):
    blk_rows = MOE_BLK * SUBLANES
    any_spec = pl.BlockSpec(memory_space=pl.ANY)
    grid_spec = pltpu.PrefetchScalarGridSpec(
        num_scalar_prefetch=3,
        grid=(1,),
        in_specs=[any_spec, any_spec, any_spec, any_spec],
        out_specs=any_spec,
        scratch_shapes=[
            pltpu.VMEM((X_AHEAD + 1, blk_rows, LANES), F32),
            pltpu.VMEM((Y_BUFS, blk_rows, LANES), F32),
            pltpu.VMEM((2, D_MODEL, D_EXPERT), F32),
            pltpu.VMEM((2, D_MODEL, D_EXPERT), F32),
            pltpu.VMEM((2, D_EXPERT, D_MODEL), F32),
            pltpu.VMEM((D_MODEL, 2 * D_EXPERT), BF16),
            pltpu.VMEM((D_EXPERT, D_MODEL), BF16),
            pltpu.SemaphoreType.DMA((X_AHEAD + 1,)),
            pltpu.SemaphoreType.DMA((Y_BUFS,)),
            pltpu.SemaphoreType.DMA((2,)),
        ],
    )
    return pl.pallas_call(
        _expert_kernel,
        grid_spec=grid_spec,
        out_shape=jax.ShapeDtypeStruct(buf2.shape, F32),
        compiler_params=pltpu.CompilerParams(
            dimension_semantics=("arbitrary",), vmem_limit_bytes=VMEM_LIMIT),
        name="experts",
    )(blk_e, run_blocks, nused, buf2, wg, wu, wd)


def _final_kernel(dcur_ref, dnext_ref, h1_ref, rw_ref, p_ref, gple_ref, wpg_ref, wpp_ref, gfin_ref,
                  ys_ref, out_ref, ybuf_a, ybuf_b, sem):
    g = pl.program_id(0)
    ng = pl.num_programs(0)
    tm = TM_ROWS
    bufs = (ybuf_a, ybuf_b)

    def row_copy(d_ref, off, r, which):
        return pltpu.make_async_copy(_row_tile(ys_ref, d_ref[0, 0, off + r]),
                                     _row_tile(bufs[which], r * SUBLANES), sem.at[which])

    def issue(d_ref, off, which):
        for r in range(2 * tm):
            row_copy(d_ref, off, r, which).start(priority=r % 2)

    def wait(which):
        pltpu.make_async_copy(ys_ref.at[pl.ds(0, 2 * tm * SUBLANES)], bufs[which],
                              sem.at[which]).wait()

    def compute(which, rows):
        y0 = _load_rows(bufs[which], tm)
        y1 = _load_rows(bufs[which], tm, first_row=tm)
        rw = rw_ref[rows, :]
        h2 = h1_ref[rows, :] + (y0 * rw[:, 0:1] + y1 * rw[:, 1:2])
        n3 = (_rms(h2) * gple_ref[...]).astype(BF16)
        gate = jax.nn.sigmoid(_dot(n3, wpg_ref[...]))
        pp = _dot(p_ref[rows, :].astype(BF16), wpp_ref[...])
        h3 = h2 + gate * pp
        out_ref[rows, :] = _rms(h3) * gfin_ref[...]

    @pl.when(g == 0)
    def _():
        def body(c, carry):
            for u in range(DMA_UNROLL):
                row_copy(dcur_ref, 0, c * DMA_UNROLL + u, 0).start()
            return carry

        lax.fori_loop(0, 2 * tm // DMA_UNROLL, body, 0)

    wait(0)
    issue(dcur_ref, 2 * tm, 1)
    compute(0, slice(0, tm))
    wait(1)
    issue(dnext_ref, 0, 0)
    compute(1, slice(tm, 2 * tm))

    @pl.when(g == ng - 1)
    def _():
        wait(0)


def _final(dest8_pairs, h1, rw, p2, g_ple, wpg_b, wpp_b, g_final, ys2):
    N = h1.shape[0]
    tm = TM_ROWS
    ng = N // (2 * tm)
    row = lambda i: (i, 0)
    const = lambda i: (0, 0)
    return pl.pallas_call(
        _final_kernel,
        grid=(ng,),
        in_specs=[
            pl.BlockSpec((1, 1, 4 * tm), lambda i: (i, 0, 0), memory_space=pltpu.SMEM),
            pl.BlockSpec((1, 1, 4 * tm), lambda i: (jnp.minimum(i + 1, ng - 1), 0, 0),
                         memory_space=pltpu.SMEM),
            pl.BlockSpec((2 * tm, D_MODEL), row),
            pl.BlockSpec((2 * tm, LANES), row),
            pl.BlockSpec((2 * tm, PLE_DIM), row),
            pl.BlockSpec((1, D_MODEL), const),
            pl.BlockSpec((D_MODEL, D_MODEL), const),
            pl.BlockSpec((PLE_DIM, D_MODEL), const),
            pl.BlockSpec((1, D_MODEL), const),
            pl.BlockSpec(memory_space=pl.ANY),
        ],
        out_specs=pl.BlockSpec((2 * tm, D_MODEL), row),
        out_shape=jax.ShapeDtypeStruct((N, D_MODEL), F32),
        scratch_shapes=[
            pltpu.VMEM((2 * tm * SUBLANES, LANES), F32),
            pltpu.VMEM((2 * tm * SUBLANES, LANES), F32),
            pltpu.SemaphoreType.DMA((2,)),
        ],
        compiler_params=pltpu.CompilerParams(
            dimension_semantics=("arbitrary",), vmem_limit_bytes=VMEM_LIMIT),
        name="final",
    )(dest8_pairs, dest8_pairs, h1, rw, p2, g_ple, wpg_b, wpp_b, g_final, ys2)


def _rope_tables(S):
    inv_freq = ROPE_THETA ** (-jnp.arange(0, QK_ROPE_DIM, 2, dtype=F32) / QK_ROPE_DIM)
    ang = jnp.arange(S, dtype=F32)[:, None] * inv_freq[None, :]
    cos, sin = jnp.cos(ang), jnp.sin(ang)
    z = jnp.zeros_like(cos)
    rc = jnp.concatenate([cos, cos, z, z], axis=1)
    rs1 = jnp.concatenate([-sin, z, z, z], axis=1)
    rs2 = jnp.concatenate([z, sin, z, z], axis=1)
    return rc, rs1, rs2


def _layer(h, p_l, g_mix, w_in, g_gv, w_spatial, b_spatial, w_gproj, g_cq, w_uq, g_ckv, w_ukv,
           w_mla_o, w_out, g_moe, w_router_g, b_router_g, w_router_e, b_router_e,
           w_e_gate, w_e_up, w_e_down, g_ple, w_ple_gate, w_ple_proj, g_out):
    B, S, D = h.shape
    N = B * S
    x2 = h.reshape(N, D)

    cu, cv, ccq, cckv, ckr, cga = (GMLP_WIDTH, 2 * GMLP_WIDTH, 2 * GMLP_WIDTH + Q_LORA,
                                   2 * GMLP_WIDTH + Q_LORA + KV_LORA,
                                   2 * GMLP_WIDTH + Q_LORA + KV_LORA + QK_ROPE_DIM,
                                   2 * GMLP_WIDTH + Q_LORA + KV_LORA + QK_ROPE_DIM + D_MODEL)
    w_in_p = jnp.concatenate(
        [w_in[:, :cckv], w_in[:, ckr:], w_in[:, cckv:ckr],
         jnp.zeros((D, LANES - QK_ROPE_DIM), w_in.dtype)], axis=1).astype(BF16)
    w_uq_h = w_uq.reshape(Q_LORA, MLA_HEADS, QK_NOPE_DIM + QK_ROPE_DIM)
    w_uq_p = jnp.concatenate(
        [w_uq_h, jnp.zeros((Q_LORA, MLA_HEADS, QK_PAD - QK_NOPE_DIM - QK_ROPE_DIM), w_uq.dtype)],
        axis=2).reshape(Q_LORA, MLA_HEADS * QK_PAD).astype(BF16)
    rc, rs1, rs2 = _rope_tables(S)

    u, v, sga, sgb, q, k, vv = _inproj(
        x2, g_mix[None], w_in_p, g_gv[None], g_cq[None], g_ckv[None], w_uq_p, w_ukv.astype(BF16),
        rc, rs1, rs2, B, S)
    ma = _gmlp(u, v, sga, w_spatial.astype(BF16), b_spatial.T, w_gproj.astype(BF16))
    o = _attention(q, k, vv)

    wr = jnp.concatenate(
        [w_router_e.T, w_router_g.T, jnp.zeros((LANES - N_EXPERTS - N_GROUPS, D), w_router_e.dtype)],
        axis=0).astype(BF16)
    br = jnp.concatenate(
        [b_router_e.reshape(-1), b_router_g, jnp.zeros((LANES - N_EXPERTS - N_GROUPS,), F32)])
    br = jnp.broadcast_to(br[:, None], (LANES, TM_OUT))
    h1, m, rt, rw, cnt = _out_route(
        o.reshape(N, D), ma, sgb, x2, w_mla_o.astype(BF16), w_out.astype(BF16), g_moe[None], wr, br)

    counts = cnt[:N_EXPERTS, 0]
    padded = (counts + MOE_BLK - 1) // MOE_BLK * MOE_BLK
    pad_end = jnp.cumsum(padded)
    pad_start = pad_end - padded
    P = 2 * N + N_EXPERTS * MOE_BLK
    nblk = P // MOE_BLK
    blk_start = jnp.arange(nblk, dtype=jnp.int32) * MOE_BLK
    blk_e = jnp.minimum(
        jnp.sum((pad_end[None, :] <= blk_start[:, None]).astype(jnp.int32), axis=1),
        N_EXPERTS - 1).astype(jnp.int32)
    nused = (pad_end[-1:] // MOE_BLK).astype(jnp.int32)
    experts = jnp.arange(N_EXPERTS, dtype=jnp.int32)[:, None, None]
    start_of = jnp.sum(jnp.where(rt[None, 0:2] == experts, pad_start[:, None, None], 0), axis=0)
    dest8 = (start_of + rt[2:4]).astype(jnp.int32) * SUBLANES
    nt = N // TM_ROWS
    dest8_tiles = dest8.reshape(2, nt, TM_ROWS).transpose(1, 0, 2).reshape(nt, 1, 2 * TM_ROWS)

    buf = _dispatch(pad_end.astype(jnp.int32), padded.astype(jnp.int32), dest8_tiles, m, P)
    ys = _experts(blk_e, (padded // MOE_BLK).astype(jnp.int32), nused, buf, w_e_gate, w_e_up, w_e_down)
    out = _final(dest8_tiles.reshape(nt // 2, 1, 4 * TM_ROWS), h1, rw, p_l.reshape(N, PLE_DIM), g_ple[None],
                 w_ple_gate.astype(BF16), w_ple_proj.astype(BF16), g_out[None], ys)
    return out.reshape(B, S, D)


def kernel(x, p, g_mix, w_in, g_gv, w_spatial, b_spatial, w_gproj, g_cq, w_uq, g_ckv, w_ukv, w_mla_o,
           w_out, g_moe, w_router_g, b_router_g, w_router_e, b_router_e, w_e_gate, w_e_up, w_e_down,
           g_ple, w_ple_gate, w_ple_proj, g_final):
    depth = p.shape[0]
    assert depth == 1, "the final rmsnorm is fused into the single layer's last kernel"
    i = 0
    return _layer(x, p[i], g_mix[i], w_in[i], g_gv[i], w_spatial[i], b_spatial[i], w_gproj[i], g_cq[i],
                  w_uq[i], g_ckv[i], w_ukv[i], w_mla_o[i], w_out[i], g_moe[i], w_router_g[i],
                  b_router_g[i], w_router_e[i], b_router_e[i], w_e_gate[i], w_e_up[i], w_e_down[i],
                  g_ple[i], w_ple_gate[i], w_ple_proj[i], g_final)
```

```python
import functools

import jax
import jax.numpy as jnp
from jax import lax
from jax.experimental import pallas as pl
from jax.experimental.pallas import tpu as pltpu

F32 = jnp.float32
BF16 = jnp.bfloat16

D_MODEL = 1024
CHUNK = 64
PLE_DIM = 256
GMLP_BLOCK = 128
GMLP_GROUPS = 12
GMLP_WIDTH = 1536
MLA_HEADS = 8
QK_NOPE_DIM = 128
QK_ROPE_DIM = 64
V_HEAD_DIM = 128
Q_LORA = 384
KV_LORA = 256
ROPE_THETA = 10000.0
N_GROUPS = 8
EXPERTS_PER_GROUP = 8
N_EXPERTS = 64
D_EXPERT = 256
EPS = 1e-6
LOG2E = 1.4426950408889634

LANES = 128
SUBLANES = 8
QK_PAD = 256
V_PAD = 256
VMEM_LIMIT = 56 * 1024 * 1024

C_U = 0
C_V = C_U + GMLP_WIDTH
C_CQ = C_V + GMLP_WIDTH
C_CKV = C_CQ + Q_LORA
C_GA = C_CKV + KV_LORA
C_GB = C_GA + D_MODEL
C_KR = C_GB + D_MODEL
C_END = C_KR + LANES

TM_IN = 512
TM_GMLP = 512
TQ = 512
HEADS_PER_STEP = 4
TM_OUT = 512
TM_ROWS = 512
MOE_BLK = 256
X_AHEAD = 6
Y_BUFS = 4
DMA_UNROLL = 8


def _rms(x):
    return x * lax.rsqrt(jnp.mean(x * x, axis=-1, keepdims=True) + EPS)


def _dot(a, b):
    return jnp.dot(a, b, preferred_element_type=F32)


def _store_rows(ref2, x):
    rows = x.shape[0]
    for j in range(SUBLANES):
        ref2[pl.ds(j, rows, stride=SUBLANES), :] = x[:, j * LANES:(j + 1) * LANES]


def _load_rows(ref2, rows, first_row=0):
    return jnp.concatenate(
        [ref2[pl.ds(first_row * SUBLANES + j, rows, stride=SUBLANES), :] for j in range(SUBLANES)],
        axis=1)


def _rope128(t, rc, rs1, rs2):
    r1 = pltpu.roll(t, 96, axis=1)
    r2 = pltpu.roll(t, 32, axis=1)
    return t * rc + r1 * rs1 + r2 * rs2


def _inproj_kernel(x_ref, gmix_ref, win_ref, ggv_ref, gcq_ref, gckv_ref, wuq_ref, wukv_ref,
                   rc_ref, rs1_ref, rs2_ref,
                   u_ref, v_ref, sga_ref, sgb_ref, q_ref, k_ref, vv_ref):
    x = x_ref[...]
    ab = (_rms(x) * gmix_ref[...]).astype(BF16)

    def proj(c0, c1):
        return _dot(ab, win_ref[:, c0:c1])

    u_ref[...] = jax.nn.gelu(proj(C_U, C_V)).astype(BF16)

    zv = jax.nn.gelu(proj(C_V, C_CQ))
    xc = zv - jnp.mean(zv, axis=-1, keepdims=True)
    vln = xc * lax.rsqrt(jnp.mean(xc * xc, axis=-1, keepdims=True) + EPS)
    v_ref[...] = (vln * ggv_ref[...]).astype(BF16)

    sga_ref[...] = jax.nn.sigmoid(proj(C_GA, C_GB)).astype(BF16)
    sgb_ref[...] = jax.nn.sigmoid(proj(C_GB, C_KR)).astype(BF16)

    rc = rc_ref[...]
    rs1 = rs1_ref[...]
    rs2 = rs2_ref[...]
    kpe = _rope128(proj(C_KR, C_END), rc, rs1, rs2).astype(BF16)

    cqn = (_rms(proj(C_CQ, C_CKV)) * gcq_ref[...]).astype(BF16)
    ckvn = (_rms(proj(C_CKV, C_GA)) * gckv_ref[...]).astype(BF16)
    scale = (QK_NOPE_DIM + QK_ROPE_DIM) ** -0.5 * LOG2E
    for h in range(MLA_HEADS):
        qh = _dot(cqn, wuq_ref[:, h * QK_PAD:(h + 1) * QK_PAD])
        q_ref[0, h, :, 0:LANES] = (qh[:, 0:LANES] * scale).astype(BF16)
        q_ref[0, h, :, LANES:QK_PAD] = (_rope128(qh[:, LANES:QK_PAD], rc, rs1, rs2) * scale).astype(BF16)
        kvh = _dot(ckvn, wukv_ref[:, h * 256:(h + 1) * 256])
        k_ref[0, h, :, 0:LANES] = kvh[:, 0:LANES].astype(BF16)
        k_ref[0, h, :, LANES:QK_PAD] = kpe
        vv_ref[0, h] = kvh[:, LANES:256].astype(BF16)


def _inproj(x2, g_mix, w_in_p, g_gv, g_cq, g_ckv, w_uq_p, w_ukv_b, rc, rs1, rs2, B, S):
    N = x2.shape[0]
    tm = TM_IN
    spt = S // tm
    row = lambda i: (i, 0)
    const = lambda i: (0, 0)
    pos = lambda i: (i % spt, 0)
    head = lambda i: (i // spt, 0, i % spt, 0)
    return pl.pallas_call(
        _inproj_kernel,
        grid=(N // tm,),
        in_specs=[
            pl.BlockSpec((tm, D_MODEL), row),
            pl.BlockSpec((1, D_MODEL), const),
            pl.BlockSpec((D_MODEL, C_END), const, pipeline_mode=pl.Buffered(1)),
            pl.BlockSpec((1, GMLP_WIDTH), const),
            pl.BlockSpec((1, Q_LORA), const),
            pl.BlockSpec((1, KV_LORA), const),
            pl.BlockSpec((Q_LORA, MLA_HEADS * QK_PAD), const),
            pl.BlockSpec((KV_LORA, MLA_HEADS * 256), const),
            pl.BlockSpec((tm, LANES), pos),
            pl.BlockSpec((tm, LANES), pos),
            pl.BlockSpec((tm, LANES), pos),
        ],
        out_specs=[
            pl.BlockSpec((tm, GMLP_WIDTH), row),
            pl.BlockSpec((tm, GMLP_WIDTH), row),
            pl.BlockSpec((tm, D_MODEL), row),
            pl.BlockSpec((tm, D_MODEL), row),
            pl.BlockSpec((1, MLA_HEADS, tm, QK_PAD), head),
            pl.BlockSpec((1, MLA_HEADS, tm, QK_PAD), head),
            pl.BlockSpec((1, MLA_HEADS, tm, V_HEAD_DIM), head),
        ],
        out_shape=[
            jax.ShapeDtypeStruct((N, GMLP_WIDTH), BF16),
            jax.ShapeDtypeStruct((N, GMLP_WIDTH), BF16),
            jax.ShapeDtypeStruct((N, D_MODEL), BF16),
            jax.ShapeDtypeStruct((N, D_MODEL), BF16),
            jax.ShapeDtypeStruct((B, MLA_HEADS, S, QK_PAD), BF16),
            jax.ShapeDtypeStruct((B, MLA_HEADS, S, QK_PAD), BF16),
            jax.ShapeDtypeStruct((B, MLA_HEADS, S, V_HEAD_DIM), BF16),
        ],
        compiler_params=pltpu.CompilerParams(
            dimension_semantics=("arbitrary",), vmem_limit_bytes=VMEM_LIMIT),
        name="inproj",
    )(x2, g_mix, w_in_p, g_gv, g_cq, g_ckv, w_uq_p, w_ukv_b, rc, rs1, rs2)


def _gmlp_kernel(u_ref, v_ref, sga_ref, wsp_ref, bsp_ref, wproj_ref, ma_ref, y_sc):
    nb = TM_GMLP // GMLP_BLOCK
    t_out = lax.broadcasted_iota(jnp.int32, (GMLP_BLOCK, GMLP_BLOCK), 0)
    s_in = lax.broadcasted_iota(jnp.int32, (GMLP_BLOCK, GMLP_BLOCK), 1)
    mask = (s_in // CHUNK) <= (t_out // CHUNK)
    for g in range(GMLP_GROUPS):
        c0 = g * LANES
        w = jnp.where(mask, wsp_ref[g], jnp.zeros((), BF16))
        rhs = jnp.concatenate(
            [v_ref[r * GMLP_BLOCK:(r + 1) * GMLP_BLOCK, c0:c0 + LANES] for r in range(nb)], axis=1)
        sv = _dot(w, rhs) + bsp_ref[:, g:g + 1]
        for r in range(nb):
            rows = slice(r * GMLP_BLOCK, (r + 1) * GMLP_BLOCK)
            ub = u_ref[rows, c0:c0 + LANES].astype(F32)
            y_sc[rows, c0:c0 + LANES] = (ub * sv[:, r * LANES:(r + 1) * LANES]).astype(BF16)
    ya = _dot(y_sc[...], wproj_ref[...])
    ma_ref[...] = (sga_ref[...].astype(F32) * ya).astype(BF16)


def _gmlp(u, v, sga, wsp_b, bsp_t, wproj_b):
    N = u.shape[0]
    tm = TM_GMLP
    row = lambda i: (i, 0)
    return pl.pallas_call(
        _gmlp_kernel,
        grid=(N // tm,),
        in_specs=[
            pl.BlockSpec((tm, GMLP_WIDTH), row),
            pl.BlockSpec((tm, GMLP_WIDTH), row),
            pl.BlockSpec((tm, D_MODEL), row),
            pl.BlockSpec((GMLP_GROUPS, GMLP_BLOCK, GMLP_BLOCK), lambda i: (0, 0, 0)),
            pl.BlockSpec((GMLP_BLOCK, GMLP_GROUPS), lambda i: (0, 0)),
            pl.BlockSpec((GMLP_WIDTH, D_MODEL), lambda i: (0, 0)),
        ],
        out_specs=pl.BlockSpec((tm, D_MODEL), row),
        out_shape=jax.ShapeDtypeStruct((N, D_MODEL), BF16),
        scratch_shapes=[pltpu.VMEM((tm, GMLP_WIDTH), BF16)],
        compiler_params=pltpu.CompilerParams(
            dimension_semantics=("arbitrary",), vmem_limit_bytes=VMEM_LIMIT),
        name="gmlp",
    )(u, v, sga, wsp_b, bsp_t, wproj_b)


NEG_BIG = -1e30


def _attn_kernel(q_ref, k_ref, v_ref, o_ref, m_sc, acc_sc):
    qi = pl.program_id(2)
    m_sc[...] = jnp.full(m_sc.shape, NEG_BIG, F32)
    acc_sc[...] = jnp.zeros(acc_sc.shape, F32)
    ones_col = (lax.broadcasted_iota(jnp.int32, (TQ, V_PAD - V_HEAD_DIM), 1) == 0).astype(BF16)

    def chunk_mask(r0, nq, nk):
        qc = (lax.broadcasted_iota(jnp.int32, (nq, nk), 0) + r0) // CHUNK
        kc = lax.broadcasted_iota(jnp.int32, (nq, nk), 1) // CHUNK
        return kc <= qc

    def update(hh, r0, nq, start, nk, mask):
        rows = pl.ds(r0, nq)
        kb = k_ref[0, hh, pl.ds(start, nk), :]
        vb = jnp.concatenate([v_ref[0, hh, pl.ds(start, nk), :], ones_col[0:nk]], axis=1)
        s = lax.dot_general(q_ref[0, hh, rows, :], kb, (((1,), (1,)), ((), ())),
                            preferred_element_type=F32)
        if mask is not None:
            s = jnp.where(mask, s, NEG_BIG)
        tiles = [s[:, c * LANES:(c + 1) * LANES] for c in range(nk // LANES)]
        tile_max = functools.reduce(jnp.maximum, tiles)
        m_prev = m_sc[hh, rows, :]
        m_new = jnp.maximum(m_prev, jnp.max(tile_max, axis=-1, keepdims=True))
        alpha = jnp.exp2(m_prev - m_new)
        p = jnp.concatenate([jnp.exp2(t - m_new).astype(BF16) for t in tiles], axis=1)
        alpha2 = jnp.concatenate([alpha] * (V_PAD // LANES), axis=1)
        acc_sc[hh, rows, :] = alpha2 * acc_sc[hh, rows, :] + _dot(p, vb)
        m_sc[hh, rows, :] = m_new

    def body(j, carry):
        start = pl.multiple_of(j * TQ, TQ)
        for hh in range(HEADS_PER_STEP):
            update(hh, 0, TQ, start, TQ, None)
        return carry

    lax.fori_loop(0, qi, body, 0)
    start = pl.multiple_of(qi * TQ, TQ)
    mask = chunk_mask(0, TQ, TQ)
    for hh in range(HEADS_PER_STEP):
        update(hh, 0, TQ, start, TQ, mask)
    for hh in range(HEADS_PER_STEP):
        l = acc_sc[hh, :, V_HEAD_DIM:V_HEAD_DIM + 1]
        o_ref[0, :, hh * V_HEAD_DIM:(hh + 1) * V_HEAD_DIM] = (
            acc_sc[hh, :, 0:V_HEAD_DIM] / l).astype(BF16)


def _attention(q, k, v):
    B, H, S, _ = q.shape
    hps = HEADS_PER_STEP
    return pl.pallas_call(
        _attn_kernel,
        grid=(B, H // hps, S // TQ),
        in_specs=[
            pl.BlockSpec((1, hps, TQ, QK_PAD), lambda b, h, i: (b, h, i, 0)),
            pl.BlockSpec((1, hps, S, QK_PAD), lambda b, h, i: (b, h, 0, 0)),
            pl.BlockSpec((1, hps, S, V_HEAD_DIM), lambda b, h, i: (b, h, 0, 0)),
        ],
        out_specs=pl.BlockSpec((1, TQ, hps * V_HEAD_DIM), lambda b, h, i: (b, i, h)),
        out_shape=jax.ShapeDtypeStruct((B, S, H * V_HEAD_DIM), BF16),
        scratch_shapes=[
            pltpu.VMEM((hps, TQ, LANES), F32),
            pltpu.VMEM((hps, TQ, V_PAD), F32),
        ],
        compiler_params=pltpu.CompilerParams(
            dimension_semantics=("arbitrary", "arbitrary", "arbitrary"),
            vmem_limit_bytes=VMEM_LIMIT),
        name="attention",
    )(q, k, v)


def _col_sum(x):
    return jnp.sum(x, axis=0, keepdims=True)


def _col_max(x):
    return jnp.max(x, axis=0, keepdims=True)


def _first_row(hit, row_f):
    return jnp.min(jnp.where(hit, row_f, float(LANES)), axis=0, keepdims=True).astype(jnp.int32)


def _out_route_kernel(o_ref, ma_ref, sgb_ref, x_ref, wo_ref, wout_ref, gmoe_ref, wr_ref, br_ref,
                      h1_ref, m_ref, rt_ref, rw_ref, cnt_ref, carry_sc):
    i = pl.program_id(0)
    tm = TM_OUT

    @pl.when(i == 0)
    def _():
        carry_sc[...] = jnp.zeros(carry_sc.shape, F32)

    yb = _dot(o_ref[...], wo_ref[...])
    merged = ma_ref[...].astype(F32) + sgb_ref[...].astype(F32) * yb
    h1 = x_ref[...] + _dot(merged.astype(BF16), wout_ref[...])
    h1_ref[...] = h1
    m = _rms(h1) * gmoe_ref[...]
    _store_rows(m_ref, m)

    logits = lax.dot_general(wr_ref[...], m.astype(BF16), (((1,), (1,)), ((), ())),
                             preferred_element_type=F32)
    row = lax.broadcasted_iota(jnp.int32, (LANES, tm), 0)
    row_f = row.astype(F32)
    bias = br_ref[...]
    is_g = (row >= N_EXPERTS) & (row < N_EXPERTS + N_GROUPS)
    neg = jnp.float32(-jnp.inf)

    gl = jnp.where(is_g, logits, neg)
    ge = jnp.where(is_g, jnp.exp(gl - _col_max(gl)), 0.0)
    g_prob = ge / _col_sum(ge)
    g_score = jnp.where(is_g, g_prob + bias, neg)
    g_row = _first_row(g_score == _col_max(g_score), row_f)
    g_w = _col_sum(jnp.where(row == g_row, g_prob, 0.0))
    g_idx = g_row - N_EXPERTS

    in_g = (row // EXPERTS_PER_GROUP) == g_idx
    el = jnp.where(in_g, logits, neg)
    ee = jnp.where(in_g, jnp.exp(el - _col_max(el)), 0.0)
    e_prob = ee / _col_sum(ee)
    e_score = jnp.where(in_g, e_prob + bias, neg)
    id1 = _first_row(e_score == _col_max(e_score), row_f)
    e_score2 = jnp.where(row == id1, neg, e_score)
    id2 = _first_row(e_score2 == _col_max(e_score2), row_f)
    p1 = _col_sum(jnp.where(row == id1, e_prob, 0.0))
    p2 = _col_sum(jnp.where(row == id2, e_prob, 0.0))
    psum = p1 + p2
    w1 = g_w * (p1 / psum)
    w2 = g_w * (p2 / psum)

    oh = ((row == id1) | (row == id2 + N_EXPERTS)).astype(BF16)
    t_in = lax.broadcasted_iota(jnp.int32, (tm, tm), 0)
    t_out = lax.broadcasted_iota(jnp.int32, (tm, tm), 1)
    tri = (t_in < t_out).astype(BF16)
    prefix = _dot(oh, tri)
    tot = jnp.sum(oh.astype(F32), axis=1, keepdims=True)
    tot_sw = jnp.concatenate([tot[N_EXPERTS:], tot[:N_EXPERTS]], axis=0)
    row1 = lax.broadcasted_iota(jnp.int32, (LANES, 1), 0)
    carry = carry_sc[...]
    base = carry + jnp.where(row1 >= N_EXPERTS, tot_sw, 0.0)
    rk = oh.astype(F32) * (base + prefix)
    rank1 = _col_sum(jnp.where(row < N_EXPERTS, rk, 0.0))
    rank2 = _col_sum(jnp.where(row >= N_EXPERTS, rk, 0.0))
    carry_new = carry + tot + tot_sw
    carry_sc[...] = carry_new
    cnt_ref[...] = carry_new.astype(jnp.int32)

    row8 = lax.broadcasted_iota(jnp.int32, (SUBLANES, tm), 0)
    rt = jnp.where(row8 == 0, id1, jnp.where(row8 == 1, id2, 0))
    rt = jnp.where(row8 == 2, rank1.astype(jnp.int32), rt)
    rt_ref[...] = jnp.where(row8 == 3, rank2.astype(jnp.int32), rt)
    wt = jnp.where(row == 0, w1, jnp.where(row == 1, w2, 0.0))
    rw_ref[...] = wt.T


def _out_route(o2, ma, sgb, x2, wo_b, wout_b, g_moe, wr_b, br):
    N = x2.shape[0]
    tm = TM_OUT
    row = lambda i: (i, 0)
    const = lambda i: (0, 0)
    return pl.pallas_call(
        _out_route_kernel,
        grid=(N // tm,),
        in_specs=[
            pl.BlockSpec((tm, D_MODEL), row),
            pl.BlockSpec((tm, D_MODEL), row),
            pl.BlockSpec((tm, D_MODEL), row),
            pl.BlockSpec((tm, D_MODEL), row),
            pl.BlockSpec((D_MODEL, D_MODEL), const),
            pl.BlockSpec((D_MODEL, D_MODEL), const),
            pl.BlockSpec((1, D_MODEL), const),
            pl.BlockSpec((LANES, D_MODEL), const),
            pl.BlockSpec((LANES, tm), const),
        ],
        out_specs=[
            pl.BlockSpec((tm, D_MODEL), row),
            pl.BlockSpec((tm * SUBLANES, LANES), row),
            pl.BlockSpec((SUBLANES, tm), lambda i: (0, i)),
            pl.BlockSpec((tm, LANES), row),
            pl.BlockSpec((LANES, 1), const),
        ],
        out_shape=[
            jax.ShapeDtypeStruct((N, D_MODEL), F32),
            jax.ShapeDtypeStruct((N * SUBLANES, LANES), F32),
            jax.ShapeDtypeStruct((SUBLANES, N), jnp.int32),
            jax.ShapeDtypeStruct((N, LANES), F32),
            jax.ShapeDtypeStruct((LANES, 1), jnp.int32),
        ],
        scratch_shapes=[pltpu.VMEM((LANES, 1), F32)],
        compiler_params=pltpu.CompilerParams(
            dimension_semantics=("arbitrary",), vmem_limit_bytes=VMEM_LIMIT),
        name="out_route",
    )(o2, ma, sgb, x2, wo_b, wout_b, g_moe, wr_b, br)


def _row_tile(ref2, row8):
    return ref2.at[pl.ds(pl.multiple_of(row8, SUBLANES), SUBLANES)]


def _dispatch_kernel(pad_end_ref, npad_ref, dest_ref, m_ref, buf_ref, zero_sc, sem, zsem):
    i = pl.program_id(0)
    tm = TM_ROWS
    blk8 = MOE_BLK * SUBLANES

    def zero_fill(go):
        def per_expert(e, carry):
            npad = npad_ref[e]
            row = pad_end_ref[e] - npad
            k = MOE_BLK // 2
            while k >= 1:
                @pl.when((npad & k) != 0)
                def _():
                    go(pltpu.make_async_copy(
                        zero_sc.at[pl.ds(0, k * SUBLANES)],
                        buf_ref.at[pl.ds(pl.multiple_of(row * SUBLANES, SUBLANES), k * SUBLANES)],
                        zsem))
                row = row + (npad & k)
                k //= 2
            return carry

        def tail_copy(b):
            return pltpu.make_async_copy(
                zero_sc, buf_ref.at[pl.ds(pl.multiple_of(b * blk8, blk8), blk8)], zsem)

        lax.fori_loop(0, N_EXPERTS, per_expert, 0)
        nused = pad_end_ref[N_EXPERTS - 1] // MOE_BLK
        lax.fori_loop(nused, buf_ref.shape[0] // blk8, lambda b, c: (go(tail_copy(b)), c)[1], 0)

    @pl.when(i == 0)
    def _():
        zero_sc[...] = jnp.zeros(zero_sc.shape, F32)
        zero_fill(lambda c: c.start())

    def issue(c, carry):
        for u in range(DMA_UNROLL):
            r = c * DMA_UNROLL + u
            src = _row_tile(m_ref, r * SUBLANES)
            pltpu.make_async_copy(src, _row_tile(buf_ref, dest_ref[0, 0, r]), sem).start(priority=0)
            pltpu.make_async_copy(src, _row_tile(buf_ref, dest_ref[0, 0, tm + r]), sem).start(priority=1)
        return carry

    lax.fori_loop(0, tm // DMA_UNROLL, issue, 0)
    for _ in range(2):
        pltpu.make_async_copy(m_ref, buf_ref.at[pl.ds(0, tm * SUBLANES)], sem).wait()

    @pl.when(i == pl.num_programs(0) - 1)
    def _():
        zero_fill(lambda c: c.wait())


def _dispatch(pad_end, npad, dest8_tiles, m2, P):
    tm = TM_ROWS
    N = m2.shape[0] // SUBLANES
    grid_spec = pltpu.PrefetchScalarGridSpec(
        num_scalar_prefetch=2,
        grid=(N // tm,),
        in_specs=[
            pl.BlockSpec((1, 1, 2 * tm), lambda i, pe, pd: (i, 0, 0), memory_space=pltpu.SMEM),
            pl.BlockSpec((tm * SUBLANES, LANES), lambda i, pe, pd: (i, 0)),
        ],
        out_specs=pl.BlockSpec(memory_space=pl.ANY),
        scratch_shapes=[
            pltpu.VMEM((MOE_BLK * SUBLANES, LANES), F32),
            pltpu.SemaphoreType.DMA(()),
            pltpu.SemaphoreType.DMA(()),
        ],
    )
    return pl.pallas_call(
        _dispatch_kernel,
        grid_spec=grid_spec,
        out_shape=jax.ShapeDtypeStruct((P * SUBLANES, LANES), F32),
        compiler_params=pltpu.CompilerParams(dimension_semantics=("arbitrary",)),
        name="dispatch",
    )(pad_end, npad, dest8_tiles, m2)


def _expert_kernel(blk_e_ref, run_blocks_ref, nused_ref, x_hbm, wg_hbm, wu_hbm, wd_hbm, y_hbm,
                   xbuf, ybuf, wg_raw, wu_raw, wd_raw, wgu_sc, wd_sc, zero_sc,
                   xsem, ysem, wsem, zsem):
    nused = nused_ref[0]
    blk8 = MOE_BLK * SUBLANES
    nblk = y_hbm.shape[0] // blk8
    nxbuf = X_AHEAD + 1

    def block_rows(ref, b):
        return ref.at[pl.ds(pl.multiple_of(b * blk8, blk8), blk8)]

    def x_copy(b):
        slot = b % nxbuf
        return pltpu.make_async_copy(block_rows(x_hbm, b), xbuf.at[slot], xsem.at[slot])

    def y_copy(b, slot):
        return pltpu.make_async_copy(ybuf.at[slot], block_rows(y_hbm, b), ysem.at[slot])

    def w_copies(e, slot):
        return (pltpu.make_async_copy(wg_hbm.at[e], wg_raw.at[slot], wsem.at[slot]),
                pltpu.make_async_copy(wu_hbm.at[e], wu_raw.at[slot], wsem.at[slot]),
                pltpu.make_async_copy(wd_hbm.at[e], wd_raw.at[slot], wsem.at[slot]))

    for b in range(X_AHEAD):
        @pl.when(b < nused)
        def _():
            x_copy(b).start()
    for c in w_copies(blk_e_ref[0], 0):
        c.start()

    def zero_copy(b):
        return pltpu.make_async_copy(zero_sc, block_rows(y_hbm, b), zsem)

    zero_sc[...] = jnp.zeros(zero_sc.shape, F32)
    lax.fori_loop(nused, nblk, lambda b, c: (zero_copy(b).start(), c)[1], 0)

    def body(i, run):
        e = blk_e_ref[i]
        new_expert = (i == 0) | (e != blk_e_ref[jnp.maximum(i - 1, 0)])
        run = run + new_expert.astype(jnp.int32)
        wslot = run % 2

        @pl.when(new_expert)
        def _():
            for c in w_copies(e, wslot):
                c.wait()
            wgu_sc[:, 0:D_EXPERT] = wg_raw[wslot].astype(BF16)
            wgu_sc[:, D_EXPERT:2 * D_EXPERT] = wu_raw[wslot].astype(BF16)
            wd_sc[...] = wd_raw[wslot].astype(BF16)
            nxt = i + run_blocks_ref[e]

            @pl.when(nxt < nused)
            def _():
                for c in w_copies(blk_e_ref[nxt], 1 - wslot):
                    c.start()

        @pl.when(i + X_AHEAD < nused)
        def _():
            x_copy(i + X_AHEAD).start()

        yslot = i % Y_BUFS
        x_copy(i).wait()

        @pl.when(i >= Y_BUFS)
        def _():
            y_copy(i - Y_BUFS, yslot).wait()

        xb = _load_rows(xbuf.at[i % nxbuf], MOE_BLK).astype(BF16)
        h = _dot(xb, wgu_sc[...])
        hdn = (jax.nn.silu(h[:, 0:D_EXPERT]) * h[:, D_EXPERT:2 * D_EXPERT]).astype(BF16)
        _store_rows(ybuf.at[yslot], _dot(hdn, wd_sc[...]))
        y_copy(i, yslot).start()
        return run

    lax.fori_loop(0, nused, body, jnp.int32(-1))

    for back in range(1, Y_BUFS + 1):
        @pl.when(nused >= back)
        def _():
            y_copy(nused - back, (nused - back) % Y_BUFS).wait()

    lax.fori_loop(nused, nblk, lambda b, c: (zero_copy(b).wait(), c)[1], 0)


def _experts(blk_e, run_blocks, nused, buf2, wg, wu, wd):
    blk_rows = MOE_BLK * SUBLANES
    any_spec = pl.BlockSpec(memory_space=pl.ANY)
    grid_spec = pltpu.PrefetchScalarGridSpec(
        num_scalar_prefetch=3,
        grid=(1,),
        in_specs=[any_spec, any_spec, any_spec, any_spec],
        out_specs=any_spec,
        scratch_shapes=[
            pltpu.VMEM((X_AHEAD + 1, blk_rows, LANES), F32),
            pltpu.VMEM((Y_BUFS, blk_rows, LANES), F32),
            pltpu.VMEM((2, D_MODEL, D_EXPERT), F32),
            pltpu.VMEM((2, D_MODEL, D_EXPERT), F32),
            pltpu.VMEM((2, D_EXPERT, D_MODEL), F32),
            pltpu.VMEM((D_MODEL, 2 * D_EXPERT), BF16),
            pltpu.VMEM((D_EXPERT, D_MODEL), BF16),
            pltpu.VMEM((blk_rows, LANES), F32),
            pltpu.SemaphoreType.DMA((X_AHEAD + 1,)),
            pltpu.SemaphoreType.DMA((Y_BUFS,)),
            pltpu.SemaphoreType.DMA((2,)),
            pltpu.SemaphoreType.DMA(()),
        ],
    )
    return pl.pallas_call(
        _expert_kernel,
        grid_spec=grid_spec,
        out_shape=jax.ShapeDtypeStruct(buf2.shape, F32),
        compiler_params=pltpu.CompilerParams(
            dimension_semantics=("arbitrary",), vmem_limit_bytes=VMEM_LIMIT),
        name="experts",
    )(blk_e, run_blocks, nused, buf2, wg, wu, wd)


def _final_kernel(dcur_ref, dnext_ref, h1_ref, rw_ref, p_ref, gple_ref, wpg_ref, wpp_ref, gfin_ref,
                  ys_ref, out_ref, ybuf_a, ybuf_b, sem):
    g = pl.program_id(0)
    ng = pl.num_programs(0)
    tm = TM_ROWS
    bufs = (ybuf_a, ybuf_b)

    def row_copy(d_ref, off, r, which):
        return pltpu.make_async_copy(_row_tile(ys_ref, d_ref[0, 0, off + r]),
                                     _row_tile(bufs[which], r * SUBLANES), sem.at[which])

    def issue(d_ref, off, which):
        for r in range(2 * tm):
            row_copy(d_ref, off, r, which).start(priority=r % 2)

    def wait(which):
        pltpu.make_async_copy(ys_ref.at[pl.ds(0, 2 * tm * SUBLANES)], bufs[which],
                              sem.at[which]).wait()

    def compute(which, rows):
        y0 = _load_rows(bufs[which], tm)
        y1 = _load_rows(bufs[which], tm, first_row=tm)
        rw = rw_ref[rows, :]
        h2 = h1_ref[rows, :] + (y0 * rw[:, 0:1] + y1 * rw[:, 1:2])
        n3 = (_rms(h2) * gple_ref[...]).astype(BF16)
        gate = jax.nn.sigmoid(_dot(n3, wpg_ref[...]))
        pp = _dot(p_ref[rows, :].astype(BF16), wpp_ref[...])
        h3 = h2 + gate * pp
        out_ref[rows, :] = _rms(h3) * gfin_ref[...]

    @pl.when(g == 0)
    def _():
        def body(c, carry):
            for u in range(DMA_UNROLL):
                row_copy(dcur_ref, 0, c * DMA_UNROLL + u, 0).start()
            return carry

        lax.fori_loop(0, 2 * tm // DMA_UNROLL, body, 0)

    wait(0)
    issue(dcur_ref, 2 * tm, 1)
    compute(0, slice(0, tm))
    wait(1)
    issue(dnext_ref, 0, 0)
    compute(1, slice(tm, 2 * tm))

    @pl.when(g == ng - 1)
    def _():
        wait(0)


def _final(dest8_pairs, h1, rw, p2, g_ple, wpg_b, wpp_b, g_final, ys2):
    N = h1.shape[0]
    tm = TM_ROWS
    ng = N // (2 * tm)
    row = lambda i: (i, 0)
    const = lambda i: (0, 0)
    return pl.pallas_call(
        _final_kernel,
        grid=(ng,),
        in_specs=[
            pl.BlockSpec((1, 1, 4 * tm), lambda i: (i, 0, 0), memory_space=pltpu.SMEM),
            pl.BlockSpec((1, 1, 4 * tm), lambda i: (jnp.minimum(i + 1, ng - 1), 0, 0),
                         memory_space=pltpu.SMEM),
            pl.BlockSpec((2 * tm, D_MODEL), row),
            pl.BlockSpec((2 * tm, LANES), row),
            pl.BlockSpec((2 * tm, PLE_DIM), row),
            pl.BlockSpec((1, D_MODEL), const),
            pl.BlockSpec((D_MODEL, D_MODEL), const),
            pl.BlockSpec((PLE_DIM, D_MODEL), const),
            pl.BlockSpec((1, D_MODEL), const),
            pl.BlockSpec(memory_space=pl.ANY),
        ],
        out_specs=pl.BlockSpec((2 * tm, D_MODEL), row),
        out_shape=jax.ShapeDtypeStruct((N, D_MODEL), F32),
        scratch_shapes=[
            pltpu.VMEM((2 * tm * SUBLANES, LANES), F32),
            pltpu.VMEM((2 * tm * SUBLANES, LANES), F32),
            pltpu.SemaphoreType.DMA((2,)),
        ],
        compiler_params=pltpu.CompilerParams(
            dimension_semantics=("arbitrary",), vmem_limit_bytes=VMEM_LIMIT),
        name="final",
    )(dest8_pairs, dest8_pairs, h1, rw, p2, g_ple, wpg_b, wpp_b, g_final, ys2)


def _rope_tables(S):
    inv_freq = ROPE_THETA ** (-jnp.arange(0, QK_ROPE_DIM, 2, dtype=F32) / QK_ROPE_DIM)
    ang = jnp.arange(S, dtype=F32)[:, None] * inv_freq[None, :]
    cos, sin = jnp.cos(ang), jnp.sin(ang)
    z = jnp.zeros_like(cos)
    rc = jnp.concatenate([cos, cos, z, z], axis=1)
    rs1 = jnp.concatenate([-sin, z, z, z], axis=1)
    rs2 = jnp.concatenate([z, sin, z, z], axis=1)
    return rc, rs1, rs2


def _layer(h, p_l, g_mix, w_in, g_gv, w_spatial, b_spatial, w_gproj, g_cq, w_uq, g_ckv, w_ukv,
           w_mla_o, w_out, g_moe, w_router_g, b_router_g, w_router_e, b_router_e,
           w_e_gate, w_e_up, w_e_down, g_ple, w_ple_gate, w_ple_proj, g_out):
    B, S, D = h.shape
    N = B * S
    x2 = h.reshape(N, D)

    cu, cv, ccq, cckv, ckr, cga = (GMLP_WIDTH, 2 * GMLP_WIDTH, 2 * GMLP_WIDTH + Q_LORA,
                                   2 * GMLP_WIDTH + Q_LORA + KV_LORA,
                                   2 * GMLP_WIDTH + Q_LORA + KV_LORA + QK_ROPE_DIM,
                                   2 * GMLP_WIDTH + Q_LORA + KV_LORA + QK_ROPE_DIM + D_MODEL)
    w_in_p = jnp.concatenate(
        [w_in[:, :cckv], w_in[:, ckr:], w_in[:, cckv:ckr],
         jnp.zeros((D, LANES - QK_ROPE_DIM), w_in.dtype)], axis=1).astype(BF16)
    w_uq_h = w_uq.reshape(Q_LORA, MLA_HEADS, QK_NOPE_DIM + QK_ROPE_DIM)
    w_uq_p = jnp.concatenate(
        [w_uq_h, jnp.zeros((Q_LORA, MLA_HEADS, QK_PAD - QK_NOPE_DIM - QK_ROPE_DIM), w_uq.dtype)],
        axis=2).reshape(Q_LORA, MLA_HEADS * QK_PAD).astype(BF16)
    rc, rs1, rs2 = _rope_tables(S)

    u, v, sga, sgb, q, k, vv = _inproj(
        x2, g_mix[None], w_in_p, g_gv[None], g_cq[None], g_ckv[None], w_uq_p, w_ukv.astype(BF16),
        rc, rs1, rs2, B, S)
    ma = _gmlp(u, v, sga, w_spatial.astype(BF16), b_spatial.T, w_gproj.astype(BF16))
    o = _attention(q, k, vv)

    wr = jnp.concatenate(
        [w_router_e.T, w_router_g.T, jnp.zeros((LANES - N_EXPERTS - N_GROUPS, D), w_router_e.dtype)],
        axis=0).astype(BF16)
    br = jnp.concatenate(
        [b_router_e.reshape(-1), b_router_g, jnp.zeros((LANES - N_EXPERTS - N_GROUPS,), F32)])
    br = jnp.broadcast_to(br[:, None], (LANES, TM_OUT))
    h1, m, rt, rw, cnt = _out_route(
        o.reshape(N, D), ma, sgb, x2, w_mla_o.astype(BF16), w_out.astype(BF16), g_moe[None], wr, br)

    counts = cnt[:N_EXPERTS, 0]
    padded = (counts + MOE_BLK - 1) // MOE_BLK * MOE_BLK
    pad_end = jnp.cumsum(padded)
    pad_start = pad_end - padded
    P = 2 * N + N_EXPERTS * MOE_BLK
    nblk = P // MOE_BLK
    blk_start = jnp.arange(nblk, dtype=jnp.int32) * MOE_BLK
    blk_e = jnp.minimum(
        jnp.sum((pad_end[None, :] <= blk_start[:, None]).astype(jnp.int32), axis=1),
        N_EXPERTS - 1).astype(jnp.int32)
    nused = (pad_end[-1:] // MOE_BLK).astype(jnp.int32)
    experts = jnp.arange(N_EXPERTS, dtype=jnp.int32)[:, None, None]
    start_of = jnp.sum(jnp.where(rt[None, 0:2] == experts, pad_start[:, None, None], 0), axis=0)
    dest8 = (start_of + rt[2:4]).astype(jnp.int32) * SUBLANES
    nt = N // TM_ROWS
    dest8_tiles = dest8.reshape(2, nt, TM_ROWS).transpose(1, 0, 2).reshape(nt, 1, 2 * TM_ROWS)

    buf = _dispatch(pad_end.astype(jnp.int32), (padded - counts).astype(jnp.int32), dest8_tiles, m, P)
    ys = _experts(blk_e, (padded // MOE_BLK).astype(jnp.int32), nused, buf, w_e_gate, w_e_up, w_e_down)
    out = _final(dest8_tiles.reshape(nt // 2, 1, 4 * TM_ROWS), h1, rw, p_l.reshape(N, PLE_DIM), g_ple[None],
                 w_ple_gate.astype(BF16), w_ple_proj.astype(BF16), g_out[None], ys)
    return out.reshape(B, S, D)


def kernel(x, p, g_mix, w_in, g_gv, w_spatial, b_spatial, w_gproj, g_cq, w_uq, g_ckv, w_ukv, w_mla_o,
           w_out, g_moe, w_router_g, b_router_g, w_router_e, b_router_e, w_e_gate, w_e_up, w_e_down,
           g_ple, w_ple_gate, w_ple_proj, g_final):
    depth = p.shape[0]
    assert depth == 1, "the final rmsnorm is fused into the single layer's last kernel"
    i = 0
    return _layer(x, p[i], g_mix[i], w_in[i], g_gv[i], w_spatial[i], b_spatial[i], w_gproj[i], g_cq[i],
                  w_uq[i], g_ckv[i], w_ukv[i], w_mla_o[i], w_out[i], g_moe[i], w_router_g[i],
                  b_router_g[i], w_router_e[i], b_router_e[i], w_e_gate[i], w_e_up[i], w_e_down[i],
                  g_ple[i], w_ple_gate[i], w_ple_proj[i], g_final)
```

```python
import functools

import jax
import jax.numpy as jnp
from jax import lax
from jax.experimental import pallas as pl
from jax.experimental.pallas import tpu as pltpu

F32 = jnp.float32
BF16 = jnp.bfloat16

D_MODEL = 1024
CHUNK = 64
PLE_DIM = 256
GMLP_BLOCK = 128
GMLP_GROUPS = 12
GMLP_WIDTH = 1536
MLA_HEADS = 8
QK_NOPE_DIM = 128
QK_ROPE_DIM = 64
V_HEAD_DIM = 128
Q_LORA = 384
KV_LORA = 256
ROPE_THETA = 10000.0
N_GROUPS = 8
EXPERTS_PER_GROUP = 8
N_EXPERTS = 64
D_EXPERT = 256
EPS = 1e-6
LOG2E = 1.4426950408889634

LANES = 128
SUBLANES = 8
QK_PAD = 256
V_PAD = 256
VMEM_LIMIT = 56 * 1024 * 1024

C_U = 0
C_V = C_U + GMLP_WIDTH
C_CQ = C_V + GMLP_WIDTH
C_CKV = C_CQ + Q_LORA
C_GA = C_CKV + KV_LORA
C_GB = C_GA + D_MODEL
C_KR = C_GB + D_MODEL
C_END = C_KR + LANES

TM_IN = 512
TM_GMLP = 1024
TQ = 512
HEADS_PER_STEP = 4
TM_OUT = 1024
TM_ROWS = 512
MOE_BLK = 256
X_AHEAD = 6
Y_BUFS = 4
DMA_UNROLL = 8


def _rms(x):
    return x * lax.rsqrt(jnp.mean(x * x, axis=-1, keepdims=True) + EPS)


def _dot(a, b):
    return jnp.dot(a, b, preferred_element_type=F32)


def _store_rows(ref2, x):
    rows = x.shape[0]
    for j in range(SUBLANES):
        ref2[pl.ds(j, rows, stride=SUBLANES), :] = x[:, j * LANES:(j + 1) * LANES]


def _load_rows(ref2, rows, first_row=0):
    return jnp.concatenate(
        [ref2[pl.ds(first_row * SUBLANES + j, rows, stride=SUBLANES), :] for j in range(SUBLANES)],
        axis=1)


def _rope128(t, rc, rs1, rs2):
    r1 = pltpu.roll(t, 96, axis=1)
    r2 = pltpu.roll(t, 32, axis=1)
    return t * rc + r1 * rs1 + r2 * rs2


def _inproj_kernel(x_ref, gmix_ref, win_ref, ggv_ref, gcq_ref, gckv_ref, wuq_ref, wukv_ref,
                   rc_ref, rs1_ref, rs2_ref,
                   u_ref, v_ref, sga_ref, sgb_ref, q_ref, k_ref, vv_ref):
    x = x_ref[...]
    ab = (_rms(x) * gmix_ref[...]).astype(BF16)

    def proj(c0, c1):
        return _dot(ab, win_ref[:, c0:c1])

    u_ref[...] = jax.nn.gelu(proj(C_U, C_V)).astype(BF16)

    zv = jax.nn.gelu(proj(C_V, C_CQ))
    xc = zv - jnp.mean(zv, axis=-1, keepdims=True)
    vln = xc * lax.rsqrt(jnp.mean(xc * xc, axis=-1, keepdims=True) + EPS)
    v_ref[...] = (vln * ggv_ref[...]).astype(BF16)

    sga_ref[...] = jax.nn.sigmoid(proj(C_GA, C_GB)).astype(BF16)
    sgb_ref[...] = jax.nn.sigmoid(proj(C_GB, C_KR)).astype(BF16)

    rc = rc_ref[...]
    rs1 = rs1_ref[...]
    rs2 = rs2_ref[...]
    kpe = _rope128(proj(C_KR, C_END), rc, rs1, rs2).astype(BF16)

    cqn = (_rms(proj(C_CQ, C_CKV)) * gcq_ref[...]).astype(BF16)
    ckvn = (_rms(proj(C_CKV, C_GA)) * gckv_ref[...]).astype(BF16)
    scale = (QK_NOPE_DIM + QK_ROPE_DIM) ** -0.5 * LOG2E
    for h in range(MLA_HEADS):
        qh = _dot(cqn, wuq_ref[:, h * QK_PAD:(h + 1) * QK_PAD])
        q_ref[0, h, :, 0:LANES] = (qh[:, 0:LANES] * scale).astype(BF16)
        q_ref[0, h, :, LANES:QK_PAD] = (_rope128(qh[:, LANES:QK_PAD], rc, rs1, rs2) * scale).astype(BF16)
        kvh = _dot(ckvn, wukv_ref[:, h * 256:(h + 1) * 256])
        k_ref[0, h, :, 0:LANES] = kvh[:, 0:LANES].astype(BF16)
        k_ref[0, h, :, LANES:QK_PAD] = kpe
        vv_ref[0, h] = kvh[:, LANES:256].astype(BF16)


def _inproj(x2, g_mix, w_in_p, g_gv, g_cq, g_ckv, w_uq_p, w_ukv_b, rc, rs1, rs2, B, S):
    N = x2.shape[0]
    tm = TM_IN
    spt = S // tm
    row = lambda i: (i, 0)
    const = lambda i: (0, 0)
    pos = lambda i: (i % spt, 0)
    head = lambda i: (i // spt, 0, i % spt, 0)
    return pl.pallas_call(
        _inproj_kernel,
        grid=(N // tm,),
        in_specs=[
            pl.BlockSpec((tm, D_MODEL), row),
            pl.BlockSpec((1, D_MODEL), const),
            pl.BlockSpec((D_MODEL, C_END), const, pipeline_mode=pl.Buffered(1)),
            pl.BlockSpec((1, GMLP_WIDTH), const),
            pl.BlockSpec((1, Q_LORA), const),
            pl.BlockSpec((1, KV_LORA), const),
            pl.BlockSpec((Q_LORA, MLA_HEADS * QK_PAD), const),
            pl.BlockSpec((KV_LORA, MLA_HEADS * 256), const),
            pl.BlockSpec((tm, LANES), pos),
            pl.BlockSpec((tm, LANES), pos),
            pl.BlockSpec((tm, LANES), pos),
        ],
        out_specs=[
            pl.BlockSpec((tm, GMLP_WIDTH), row),
            pl.BlockSpec((tm, GMLP_WIDTH), row),
            pl.BlockSpec((tm, D_MODEL), row),
            pl.BlockSpec((tm, D_MODEL), row),
            pl.BlockSpec((1, MLA_HEADS, tm, QK_PAD), head),
            pl.BlockSpec((1, MLA_HEADS, tm, QK_PAD), head),
            pl.BlockSpec((1, MLA_HEADS, tm, V_HEAD_DIM), head),
        ],
        out_shape=[
            jax.ShapeDtypeStruct((N, GMLP_WIDTH), BF16),
            jax.ShapeDtypeStruct((N, GMLP_WIDTH), BF16),
            jax.ShapeDtypeStruct((N, D_MODEL), BF16),
            jax.ShapeDtypeStruct((N, D_MODEL), BF16),
            jax.ShapeDtypeStruct((B, MLA_HEADS, S, QK_PAD), BF16),
            jax.ShapeDtypeStruct((B, MLA_HEADS, S, QK_PAD), BF16),
            jax.ShapeDtypeStruct((B, MLA_HEADS, S, V_HEAD_DIM), BF16),
        ],
        compiler_params=pltpu.CompilerParams(
            dimension_semantics=("arbitrary",), vmem_limit_bytes=VMEM_LIMIT),
        name="inproj",
    )(x2, g_mix, w_in_p, g_gv, g_cq, g_ckv, w_uq_p, w_ukv_b, rc, rs1, rs2)


def _gmlp_kernel(u_ref, v_ref, sga_ref, wsp_ref, bsp_ref, wproj_ref, ma_ref, y_sc):
    nb = TM_GMLP // GMLP_BLOCK
    t_out = lax.broadcasted_iota(jnp.int32, (GMLP_BLOCK, GMLP_BLOCK), 0)
    s_in = lax.broadcasted_iota(jnp.int32, (GMLP_BLOCK, GMLP_BLOCK), 1)
    mask = (s_in // CHUNK) <= (t_out // CHUNK)
    for g in range(GMLP_GROUPS):
        c0 = g * LANES
        w = jnp.where(mask, wsp_ref[g], jnp.zeros((), BF16))
        rhs = jnp.concatenate(
            [v_ref[r * GMLP_BLOCK:(r + 1) * GMLP_BLOCK, c0:c0 + LANES] for r in range(nb)], axis=1)
        sv = _dot(w, rhs) + bsp_ref[:, g:g + 1]
        for r in range(nb):
            rows = slice(r * GMLP_BLOCK, (r + 1) * GMLP_BLOCK)
            ub = u_ref[rows, c0:c0 + LANES].astype(F32)
            y_sc[rows, c0:c0 + LANES] = (ub * sv[:, r * LANES:(r + 1) * LANES]).astype(BF16)
    ya = _dot(y_sc[...], wproj_ref[...])
    ma_ref[...] = (sga_ref[...].astype(F32) * ya).astype(BF16)


def _gmlp(u, v, sga, wsp_b, bsp_t, wproj_b):
    N = u.shape[0]
    tm = TM_GMLP
    row = lambda i: (i, 0)
    return pl.pallas_call(
        _gmlp_kernel,
        grid=(N // tm,),
        in_specs=[
            pl.BlockSpec((tm, GMLP_WIDTH), row),
            pl.BlockSpec((tm, GMLP_WIDTH), row),
            pl.BlockSpec((tm, D_MODEL), row),
            pl.BlockSpec((GMLP_GROUPS, GMLP_BLOCK, GMLP_BLOCK), lambda i: (0, 0, 0)),
            pl.BlockSpec((GMLP_BLOCK, GMLP_GROUPS), lambda i: (0, 0)),
            pl.BlockSpec((GMLP_WIDTH, D_MODEL), lambda i: (0, 0)),
        ],
        out_specs=pl.BlockSpec((tm, D_MODEL), row),
        out_shape=jax.ShapeDtypeStruct((N, D_MODEL), BF16),
        scratch_shapes=[pltpu.VMEM((tm, GMLP_WIDTH), BF16)],
        compiler_params=pltpu.CompilerParams(
            dimension_semantics=("arbitrary",), vmem_limit_bytes=VMEM_LIMIT),
        name="gmlp",
    )(u, v, sga, wsp_b, bsp_t, wproj_b)


NEG_BIG = -1e30


def _attn_kernel(q_ref, k_ref, v_ref, o_ref, m_sc, acc_sc):
    qi = pl.program_id(2)
    m_sc[...] = jnp.full(m_sc.shape, NEG_BIG, F32)
    acc_sc[...] = jnp.zeros(acc_sc.shape, F32)
    ones_col = (lax.broadcasted_iota(jnp.int32, (TQ, V_PAD - V_HEAD_DIM), 1) == 0).astype(BF16)

    def chunk_mask(r0, nq, nk):
        qc = (lax.broadcasted_iota(jnp.int32, (nq, nk), 0) + r0) // CHUNK
        kc = lax.broadcasted_iota(jnp.int32, (nq, nk), 1) // CHUNK
        return kc <= qc

    def update(hh, r0, nq, start, nk, mask):
        rows = pl.ds(r0, nq)
        kb = k_ref[0, hh, pl.ds(start, nk), :]
        vb = jnp.concatenate([v_ref[0, hh, pl.ds(start, nk), :], ones_col[0:nk]], axis=1)
        s = lax.dot_general(q_ref[0, hh, rows, :], kb, (((1,), (1,)), ((), ())),
                            preferred_element_type=F32)
        if mask is not None:
            s = jnp.where(mask, s, NEG_BIG)
        tiles = [s[:, c * LANES:(c + 1) * LANES] for c in range(nk // LANES)]
        tile_max = functools.reduce(jnp.maximum, tiles)
        m_prev = m_sc[hh, rows, :]
        m_new = jnp.maximum(m_prev, jnp.max(tile_max, axis=-1, keepdims=True))
        alpha = jnp.exp2(m_prev - m_new)
        p = jnp.concatenate([jnp.exp2(t - m_new).astype(BF16) for t in tiles], axis=1)
        alpha2 = jnp.concatenate([alpha] * (V_PAD // LANES), axis=1)
        acc_sc[hh, rows, :] = alpha2 * acc_sc[hh, rows, :] + _dot(p, vb)
        m_sc[hh, rows, :] = m_new

    def body(j, carry):
        start = pl.multiple_of(j * TQ, TQ)
        for hh in range(HEADS_PER_STEP):
            update(hh, 0, TQ, start, TQ, None)
        return carry

    lax.fori_loop(0, qi, body, 0)
    start = pl.multiple_of(qi * TQ, TQ)
    mask = chunk_mask(0, TQ, TQ)
    for hh in range(HEADS_PER_STEP):
        update(hh, 0, TQ, start, TQ, mask)
    for hh in range(HEADS_PER_STEP):
        l = acc_sc[hh, :, V_HEAD_DIM:V_HEAD_DIM + 1]
        o_ref[0, :, hh * V_HEAD_DIM:(hh + 1) * V_HEAD_DIM] = (
            acc_sc[hh, :, 0:V_HEAD_DIM] / l).astype(BF16)


def _attention(q, k, v):
    B, H, S, _ = q.shape
    hps = HEADS_PER_STEP
    return pl.pallas_call(
        _attn_kernel,
        grid=(B, H // hps, S // TQ),
        in_specs=[
            pl.BlockSpec((1, hps, TQ, QK_PAD), lambda b, h, i: (b, h, i, 0)),
            pl.BlockSpec((1, hps, S, QK_PAD), lambda b, h, i: (b, h, 0, 0)),
            pl.BlockSpec((1, hps, S, V_HEAD_DIM), lambda b, h, i: (b, h, 0, 0)),
        ],
        out_specs=pl.BlockSpec((1, TQ, hps * V_HEAD_DIM), lambda b, h, i: (b, i, h)),
        out_shape=jax.ShapeDtypeStruct((B, S, H * V_HEAD_DIM), BF16),
        scratch_shapes=[
            pltpu.VMEM((hps, TQ, LANES), F32),
            pltpu.VMEM((hps, TQ, V_PAD), F32),
        ],
        compiler_params=pltpu.CompilerParams(
            dimension_semantics=("arbitrary", "arbitrary", "arbitrary"),
            vmem_limit_bytes=VMEM_LIMIT),
        name="attention",
    )(q, k, v)


def _col_sum(x):
    return jnp.sum(x, axis=0, keepdims=True)


def _col_max(x):
    return jnp.max(x, axis=0, keepdims=True)


def _first_row(hit, row_f):
    return jnp.min(jnp.where(hit, row_f, float(LANES)), axis=0, keepdims=True).astype(jnp.int32)


def _out_route_kernel(o_ref, ma_ref, sgb_ref, x_ref, wo_ref, wout_ref, gmoe_ref, wr_ref, br_ref,
                      h1_ref, m_ref, rt_ref, rw_ref, cnt_ref, carry_sc):
    i = pl.program_id(0)
    tm = TM_OUT

    @pl.when(i == 0)
    def _():
        carry_sc[...] = jnp.zeros(carry_sc.shape, F32)

    yb = _dot(o_ref[...], wo_ref[...])
    merged = ma_ref[...].astype(F32) + sgb_ref[...].astype(F32) * yb
    h1 = x_ref[...] + _dot(merged.astype(BF16), wout_ref[...])
    h1_ref[...] = h1
    m = _rms(h1) * gmoe_ref[...]
    _store_rows(m_ref, m)

    logits = lax.dot_general(wr_ref[...], m.astype(BF16), (((1,), (1,)), ((), ())),
                             preferred_element_type=F32)
    row = lax.broadcasted_iota(jnp.int32, (LANES, tm), 0)
    row_f = row.astype(F32)
    bias = br_ref[...]
    is_g = (row >= N_EXPERTS) & (row < N_EXPERTS + N_GROUPS)
    neg = jnp.float32(-jnp.inf)

    gl = jnp.where(is_g, logits, neg)
    ge = jnp.where(is_g, jnp.exp(gl - _col_max(gl)), 0.0)
    g_prob = ge / _col_sum(ge)
    g_score = jnp.where(is_g, g_prob + bias, neg)
    g_row = _first_row(g_score == _col_max(g_score), row_f)
    g_w = _col_sum(jnp.where(row == g_row, g_prob, 0.0))
    g_idx = g_row - N_EXPERTS

    in_g = (row // EXPERTS_PER_GROUP) == g_idx
    el = jnp.where(in_g, logits, neg)
    ee = jnp.where(in_g, jnp.exp(el - _col_max(el)), 0.0)
    e_prob = ee / _col_sum(ee)
    e_score = jnp.where(in_g, e_prob + bias, neg)
    id1 = _first_row(e_score == _col_max(e_score), row_f)
    e_score2 = jnp.where(row == id1, neg, e_score)
    id2 = _first_row(e_score2 == _col_max(e_score2), row_f)
    p1 = _col_sum(jnp.where(row == id1, e_prob, 0.0))
    p2 = _col_sum(jnp.where(row == id2, e_prob, 0.0))
    psum = p1 + p2
    w1 = g_w * (p1 / psum)
    w2 = g_w * (p2 / psum)

    oh = ((row == id1) | (row == id2 + N_EXPERTS)).astype(BF16)
    t_in = lax.broadcasted_iota(jnp.int32, (tm, tm), 0)
    t_out = lax.broadcasted_iota(jnp.int32, (tm, tm), 1)
    tri = (t_in < t_out).astype(BF16)
    prefix = _dot(oh, tri)
    tot = jnp.sum(oh.astype(F32), axis=1, keepdims=True)
    tot_sw = jnp.concatenate([tot[N_EXPERTS:], tot[:N_EXPERTS]], axis=0)
    row1 = lax.broadcasted_iota(jnp.int32, (LANES, 1), 0)
    carry = carry_sc[...]
    base = carry + jnp.where(row1 >= N_EXPERTS, tot_sw, 0.0)
    rk = oh.astype(F32) * (base + prefix)
    rank1 = _col_sum(jnp.where(row < N_EXPERTS, rk, 0.0))
    rank2 = _col_sum(jnp.where(row >= N_EXPERTS, rk, 0.0))
    carry_new = carry + tot + tot_sw
    carry_sc[...] = carry_new
    cnt_ref[...] = carry_new.astype(jnp.int32)

    row8 = lax.broadcasted_iota(jnp.int32, (SUBLANES, tm), 0)
    rt = jnp.where(row8 == 0, id1, jnp.where(row8 == 1, id2, 0))
    rt = jnp.where(row8 == 2, rank1.astype(jnp.int32), rt)
    rt_ref[...] = jnp.where(row8 == 3, rank2.astype(jnp.int32), rt)
    wt = jnp.where(row == 0, w1, jnp.where(row == 1, w2, 0.0))
    rw_ref[...] = wt.T


def _out_route(o2, ma, sgb, x2, wo_b, wout_b, g_moe, wr_b, br):
    N = x2.shape[0]
    tm = TM_OUT
    row = lambda i: (i, 0)
    const = lambda i: (0, 0)
    return pl.pallas_call(
        _out_route_kernel,
        grid=(N // tm,),
        in_specs=[
            pl.BlockSpec((tm, D_MODEL), row),
            pl.BlockSpec((tm, D_MODEL), row),
            pl.BlockSpec((tm, D_MODEL), row),
            pl.BlockSpec((tm, D_MODEL), row),
            pl.BlockSpec((D_MODEL, D_MODEL), const),
            pl.BlockSpec((D_MODEL, D_MODEL), const),
            pl.BlockSpec((1, D_MODEL), const),
            pl.BlockSpec((LANES, D_MODEL), const),
            pl.BlockSpec((LANES, tm), const),
        ],
        out_specs=[
            pl.BlockSpec((tm, D_MODEL), row),
            pl.BlockSpec((tm * SUBLANES, LANES), row),
            pl.BlockSpec((SUBLANES, tm), lambda i: (0, i)),
            pl.BlockSpec((tm, LANES), row),
            pl.BlockSpec((LANES, 1), const),
        ],
        out_shape=[
            jax.ShapeDtypeStruct((N, D_MODEL), F32),
            jax.ShapeDtypeStruct((N * SUBLANES, LANES), F32),
            jax.ShapeDtypeStruct((SUBLANES, N), jnp.int32),
            jax.ShapeDtypeStruct((N, LANES), F32),
            jax.ShapeDtypeStruct((LANES, 1), jnp.int32),
        ],
        scratch_shapes=[pltpu.VMEM((LANES, 1), F32)],
        compiler_params=pltpu.CompilerParams(
            dimension_semantics=("arbitrary",), vmem_limit_bytes=VMEM_LIMIT),
        name="out_route",
    )(o2, ma, sgb, x2, wo_b, wout_b, g_moe, wr_b, br)


def _row_tile(ref2, row8):
    return ref2.at[pl.ds(pl.multiple_of(row8, SUBLANES), SUBLANES)]


def _dispatch_kernel(pad_end_ref, npad_ref, dest_ref, m_ref, buf_ref, zero_sc, sem, zsem):
    i = pl.program_id(0)
    tm = TM_ROWS
    blk8 = MOE_BLK * SUBLANES

    def zero_fill(go):
        def per_expert(e, carry):
            npad = npad_ref[e]
            row = pad_end_ref[e] - npad
            k = MOE_BLK // 2
            while k >= 1:
                @pl.when((npad & k) != 0)
                def _():
                    go(pltpu.make_async_copy(
                        zero_sc.at[pl.ds(0, k * SUBLANES)],
                        buf_ref.at[pl.ds(pl.multiple_of(row * SUBLANES, SUBLANES), k * SUBLANES)],
                        zsem))
                row = row + (npad & k)
                k //= 2
            return carry

        def tail_copy(b):
            return pltpu.make_async_copy(
                zero_sc, buf_ref.at[pl.ds(pl.multiple_of(b * blk8, blk8), blk8)], zsem)

        lax.fori_loop(0, N_EXPERTS, per_expert, 0)
        nused = pad_end_ref[N_EXPERTS - 1] // MOE_BLK
        lax.fori_loop(nused, buf_ref.shape[0] // blk8, lambda b, c: (go(tail_copy(b)), c)[1], 0)

    @pl.when(i == 0)
    def _():
        zero_sc[...] = jnp.zeros(zero_sc.shape, F32)
        zero_fill(lambda c: c.start())

    def issue(c, carry):
        for u in range(DMA_UNROLL):
            r = c * DMA_UNROLL + u
            src = _row_tile(m_ref, r * SUBLANES)
            pltpu.make_async_copy(src, _row_tile(buf_ref, dest_ref[0, 0, r]), sem).start(priority=0)
            pltpu.make_async_copy(src, _row_tile(buf_ref, dest_ref[0, 0, tm + r]), sem).start(priority=1)
        return carry

    lax.fori_loop(0, tm // DMA_UNROLL, issue, 0)
    for _ in range(2):
        pltpu.make_async_copy(m_ref, buf_ref.at[pl.ds(0, tm * SUBLANES)], sem).wait()

    @pl.when(i == pl.num_programs(0) - 1)
    def _():
        zero_fill(lambda c: c.wait())


def _dispatch(pad_end, npad, dest8_tiles, m2, P):
    tm = TM_ROWS
    N = m2.shape[0] // SUBLANES
    grid_spec = pltpu.PrefetchScalarGridSpec(
        num_scalar_prefetch=2,
        grid=(N // tm,),
        in_specs=[
            pl.BlockSpec((1, 1, 2 * tm), lambda i, pe, pd: (i, 0, 0), memory_space=pltpu.SMEM),
            pl.BlockSpec((tm * SUBLANES, LANES), lambda i, pe, pd: (i, 0)),
        ],
        out_specs=pl.BlockSpec(memory_space=pl.ANY),
        scratch_shapes=[
            pltpu.VMEM((MOE_BLK * SUBLANES, LANES), F32),
            pltpu.SemaphoreType.DMA(()),
            pltpu.SemaphoreType.DMA(()),
        ],
    )
    return pl.pallas_call(
        _dispatch_kernel,
        grid_spec=grid_spec,
        out_shape=jax.ShapeDtypeStruct((P * SUBLANES, LANES), F32),
        compiler_params=pltpu.CompilerParams(dimension_semantics=("arbitrary",)),
        name="dispatch",
    )(pad_end, npad, dest8_tiles, m2)


def _expert_kernel(blk_e_ref, run_blocks_ref, nused_ref, x_hbm, wg_hbm, wu_hbm, wd_hbm, y_hbm,
                   xbuf, ybuf, wg_raw, wu_raw, wd_raw, wgu_sc, wd_sc, zero_sc,
                   xsem, ysem, wsem, zsem):
    nused = nused_ref[0]
    blk8 = MOE_BLK * SUBLANES
    nblk = y_hbm.shape[0] // blk8
    nxbuf = X_AHEAD + 1

    def block_rows(ref, b):
        return ref.at[pl.ds(pl.multiple_of(b * blk8, blk8), blk8)]

    def x_copy(b):
        slot = b % nxbuf
        return pltpu.make_async_copy(block_rows(x_hbm, b), xbuf.at[slot], xsem.at[slot])

    def y_copy(b, slot):
        return pltpu.make_async_copy(ybuf.at[slot], block_rows(y_hbm, b), ysem.at[slot])

    def w_copies(e, slot):
        return (pltpu.make_async_copy(wg_hbm.at[e], wg_raw.at[slot], wsem.at[slot]),
                pltpu.make_async_copy(wu_hbm.at[e], wu_raw.at[slot], wsem.at[slot]),
                pltpu.make_async_copy(wd_hbm.at[e], wd_raw.at[slot], wsem.at[slot]))

    for b in range(X_AHEAD):
        @pl.when(b < nused)
        def _():
            x_copy(b).start()
    for c in w_copies(blk_e_ref[0], 0):
        c.start()

    def zero_copy(b):
        return pltpu.make_async_copy(zero_sc, block_rows(y_hbm, b), zsem)

    zero_sc[...] = jnp.zeros(zero_sc.shape, F32)
    lax.fori_loop(nused, nblk, lambda b, c: (zero_copy(b).start(), c)[1], 0)

    def body(i, run):
        e = blk_e_ref[i]
        new_expert = (i == 0) | (e != blk_e_ref[jnp.maximum(i - 1, 0)])
        run = run + new_expert.astype(jnp.int32)
        wslot = run % 2

        @pl.when(new_expert)
        def _():
            for c in w_copies(e, wslot):
                c.wait()
            wgu_sc[:, 0:D_EXPERT] = wg_raw[wslot].astype(BF16)
            wgu_sc[:, D_EXPERT:2 * D_EXPERT] = wu_raw[wslot].astype(BF16)
            wd_sc[...] = wd_raw[wslot].astype(BF16)
            nxt = i + run_blocks_ref[e]

            @pl.when(nxt < nused)
            def _():
                for c in w_copies(blk_e_ref[nxt], 1 - wslot):
                    c.start()

        @pl.when(i + X_AHEAD < nused)
        def _():
            x_copy(i + X_AHEAD).start()

        yslot = i % Y_BUFS
        x_copy(i).wait()

        @pl.when(i >= Y_BUFS)
        def _():
            y_copy(i - Y_BUFS, yslot).wait()

        xb = _load_rows(xbuf.at[i % nxbuf], MOE_BLK).astype(BF16)
        h = _dot(xb, wgu_sc[...])
        hdn = (jax.nn.silu(h[:, 0:D_EXPERT]) * h[:, D_EXPERT:2 * D_EXPERT]).astype(BF16)
        _store_rows(ybuf.at[yslot], _dot(hdn, wd_sc[...]))
        y_copy(i, yslot).start()
        return run

    lax.fori_loop(0, nused, body, jnp.int32(-1))

    for back in range(1, Y_BUFS + 1):
        @pl.when(nused >= back)
        def _():
            y_copy(nused - back, (nused - back) % Y_BUFS).wait()

    lax.fori_loop(nused, nblk, lambda b, c: (zero_copy(b).wait(), c)[1], 0)


def _experts(blk_e, run_blocks, nused, buf2, wg, wu, wd):
    blk_rows = MOE_BLK * SUBLANES
    any_spec = pl.BlockSpec(memory_space=pl.ANY)
    grid_spec = pltpu.PrefetchScalarGridSpec(
        num_scalar_prefetch=3,
        grid=(1,),
        in_specs=[any_spec, any_spec, any_spec, any_spec],
        out_specs=any_spec,
        scratch_shapes=[
            pltpu.VMEM((X_AHEAD + 1, blk_rows, LANES), F32),
            pltpu.VMEM((Y_BUFS, blk_rows, LANES), F32),
            pltpu.VMEM((2, D_MODEL, D_EXPERT), F32),
            pltpu.VMEM((2, D_MODEL, D_EXPERT), F32),
            pltpu.VMEM((2, D_EXPERT, D_MODEL), F32),
            pltpu.VMEM((D_MODEL, 2 * D_EXPERT), BF16),
            pltpu.VMEM((D_EXPERT, D_MODEL), BF16),
            pltpu.VMEM((blk_rows, LANES), F32),
            pltpu.SemaphoreType.DMA((X_AHEAD + 1,)),
            pltpu.SemaphoreType.DMA((Y_BUFS,)),
            pltpu.SemaphoreType.DMA((2,)),
            pltpu.SemaphoreType.DMA(()),
        ],
    )
    return pl.pallas_call(
        _expert_kernel,
        grid_spec=grid_spec,
        out_shape=jax.ShapeDtypeStruct(buf2.shape, F32),
        compiler_params=pltpu.CompilerParams(
            dimension_semantics=("arbitrary",), vmem_limit_bytes=VMEM_LIMIT),
        name="experts",
    )(blk_e, run_blocks, nused, buf2, wg, wu, wd)


def _final_kernel(dcur_ref, dnext_ref, h1_ref, rw_ref, p_ref, gple_ref, wpg_ref, wpp_ref, gfin_ref,
                  ys_ref, out_ref, ybuf_a, ybuf_b, sem):
    g = pl.program_id(0)
    ng = pl.num_programs(0)
    tm = TM_ROWS
    bufs = (ybuf_a, ybuf_b)

    def row_copy(d_ref, off, r, which):
        return pltpu.make_async_copy(_row_tile(ys_ref, d_ref[0, 0, off + r]),
                                     _row_tile(bufs[which], r * SUBLANES), sem.at[which])

    def issue(d_ref, off, which):
        for r in range(2 * tm):
            row_copy(d_ref, off, r, which).start(priority=r % 2)

    def wait(which):
        pltpu.make_async_copy(ys_ref.at[pl.ds(0, 2 * tm * SUBLANES)], bufs[which],
                              sem.at[which]).wait()

    def compute(which, rows):
        y0 = _load_rows(bufs[which], tm)
        y1 = _load_rows(bufs[which], tm, first_row=tm)
        rw = rw_ref[rows, :]
        h2 = h1_ref[rows, :] + (y0 * rw[:, 0:1] + y1 * rw[:, 1:2])
        n3 = (_rms(h2) * gple_ref[...]).astype(BF16)
        gate = jax.nn.sigmoid(_dot(n3, wpg_ref[...]))
        pp = _dot(p_ref[rows, :].astype(BF16), wpp_ref[...])
        h3 = h2 + gate * pp
        out_ref[rows, :] = _rms(h3) * gfin_ref[...]

    @pl.when(g == 0)
    def _():
        def body(c, carry):
            for u in range(DMA_UNROLL):
                row_copy(dcur_ref, 0, c * DMA_UNROLL + u, 0).start()
            return carry

        lax.fori_loop(0, 2 * tm // DMA_UNROLL, body, 0)

    wait(0)
    issue(dcur_ref, 2 * tm, 1)
    compute(0, slice(0, tm))
    wait(1)
    issue(dnext_ref, 0, 0)
    compute(1, slice(tm, 2 * tm))

    @pl.when(g == ng - 1)
    def _():
        wait(0)


def _final(dest8_pairs, h1, rw, p2, g_ple, wpg_b, wpp_b, g_final, ys2):
    N = h1.shape[0]
    tm = TM_ROWS
    ng = N // (2 * tm)
    row = lambda i: (i, 0)
    const = lambda i: (0, 0)
    return pl.pallas_call(
        _final_kernel,
        grid=(ng,),
        in_specs=[
            pl.BlockSpec((1, 1, 4 * tm), lambda i: (i, 0, 0), memory_space=pltpu.SMEM),
            pl.BlockSpec((1, 1, 4 * tm), lambda i: (jnp.minimum(i + 1, ng - 1), 0, 0),
                         memory_space=pltpu.SMEM),
            pl.BlockSpec((2 * tm, D_MODEL), row),
            pl.BlockSpec((2 * tm, LANES), row),
            pl.BlockSpec((2 * tm, PLE_DIM), row),
            pl.BlockSpec((1, D_MODEL), const),
            pl.BlockSpec((D_MODEL, D_MODEL), const),
            pl.BlockSpec((PLE_DIM, D_MODEL), const),
            pl.BlockSpec((1, D_MODEL), const),
            pl.BlockSpec(memory_space=pl.ANY),
        ],
        out_specs=pl.BlockSpec((2 * tm, D_MODEL), row),
        out_shape=jax.ShapeDtypeStruct((N, D_MODEL), F32),
        scratch_shapes=[
            pltpu.VMEM((2 * tm * SUBLANES, LANES), F32),
            pltpu.VMEM((2 * tm * SUBLANES, LANES), F32),
            pltpu.SemaphoreType.DMA((2,)),
        ],
        compiler_params=pltpu.CompilerParams(
            dimension_semantics=("arbitrary",), vmem_limit_bytes=VMEM_LIMIT),
        name="final",
    )(dest8_pairs, dest8_pairs, h1, rw, p2, g_ple, wpg_b, wpp_b, g_final, ys2)


def _rope_tables(S):
    inv_freq = ROPE_THETA ** (-jnp.arange(0, QK_ROPE_DIM, 2, dtype=F32) / QK_ROPE_DIM)
    ang = jnp.arange(S, dtype=F32)[:, None] * inv_freq[None, :]
    cos, sin = jnp.cos(ang), jnp.sin(ang)
    z = jnp.zeros_like(cos)
    rc = jnp.concatenate([cos, cos, z, z], axis=1)
    rs1 = jnp.concatenate([-sin, z, z, z], axis=1)
    rs2 = jnp.concatenate([z, sin, z, z], axis=1)
    return rc, rs1, rs2


def _layer(h, p_l, g_mix, w_in, g_gv, w_spatial, b_spatial, w_gproj, g_cq, w_uq, g_ckv, w_ukv,
           w_mla_o, w_out, g_moe, w_router_g, b_router_g, w_router_e, b_router_e,
           w_e_gate, w_e_up, w_e_down, g_ple, w_ple_gate, w_ple_proj, g_out):
    B, S, D = h.shape
    N = B * S
    x2 = h.reshape(N, D)

    cu, cv, ccq, cckv, ckr, cga = (GMLP_WIDTH, 2 * GMLP_WIDTH, 2 * GMLP_WIDTH + Q_LORA,
                                   2 * GMLP_WIDTH + Q_LORA + KV_LORA,
                                   2 * GMLP_WIDTH + Q_LORA + KV_LORA + QK_ROPE_DIM,
                                   2 * GMLP_WIDTH + Q_LORA + KV_LORA + QK_ROPE_DIM + D_MODEL)
    w_in_p = jnp.concatenate(
        [w_in[:, :cckv], w_in[:, ckr:], w_in[:, cckv:ckr],
         jnp.zeros((D, LANES - QK_ROPE_DIM), w_in.dtype)], axis=1).astype(BF16)
    w_uq_h = w_uq.reshape(Q_LORA, MLA_HEADS, QK_NOPE_DIM + QK_ROPE_DIM)
    w_uq_p = jnp.concatenate(
        [w_uq_h, jnp.zeros((Q_LORA, MLA_HEADS, QK_PAD - QK_NOPE_DIM - QK_ROPE_DIM), w_uq.dtype)],
        axis=2).reshape(Q_LORA, MLA_HEADS * QK_PAD).astype(BF16)
    rc, rs1, rs2 = _rope_tables(S)

    u, v, sga, sgb, q, k, vv = _inproj(
        x2, g_mix[None], w_in_p, g_gv[None], g_cq[None], g_ckv[None], w_uq_p, w_ukv.astype(BF16),
        rc, rs1, rs2, B, S)
    ma = _gmlp(u, v, sga, w_spatial.astype(BF16), b_spatial.T, w_gproj.astype(BF16))
    o = _attention(q, k, vv)

    wr = jnp.concatenate(
        [w_router_e.T, w_router_g.T, jnp.zeros((LANES - N_EXPERTS - N_GROUPS, D), w_router_e.dtype)],
        axis=0).astype(BF16)
    br = jnp.concatenate(
        [b_router_e.reshape(-1), b_router_g, jnp.zeros((LANES - N_EXPERTS - N_GROUPS,), F32)])
    br = jnp.broadcast_to(br[:, None], (LANES, TM_OUT))
    h1, m, rt, rw, cnt = _out_route(
        o.reshape(N, D), ma, sgb, x2, w_mla_o.astype(BF16), w_out.astype(BF16), g_moe[None], wr, br)

    counts = cnt[:N_EXPERTS, 0]
    padded = (counts + MOE_BLK - 1) // MOE_BLK * MOE_BLK
    pad_end = jnp.cumsum(padded)
    pad_start = pad_end - padded
    P = 2 * N + N_EXPERTS * MOE_BLK
    nblk = P // MOE_BLK
    blk_start = jnp.arange(nblk, dtype=jnp.int32) * MOE_BLK
    blk_e = jnp.minimum(
        jnp.sum((pad_end[None, :] <= blk_start[:, None]).astype(jnp.int32), axis=1),
        N_EXPERTS - 1).astype(jnp.int32)
    nused = (pad_end[-1:] // MOE_BLK).astype(jnp.int32)
    experts = jnp.arange(N_EXPERTS, dtype=jnp.int32)[:, None, None]
    start_of = jnp.sum(jnp.where(rt[None, 0:2] == experts, pad_start[:, None, None], 0), axis=0)
    dest8 = (start_of + rt[2:4]).astype(jnp.int32) * SUBLANES
    nt = N // TM_ROWS
    dest8_tiles = dest8.reshape(2, nt, TM_ROWS).transpose(1, 0, 2).reshape(nt, 1, 2 * TM_ROWS)

    buf = _dispatch(pad_end.astype(jnp.int32), (padded - counts).astype(jnp.int32), dest8_tiles, m, P)
    ys = _experts(blk_e, (padded // MOE_BLK).astype(jnp.int32), nused, buf, w_e_gate, w_e_up, w_e_down)
    out = _final(dest8_tiles.reshape(nt // 2, 1, 4 * TM_ROWS), h1, rw, p_l.reshape(N, PLE_DIM), g_ple[None],
                 w_ple_gate.astype(BF16), w_ple_proj.astype(BF16), g_out[None], ys)
    return out.reshape(B, S, D)


def kernel(x, p, g_mix, w_in, g_gv, w_spatial, b_spatial, w_gproj, g_cq, w_uq, g_ckv, w_ukv, w_mla_o,
           w_out, g_moe, w_router_g, b_router_g, w_router_e, b_router_e, w_e_gate, w_e_up, w_e_down,
           g_ple, w_ple_gate, w_ple_proj, g_final):
    depth = p.shape[0]
    assert depth == 1, "the final rmsnorm is fused into the single layer's last kernel"
    i = 0
    return _layer(x, p[i], g_mix[i], w_in[i], g_gv[i], w_spatial[i], b_spatial[i], w_gproj[i], g_cq[i],
                  w_uq[i], g_ckv[i], w_ukv[i], w_mla_o[i], w_out[i], g_moe[i], w_router_g[i],
                  b_router_g[i], w_router_e[i], b_router_e[i], w_e_gate[i], w_e_up[i], w_e_down[i],
                  g_ple[i], w_ple_gate[i], w_ple_proj[i], g_final)
```

```python
import functools

import jax
import jax.numpy as jnp
from jax import lax
from jax.experimental import pallas as pl
from jax.experimental.pallas import tpu as pltpu

F32 = jnp.float32
BF16 = jnp.bfloat16

D_MODEL = 1024
CHUNK = 64
PLE_DIM = 256
GMLP_BLOCK = 128
GMLP_GROUPS = 12
GMLP_WIDTH = 1536
MLA_HEADS = 8
QK_NOPE_DIM = 128
QK_ROPE_DIM = 64
V_HEAD_DIM = 128
Q_LORA = 384
KV_LORA = 256
ROPE_THETA = 10000.0
N_GROUPS = 8
EXPERTS_PER_GROUP = 8
N_EXPERTS = 64
D_EXPERT = 256
EPS = 1e-6
LOG2E = 1.4426950408889634

LANES = 128
SUBLANES = 8
QK_PAD = 256
V_PAD = 256
VMEM_LIMIT = 56 * 1024 * 1024

C_U = 0
C_V = C_U + GMLP_WIDTH
C_CQ = C_V + GMLP_WIDTH
C_CKV = C_CQ + Q_LORA
C_GA = C_CKV + KV_LORA
C_GB = C_GA + D_MODEL
C_KR = C_GB + D_MODEL
C_END = C_KR + LANES

TM_IN = 512
TM_GMLP = 1024
TQ = 512
HEADS_PER_STEP = 4
TM_OUT = 1024
TM_ROWS = 512
MOE_BLK = 256
X_AHEAD = 6
Y_BUFS = 4
DMA_UNROLL = 8


def _rms(x):
    return x * lax.rsqrt(jnp.mean(x * x, axis=-1, keepdims=True) + EPS)


def _dot(a, b):
    return jnp.dot(a, b, preferred_element_type=F32)


def _store_rows(ref2, x):
    rows = x.shape[0]
    for j in range(SUBLANES):
        ref2[pl.ds(j, rows, stride=SUBLANES), :] = x[:, j * LANES:(j + 1) * LANES]


def _load_rows(ref2, rows, first_row=0):
    return jnp.concatenate(
        [ref2[pl.ds(first_row * SUBLANES + j, rows, stride=SUBLANES), :] for j in range(SUBLANES)],
        axis=1)


def _rope128(t, rc, rs1, rs2):
    r1 = pltpu.roll(t, 96, axis=1)
    r2 = pltpu.roll(t, 32, axis=1)
    return t * rc + r1 * rs1 + r2 * rs2


def _inproj_kernel(x_ref, gmix_ref, win_ref, ggv_ref, gcq_ref, gckv_ref, wuq_ref, wukv_ref,
                   rc_ref, rs1_ref, rs2_ref,
                   u_ref, v_ref, sga_ref, sgb_ref, q_ref, k_ref, vv_ref):
    x = x_ref[...]
    ab = (_rms(x) * gmix_ref[...]).astype(BF16)

    z_all = _dot(ab, win_ref[...])

    def proj(c0, c1):
        return z_all[:, c0:c1]

    u_ref[...] = jax.nn.gelu(proj(C_U, C_V)).astype(BF16)

    zv = jax.nn.gelu(proj(C_V, C_CQ))
    xc = zv - jnp.mean(zv, axis=-1, keepdims=True)
    vln = xc * lax.rsqrt(jnp.mean(xc * xc, axis=-1, keepdims=True) + EPS)
    v_ref[...] = (vln * ggv_ref[...]).astype(BF16)

    sga_ref[...] = jax.nn.sigmoid(proj(C_GA, C_GB)).astype(BF16)
    sgb_ref[...] = jax.nn.sigmoid(proj(C_GB, C_KR)).astype(BF16)

    rc = rc_ref[...]
    rs1 = rs1_ref[...]
    rs2 = rs2_ref[...]
    kpe = _rope128(proj(C_KR, C_END), rc, rs1, rs2).astype(BF16)

    cqn = (_rms(proj(C_CQ, C_CKV)) * gcq_ref[...]).astype(BF16)
    ckvn = (_rms(proj(C_CKV, C_GA)) * gckv_ref[...]).astype(BF16)
    scale = (QK_NOPE_DIM + QK_ROPE_DIM) ** -0.5 * LOG2E
    q_all = _dot(cqn, wuq_ref[...])
    kv_all = _dot(ckvn, wukv_ref[...])
    for h in range(MLA_HEADS):
        qh = q_all[:, h * QK_PAD:(h + 1) * QK_PAD]
        q_ref[0, h, :, 0:LANES] = (qh[:, 0:LANES] * scale).astype(BF16)
        q_ref[0, h, :, LANES:QK_PAD] = (_rope128(qh[:, LANES:QK_PAD], rc, rs1, rs2) * scale).astype(BF16)
        kvh = kv_all[:, h * 256:(h + 1) * 256]
        k_ref[0, h, :, 0:LANES] = kvh[:, 0:LANES].astype(BF16)
        k_ref[0, h, :, LANES:QK_PAD] = kpe
        vv_ref[0, h] = kvh[:, LANES:256].astype(BF16)


def _inproj(x2, g_mix, w_in_p, g_gv, g_cq, g_ckv, w_uq_p, w_ukv_b, rc, rs1, rs2, B, S):
    N = x2.shape[0]
    tm = TM_IN
    spt = S // tm
    row = lambda i: (i, 0)
    const = lambda i: (0, 0)
    pos = lambda i: (i % spt, 0)
    head = lambda i: (i // spt, 0, i % spt, 0)
    return pl.pallas_call(
        _inproj_kernel,
        grid=(N // tm,),
        in_specs=[
            pl.BlockSpec((tm, D_MODEL), row),
            pl.BlockSpec((1, D_MODEL), const),
            pl.BlockSpec((D_MODEL, C_END), const, pipeline_mode=pl.Buffered(1)),
            pl.BlockSpec((1, GMLP_WIDTH), const),
            pl.BlockSpec((1, Q_LORA), const),
            pl.BlockSpec((1, KV_LORA), const),
            pl.BlockSpec((Q_LORA, MLA_HEADS * QK_PAD), const),
            pl.BlockSpec((KV_LORA, MLA_HEADS * 256), const),
            pl.BlockSpec((tm, LANES), pos),
            pl.BlockSpec((tm, LANES), pos),
            pl.BlockSpec((tm, LANES), pos),
        ],
        out_specs=[
            pl.BlockSpec((tm, GMLP_WIDTH), row),
            pl.BlockSpec((tm, GMLP_WIDTH), row),
            pl.BlockSpec((tm, D_MODEL), row),
            pl.BlockSpec((tm, D_MODEL), row),
            pl.BlockSpec((1, MLA_HEADS, tm, QK_PAD), head),
            pl.BlockSpec((1, MLA_HEADS, tm, QK_PAD), head),
            pl.BlockSpec((1, MLA_HEADS, tm, V_HEAD_DIM), head),
        ],
        out_shape=[
            jax.ShapeDtypeStruct((N, GMLP_WIDTH), BF16),
            jax.ShapeDtypeStruct((N, GMLP_WIDTH), BF16),
            jax.ShapeDtypeStruct((N, D_MODEL), BF16),
            jax.ShapeDtypeStruct((N, D_MODEL), BF16),
            jax.ShapeDtypeStruct((B, MLA_HEADS, S, QK_PAD), BF16),
            jax.ShapeDtypeStruct((B, MLA_HEADS, S, QK_PAD), BF16),
            jax.ShapeDtypeStruct((B, MLA_HEADS, S, V_HEAD_DIM), BF16),
        ],
        compiler_params=pltpu.CompilerParams(
            dimension_semantics=("arbitrary",), vmem_limit_bytes=VMEM_LIMIT),
        name="inproj",
    )(x2, g_mix, w_in_p, g_gv, g_cq, g_ckv, w_uq_p, w_ukv_b, rc, rs1, rs2)


def _gmlp_kernel(u_ref, v_ref, sga_ref, wsp_ref, bsp_ref, wproj_ref, ma_ref, y_sc):
    nb = TM_GMLP // GMLP_BLOCK
    t_out = lax.broadcasted_iota(jnp.int32, (GMLP_BLOCK, GMLP_BLOCK), 0)
    s_in = lax.broadcasted_iota(jnp.int32, (GMLP_BLOCK, GMLP_BLOCK), 1)
    mask = (s_in // CHUNK) <= (t_out // CHUNK)
    for g in range(GMLP_GROUPS):
        c0 = g * LANES
        w = jnp.where(mask, wsp_ref[g], jnp.zeros((), BF16))
        rhs = jnp.concatenate(
            [v_ref[r * GMLP_BLOCK:(r + 1) * GMLP_BLOCK, c0:c0 + LANES] for r in range(nb)], axis=1)
        sv = _dot(w, rhs) + bsp_ref[:, g:g + 1]
        for r in range(nb):
            rows = slice(r * GMLP_BLOCK, (r + 1) * GMLP_BLOCK)
            ub = u_ref[rows, c0:c0 + LANES].astype(F32)
            y_sc[rows, c0:c0 + LANES] = (ub * sv[:, r * LANES:(r + 1) * LANES]).astype(BF16)
    ya = _dot(y_sc[...], wproj_ref[...])
    ma_ref[...] = (sga_ref[...].astype(F32) * ya).astype(BF16)


def _gmlp(u, v, sga, wsp_b, bsp_t, wproj_b):
    N = u.shape[0]
    tm = TM_GMLP
    row = lambda i: (i, 0)
    return pl.pallas_call(
        _gmlp_kernel,
        grid=(N // tm,),
        in_specs=[
            pl.BlockSpec((tm, GMLP_WIDTH), row),
            pl.BlockSpec((tm, GMLP_WIDTH), row),
            pl.BlockSpec((tm, D_MODEL), row),
            pl.BlockSpec((GMLP_GROUPS, GMLP_BLOCK, GMLP_BLOCK), lambda i: (0, 0, 0)),
            pl.BlockSpec((GMLP_BLOCK, GMLP_GROUPS), lambda i: (0, 0)),
            pl.BlockSpec((GMLP_WIDTH, D_MODEL), lambda i: (0, 0)),
        ],
        out_specs=pl.BlockSpec((tm, D_MODEL), row),
        out_shape=jax.ShapeDtypeStruct((N, D_MODEL), BF16),
        scratch_shapes=[pltpu.VMEM((tm, GMLP_WIDTH), BF16)],
        compiler_params=pltpu.CompilerParams(
            dimension_semantics=("arbitrary",), vmem_limit_bytes=VMEM_LIMIT),
        name="gmlp",
    )(u, v, sga, wsp_b, bsp_t, wproj_b)


NEG_BIG = -1e30


def _attn_kernel(q_ref, k_ref, v_ref, o_ref, m_sc, acc_sc):
    qi = pl.program_id(2)
    m_sc[...] = jnp.full(m_sc.shape, NEG_BIG, F32)
    acc_sc[...] = jnp.zeros(acc_sc.shape, F32)
    ones_col = (lax.broadcasted_iota(jnp.int32, (TQ, V_PAD - V_HEAD_DIM), 1) == 0).astype(BF16)

    def chunk_mask(r0, nq, nk):
        qc = (lax.broadcasted_iota(jnp.int32, (nq, nk), 0) + r0) // CHUNK
        kc = lax.broadcasted_iota(jnp.int32, (nq, nk), 1) // CHUNK
        return kc <= qc

    def update(hh, r0, nq, start, nk, mask):
        rows = pl.ds(r0, nq)
        kb = k_ref[0, hh, pl.ds(start, nk), :]
        vb = jnp.concatenate([v_ref[0, hh, pl.ds(start, nk), :], ones_col[0:nk]], axis=1)
        s = lax.dot_general(q_ref[0, hh, rows, :], kb, (((1,), (1,)), ((), ())),
                            preferred_element_type=F32)
        if mask is not None:
            s = jnp.where(mask, s, NEG_BIG)
        tiles = [s[:, c * LANES:(c + 1) * LANES] for c in range(nk // LANES)]
        tile_max = functools.reduce(jnp.maximum, tiles)
        m_prev = m_sc[hh, rows, :]
        m_new = jnp.maximum(m_prev, jnp.max(tile_max, axis=-1, keepdims=True))
        alpha = jnp.exp2(m_prev - m_new)
        p = jnp.concatenate([jnp.exp2(t - m_new).astype(BF16) for t in tiles], axis=1)
        alpha2 = jnp.concatenate([alpha] * (V_PAD // LANES), axis=1)
        acc_sc[hh, rows, :] = alpha2 * acc_sc[hh, rows, :] + _dot(p, vb)
        m_sc[hh, rows, :] = m_new

    def body(j, carry):
        start = pl.multiple_of(j * TQ, TQ)
        for hh in range(HEADS_PER_STEP):
            update(hh, 0, TQ, start, TQ, None)
        return carry

    lax.fori_loop(0, qi, body, 0)
    start = pl.multiple_of(qi * TQ, TQ)
    mask = chunk_mask(0, TQ, TQ)
    for hh in range(HEADS_PER_STEP):
        update(hh, 0, TQ, start, TQ, mask)
    for hh in range(HEADS_PER_STEP):
        l = acc_sc[hh, :, V_HEAD_DIM:V_HEAD_DIM + 1]
        o_ref[0, :, hh * V_HEAD_DIM:(hh + 1) * V_HEAD_DIM] = (
            acc_sc[hh, :, 0:V_HEAD_DIM] / l).astype(BF16)


def _attention(q, k, v):
    B, H, S, _ = q.shape
    hps = HEADS_PER_STEP
    return pl.pallas_call(
        _attn_kernel,
        grid=(B, H // hps, S // TQ),
        in_specs=[
            pl.BlockSpec((1, hps, TQ, QK_PAD), lambda b, h, i: (b, h, i, 0)),
            pl.BlockSpec((1, hps, S, QK_PAD), lambda b, h, i: (b, h, 0, 0)),
            pl.BlockSpec((1, hps, S, V_HEAD_DIM), lambda b, h, i: (b, h, 0, 0)),
        ],
        out_specs=pl.BlockSpec((1, TQ, hps * V_HEAD_DIM), lambda b, h, i: (b, i, h)),
        out_shape=jax.ShapeDtypeStruct((B, S, H * V_HEAD_DIM), BF16),
        scratch_shapes=[
            pltpu.VMEM((hps, TQ, LANES), F32),
            pltpu.VMEM((hps, TQ, V_PAD), F32),
        ],
        compiler_params=pltpu.CompilerParams(
            dimension_semantics=("arbitrary", "arbitrary", "arbitrary"),
            vmem_limit_bytes=VMEM_LIMIT),
        name="attention",
    )(q, k, v)


def _col_sum(x):
    return jnp.sum(x, axis=0, keepdims=True)


def _col_max(x):
    return jnp.max(x, axis=0, keepdims=True)


def _first_row(hit, row_f):
    return jnp.min(jnp.where(hit, row_f, float(LANES)), axis=0, keepdims=True).astype(jnp.int32)


def _out_route_kernel(o_ref, ma_ref, sgb_ref, x_ref, wo_ref, wout_ref, gmoe_ref, wr_ref, br_ref,
                      h1_ref, m_ref, rt_ref, rw_ref, cnt_ref, carry_sc):
    i = pl.program_id(0)
    tm = TM_OUT

    @pl.when(i == 0)
    def _():
        carry_sc[...] = jnp.zeros(carry_sc.shape, F32)

    yb = _dot(o_ref[...], wo_ref[...])
    merged = ma_ref[...].astype(F32) + sgb_ref[...].astype(F32) * yb
    h1 = x_ref[...] + _dot(merged.astype(BF16), wout_ref[...])
    h1_ref[...] = h1
    m = _rms(h1) * gmoe_ref[...]
    _store_rows(m_ref, m)

    logits = lax.dot_general(wr_ref[...], m.astype(BF16), (((1,), (1,)), ((), ())),
                             preferred_element_type=F32)
    row = lax.broadcasted_iota(jnp.int32, (LANES, tm), 0)
    row_f = row.astype(F32)
    bias = br_ref[...]
    is_g = (row >= N_EXPERTS) & (row < N_EXPERTS + N_GROUPS)
    neg = jnp.float32(-jnp.inf)

    gl = jnp.where(is_g, logits, neg)
    ge = jnp.where(is_g, jnp.exp(gl - _col_max(gl)), 0.0)
    g_prob = ge / _col_sum(ge)
    g_score = jnp.where(is_g, g_prob + bias, neg)
    g_row = _first_row(g_score == _col_max(g_score), row_f)
    g_w = _col_sum(jnp.where(row == g_row, g_prob, 0.0))
    g_idx = g_row - N_EXPERTS

    in_g = (row // EXPERTS_PER_GROUP) == g_idx
    el = jnp.where(in_g, logits, neg)
    ee = jnp.where(in_g, jnp.exp(el - _col_max(el)), 0.0)
    e_prob = ee / _col_sum(ee)
    e_score = jnp.where(in_g, e_prob + bias, neg)
    id1 = _first_row(e_score == _col_max(e_score), row_f)
    e_score2 = jnp.where(row == id1, neg, e_score)
    id2 = _first_row(e_score2 == _col_max(e_score2), row_f)
    p1 = _col_sum(jnp.where(row == id1, e_prob, 0.0))
    p2 = _col_sum(jnp.where(row == id2, e_prob, 0.0))
    psum = p1 + p2
    w1 = g_w * (p1 / psum)
    w2 = g_w * (p2 / psum)

    oh = ((row == id1) | (row == id2 + N_EXPERTS)).astype(BF16)
    t_in = lax.broadcasted_iota(jnp.int32, (tm, tm), 0)
    t_out = lax.broadcasted_iota(jnp.int32, (tm, tm), 1)
    tri = (t_in < t_out).astype(BF16)
    prefix = _dot(oh, tri)
    tot = jnp.sum(oh.astype(F32), axis=1, keepdims=True)
    tot_sw = jnp.concatenate([tot[N_EXPERTS:], tot[:N_EXPERTS]], axis=0)
    row1 = lax.broadcasted_iota(jnp.int32, (LANES, 1), 0)
    carry = carry_sc[...]
    base = carry + jnp.where(row1 >= N_EXPERTS, tot_sw, 0.0)
    rk = oh.astype(F32) * (base + prefix)
    rank1 = _col_sum(jnp.where(row < N_EXPERTS, rk, 0.0))
    rank2 = _col_sum(jnp.where(row >= N_EXPERTS, rk, 0.0))
    carry_new = carry + tot + tot_sw
    carry_sc[...] = carry_new
    cnt_ref[...] = carry_new.astype(jnp.int32)

    row8 = lax.broadcasted_iota(jnp.int32, (SUBLANES, tm), 0)
    rt = jnp.where(row8 == 0, id1, jnp.where(row8 == 1, id2, 0))
    rt = jnp.where(row8 == 2, rank1.astype(jnp.int32), rt)
    rt_ref[...] = jnp.where(row8 == 3, rank2.astype(jnp.int32), rt)
    wt = jnp.where(row == 0, w1, jnp.where(row == 1, w2, 0.0))
    rw_ref[...] = wt.T


def _out_route(o2, ma, sgb, x2, wo_b, wout_b, g_moe, wr_b, br):
    N = x2.shape[0]
    tm = TM_OUT
    row = lambda i: (i, 0)
    const = lambda i: (0, 0)
    return pl.pallas_call(
        _out_route_kernel,
        grid=(N // tm,),
        in_specs=[
            pl.BlockSpec((tm, D_MODEL), row),
            pl.BlockSpec((tm, D_MODEL), row),
            pl.BlockSpec((tm, D_MODEL), row),
            pl.BlockSpec((tm, D_MODEL), row),
            pl.BlockSpec((D_MODEL, D_MODEL), const),
            pl.BlockSpec((D_MODEL, D_MODEL), const),
            pl.BlockSpec((1, D_MODEL), const),
            pl.BlockSpec((LANES, D_MODEL), const),
            pl.BlockSpec((LANES, tm), const),
        ],
        out_specs=[
            pl.BlockSpec((tm, D_MODEL), row),
            pl.BlockSpec((tm * SUBLANES, LANES), row),
            pl.BlockSpec((SUBLANES, tm), lambda i: (0, i)),
            pl.BlockSpec((tm, LANES), row),
            pl.BlockSpec((LANES, 1), const),
        ],
        out_shape=[
            jax.ShapeDtypeStruct((N, D_MODEL), F32),
            jax.ShapeDtypeStruct((N * SUBLANES, LANES), F32),
            jax.ShapeDtypeStruct((SUBLANES, N), jnp.int32),
            jax.ShapeDtypeStruct((N, LANES), F32),
            jax.ShapeDtypeStruct((LANES, 1), jnp.int32),
        ],
        scratch_shapes=[pltpu.VMEM((LANES, 1), F32)],
        compiler_params=pltpu.CompilerParams(
            dimension_semantics=("arbitrary",), vmem_limit_bytes=VMEM_LIMIT),
        name="out_route",
    )(o2, ma, sgb, x2, wo_b, wout_b, g_moe, wr_b, br)


def _row_tile(ref2, row8):
    return ref2.at[pl.ds(pl.multiple_of(row8, SUBLANES), SUBLANES)]


def _dispatch_kernel(pad_end_ref, npad_ref, dest_ref, m_ref, buf_ref, zero_sc, sem, zsem):
    i = pl.program_id(0)
    tm = TM_ROWS
    blk8 = MOE_BLK * SUBLANES

    def zero_fill(go):
        def per_expert(e, carry):
            npad = npad_ref[e]
            row = pad_end_ref[e] - npad
            k = MOE_BLK // 2
            while k >= 1:
                @pl.when((npad & k) != 0)
                def _():
                    go(pltpu.make_async_copy(
                        zero_sc.at[pl.ds(0, k * SUBLANES)],
                        buf_ref.at[pl.ds(pl.multiple_of(row * SUBLANES, SUBLANES), k * SUBLANES)],
                        zsem))
                row = row + (npad & k)
                k //= 2
            return carry

        def tail_copy(b):
            return pltpu.make_async_copy(
                zero_sc, buf_ref.at[pl.ds(pl.multiple_of(b * blk8, blk8), blk8)], zsem)

        lax.fori_loop(0, N_EXPERTS, per_expert, 0)
        nused = pad_end_ref[N_EXPERTS - 1] // MOE_BLK
        lax.fori_loop(nused, buf_ref.shape[0] // blk8, lambda b, c: (go(tail_copy(b)), c)[1], 0)

    @pl.when(i == 0)
    def _():
        zero_sc[...] = jnp.zeros(zero_sc.shape, F32)
        zero_fill(lambda c: c.start())

    def issue(c, carry):
        for u in range(DMA_UNROLL):
            r = c * DMA_UNROLL + u
            src = _row_tile(m_ref, r * SUBLANES)
            pltpu.make_async_copy(src, _row_tile(buf_ref, dest_ref[0, 0, r]), sem).start(priority=0)
            pltpu.make_async_copy(src, _row_tile(buf_ref, dest_ref[0, 0, tm + r]), sem).start(priority=1)
        return carry

    lax.fori_loop(0, tm // DMA_UNROLL, issue, 0)
    for _ in range(2):
        pltpu.make_async_copy(m_ref, buf_ref.at[pl.ds(0, tm * SUBLANES)], sem).wait()

    @pl.when(i == pl.num_programs(0) - 1)
    def _():
        zero_fill(lambda c: c.wait())


def _dispatch(pad_end, npad, dest8_tiles, m2, P):
    tm = TM_ROWS
    N = m2.shape[0] // SUBLANES
    grid_spec = pltpu.PrefetchScalarGridSpec(
        num_scalar_prefetch=2,
        grid=(N // tm,),
        in_specs=[
            pl.BlockSpec((1, 1, 2 * tm), lambda i, pe, pd: (i, 0, 0), memory_space=pltpu.SMEM),
            pl.BlockSpec((tm * SUBLANES, LANES), lambda i, pe, pd: (i, 0)),
        ],
        out_specs=pl.BlockSpec(memory_space=pl.ANY),
        scratch_shapes=[
            pltpu.VMEM((MOE_BLK * SUBLANES, LANES), F32),
            pltpu.SemaphoreType.DMA(()),
            pltpu.SemaphoreType.DMA(()),
        ],
    )
    return pl.pallas_call(
        _dispatch_kernel,
        grid_spec=grid_spec,
        out_shape=jax.ShapeDtypeStruct((P * SUBLANES, LANES), F32),
        compiler_params=pltpu.CompilerParams(dimension_semantics=("arbitrary",)),
        name="dispatch",
    )(pad_end, npad, dest8_tiles, m2)


def _expert_kernel(blk_e_ref, run_blocks_ref, nused_ref, x_hbm, wg_hbm, wu_hbm, wd_hbm, y_hbm,
                   xbuf, ybuf, wg_raw, wu_raw, wd_raw, wgu_sc, wd_sc, zero_sc,
                   xsem, ysem, wsem, zsem):
    nused = nused_ref[0]
    blk8 = MOE_BLK * SUBLANES
    nblk = y_hbm.shape[0] // blk8
    nxbuf = X_AHEAD + 1

    def block_rows(ref, b):
        return ref.at[pl.ds(pl.multiple_of(b * blk8, blk8), blk8)]

    def x_copy(b):
        slot = b % nxbuf
        return pltpu.make_async_copy(block_rows(x_hbm, b), xbuf.at[slot], xsem.at[slot])

    def y_copy(b, slot):
        return pltpu.make_async_copy(ybuf.at[slot], block_rows(y_hbm, b), ysem.at[slot])

    def w_copies(e, slot):
        return (pltpu.make_async_copy(wg_hbm.at[e], wg_raw.at[slot], wsem.at[slot]),
                pltpu.make_async_copy(wu_hbm.at[e], wu_raw.at[slot], wsem.at[slot]),
                pltpu.make_async_copy(wd_hbm.at[e], wd_raw.at[slot], wsem.at[slot]))

    for b in range(X_AHEAD):
        @pl.when(b < nused)
        def _():
            x_copy(b).start()
    for c in w_copies(blk_e_ref[0], 0):
        c.start()

    def zero_copy(b):
        return pltpu.make_async_copy(zero_sc, block_rows(y_hbm, b), zsem)

    zero_sc[...] = jnp.zeros(zero_sc.shape, F32)
    lax.fori_loop(nused, nblk, lambda b, c: (zero_copy(b).start(), c)[1], 0)

    def body(i, run):
        e = blk_e_ref[i]
        new_expert = (i == 0) | (e != blk_e_ref[jnp.maximum(i - 1, 0)])
        run = run + new_expert.astype(jnp.int32)
        wslot = run % 2

        @pl.when(new_expert)
        def _():
            for c in w_copies(e, wslot):
                c.wait()
            wgu_sc[:, 0:D_EXPERT] = wg_raw[wslot].astype(BF16)
            wgu_sc[:, D_EXPERT:2 * D_EXPERT] = wu_raw[wslot].astype(BF16)
            wd_sc[...] = wd_raw[wslot].astype(BF16)
            nxt = i + run_blocks_ref[e]

            @pl.when(nxt < nused)
            def _():
                for c in w_copies(blk_e_ref[nxt], 1 - wslot):
                    c.start()

        @pl.when(i + X_AHEAD < nused)
        def _():
            x_copy(i + X_AHEAD).start()

        yslot = i % Y_BUFS
        x_copy(i).wait()

        @pl.when(i >= Y_BUFS)
        def _():
            y_copy(i - Y_BUFS, yslot).wait()

        xb = _load_rows(xbuf.at[i % nxbuf], MOE_BLK).astype(BF16)
        h = _dot(xb, wgu_sc[...])
        hdn = (jax.nn.silu(h[:, 0:D_EXPERT]) * h[:, D_EXPERT:2 * D_EXPERT]).astype(BF16)
        _store_rows(ybuf.at[yslot], _dot(hdn, wd_sc[...]))
        y_copy(i, yslot).start()
        return run

    lax.fori_loop(0, nused, body, jnp.int32(-1))

    for back in range(1, Y_BUFS + 1):
        @pl.when(nused >= back)
        def _():
            y_copy(nused - back, (nused - back) % Y_BUFS).wait()

    lax.fori_loop(nused, nblk, lambda b, c: (zero_copy(b).wait(), c)[1], 0)


def _experts(blk_e, run_blocks, nused, buf2, wg, wu, wd):
    blk_rows = MOE_BLK * SUBLANES
    any_spec = pl.BlockSpec(memory_space=pl.ANY)
    grid_spec = pltpu.PrefetchScalarGridSpec(
        num_scalar_prefetch=3,
        grid=(1,),
        in_specs=[any_spec, any_spec, any_spec, any_spec],
        out_specs=any_spec,
        scratch_shapes=[
            pltpu.VMEM((X_AHEAD + 1, blk_rows, LANES), F32),
            pltpu.VMEM((Y_BUFS, blk_rows, LANES), F32),
            pltpu.VMEM((2, D_MODEL, D_EXPERT), F32),
            pltpu.VMEM((2, D_MODEL, D_EXPERT), F32),
            pltpu.VMEM((2, D_EXPERT, D_MODEL), F32),
            pltpu.VMEM((D_MODEL, 2 * D_EXPERT), BF16),
            pltpu.VMEM((D_EXPERT, D_MODEL), BF16),
            pltpu.VMEM((blk_rows, LANES), F32),
            pltpu.SemaphoreType.DMA((X_AHEAD + 1,)),
            pltpu.SemaphoreType.DMA((Y_BUFS,)),
            pltpu.SemaphoreType.DMA((2,)),
            pltpu.SemaphoreType.DMA(()),
        ],
    )
    return pl.pallas_call(
        _expert_kernel,
        grid_spec=grid_spec,
        out_shape=jax.ShapeDtypeStruct(buf2.shape, F32),
        compiler_params=pltpu.CompilerParams(
            dimension_semantics=("arbitrary",), vmem_limit_bytes=VMEM_LIMIT),
        name="experts",
    )(blk_e, run_blocks, nused, buf2, wg, wu, wd)


def _final_kernel(dcur_ref, dnext_ref, h1_ref, rw_ref, p_ref, gple_ref, wpg_ref, wpp_ref, gfin_ref,
                  ys_ref, out_ref, ybuf_a, ybuf_b, sem):
    g = pl.program_id(0)
    ng = pl.num_programs(0)
    tm = TM_ROWS
    bufs = (ybuf_a, ybuf_b)

    def row_copy(d_ref, off, r, which):
        return pltpu.make_async_copy(_row_tile(ys_ref, d_ref[0, 0, off + r]),
                                     _row_tile(bufs[which], r * SUBLANES), sem.at[which])

    def issue(d_ref, off, which):
        for r in range(2 * tm):
            row_copy(d_ref, off, r, which).start(priority=r % 2)

    def wait(which):
        pltpu.make_async_copy(ys_ref.at[pl.ds(0, 2 * tm * SUBLANES)], bufs[which],
                              sem.at[which]).wait()

    def compute(which, rows):
        y0 = _load_rows(bufs[which], tm)
        y1 = _load_rows(bufs[which], tm, first_row=tm)
        rw = rw_ref[rows, :]
        h2 = h1_ref[rows, :] + (y0 * rw[:, 0:1] + y1 * rw[:, 1:2])
        n3 = (_rms(h2) * gple_ref[...]).astype(BF16)
        gate = jax.nn.sigmoid(_dot(n3, wpg_ref[...]))
        pp = _dot(p_ref[rows, :].astype(BF16), wpp_ref[...])
        h3 = h2 + gate * pp
        out_ref[rows, :] = _rms(h3) * gfin_ref[...]

    @pl.when(g == 0)
    def _():
        def body(c, carry):
            for u in range(DMA_UNROLL):
                row_copy(dcur_ref, 0, c * DMA_UNROLL + u, 0).start()
            return carry

        lax.fori_loop(0, 2 * tm // DMA_UNROLL, body, 0)

    wait(0)
    issue(dcur_ref, 2 * tm, 1)
    compute(0, slice(0, tm))
    wait(1)
    issue(dnext_ref, 0, 0)
    compute(1, slice(tm, 2 * tm))

    @pl.when(g == ng - 1)
    def _():
        wait(0)


def _final(dest8_pairs, h1, rw, p2, g_ple, wpg_b, wpp_b, g_final, ys2):
    N = h1.shape[0]
    tm = TM_ROWS
    ng = N // (2 * tm)
    row = lambda i: (i, 0)
    const = lambda i: (0, 0)
    return pl.pallas_call(
        _final_kernel,
        grid=(ng,),
        in_specs=[
            pl.BlockSpec((1, 1, 4 * tm), lambda i: (i, 0, 0), memory_space=pltpu.SMEM),
            pl.BlockSpec((1, 1, 4 * tm), lambda i: (jnp.minimum(i + 1, ng - 1), 0, 0),
                         memory_space=pltpu.SMEM),
            pl.BlockSpec((2 * tm, D_MODEL), row),
            pl.BlockSpec((2 * tm, LANES), row),
            pl.BlockSpec((2 * tm, PLE_DIM), row),
            pl.BlockSpec((1, D_MODEL), const),
            pl.BlockSpec((D_MODEL, D_MODEL), const),
            pl.BlockSpec((PLE_DIM, D_MODEL), const),
            pl.BlockSpec((1, D_MODEL), const),
            pl.BlockSpec(memory_space=pl.ANY),
        ],
        out_specs=pl.BlockSpec((2 * tm, D_MODEL), row),
        out_shape=jax.ShapeDtypeStruct((N, D_MODEL), F32),
        scratch_shapes=[
            pltpu.VMEM((2 * tm * SUBLANES, LANES), F32),
            pltpu.VMEM((2 * tm * SUBLANES, LANES), F32),
            pltpu.SemaphoreType.DMA((2,)),
        ],
        compiler_params=pltpu.CompilerParams(
            dimension_semantics=("arbitrary",), vmem_limit_bytes=VMEM_LIMIT),
        name="final",
    )(dest8_pairs, dest8_pairs, h1, rw, p2, g_ple, wpg_b, wpp_b, g_final, ys2)


def _rope_tables(S):
    inv_freq = ROPE_THETA ** (-jnp.arange(0, QK_ROPE_DIM, 2, dtype=F32) / QK_ROPE_DIM)
    ang = jnp.arange(S, dtype=F32)[:, None] * inv_freq[None, :]
    cos, sin = jnp.cos(ang), jnp.sin(ang)
    z = jnp.zeros_like(cos)
    rc = jnp.concatenate([cos, cos, z, z], axis=1)
    rs1 = jnp.concatenate([-sin, z, z, z], axis=1)
    rs2 = jnp.concatenate([z, sin, z, z], axis=1)
    return rc, rs1, rs2


def _layer(h, p_l, g_mix, w_in, g_gv, w_spatial, b_spatial, w_gproj, g_cq, w_uq, g_ckv, w_ukv,
           w_mla_o, w_out, g_moe, w_router_g, b_router_g, w_router_e, b_router_e,
           w_e_gate, w_e_up, w_e_down, g_ple, w_ple_gate, w_ple_proj, g_out):
    B, S, D = h.shape
    N = B * S
    x2 = h.reshape(N, D)

    cu, cv, ccq, cckv, ckr, cga = (GMLP_WIDTH, 2 * GMLP_WIDTH, 2 * GMLP_WIDTH + Q_LORA,
                                   2 * GMLP_WIDTH + Q_LORA + KV_LORA,
                                   2 * GMLP_WIDTH + Q_LORA + KV_LORA + QK_ROPE_DIM,
                                   2 * GMLP_WIDTH + Q_LORA + KV_LORA + QK_ROPE_DIM + D_MODEL)
    w_in_p = jnp.concatenate(
        [w_in[:, :cckv], w_in[:, ckr:], w_in[:, cckv:ckr],
         jnp.zeros((D, LANES - QK_ROPE_DIM), w_in.dtype)], axis=1).astype(BF16)
    w_uq_h = w_uq.reshape(Q_LORA, MLA_HEADS, QK_NOPE_DIM + QK_ROPE_DIM)
    w_uq_p = jnp.concatenate(
        [w_uq_h, jnp.zeros((Q_LORA, MLA_HEADS, QK_PAD - QK_NOPE_DIM - QK_ROPE_DIM), w_uq.dtype)],
        axis=2).reshape(Q_LORA, MLA_HEADS * QK_PAD).astype(BF16)
    rc, rs1, rs2 = _rope_tables(S)

    u, v, sga, sgb, q, k, vv = _inproj(
        x2, g_mix[None], w_in_p, g_gv[None], g_cq[None], g_ckv[None], w_uq_p, w_ukv.astype(BF16),
        rc, rs1, rs2, B, S)
    ma = _gmlp(u, v, sga, w_spatial.astype(BF16), b_spatial.T, w_gproj.astype(BF16))
    o = _attention(q, k, vv)

    wr = jnp.concatenate(
        [w_router_e.T, w_router_g.T, jnp.zeros((LANES - N_EXPERTS - N_GROUPS, D), w_router_e.dtype)],
        axis=0).astype(BF16)
    br = jnp.concatenate(
        [b_router_e.reshape(-1), b_router_g, jnp.zeros((LANES - N_EXPERTS - N_GROUPS,), F32)])
    br = jnp.broadcast_to(br[:, None], (LANES, TM_OUT))
    h1, m, rt, rw, cnt = _out_route(
        o.reshape(N, D), ma, sgb, x2, w_mla_o.astype(BF16), w_out.astype(BF16), g_moe[None], wr, br)

    counts = cnt[:N_EXPERTS, 0]
    padded = (counts + MOE_BLK - 1) // MOE_BLK * MOE_BLK
    pad_end = jnp.cumsum(padded)
    pad_start = pad_end - padded
    P = 2 * N + N_EXPERTS * MOE_BLK
    nblk = P // MOE_BLK
    blk_start = jnp.arange(nblk, dtype=jnp.int32) * MOE_BLK
    blk_e = jnp.minimum(
        jnp.sum((pad_end[None, :] <= blk_start[:, None]).astype(jnp.int32), axis=1),
        N_EXPERTS - 1).astype(jnp.int32)
    nused = (pad_end[-1:] // MOE_BLK).astype(jnp.int32)
    experts = jnp.arange(N_EXPERTS, dtype=jnp.int32)[:, None, None]
    start_of = jnp.sum(jnp.where(rt[None, 0:2] == experts, pad_start[:, None, None], 0), axis=0)
    dest8 = (start_of + rt[2:4]).astype(jnp.int32) * SUBLANES
    nt = N // TM_ROWS
    dest8_tiles = dest8.reshape(2, nt, TM_ROWS).transpose(1, 0, 2).reshape(nt, 1, 2 * TM_ROWS)

    buf = _dispatch(pad_end.astype(jnp.int32), (padded - counts).astype(jnp.int32), dest8_tiles, m, P)
    ys = _experts(blk_e, (padded // MOE_BLK).astype(jnp.int32), nused, buf, w_e_gate, w_e_up, w_e_down)
    out = _final(dest8_tiles.reshape(nt // 2, 1, 4 * TM_ROWS), h1, rw, p_l.reshape(N, PLE_DIM), g_ple[None],
                 w_ple_gate.astype(BF16), w_ple_proj.astype(BF16), g_out[None], ys)
    return out.reshape(B, S, D)


def kernel(x, p, g_mix, w_in, g_gv, w_spatial, b_spatial, w_gproj, g_cq, w_uq, g_ckv, w_ukv, w_mla_o,
           w_out, g_moe, w_router_g, b_router_g, w_router_e, b_router_e, w_e_gate, w_e_up, w_e_down,
           g_ple, w_ple_gate, w_ple_proj, g_final):
    depth = p.shape[0]
    assert depth == 1, "the final rmsnorm is fused into the single layer's last kernel"
    i = 0
    return _layer(x, p[i], g_mix[i], w_in[i], g_gv[i], w_spatial[i], b_spatial[i], w_gproj[i], g_cq[i],
                  w_uq[i], g_ckv[i], w_ukv[i], w_mla_o[i], w_out[i], g_moe[i], w_router_g[i],
                  b_router_g[i], w_router_e[i], b_router_e[i], w_e_gate[i], w_e_up[i], w_e_down[i],
                  g_ple[i], w_ple_gate[i], w_ple_proj[i], g_final)
```

```python
import functools

import jax
import jax.numpy as jnp
from jax import lax
from jax.experimental import pallas as pl
from jax.experimental.pallas import tpu as pltpu

F32 = jnp.float32
BF16 = jnp.bfloat16

D_MODEL = 1024
CHUNK = 64
PLE_DIM = 256
GMLP_BLOCK = 128
GMLP_GROUPS = 12
GMLP_WIDTH = 1536
MLA_HEADS = 8
QK_NOPE_DIM = 128
QK_ROPE_DIM = 64
V_HEAD_DIM = 128
Q_LORA = 384
KV_LORA = 256
ROPE_THETA = 10000.0
N_GROUPS = 8
EXPERTS_PER_GROUP = 8
N_EXPERTS = 64
D_EXPERT = 256
EPS = 1e-6
LOG2E = 1.4426950408889634

LANES = 128
SUBLANES = 8
QK_PAD = 256
V_PAD = 256
VMEM_LIMIT = 56 * 1024 * 1024

C_U = 0
C_V = C_U + GMLP_WIDTH
C_CQ = C_V + GMLP_WIDTH
C_CKV = C_CQ + Q_LORA
C_GA = C_CKV + KV_LORA
C_GB = C_GA + D_MODEL
C_KR = C_GB + D_MODEL
C_END = C_KR + LANES

TM_IN = 512
TM_GMLP = 1024
TQ = 512
HEADS_PER_STEP = 4
TM_OUT = 1024
TM_ROWS = 512
MOE_BLK = 256
X_AHEAD = 6
Y_BUFS = 4
DMA_UNROLL = 8


def _rms(x):
    return x * lax.rsqrt(jnp.mean(x * x, axis=-1, keepdims=True) + EPS)


def _dot(a, b):
    return jnp.dot(a, b, preferred_element_type=F32)


def _store_rows(ref2, x):
    rows = x.shape[0]
    for j in range(SUBLANES):
        ref2[pl.ds(j, rows, stride=SUBLANES), :] = x[:, j * LANES:(j + 1) * LANES]


def _load_rows(ref2, rows, first_row=0):
    return jnp.concatenate(
        [ref2[pl.ds(first_row * SUBLANES + j, rows, stride=SUBLANES), :] for j in range(SUBLANES)],
        axis=1)


def _rope128(t, rc, rs1, rs2):
    r1 = pltpu.roll(t, 96, axis=1)
    r2 = pltpu.roll(t, 32, axis=1)
    return t * rc + r1 * rs1 + r2 * rs2


def _inproj_kernel(x_ref, gmix_ref, win_ref, ggv_ref, gcq_ref, gckv_ref, wuq_ref, wukv_ref,
                   rc_ref, rs1_ref, rs2_ref,
                   u_ref, v_ref, sga_ref, sgb_ref, q_ref, k_ref, vv_ref):
    x = x_ref[...]
    ab = (_rms(x) * gmix_ref[...]).astype(BF16)

    z_all = _dot(ab, win_ref[...])

    def proj(c0, c1):
        return z_all[:, c0:c1]

    u_ref[...] = jax.nn.gelu(proj(C_U, C_V)).astype(BF16)

    zv = jax.nn.gelu(proj(C_V, C_CQ))
    xc = zv - jnp.mean(zv, axis=-1, keepdims=True)
    vln = xc * lax.rsqrt(jnp.mean(xc * xc, axis=-1, keepdims=True) + EPS)
    v_ref[...] = (vln * ggv_ref[...]).astype(BF16)

    sga_ref[...] = jax.nn.sigmoid(proj(C_GA, C_GB)).astype(BF16)
    sgb_ref[...] = jax.nn.sigmoid(proj(C_GB, C_KR)).astype(BF16)

    rc = rc_ref[...]
    rs1 = rs1_ref[...]
    rs2 = rs2_ref[...]
    kpe = _rope128(proj(C_KR, C_END), rc, rs1, rs2).astype(BF16)

    cqn = (_rms(proj(C_CQ, C_CKV)) * gcq_ref[...]).astype(BF16)
    ckvn = (_rms(proj(C_CKV, C_GA)) * gckv_ref[...]).astype(BF16)
    scale = (QK_NOPE_DIM + QK_ROPE_DIM) ** -0.5 * LOG2E
    q_all = _dot(cqn, wuq_ref[...])
    kv_all = _dot(ckvn, wukv_ref[...])
    for h in range(MLA_HEADS):
        qh = q_all[:, h * QK_PAD:(h + 1) * QK_PAD]
        q_ref[0, h, :, 0:LANES] = (qh[:, 0:LANES] * scale).astype(BF16)
        q_ref[0, h, :, LANES:QK_PAD] = (_rope128(qh[:, LANES:QK_PAD], rc, rs1, rs2) * scale).astype(BF16)
        kvh = kv_all[:, h * 256:(h + 1) * 256]
        k_ref[0, h, :, 0:LANES] = kvh[:, 0:LANES].astype(BF16)
        k_ref[0, h, :, LANES:QK_PAD] = kpe
        vv_ref[0, h] = kvh[:, LANES:256].astype(BF16)


def _inproj(x2, g_mix, w_in_p, g_gv, g_cq, g_ckv, w_uq_p, w_ukv_b, rc, rs1, rs2, B, S):
    N = x2.shape[0]
    tm = TM_IN
    spt = S // tm
    row = lambda i: (i, 0)
    const = lambda i: (0, 0)
    pos = lambda i: (i % spt, 0)
    head = lambda i: (i // spt, 0, i % spt, 0)
    return pl.pallas_call(
        _inproj_kernel,
        grid=(N // tm,),
        in_specs=[
            pl.BlockSpec((tm, D_MODEL), row),
            pl.BlockSpec((1, D_MODEL), const),
            pl.BlockSpec((D_MODEL, C_END), const, pipeline_mode=pl.Buffered(1)),
            pl.BlockSpec((1, GMLP_WIDTH), const),
            pl.BlockSpec((1, Q_LORA), const),
            pl.BlockSpec((1, KV_LORA), const),
            pl.BlockSpec((Q_LORA, MLA_HEADS * QK_PAD), const),
            pl.BlockSpec((KV_LORA, MLA_HEADS * 256), const),
            pl.BlockSpec((tm, LANES), pos),
            pl.BlockSpec((tm, LANES), pos),
            pl.BlockSpec((tm, LANES), pos),
        ],
        out_specs=[
            pl.BlockSpec((tm, GMLP_WIDTH), row),
            pl.BlockSpec((tm, GMLP_WIDTH), row),
            pl.BlockSpec((tm, D_MODEL), row),
            pl.BlockSpec((tm, D_MODEL), row),
            pl.BlockSpec((1, MLA_HEADS, tm, QK_PAD), head),
            pl.BlockSpec((1, MLA_HEADS, tm, QK_PAD), head),
            pl.BlockSpec((1, MLA_HEADS, tm, V_HEAD_DIM), head),
        ],
        out_shape=[
            jax.ShapeDtypeStruct((N, GMLP_WIDTH), BF16),
            jax.ShapeDtypeStruct((N, GMLP_WIDTH), BF16),
            jax.ShapeDtypeStruct((N, D_MODEL), BF16),
            jax.ShapeDtypeStruct((N, D_MODEL), BF16),
            jax.ShapeDtypeStruct((B, MLA_HEADS, S, QK_PAD), BF16),
            jax.ShapeDtypeStruct((B, MLA_HEADS, S, QK_PAD), BF16),
            jax.ShapeDtypeStruct((B, MLA_HEADS, S, V_HEAD_DIM), BF16),
        ],
        compiler_params=pltpu.CompilerParams(
            dimension_semantics=("arbitrary",), vmem_limit_bytes=VMEM_LIMIT),
        name="inproj",
    )(x2, g_mix, w_in_p, g_gv, g_cq, g_ckv, w_uq_p, w_ukv_b, rc, rs1, rs2)


def _gmlp_kernel(u_ref, v_ref, sga_ref, wsp_ref, bsp_ref, wproj_ref, ma_ref, y_sc):
    nb = TM_GMLP // GMLP_BLOCK
    t_out = lax.broadcasted_iota(jnp.int32, (GMLP_BLOCK, GMLP_BLOCK), 0)
    s_in = lax.broadcasted_iota(jnp.int32, (GMLP_BLOCK, GMLP_BLOCK), 1)
    mask = (s_in // CHUNK) <= (t_out // CHUNK)
    for g in range(GMLP_GROUPS):
        c0 = g * LANES
        w = jnp.where(mask, wsp_ref[g], jnp.zeros((), BF16))
        rhs = jnp.concatenate(
            [v_ref[r * GMLP_BLOCK:(r + 1) * GMLP_BLOCK, c0:c0 + LANES] for r in range(nb)], axis=1)
        sv = _dot(w, rhs) + bsp_ref[:, g:g + 1]
        for r in range(nb):
            rows = slice(r * GMLP_BLOCK, (r + 1) * GMLP_BLOCK)
            ub = u_ref[rows, c0:c0 + LANES].astype(F32)
            y_sc[rows, c0:c0 + LANES] = (ub * sv[:, r * LANES:(r + 1) * LANES]).astype(BF16)
    ya = _dot(y_sc[...], wproj_ref[...])
    ma_ref[...] = (sga_ref[...].astype(F32) * ya).astype(BF16)


def _gmlp(u, v, sga, wsp_b, bsp_t, wproj_b):
    N = u.shape[0]
    tm = TM_GMLP
    row = lambda i: (i, 0)
    return pl.pallas_call(
        _gmlp_kernel,
        grid=(N // tm,),
        in_specs=[
            pl.BlockSpec((tm, GMLP_WIDTH), row),
            pl.BlockSpec((tm, GMLP_WIDTH), row),
            pl.BlockSpec((tm, D_MODEL), row),
            pl.BlockSpec((GMLP_GROUPS, GMLP_BLOCK, GMLP_BLOCK), lambda i: (0, 0, 0)),
            pl.BlockSpec((GMLP_BLOCK, GMLP_GROUPS), lambda i: (0, 0)),
            pl.BlockSpec((GMLP_WIDTH, D_MODEL), lambda i: (0, 0)),
        ],
        out_specs=pl.BlockSpec((tm, D_MODEL), row),
        out_shape=jax.ShapeDtypeStruct((N, D_MODEL), BF16),
        scratch_shapes=[pltpu.VMEM((tm, GMLP_WIDTH), BF16)],
        compiler_params=pltpu.CompilerParams(
            dimension_semantics=("arbitrary",), vmem_limit_bytes=VMEM_LIMIT),
        name="gmlp",
    )(u, v, sga, wsp_b, bsp_t, wproj_b)


NEG_BIG = -1e30


def _attn_kernel(q_ref, k_ref, v_ref, o_ref, m_sc, acc_sc):
    qi = pl.program_id(2)
    m_sc[...] = jnp.full(m_sc.shape, NEG_BIG, F32)
    acc_sc[...] = jnp.zeros(acc_sc.shape, F32)
    ones_col = (lax.broadcasted_iota(jnp.int32, (TQ, V_PAD - V_HEAD_DIM), 1) == 0).astype(BF16)

    def chunk_mask(r0, nq, nk):
        qc = (lax.broadcasted_iota(jnp.int32, (nq, nk), 0) + r0) // CHUNK
        kc = lax.broadcasted_iota(jnp.int32, (nq, nk), 1) // CHUNK
        return kc <= qc

    def update(hh, r0, nq, start, nk, mask):
        rows = pl.ds(r0, nq)
        kb = k_ref[0, hh, pl.ds(start, nk), :]
        vb = jnp.concatenate([v_ref[0, hh, pl.ds(start, nk), :], ones_col[0:nk]], axis=1)
        s = lax.dot_general(q_ref[0, hh, rows, :], kb, (((1,), (1,)), ((), ())),
                            preferred_element_type=F32)
        if mask is not None:
            s = jnp.where(mask, s, NEG_BIG)
        tiles = [s[:, c * LANES:(c + 1) * LANES] for c in range(nk // LANES)]
        tile_max = functools.reduce(jnp.maximum, tiles)
        m_prev = m_sc[hh, rows, :]
        m_new = jnp.maximum(m_prev, jnp.max(tile_max, axis=-1, keepdims=True))
        alpha = jnp.exp2(m_prev - m_new)
        p = jnp.concatenate([jnp.exp2(t - m_new).astype(BF16) for t in tiles], axis=1)
        alpha2 = jnp.concatenate([alpha] * (V_PAD // LANES), axis=1)
        acc_sc[hh, rows, :] = alpha2 * acc_sc[hh, rows, :] + _dot(p, vb)
        m_sc[hh, rows, :] = m_new

    def block(j):
        start = pl.multiple_of(j * TQ, TQ)
        for hh in range(HEADS_PER_STEP):
            update(hh, 0, TQ, start, TQ, None)

    def body(jj, carry):
        for u in range(4):
            block(4 * jj + u)
        return carry

    quads = lax.shift_right_logical(qi, 2)
    lax.fori_loop(0, quads, body, 0)

    @pl.when((qi & 2) != 0)
    def _():
        block(4 * quads)
        block(4 * quads + 1)

    @pl.when((qi & 1) != 0)
    def _():
        block(qi - 1)
    start = pl.multiple_of(qi * TQ, TQ)
    mask = chunk_mask(0, TQ, TQ)
    for hh in range(HEADS_PER_STEP):
        update(hh, 0, TQ, start, TQ, mask)
    for hh in range(HEADS_PER_STEP):
        l = acc_sc[hh, :, V_HEAD_DIM:V_HEAD_DIM + 1]
        o_ref[0, :, hh * V_HEAD_DIM:(hh + 1) * V_HEAD_DIM] = (
            acc_sc[hh, :, 0:V_HEAD_DIM] / l).astype(BF16)


def _attention(q, k, v):
    B, H, S, _ = q.shape
    hps = HEADS_PER_STEP
    return pl.pallas_call(
        _attn_kernel,
        grid=(B, H // hps, S // TQ),
        in_specs=[
            pl.BlockSpec((1, hps, TQ, QK_PAD), lambda b, h, i: (b, h, i, 0)),
            pl.BlockSpec((1, hps, S, QK_PAD), lambda b, h, i: (b, h, 0, 0)),
            pl.BlockSpec((1, hps, S, V_HEAD_DIM), lambda b, h, i: (b, h, 0, 0)),
        ],
        out_specs=pl.BlockSpec((1, TQ, hps * V_HEAD_DIM), lambda b, h, i: (b, i, h)),
        out_shape=jax.ShapeDtypeStruct((B, S, H * V_HEAD_DIM), BF16),
        scratch_shapes=[
            pltpu.VMEM((hps, TQ, LANES), F32),
            pltpu.VMEM((hps, TQ, V_PAD), F32),
        ],
        compiler_params=pltpu.CompilerParams(
            dimension_semantics=("arbitrary", "arbitrary", "arbitrary"),
            vmem_limit_bytes=VMEM_LIMIT),
        name="attention",
    )(q, k, v)


def _col_sum(x):
    return jnp.sum(x, axis=0, keepdims=True)


def _col_max(x):
    return jnp.max(x, axis=0, keepdims=True)


def _first_row(hit, row_f):
    return jnp.min(jnp.where(hit, row_f, float(LANES)), axis=0, keepdims=True).astype(jnp.int32)


def _out_route_kernel(o_ref, ma_ref, sgb_ref, x_ref, wo_ref, wout_ref, gmoe_ref, wr_ref, br_ref,
                      h1_ref, m_ref, rt_ref, rw_ref, cnt_ref, carry_sc):
    i = pl.program_id(0)
    tm = TM_OUT

    @pl.when(i == 0)
    def _():
        carry_sc[...] = jnp.zeros(carry_sc.shape, F32)

    yb = _dot(o_ref[...], wo_ref[...])
    merged = ma_ref[...].astype(F32) + sgb_ref[...].astype(F32) * yb
    h1 = x_ref[...] + _dot(merged.astype(BF16), wout_ref[...])
    h1_ref[...] = h1
    m = _rms(h1) * gmoe_ref[...]
    _store_rows(m_ref, m)

    logits = lax.dot_general(wr_ref[...], m.astype(BF16), (((1,), (1,)), ((), ())),
                             preferred_element_type=F32)
    row = lax.broadcasted_iota(jnp.int32, (LANES, tm), 0)
    row_f = row.astype(F32)
    bias = br_ref[...]
    is_g = (row >= N_EXPERTS) & (row < N_EXPERTS + N_GROUPS)
    neg = jnp.float32(-jnp.inf)

    gl = jnp.where(is_g, logits, neg)
    ge = jnp.where(is_g, jnp.exp(gl - _col_max(gl)), 0.0)
    g_prob = ge / _col_sum(ge)
    g_score = jnp.where(is_g, g_prob + bias, neg)
    g_row = _first_row(g_score == _col_max(g_score), row_f)
    g_w = _col_sum(jnp.where(row == g_row, g_prob, 0.0))
    g_idx = g_row - N_EXPERTS

    in_g = (row // EXPERTS_PER_GROUP) == g_idx
    el = jnp.where(in_g, logits, neg)
    ee = jnp.where(in_g, jnp.exp(el - _col_max(el)), 0.0)
    e_prob = ee / _col_sum(ee)
    e_score = jnp.where(in_g, e_prob + bias, neg)
    id1 = _first_row(e_score == _col_max(e_score), row_f)
    e_score2 = jnp.where(row == id1, neg, e_score)
    id2 = _first_row(e_score2 == _col_max(e_score2), row_f)
    p1 = _col_sum(jnp.where(row == id1, e_prob, 0.0))
    p2 = _col_sum(jnp.where(row == id2, e_prob, 0.0))
    psum = p1 + p2
    w1 = g_w * (p1 / psum)
    w2 = g_w * (p2 / psum)

    oh = ((row == id1) | (row == id2 + N_EXPERTS)).astype(BF16)
    t_in = lax.broadcasted_iota(jnp.int32, (tm, tm), 0)
    t_out = lax.broadcasted_iota(jnp.int32, (tm, tm), 1)
    tri = (t_in < t_out).astype(BF16)
    prefix = _dot(oh, tri)
    tot = jnp.sum(oh.astype(F32), axis=1, keepdims=True)
    tot_sw = jnp.concatenate([tot[N_EXPERTS:], tot[:N_EXPERTS]], axis=0)
    row1 = lax.broadcasted_iota(jnp.int32, (LANES, 1), 0)
    carry = carry_sc[...]
    base = carry + jnp.where(row1 >= N_EXPERTS, tot_sw, 0.0)
    rk = oh.astype(F32) * (base + prefix)
    rank1 = _col_sum(jnp.where(row < N_EXPERTS, rk, 0.0))
    rank2 = _col_sum(jnp.where(row >= N_EXPERTS, rk, 0.0))
    carry_new = carry + tot + tot_sw
    carry_sc[...] = carry_new
    cnt_ref[...] = carry_new.astype(jnp.int32)

    row8 = lax.broadcasted_iota(jnp.int32, (SUBLANES, tm), 0)
    rt = jnp.where(row8 == 0, id1, jnp.where(row8 == 1, id2, 0))
    rt = jnp.where(row8 == 2, rank1.astype(jnp.int32), rt)
    rt_ref[...] = jnp.where(row8 == 3, rank2.astype(jnp.int32), rt)
    wt = jnp.where(row == 0, w1, jnp.where(row == 1, w2, 0.0))
    rw_ref[...] = wt.T


def _out_route(o2, ma, sgb, x2, wo_b, wout_b, g_moe, wr_b, br):
    N = x2.shape[0]
    tm = TM_OUT
    row = lambda i: (i, 0)
    const = lambda i: (0, 0)
    return pl.pallas_call(
        _out_route_kernel,
        grid=(N // tm,),
        in_specs=[
            pl.BlockSpec((tm, D_MODEL), row),
            pl.BlockSpec((tm, D_MODEL), row),
            pl.BlockSpec((tm, D_MODEL), row),
            pl.BlockSpec((tm, D_MODEL), row),
            pl.BlockSpec((D_MODEL, D_MODEL), const),
            pl.BlockSpec((D_MODEL, D_MODEL), const),
            pl.BlockSpec((1, D_MODEL), const),
            pl.BlockSpec((LANES, D_MODEL), const),
            pl.BlockSpec((LANES, tm), const),
        ],
        out_specs=[
            pl.BlockSpec((tm, D_MODEL), row),
            pl.BlockSpec((tm * SUBLANES, LANES), row),
            pl.BlockSpec((SUBLANES, tm), lambda i: (0, i)),
            pl.BlockSpec((tm, LANES), row),
            pl.BlockSpec((LANES, 1), const),
        ],
        out_shape=[
            jax.ShapeDtypeStruct((N, D_MODEL), F32),
            jax.ShapeDtypeStruct((N * SUBLANES, LANES), F32),
            jax.ShapeDtypeStruct((SUBLANES, N), jnp.int32),
            jax.ShapeDtypeStruct((N, LANES), F32),
            jax.ShapeDtypeStruct((LANES, 1), jnp.int32),
        ],
        scratch_shapes=[pltpu.VMEM((LANES, 1), F32)],
        compiler_params=pltpu.CompilerParams(
            dimension_semantics=("arbitrary",), vmem_limit_bytes=VMEM_LIMIT),
        name="out_route",
    )(o2, ma, sgb, x2, wo_b, wout_b, g_moe, wr_b, br)


def _row_tile(ref2, row8):
    return ref2.at[pl.ds(pl.multiple_of(row8, SUBLANES), SUBLANES)]


def _dispatch_kernel(pad_end_ref, npad_ref, dest_ref, m_ref, buf_ref, zero_sc, sem, zsem):
    i = pl.program_id(0)
    tm = TM_ROWS
    blk8 = MOE_BLK * SUBLANES

    def zero_fill(go):
        def per_expert(e, carry):
            npad = npad_ref[e]
            row = pad_end_ref[e] - npad
            k = MOE_BLK // 2
            while k >= 1:
                @pl.when((npad & k) != 0)
                def _():
                    go(pltpu.make_async_copy(
                        zero_sc.at[pl.ds(0, k * SUBLANES)],
                        buf_ref.at[pl.ds(pl.multiple_of(row * SUBLANES, SUBLANES), k * SUBLANES)],
                        zsem))
                row = row + (npad & k)
                k //= 2
            return carry

        def tail_copy(b):
            return pltpu.make_async_copy(
                zero_sc, buf_ref.at[pl.ds(pl.multiple_of(b * blk8, blk8), blk8)], zsem)

        lax.fori_loop(0, N_EXPERTS, per_expert, 0)
        nused = pad_end_ref[N_EXPERTS - 1] // MOE_BLK
        lax.fori_loop(nused, buf_ref.shape[0] // blk8, lambda b, c: (go(tail_copy(b)), c)[1], 0)

    @pl.when(i == 0)
    def _():
        zero_sc[...] = jnp.zeros(zero_sc.shape, F32)
        zero_fill(lambda c: c.start())

    def issue(c, carry):
        for u in range(DMA_UNROLL):
            r = c * DMA_UNROLL + u
            src = _row_tile(m_ref, r * SUBLANES)
            pltpu.make_async_copy(src, _row_tile(buf_ref, dest_ref[0, 0, r]), sem).start(priority=0)
            pltpu.make_async_copy(src, _row_tile(buf_ref, dest_ref[0, 0, tm + r]), sem).start(priority=1)
        return carry

    lax.fori_loop(0, tm // DMA_UNROLL, issue, 0)
    for _ in range(2):
        pltpu.make_async_copy(m_ref, buf_ref.at[pl.ds(0, tm * SUBLANES)], sem).wait()

    @pl.when(i == pl.num_programs(0) - 1)
    def _():
        zero_fill(lambda c: c.wait())


def _dispatch(pad_end, npad, dest8_tiles, m2, P):
    tm = TM_ROWS
    N = m2.shape[0] // SUBLANES
    grid_spec = pltpu.PrefetchScalarGridSpec(
        num_scalar_prefetch=2,
        grid=(N // tm,),
        in_specs=[
            pl.BlockSpec((1, 1, 2 * tm), lambda i, pe, pd: (i, 0, 0), memory_space=pltpu.SMEM),
            pl.BlockSpec((tm * SUBLANES, LANES), lambda i, pe, pd: (i, 0)),
        ],
        out_specs=pl.BlockSpec(memory_space=pl.ANY),
        scratch_shapes=[
            pltpu.VMEM((MOE_BLK * SUBLANES, LANES), F32),
            pltpu.SemaphoreType.DMA(()),
            pltpu.SemaphoreType.DMA(()),
        ],
    )
    return pl.pallas_call(
        _dispatch_kernel,
        grid_spec=grid_spec,
        out_shape=jax.ShapeDtypeStruct((P * SUBLANES, LANES), F32),
        compiler_params=pltpu.CompilerParams(dimension_semantics=("arbitrary",)),
        name="dispatch",
    )(pad_end, npad, dest8_tiles, m2)


def _expert_kernel(blk_e_ref, run_blocks_ref, nused_ref, x_hbm, wg_hbm, wu_hbm, wd_hbm, y_hbm,
                   xbuf, ybuf, wg_raw, wu_raw, wd_raw, wgu_sc, wd_sc, zero_sc,
                   xsem, ysem, wsem, zsem):
    nused = nused_ref[0]
    blk8 = MOE_BLK * SUBLANES
    nblk = y_hbm.shape[0] // blk8
    nxbuf = X_AHEAD + 1

    def block_rows(ref, b):
        return ref.at[pl.ds(pl.multiple_of(b * blk8, blk8), blk8)]

    def x_copy(b):
        slot = b % nxbuf
        return pltpu.make_async_copy(block_rows(x_hbm, b), xbuf.at[slot], xsem.at[slot])

    def y_copy(b, slot):
        return pltpu.make_async_copy(ybuf.at[slot], block_rows(y_hbm, b), ysem.at[slot])

    def w_copies(e, slot):
        return (pltpu.make_async_copy(wg_hbm.at[e], wg_raw.at[slot], wsem.at[slot]),
                pltpu.make_async_copy(wu_hbm.at[e], wu_raw.at[slot], wsem.at[slot]),
                pltpu.make_async_copy(wd_hbm.at[e], wd_raw.at[slot], wsem.at[slot]))

    for b in range(X_AHEAD):
        @pl.when(b < nused)
        def _():
            x_copy(b).start()
    for c in w_copies(blk_e_ref[0], 0):
        c.start()

    def zero_copy(b):
        return pltpu.make_async_copy(zero_sc, block_rows(y_hbm, b), zsem)

    zero_sc[...] = jnp.zeros(zero_sc.shape, F32)
    lax.fori_loop(nused, nblk, lambda b, c: (zero_copy(b).start(), c)[1], 0)

    def body(i, run):
        e = blk_e_ref[i]
        new_expert = (i == 0) | (e != blk_e_ref[jnp.maximum(i - 1, 0)])
        run = run + new_expert.astype(jnp.int32)
        wslot = run % 2

        @pl.when(new_expert)
        def _():
            for c in w_copies(e, wslot):
                c.wait()
            wgu_sc[:, 0:D_EXPERT] = wg_raw[wslot].astype(BF16)
            wgu_sc[:, D_EXPERT:2 * D_EXPERT] = wu_raw[wslot].astype(BF16)
            wd_sc[...] = wd_raw[wslot].astype(BF16)
            nxt = i + run_blocks_ref[e]

            @pl.when(nxt < nused)
            def _():
                for c in w_copies(blk_e_ref[nxt], 1 - wslot):
                    c.start()

        @pl.when(i + X_AHEAD < nused)
        def _():
            x_copy(i + X_AHEAD).start()

        yslot = i % Y_BUFS
        x_copy(i).wait()

        @pl.when(i >= Y_BUFS)
        def _():
            y_copy(i - Y_BUFS, yslot).wait()

        xb = _load_rows(xbuf.at[i % nxbuf], MOE_BLK).astype(BF16)
        h = _dot(xb, wgu_sc[...])
        hdn = (jax.nn.silu(h[:, 0:D_EXPERT]) * h[:, D_EXPERT:2 * D_EXPERT]).astype(BF16)
        _store_rows(ybuf.at[yslot], _dot(hdn, wd_sc[...]))
        y_copy(i, yslot).start()
        return run

    lax.fori_loop(0, nused, body, jnp.int32(-1))

    for back in range(1, Y_BUFS + 1):
        @pl.when(nused >= back)
        def _():
            y_copy(nused - back, (nused - back) % Y_BUFS).wait()

    lax.fori_loop(nused, nblk, lambda b, c: (zero_copy(b).wait(), c)[1], 0)


def _experts(blk_e, run_blocks, nused, buf2, wg, wu, wd):
    blk_rows = MOE_BLK * SUBLANES
    any_spec = pl.BlockSpec(memory_space=pl.ANY)
    grid_spec = pltpu.PrefetchScalarGridSpec(
        num_scalar_prefetch=3,
        grid=(1,),
        in_specs=[any_spec, any_spec, any_spec, any_spec],
        out_specs=any_spec,
        scratch_shapes=[
            pltpu.VMEM((X_AHEAD + 1, blk_rows, LANES), F32),
            pltpu.VMEM((Y_BUFS, blk_rows, LANES), F32),
            pltpu.VMEM((2, D_MODEL, D_EXPERT), F32),
            pltpu.VMEM((2, D_MODEL, D_EXPERT), F32),
            pltpu.VMEM((2, D_EXPERT, D_MODEL), F32),
            pltpu.VMEM((D_MODEL, 2 * D_EXPERT), BF16),
            pltpu.VMEM((D_EXPERT, D_MODEL), BF16),
            pltpu.VMEM((blk_rows, LANES), F32),
            pltpu.SemaphoreType.DMA((X_AHEAD + 1,)),
            pltpu.SemaphoreType.DMA((Y_BUFS,)),
            pltpu.SemaphoreType.DMA((2,)),
            pltpu.SemaphoreType.DMA(()),
        ],
    )
    return pl.pallas_call(
        _expert_kernel,
        grid_spec=grid_spec,
        out_shape=jax.ShapeDtypeStruct(buf2.shape, F32),
        compiler_params=pltpu.CompilerParams(
            dimension_semantics=("arbitrary",), vmem_limit_bytes=VMEM_LIMIT),
        name="experts",
    )(blk_e, run_blocks, nused, buf2, wg, wu, wd)


def _final_kernel(dcur_ref, dnext_ref, h1_ref, rw_ref, p_ref, gple_ref, wpg_ref, wpp_ref, gfin_ref,
                  ys_ref, out_ref, ybuf_a, ybuf_b, sem):
    g = pl.program_id(0)
    ng = pl.num_programs(0)
    tm = TM_ROWS
    bufs = (ybuf_a, ybuf_b)

    def row_copy(d_ref, off, r, which):
        return pltpu.make_async_copy(_row_tile(ys_ref, d_ref[0, 0, off + r]),
                                     _row_tile(bufs[which], r * SUBLANES), sem.at[which])

    def issue(d_ref, off, which):
        for r in range(2 * tm):
            row_copy(d_ref, off, r, which).start(priority=r % 2)

    def wait(which):
        pltpu.make_async_copy(ys_ref.at[pl.ds(0, 2 * tm * SUBLANES)], bufs[which],
                              sem.at[which]).wait()

    def compute(which, rows):
        y0 = _load_rows(bufs[which], tm)
        y1 = _load_rows(bufs[which], tm, first_row=tm)
        rw = rw_ref[rows, :]
        h2 = h1_ref[rows, :] + (y0 * rw[:, 0:1] + y1 * rw[:, 1:2])
        n3 = (_rms(h2) * gple_ref[...]).astype(BF16)
        gate = jax.nn.sigmoid(_dot(n3, wpg_ref[...]))
        pp = _dot(p_ref[rows, :].astype(BF16), wpp_ref[...])
        h3 = h2 + gate * pp
        out_ref[rows, :] = _rms(h3) * gfin_ref[...]

    @pl.when(g == 0)
    def _():
        def body(c, carry):
            for u in range(DMA_UNROLL):
                row_copy(dcur_ref, 0, c * DMA_UNROLL + u, 0).start()
            return carry

        lax.fori_loop(0, 2 * tm // DMA_UNROLL, body, 0)

    wait(0)
    issue(dcur_ref, 2 * tm, 1)
    compute(0, slice(0, tm))
    wait(1)
    issue(dnext_ref, 0, 0)
    compute(1, slice(tm, 2 * tm))

    @pl.when(g == ng - 1)
    def _():
        wait(0)


def _final(dest8_pairs, h1, rw, p2, g_ple, wpg_b, wpp_b, g_final, ys2):
    N = h1.shape[0]
    tm = TM_ROWS
    ng = N // (2 * tm)
    row = lambda i: (i, 0)
    const = lambda i: (0, 0)
    return pl.pallas_call(
        _final_kernel,
        grid=(ng,),
        in_specs=[
            pl.BlockSpec((1, 1, 4 * tm), lambda i: (i, 0, 0), memory_space=pltpu.SMEM),
            pl.BlockSpec((1, 1, 4 * tm), lambda i: (jnp.minimum(i + 1, ng - 1), 0, 0),
                         memory_space=pltpu.SMEM),
            pl.BlockSpec((2 * tm, D_MODEL), row),
            pl.BlockSpec((2 * tm, LANES), row),
            pl.BlockSpec((2 * tm, PLE_DIM), row),
            pl.BlockSpec((1, D_MODEL), const),
            pl.BlockSpec((D_MODEL, D_MODEL), const),
            pl.BlockSpec((PLE_DIM, D_MODEL), const),
            pl.BlockSpec((1, D_MODEL), const),
            pl.BlockSpec(memory_space=pl.ANY),
        ],
        out_specs=pl.BlockSpec((2 * tm, D_MODEL), row),
        out_shape=jax.ShapeDtypeStruct((N, D_MODEL), F32),
        scratch_shapes=[
            pltpu.VMEM((2 * tm * SUBLANES, LANES), F32),
            pltpu.VMEM((2 * tm * SUBLANES, LANES), F32),
            pltpu.SemaphoreType.DMA((2,)),
        ],
        compiler_params=pltpu.CompilerParams(
            dimension_semantics=("arbitrary",), vmem_limit_bytes=VMEM_LIMIT),
        name="final",
    )(dest8_pairs, dest8_pairs, h1, rw, p2, g_ple, wpg_b, wpp_b, g_final, ys2)


def _rope_tables(S):
    inv_freq = ROPE_THETA ** (-jnp.arange(0, QK_ROPE_DIM, 2, dtype=F32) / QK_ROPE_DIM)
    ang = jnp.arange(S, dtype=F32)[:, None] * inv_freq[None, :]
    cos, sin = jnp.cos(ang), jnp.sin(ang)
    z = jnp.zeros_like(cos)
    rc = jnp.concatenate([cos, cos, z, z], axis=1)
    rs1 = jnp.concatenate([-sin, z, z, z], axis=1)
    rs2 = jnp.concatenate([z, sin, z, z], axis=1)
    return rc, rs1, rs2


def _layer(h, p_l, g_mix, w_in, g_gv, w_spatial, b_spatial, w_gproj, g_cq, w_uq, g_ckv, w_ukv,
           w_mla_o, w_out, g_moe, w_router_g, b_router_g, w_router_e, b_router_e,
           w_e_gate, w_e_up, w_e_down, g_ple, w_ple_gate, w_ple_proj, g_out):
    B, S, D = h.shape
    N = B * S
    x2 = h.reshape(N, D)

    cu, cv, ccq, cckv, ckr, cga = (GMLP_WIDTH, 2 * GMLP_WIDTH, 2 * GMLP_WIDTH + Q_LORA,
                                   2 * GMLP_WIDTH + Q_LORA + KV_LORA,
                                   2 * GMLP_WIDTH + Q_LORA + KV_LORA + QK_ROPE_DIM,
                                   2 * GMLP_WIDTH + Q_LORA + KV_LORA + QK_ROPE_DIM + D_MODEL)
    w_in_p = jnp.concatenate(
        [w_in[:, :cckv], w_in[:, ckr:], w_in[:, cckv:ckr],
         jnp.zeros((D, LANES - QK_ROPE_DIM), w_in.dtype)], axis=1).astype(BF16)
    w_uq_h = w_uq.reshape(Q_LORA, MLA_HEADS, QK_NOPE_DIM + QK_ROPE_DIM)
    w_uq_p = jnp.concatenate(
        [w_uq_h, jnp.zeros((Q_LORA, MLA_HEADS, QK_PAD - QK_NOPE_DIM - QK_ROPE_DIM), w_uq.dtype)],
        axis=2).reshape(Q_LORA, MLA_HEADS * QK_PAD).astype(BF16)
    rc, rs1, rs2 = _rope_tables(S)

    u, v, sga, sgb, q, k, vv = _inproj(
        x2, g_mix[None], w_in_p, g_gv[None], g_cq[None], g_ckv[None], w_uq_p, w_ukv.astype(BF16),
        rc, rs1, rs2, B, S)
    ma = _gmlp(u, v, sga, w_spatial.astype(BF16), b_spatial.T, w_gproj.astype(BF16))
    o = _attention(q, k, vv)

    wr = jnp.concatenate(
        [w_router_e.T, w_router_g.T, jnp.zeros((LANES - N_EXPERTS - N_GROUPS, D), w_router_e.dtype)],
        axis=0).astype(BF16)
    br = jnp.concatenate(
        [b_router_e.reshape(-1), b_router_g, jnp.zeros((LANES - N_EXPERTS - N_GROUPS,), F32)])
    br = jnp.broadcast_to(br[:, None], (LANES, TM_OUT))
    h1, m, rt, rw, cnt = _out_route(
        o.reshape(N, D), ma, sgb, x2, w_mla_o.astype(BF16), w_out.astype(BF16), g_moe[None], wr, br)

    counts = cnt[:N_EXPERTS, 0]
    padded = (counts + MOE_BLK - 1) // MOE_BLK * MOE_BLK
    pad_end = jnp.cumsum(padded)
    pad_start = pad_end - padded
    P = 2 * N + N_EXPERTS * MOE_BLK
    nblk = P // MOE_BLK
    blk_start = jnp.arange(nblk, dtype=jnp.int32) * MOE_BLK
    blk_e = jnp.minimum(
        jnp.sum((pad_end[None, :] <= blk_start[:, None]).astype(jnp.int32), axis=1),
        N_EXPERTS - 1).astype(jnp.int32)
    nused = (pad_end[-1:] // MOE_BLK).astype(jnp.int32)
    experts = jnp.arange(N_EXPERTS, dtype=jnp.int32)[:, None, None]
    start_of = jnp.sum(jnp.where(rt[None, 0:2] == experts, pad_start[:, None, None], 0), axis=0)
    dest8 = (start_of + rt[2:4]).astype(jnp.int32) * SUBLANES
    nt = N // TM_ROWS
    dest8_tiles = dest8.reshape(2, nt, TM_ROWS).transpose(1, 0, 2).reshape(nt, 1, 2 * TM_ROWS)

    buf = _dispatch(pad_end.astype(jnp.int32), (padded - counts).astype(jnp.int32), dest8_tiles, m, P)
    ys = _experts(blk_e, (padded // MOE_BLK).astype(jnp.int32), nused, buf, w_e_gate, w_e_up, w_e_down)
    out = _final(dest8_tiles.reshape(nt // 2, 1, 4 * TM_ROWS), h1, rw, p_l.reshape(N, PLE_DIM), g_ple[None],
                 w_ple_gate.astype(BF16), w_ple_proj.astype(BF16), g_out[None], ys)
    return out.reshape(B, S, D)


def kernel(x, p, g_mix, w_in, g_gv, w_spatial, b_spatial, w_gproj, g_cq, w_uq, g_ckv, w_ukv, w_mla_o,
           w_out, g_moe, w_router_g, b_router_g, w_router_e, b_router_e, w_e_gate, w_e_up, w_e_down,
           g_ple, w_ple_gate, w_ple_proj, g_final):
    depth = p.shape[0]
    assert depth == 1, "the final rmsnorm is fused into the single layer's last kernel"
    i = 0
    return _layer(x, p[i], g_mix[i], w_in[i], g_gv[i], w_spatial[i], b_spatial[i], w_gproj[i], g_cq[i],
                  w_uq[i], g_ckv[i], w_ukv[i], w_mla_o[i], w_out[i], g_moe[i], w_router_g[i],
                  b_router_g[i], w_router_e[i], b_router_e[i], w_e_gate[i], w_e_up[i], w_e_down[i],
                  g_ple[i], w_ple_gate[i], w_ple_proj[i], g_final)
```

```python
import functools

import jax
import jax.numpy as jnp
from jax import lax
from jax.experimental import pallas as pl
from jax.experimental.pallas import tpu as pltpu

F32 = jnp.float32
BF16 = jnp.bfloat16

D_MODEL = 1024
CHUNK = 64
PLE_DIM = 256
GMLP_BLOCK = 128
GMLP_GROUPS = 12
GMLP_WIDTH = 1536
MLA_HEADS = 8
QK_NOPE_DIM = 128
QK_ROPE_DIM = 64
V_HEAD_DIM = 128
Q_LORA = 384
KV_LORA = 256
ROPE_THETA = 10000.0
N_GROUPS = 8
EXPERTS_PER_GROUP = 8
N_EXPERTS = 64
D_EXPERT = 256
EPS = 1e-6
LOG2E = 1.4426950408889634

LANES = 128
SUBLANES = 8
QK_PAD = 256
V_PAD = 256
VMEM_LIMIT = 56 * 1024 * 1024

C_U = 0
C_V = C_U + GMLP_WIDTH
C_CQ = C_V + GMLP_WIDTH
C_CKV = C_CQ + Q_LORA
C_GA = C_CKV + KV_LORA
C_GB = C_GA + D_MODEL
C_KR = C_GB + D_MODEL
C_END = C_KR + LANES

TM_IN = 512
TM_GMLP = 1024
TQ = 512
HEADS_PER_STEP = 4
TM_OUT = 1024
TM_ROWS = 512
MOE_BLK = 256
X_AHEAD = 6
Y_BUFS = 4
DMA_UNROLL = 8


def _rms(x):
    return x * lax.rsqrt(jnp.mean(x * x, axis=-1, keepdims=True) + EPS)


def _dot(a, b):
    return jnp.dot(a, b, preferred_element_type=F32)


def _store_rows(ref2, x):
    rows = x.shape[0]
    for j in range(SUBLANES):
        ref2[pl.ds(j, rows, stride=SUBLANES), :] = x[:, j * LANES:(j + 1) * LANES]


def _load_rows(ref2, rows, first_row=0):
    return jnp.concatenate(
        [ref2[pl.ds(first_row * SUBLANES + j, rows, stride=SUBLANES), :] for j in range(SUBLANES)],
        axis=1)


def _rope128(t, rc, rs1, rs2):
    r1 = pltpu.roll(t, 96, axis=1)
    r2 = pltpu.roll(t, 32, axis=1)
    return t * rc + r1 * rs1 + r2 * rs2


def _inproj_kernel(x_ref, gmix_ref, win_ref, ggv_ref, gcq_ref, gckv_ref, wuq_ref, wukv_ref,
                   rc_ref, rs1_ref, rs2_ref,
                   u_ref, v_ref, sga_ref, sgb_ref, q_ref, k_ref, vv_ref):
    x = x_ref[...]
    ab = (_rms(x) * gmix_ref[...]).astype(BF16)

    z_all = _dot(ab, win_ref[...])

    def proj(c0, c1):
        return z_all[:, c0:c1]

    u_ref[...] = jax.nn.gelu(proj(C_U, C_V)).astype(BF16)

    zv = jax.nn.gelu(proj(C_V, C_CQ))
    xc = zv - jnp.mean(zv, axis=-1, keepdims=True)
    vln = xc * lax.rsqrt(jnp.mean(xc * xc, axis=-1, keepdims=True) + EPS)
    v_ref[...] = (vln * ggv_ref[...]).astype(BF16)

    sga_ref[...] = jax.nn.sigmoid(proj(C_GA, C_GB)).astype(BF16)
    sgb_ref[...] = jax.nn.sigmoid(proj(C_GB, C_KR)).astype(BF16)

    rc = rc_ref[...]
    rs1 = rs1_ref[...]
    rs2 = rs2_ref[...]
    kpe = _rope128(proj(C_KR, C_END), rc, rs1, rs2).astype(BF16)

    cqn = (_rms(proj(C_CQ, C_CKV)) * gcq_ref[...]).astype(BF16)
    ckvn = (_rms(proj(C_CKV, C_GA)) * gckv_ref[...]).astype(BF16)
    scale = (QK_NOPE_DIM + QK_ROPE_DIM) ** -0.5 * LOG2E
    q_all = _dot(cqn, wuq_ref[...])
    kv_all = _dot(ckvn, wukv_ref[...])
    for h in range(MLA_HEADS):
        qh = q_all[:, h * QK_PAD:(h + 1) * QK_PAD]
        q_ref[0, h, :, 0:LANES] = (qh[:, 0:LANES] * scale).astype(BF16)
        q_ref[0, h, :, LANES:QK_PAD] = (_rope128(qh[:, LANES:QK_PAD], rc, rs1, rs2) * scale).astype(BF16)
        kvh = kv_all[:, h * 256:(h + 1) * 256]
        k_ref[0, h, :, 0:LANES] = kvh[:, 0:LANES].astype(BF16)
        k_ref[0, h, :, LANES:QK_PAD] = kpe
        vv_ref[0, h] = kvh[:, LANES:256].astype(BF16)


def _inproj(x2, g_mix, w_in_p, g_gv, g_cq, g_ckv, w_uq_p, w_ukv_b, rc, rs1, rs2, B, S):
    N = x2.shape[0]
    tm = TM_IN
    spt = S // tm
    row = lambda i: (i, 0)
    const = lambda i: (0, 0)
    pos = lambda i: (i % spt, 0)
    head = lambda i: (i // spt, 0, i % spt, 0)
    return pl.pallas_call(
        _inproj_kernel,
        grid=(N // tm,),
        in_specs=[
            pl.BlockSpec((tm, D_MODEL), row),
            pl.BlockSpec((1, D_MODEL), const),
            pl.BlockSpec((D_MODEL, C_END), const, pipeline_mode=pl.Buffered(1)),
            pl.BlockSpec((1, GMLP_WIDTH), const),
            pl.BlockSpec((1, Q_LORA), const),
            pl.BlockSpec((1, KV_LORA), const),
            pl.BlockSpec((Q_LORA, MLA_HEADS * QK_PAD), const),
            pl.BlockSpec((KV_LORA, MLA_HEADS * 256), const),
            pl.BlockSpec((tm, LANES), pos),
            pl.BlockSpec((tm, LANES), pos),
            pl.BlockSpec((tm, LANES), pos),
        ],
        out_specs=[
            pl.BlockSpec((tm, GMLP_WIDTH), row),
            pl.BlockSpec((tm, GMLP_WIDTH), row),
            pl.BlockSpec((tm, D_MODEL), row),
            pl.BlockSpec((tm, D_MODEL), row),
            pl.BlockSpec((1, MLA_HEADS, tm, QK_PAD), head),
            pl.BlockSpec((1, MLA_HEADS, tm, QK_PAD), head),
            pl.BlockSpec((1, MLA_HEADS, tm, V_HEAD_DIM), head),
        ],
        out_shape=[
            jax.ShapeDtypeStruct((N, GMLP_WIDTH), BF16),
            jax.ShapeDtypeStruct((N, GMLP_WIDTH), BF16),
            jax.ShapeDtypeStruct((N, D_MODEL), BF16),
            jax.ShapeDtypeStruct((N, D_MODEL), BF16),
            jax.ShapeDtypeStruct((B, MLA_HEADS, S, QK_PAD), BF16),
            jax.ShapeDtypeStruct((B, MLA_HEADS, S, QK_PAD), BF16),
            jax.ShapeDtypeStruct((B, MLA_HEADS, S, V_HEAD_DIM), BF16),
        ],
        compiler_params=pltpu.CompilerParams(
            dimension_semantics=("arbitrary",), vmem_limit_bytes=VMEM_LIMIT),
        name="inproj",
    )(x2, g_mix, w_in_p, g_gv, g_cq, g_ckv, w_uq_p, w_ukv_b, rc, rs1, rs2)


def _gmlp_kernel(u_ref, v_ref, sga_ref, wsp_ref, bsp_ref, wproj_ref, ma_ref, y_sc):
    nb = TM_GMLP // GMLP_BLOCK
    t_out = lax.broadcasted_iota(jnp.int32, (GMLP_BLOCK, GMLP_BLOCK), 0)
    s_in = lax.broadcasted_iota(jnp.int32, (GMLP_BLOCK, GMLP_BLOCK), 1)
    mask = (s_in // CHUNK) <= (t_out // CHUNK)
    for g in range(GMLP_GROUPS):
        c0 = g * LANES
        w = jnp.where(mask, wsp_ref[g], jnp.zeros((), BF16))
        rhs = jnp.concatenate(
            [v_ref[r * GMLP_BLOCK:(r + 1) * GMLP_BLOCK, c0:c0 + LANES] for r in range(nb)], axis=1)
        sv = _dot(w, rhs) + bsp_ref[:, g:g + 1]
        for r in range(nb):
            rows = slice(r * GMLP_BLOCK, (r + 1) * GMLP_BLOCK)
            ub = u_ref[rows, c0:c0 + LANES].astype(F32)
            y_sc[rows, c0:c0 + LANES] = (ub * sv[:, r * LANES:(r + 1) * LANES]).astype(BF16)
    ya = _dot(y_sc[...], wproj_ref[...])
    ma_ref[...] = (sga_ref[...].astype(F32) * ya).astype(BF16)


def _gmlp(u, v, sga, wsp_b, bsp_t, wproj_b):
    N = u.shape[0]
    tm = TM_GMLP
    row = lambda i: (i, 0)
    return pl.pallas_call(
        _gmlp_kernel,
        grid=(N // tm,),
        in_specs=[
            pl.BlockSpec((tm, GMLP_WIDTH), row),
            pl.BlockSpec((tm, GMLP_WIDTH), row),
            pl.BlockSpec((tm, D_MODEL), row),
            pl.BlockSpec((GMLP_GROUPS, GMLP_BLOCK, GMLP_BLOCK), lambda i: (0, 0, 0)),
            pl.BlockSpec((GMLP_BLOCK, GMLP_GROUPS), lambda i: (0, 0)),
            pl.BlockSpec((GMLP_WIDTH, D_MODEL), lambda i: (0, 0)),
        ],
        out_specs=pl.BlockSpec((tm, D_MODEL), row),
        out_shape=jax.ShapeDtypeStruct((N, D_MODEL), BF16),
        scratch_shapes=[pltpu.VMEM((tm, GMLP_WIDTH), BF16)],
        compiler_params=pltpu.CompilerParams(
            dimension_semantics=("arbitrary",), vmem_limit_bytes=VMEM_LIMIT),
        name="gmlp",
    )(u, v, sga, wsp_b, bsp_t, wproj_b)


NEG_BIG = -1e30


def _attn_kernel(q_ref, k_ref, v_ref, o_ref, m_sc, acc_sc):
    qi = pl.program_id(2)
    m_sc[...] = jnp.full(m_sc.shape, NEG_BIG, F32)
    acc_sc[...] = jnp.zeros(acc_sc.shape, F32)
    ones_col = (lax.broadcasted_iota(jnp.int32, (TQ, V_PAD - V_HEAD_DIM), 1) == 0).astype(BF16)

    def chunk_mask(r0, nq, nk):
        qc = (lax.broadcasted_iota(jnp.int32, (nq, nk), 0) + r0) // CHUNK
        kc = lax.broadcasted_iota(jnp.int32, (nq, nk), 1) // CHUNK
        return kc <= qc

    def update(hh, r0, nq, start, nk, mask):
        rows = pl.ds(r0, nq)
        kb = k_ref[0, hh, pl.ds(start, nk), :]
        vb = jnp.concatenate([v_ref[0, hh, pl.ds(start, nk), :], ones_col[0:nk]], axis=1)
        s = lax.dot_general(q_ref[0, hh, rows, :], kb, (((1,), (1,)), ((), ())),
                            preferred_element_type=F32)
        if mask is not None:
            s = jnp.where(mask, s, NEG_BIG)
        tiles = [s[:, c * LANES:(c + 1) * LANES] for c in range(nk // LANES)]
        tile_max = functools.reduce(jnp.maximum, tiles)
        m_prev = m_sc[hh, rows, :]
        m_new = jnp.maximum(m_prev, jnp.max(tile_max, axis=-1, keepdims=True))
        alpha = jnp.exp2(m_prev - m_new)
        p = jnp.concatenate([jnp.exp2(t - m_new).astype(BF16) for t in tiles], axis=1)
        alpha2 = jnp.concatenate([alpha] * (V_PAD // LANES), axis=1)
        acc_sc[hh, rows, :] = alpha2 * acc_sc[hh, rows, :] + _dot(p, vb)
        m_sc[hh, rows, :] = m_new

    def block(j):
        start = pl.multiple_of(j * TQ, TQ)
        for hh in range(HEADS_PER_STEP):
            update(hh, 0, TQ, start, TQ, None)

    def body(jj, carry):
        for u in range(4):
            block(4 * jj + u)
        return carry

    quads = lax.shift_right_logical(qi, 2)
    lax.fori_loop(0, quads, body, 0)

    mask = chunk_mask(0, TQ, TQ)
    for rem in range(4):
        @pl.when((qi & 3) == rem)
        def _():
            for u in range(rem):
                block(4 * quads + u)
            start = pl.multiple_of(qi * TQ, TQ)
            for hh in range(HEADS_PER_STEP):
                update(hh, 0, TQ, start, TQ, mask)
    for hh in range(HEADS_PER_STEP):
        l = acc_sc[hh, :, V_HEAD_DIM:V_HEAD_DIM + 1]
        o_ref[0, :, hh * V_HEAD_DIM:(hh + 1) * V_HEAD_DIM] = (
            acc_sc[hh, :, 0:V_HEAD_DIM] / l).astype(BF16)


def _attention(q, k, v):
    B, H, S, _ = q.shape
    hps = HEADS_PER_STEP
    return pl.pallas_call(
        _attn_kernel,
        grid=(B, H // hps, S // TQ),
        in_specs=[
            pl.BlockSpec((1, hps, TQ, QK_PAD), lambda b, h, i: (b, h, i, 0)),
            pl.BlockSpec((1, hps, S, QK_PAD), lambda b, h, i: (b, h, 0, 0)),
            pl.BlockSpec((1, hps, S, V_HEAD_DIM), lambda b, h, i: (b, h, 0, 0)),
        ],
        out_specs=pl.BlockSpec((1, TQ, hps * V_HEAD_DIM), lambda b, h, i: (b, i, h)),
        out_shape=jax.ShapeDtypeStruct((B, S, H * V_HEAD_DIM), BF16),
        scratch_shapes=[
            pltpu.VMEM((hps, TQ, LANES), F32),
            pltpu.VMEM((hps, TQ, V_PAD), F32),
        ],
        compiler_params=pltpu.CompilerParams(
            dimension_semantics=("arbitrary", "arbitrary", "arbitrary"),
            vmem_limit_bytes=VMEM_LIMIT),
        name="attention",
    )(q, k, v)


def _col_sum(x):
    return jnp.sum(x, axis=0, keepdims=True)


def _col_max(x):
    return jnp.max(x, axis=0, keepdims=True)


def _first_row(hit, row_f):
    return jnp.min(jnp.where(hit, row_f, float(LANES)), axis=0, keepdims=True).astype(jnp.int32)


def _out_route_kernel(o_ref, ma_ref, sgb_ref, x_ref, wo_ref, wout_ref, gmoe_ref, wr_ref, br_ref,
                      h1_ref, m_ref, rt_ref, rw_ref, cnt_ref, carry_sc):
    i = pl.program_id(0)
    tm = TM_OUT

    @pl.when(i == 0)
    def _():
        carry_sc[...] = jnp.zeros(carry_sc.shape, F32)

    yb = _dot(o_ref[...], wo_ref[...])
    merged = ma_ref[...].astype(F32) + sgb_ref[...].astype(F32) * yb
    h1 = x_ref[...] + _dot(merged.astype(BF16), wout_ref[...])
    h1_ref[...] = h1
    m = _rms(h1) * gmoe_ref[...]
    _store_rows(m_ref, m)

    logits = lax.dot_general(wr_ref[...], m.astype(BF16), (((1,), (1,)), ((), ())),
                             preferred_element_type=F32)
    row = lax.broadcasted_iota(jnp.int32, (LANES, tm), 0)
    row_f = row.astype(F32)
    bias = br_ref[...]
    is_g = (row >= N_EXPERTS) & (row < N_EXPERTS + N_GROUPS)
    neg = jnp.float32(-jnp.inf)

    gl = jnp.where(is_g, logits, neg)
    ge = jnp.where(is_g, jnp.exp(gl - _col_max(gl)), 0.0)
    g_prob = ge / _col_sum(ge)
    g_score = jnp.where(is_g, g_prob + bias, neg)
    g_row = _first_row(g_score == _col_max(g_score), row_f)
    g_w = _col_sum(jnp.where(row == g_row, g_prob, 0.0))
    g_idx = g_row - N_EXPERTS

    in_g = (row // EXPERTS_PER_GROUP) == g_idx
    el = jnp.where(in_g, logits, neg)
    ee = jnp.where(in_g, jnp.exp(el - _col_max(el)), 0.0)
    e_prob = ee / _col_sum(ee)
    e_score = jnp.where(in_g, e_prob + bias, neg)
    id1 = _first_row(e_score == _col_max(e_score), row_f)
    e_score2 = jnp.where(row == id1, neg, e_score)
    id2 = _first_row(e_score2 == _col_max(e_score2), row_f)
    p1 = _col_sum(jnp.where(row == id1, e_prob, 0.0))
    p2 = _col_sum(jnp.where(row == id2, e_prob, 0.0))
    psum = p1 + p2
    w1 = g_w * (p1 / psum)
    w2 = g_w * (p2 / psum)

    oh = ((row == id1) | (row == id2 + N_EXPERTS)).astype(BF16)
    t_in = lax.broadcasted_iota(jnp.int32, (tm, tm), 0)
    t_out = lax.broadcasted_iota(jnp.int32, (tm, tm), 1)
    tri = (t_in < t_out).astype(BF16)
    prefix = _dot(oh, tri)
    tot = jnp.sum(oh.astype(F32), axis=1, keepdims=True)
    tot_sw = jnp.concatenate([tot[N_EXPERTS:], tot[:N_EXPERTS]], axis=0)
    row1 = lax.broadcasted_iota(jnp.int32, (LANES, 1), 0)
    carry = carry_sc[...]
    base = carry + jnp.where(row1 >= N_EXPERTS, tot_sw, 0.0)
    rk = oh.astype(F32) * (base + prefix)
    rank1 = _col_sum(jnp.where(row < N_EXPERTS, rk, 0.0))
    rank2 = _col_sum(jnp.where(row >= N_EXPERTS, rk, 0.0))
    carry_new = carry + tot + tot_sw
    carry_sc[...] = carry_new
    cnt_ref[...] = carry_new.astype(jnp.int32)

    row8 = lax.broadcasted_iota(jnp.int32, (SUBLANES, tm), 0)
    rt = jnp.where(row8 == 0, id1, jnp.where(row8 == 1, id2, 0))
    rt = jnp.where(row8 == 2, rank1.astype(jnp.int32), rt)
    rt_ref[...] = jnp.where(row8 == 3, rank2.astype(jnp.int32), rt)
    wt = jnp.where(row == 0, w1, jnp.where(row == 1, w2, 0.0))
    rw_ref[...] = wt.T


def _out_route(o2, ma, sgb, x2, wo_b, wout_b, g_moe, wr_b, br):
    N = x2.shape[0]
    tm = TM_OUT
    row = lambda i: (i, 0)
    const = lambda i: (0, 0)
    return pl.pallas_call(
        _out_route_kernel,
        grid=(N // tm,),
        in_specs=[
            pl.BlockSpec((tm, D_MODEL), row),
            pl.BlockSpec((tm, D_MODEL), row),
            pl.BlockSpec((tm, D_MODEL), row),
            pl.BlockSpec((tm, D_MODEL), row),
            pl.BlockSpec((D_MODEL, D_MODEL), const),
            pl.BlockSpec((D_MODEL, D_MODEL), const),
            pl.BlockSpec((1, D_MODEL), const),
            pl.BlockSpec((LANES, D_MODEL), const),
            pl.BlockSpec((LANES, tm), const),
        ],
        out_specs=[
            pl.BlockSpec((tm, D_MODEL), row),
            pl.BlockSpec((tm * SUBLANES, LANES), row),
            pl.BlockSpec((SUBLANES, tm), lambda i: (0, i)),
            pl.BlockSpec((tm, LANES), row),
            pl.BlockSpec((LANES, 1), const),
        ],
        out_shape=[
            jax.ShapeDtypeStruct((N, D_MODEL), F32),
            jax.ShapeDtypeStruct((N * SUBLANES, LANES), F32),
            jax.ShapeDtypeStruct((SUBLANES, N), jnp.int32),
            jax.ShapeDtypeStruct((N, LANES), F32),
            jax.ShapeDtypeStruct((LANES, 1), jnp.int32),
        ],
        scratch_shapes=[pltpu.VMEM((LANES, 1), F32)],
        compiler_params=pltpu.CompilerParams(
            dimension_semantics=("arbitrary",), vmem_limit_bytes=VMEM_LIMIT),
        name="out_route",
    )(o2, ma, sgb, x2, wo_b, wout_b, g_moe, wr_b, br)


def _row_tile(ref2, row8):
    return ref2.at[pl.ds(pl.multiple_of(row8, SUBLANES), SUBLANES)]


def _dispatch_kernel(pad_end_ref, npad_ref, dest_ref, m_ref, buf_ref, zero_sc, sem, zsem):
    i = pl.program_id(0)
    tm = TM_ROWS
    blk8 = MOE_BLK * SUBLANES

    def zero_fill(go):
        def per_expert(e, carry):
            npad = npad_ref[e]
            row = pad_end_ref[e] - npad
            k = MOE_BLK // 2
            while k >= 1:
                @pl.when((npad & k) != 0)
                def _():
                    go(pltpu.make_async_copy(
                        zero_sc.at[pl.ds(0, k * SUBLANES)],
                        buf_ref.at[pl.ds(pl.multiple_of(row * SUBLANES, SUBLANES), k * SUBLANES)],
                        zsem))
                row = row + (npad & k)
                k //= 2
            return carry

        def tail_copy(b):
            return pltpu.make_async_copy(
                zero_sc, buf_ref.at[pl.ds(pl.multiple_of(b * blk8, blk8), blk8)], zsem)

        lax.fori_loop(0, N_EXPERTS, per_expert, 0)
        nused = pad_end_ref[N_EXPERTS - 1] // MOE_BLK
        lax.fori_loop(nused, buf_ref.shape[0] // blk8, lambda b, c: (go(tail_copy(b)), c)[1], 0)

    @pl.when(i == 0)
    def _():
        zero_sc[...] = jnp.zeros(zero_sc.shape, F32)
        zero_fill(lambda c: c.start())

    def issue(c, carry):
        for u in range(DMA_UNROLL):
            r = c * DMA_UNROLL + u
            src = _row_tile(m_ref, r * SUBLANES)
            pltpu.make_async_copy(src, _row_tile(buf_ref, dest_ref[0, 0, r]), sem).start(priority=0)
            pltpu.make_async_copy(src, _row_tile(buf_ref, dest_ref[0, 0, tm + r]), sem).start(priority=1)
        return carry

    lax.fori_loop(0, tm // DMA_UNROLL, issue, 0)
    for _ in range(2):
        pltpu.make_async_copy(m_ref, buf_ref.at[pl.ds(0, tm * SUBLANES)], sem).wait()

    @pl.when(i == pl.num_programs(0) - 1)
    def _():
        zero_fill(lambda c: c.wait())


def _dispatch(pad_end, npad, dest8_tiles, m2, P):
    tm = TM_ROWS
    N = m2.shape[0] // SUBLANES
    grid_spec = pltpu.PrefetchScalarGridSpec(
        num_scalar_prefetch=2,
        grid=(N // tm,),
        in_specs=[
            pl.BlockSpec((1, 1, 2 * tm), lambda i, pe, pd: (i, 0, 0), memory_space=pltpu.SMEM),
            pl.BlockSpec((tm * SUBLANES, LANES), lambda i, pe, pd: (i, 0)),
        ],
        out_specs=pl.BlockSpec(memory_space=pl.ANY),
        scratch_shapes=[
            pltpu.VMEM((MOE_BLK * SUBLANES, LANES), F32),
            pltpu.SemaphoreType.DMA(()),
            pltpu.SemaphoreType.DMA(()),
        ],
    )
    return pl.pallas_call(
        _dispatch_kernel,
        grid_spec=grid_spec,
        out_shape=jax.ShapeDtypeStruct((P * SUBLANES, LANES), F32),
        compiler_params=pltpu.CompilerParams(dimension_semantics=("arbitrary",)),
        name="dispatch",
    )(pad_end, npad, dest8_tiles, m2)


def _expert_kernel(blk_e_ref, run_blocks_ref, nused_ref, x_hbm, wg_hbm, wu_hbm, wd_hbm, y_hbm,
                   xbuf, ybuf, wg_raw, wu_raw, wd_raw, wgu_sc, wd_sc, zero_sc,
                   xsem, ysem, wsem, zsem):
    nused = nused_ref[0]
    blk8 = MOE_BLK * SUBLANES
    nblk = y_hbm.shape[0] // blk8
    nxbuf = X_AHEAD + 1

    def block_rows(ref, b):
        return ref.at[pl.ds(pl.multiple_of(b * blk8, blk8), blk8)]

    def x_copy(b):
        slot = b % nxbuf
        return pltpu.make_async_copy(block_rows(x_hbm, b), xbuf.at[slot], xsem.at[slot])

    def y_copy(b, slot):
        return pltpu.make_async_copy(ybuf.at[slot], block_rows(y_hbm, b), ysem.at[slot])

    def w_copies(e, slot):
        return (pltpu.make_async_copy(wg_hbm.at[e], wg_raw.at[slot], wsem.at[slot]),
                pltpu.make_async_copy(wu_hbm.at[e], wu_raw.at[slot], wsem.at[slot]),
                pltpu.make_async_copy(wd_hbm.at[e], wd_raw.at[slot], wsem.at[slot]))

    for b in range(X_AHEAD):
        @pl.when(b < nused)
        def _():
            x_copy(b).start()
    for c in w_copies(blk_e_ref[0], 0):
        c.start()

    def zero_copy(b):
        return pltpu.make_async_copy(zero_sc, block_rows(y_hbm, b), zsem)

    zero_sc[...] = jnp.zeros(zero_sc.shape, F32)
    lax.fori_loop(nused, nblk, lambda b, c: (zero_copy(b).start(), c)[1], 0)

    def body(i, run):
        e = blk_e_ref[i]
        new_expert = (i == 0) | (e != blk_e_ref[jnp.maximum(i - 1, 0)])
        run = run + new_expert.astype(jnp.int32)
        wslot = run % 2

        @pl.when(new_expert)
        def _():
            for c in w_copies(e, wslot):
                c.wait()
            wgu_sc[:, 0:D_EXPERT] = wg_raw[wslot].astype(BF16)
            wgu_sc[:, D_EXPERT:2 * D_EXPERT] = wu_raw[wslot].astype(BF16)
            wd_sc[...] = wd_raw[wslot].astype(BF16)
            nxt = i + run_blocks_ref[e]

            @pl.when(nxt < nused)
            def _():
                for c in w_copies(blk_e_ref[nxt], 1 - wslot):
                    c.start()

        @pl.when(i + X_AHEAD < nused)
        def _():
            x_copy(i + X_AHEAD).start()

        yslot = i % Y_BUFS
        x_copy(i).wait()

        @pl.when(i >= Y_BUFS)
        def _():
            y_copy(i - Y_BUFS, yslot).wait()

        xb = _load_rows(xbuf.at[i % nxbuf], MOE_BLK).astype(BF16)
        h = _dot(xb, wgu_sc[...])
        hdn = (jax.nn.silu(h[:, 0:D_EXPERT]) * h[:, D_EXPERT:2 * D_EXPERT]).astype(BF16)
        _store_rows(ybuf.at[yslot], _dot(hdn, wd_sc[...]))
        y_copy(i, yslot).start()
        return run

    lax.fori_loop(0, nused, body, jnp.int32(-1))

    for back in range(1, Y_BUFS + 1):
        @pl.when(nused >= back)
        def _():
            y_copy(nused - back, (nused - back) % Y_BUFS).wait()

    lax.fori_loop(nused, nblk, lambda b, c: (zero_copy(b).wait(), c)[1], 0)


def _experts(blk_e, run_blocks, nused, buf2, wg, wu, wd):
    blk_rows = MOE_BLK * SUBLANES
    any_spec = pl.BlockSpec(memory_space=pl.ANY)
    grid_spec = pltpu.PrefetchScalarGridSpec(
        num_scalar_prefetch=3,
        grid=(1,),
        in_specs=[any_spec, any_spec, any_spec, any_spec],
        out_specs=any_spec,
        scratch_shapes=[
            pltpu.VMEM((X_AHEAD + 1, blk_rows, LANES), F32),
            pltpu.VMEM((Y_BUFS, blk_rows, LANES), F32),
            pltpu.VMEM((2, D_MODEL, D_EXPERT), F32),
            pltpu.VMEM((2, D_MODEL, D_EXPERT), F32),
            pltpu.VMEM((2, D_EXPERT, D_MODEL), F32),
            pltpu.VMEM((D_MODEL, 2 * D_EXPERT), BF16),
            pltpu.VMEM((D_EXPERT, D_MODEL), BF16),
            pltpu.VMEM((blk_rows, LANES), F32),
            pltpu.SemaphoreType.DMA((X_AHEAD + 1,)),
            pltpu.SemaphoreType.DMA((Y_BUFS,)),
            pltpu.SemaphoreType.DMA((2,)),
            pltpu.SemaphoreType.DMA(()),
        ],
    )
    return pl.pallas_call(
        _expert_kernel,
        grid_spec=grid_spec,
        out_shape=jax.ShapeDtypeStruct(buf2.shape, F32),
        compiler_params=pltpu.CompilerParams(
            dimension_semantics=("arbitrary",), vmem_limit_bytes=VMEM_LIMIT),
        name="experts",
    )(blk_e, run_blocks, nused, buf2, wg, wu, wd)


def _final_kernel(dcur_ref, dnext_ref, h1_ref, rw_ref, p_ref, gple_ref, wpg_ref, wpp_ref, gfin_ref,
                  ys_ref, out_ref, ybuf_a, ybuf_b, sem):
    g = pl.program_id(0)
    ng = pl.num_programs(0)
    tm = TM_ROWS
    bufs = (ybuf_a, ybuf_b)

    def row_copy(d_ref, off, r, which):
        return pltpu.make_async_copy(_row_tile(ys_ref, d_ref[0, 0, off + r]),
                                     _row_tile(bufs[which], r * SUBLANES), sem.at[which])

    def issue(d_ref, off, which):
        for r in range(2 * tm):
            row_copy(d_ref, off, r, which).start(priority=r % 2)

    def wait(which):
        pltpu.make_async_copy(ys_ref.at[pl.ds(0, 2 * tm * SUBLANES)], bufs[which],
                              sem.at[which]).wait()

    def compute(which, rows):
        y0 = _load_rows(bufs[which], tm)
        y1 = _load_rows(bufs[which], tm, first_row=tm)
        rw = rw_ref[rows, :]
        h2 = h1_ref[rows, :] + (y0 * rw[:, 0:1] + y1 * rw[:, 1:2])
        n3 = (_rms(h2) * gple_ref[...]).astype(BF16)
        gate = jax.nn.sigmoid(_dot(n3, wpg_ref[...]))
        pp = _dot(p_ref[rows, :].astype(BF16), wpp_ref[...])
        h3 = h2 + gate * pp
        out_ref[rows, :] = _rms(h3) * gfin_ref[...]

    @pl.when(g == 0)
    def _():
        def body(c, carry):
            for u in range(DMA_UNROLL):
                row_copy(dcur_ref, 0, c * DMA_UNROLL + u, 0).start()
            return carry

        lax.fori_loop(0, 2 * tm // DMA_UNROLL, body, 0)

    wait(0)
    issue(dcur_ref, 2 * tm, 1)
    compute(0, slice(0, tm))
    wait(1)
    issue(dnext_ref, 0, 0)
    compute(1, slice(tm, 2 * tm))

    @pl.when(g == ng - 1)
    def _():
        wait(0)


def _final(dest8_pairs, h1, rw, p2, g_ple, wpg_b, wpp_b, g_final, ys2):
    N = h1.shape[0]
    tm = TM_ROWS
    ng = N // (2 * tm)
    row = lambda i: (i, 0)
    const = lambda i: (0, 0)
    return pl.pallas_call(
        _final_kernel,
        grid=(ng,),
        in_specs=[
            pl.BlockSpec((1, 1, 4 * tm), lambda i: (i, 0, 0), memory_space=pltpu.SMEM),
            pl.BlockSpec((1, 1, 4 * tm), lambda i: (jnp.minimum(i + 1, ng - 1), 0, 0),
                         memory_space=pltpu.SMEM),
            pl.BlockSpec((2 * tm, D_MODEL), row),
            pl.BlockSpec((2 * tm, LANES), row),
            pl.BlockSpec((2 * tm, PLE_DIM), row),
            pl.BlockSpec((1, D_MODEL), const),
            pl.BlockSpec((D_MODEL, D_MODEL), const),
            pl.BlockSpec((PLE_DIM, D_MODEL), const),
            pl.BlockSpec((1, D_MODEL), const),
            pl.BlockSpec(memory_space=pl.ANY),
        ],
        out_specs=pl.BlockSpec((2 * tm, D_MODEL), row),
        out_shape=jax.ShapeDtypeStruct((N, D_MODEL), F32),
        scratch_shapes=[
            pltpu.VMEM((2 * tm * SUBLANES, LANES), F32),
            pltpu.VMEM((2 * tm * SUBLANES, LANES), F32),
            pltpu.SemaphoreType.DMA((2,)),
        ],
        compiler_params=pltpu.CompilerParams(
            dimension_semantics=("arbitrary",), vmem_limit_bytes=VMEM_LIMIT),
        name="final",
    )(dest8_pairs, dest8_pairs, h1, rw, p2, g_ple, wpg_b, wpp_b, g_final, ys2)


def _rope_tables(S):
    inv_freq = ROPE_THETA ** (-jnp.arange(0, QK_ROPE_DIM, 2, dtype=F32) / QK_ROPE_DIM)
    ang = jnp.arange(S, dtype=F32)[:, None] * inv_freq[None, :]
    cos, sin = jnp.cos(ang), jnp.sin(ang)
    z = jnp.zeros_like(cos)
    rc = jnp.concatenate([cos, cos, z, z], axis=1)
    rs1 = jnp.concatenate([-sin, z, z, z], axis=1)
    rs2 = jnp.concatenate([z, sin, z, z], axis=1)
    return rc, rs1, rs2


def _layer(h, p_l, g_mix, w_in, g_gv, w_spatial, b_spatial, w_gproj, g_cq, w_uq, g_ckv, w_ukv,
           w_mla_o, w_out, g_moe, w_router_g, b_router_g, w_router_e, b_router_e,
           w_e_gate, w_e_up, w_e_down, g_ple, w_ple_gate, w_ple_proj, g_out):
    B, S, D = h.shape
    N = B * S
    x2 = h.reshape(N, D)

    cu, cv, ccq, cckv, ckr, cga = (GMLP_WIDTH, 2 * GMLP_WIDTH, 2 * GMLP_WIDTH + Q_LORA,
                                   2 * GMLP_WIDTH + Q_LORA + KV_LORA,
                                   2 * GMLP_WIDTH + Q_LORA + KV_LORA + QK_ROPE_DIM,
                                   2 * GMLP_WIDTH + Q_LORA + KV_LORA + QK_ROPE_DIM + D_MODEL)
    w_in_p = jnp.concatenate(
        [w_in[:, :cckv], w_in[:, ckr:], w_in[:, cckv:ckr],
         jnp.zeros((D, LANES - QK_ROPE_DIM), w_in.dtype)], axis=1).astype(BF16)
    w_uq_h = w_uq.reshape(Q_LORA, MLA_HEADS, QK_NOPE_DIM + QK_ROPE_DIM)
    w_uq_p = jnp.concatenate(
        [w_uq_h, jnp.zeros((Q_LORA, MLA_HEADS, QK_PAD - QK_NOPE_DIM - QK_ROPE_DIM), w_uq.dtype)],
        axis=2).reshape(Q_LORA, MLA_HEADS * QK_PAD).astype(BF16)
    rc, rs1, rs2 = _rope_tables(S)

    u, v, sga, sgb, q, k, vv = _inproj(
        x2, g_mix[None], w_in_p, g_gv[None], g_cq[None], g_ckv[None], w_uq_p, w_ukv.astype(BF16),
        rc, rs1, rs2, B, S)
    ma = _gmlp(u, v, sga, w_spatial.astype(BF16), b_spatial.T, w_gproj.astype(BF16))
    o = _attention(q, k, vv)

    wr = jnp.concatenate(
        [w_router_e.T, w_router_g.T, jnp.zeros((LANES - N_EXPERTS - N_GROUPS, D), w_router_e.dtype)],
        axis=0).astype(BF16)
    br = jnp.concatenate(
        [b_router_e.reshape(-1), b_router_g, jnp.zeros((LANES - N_EXPERTS - N_GROUPS,), F32)])
    br = jnp.broadcast_to(br[:, None], (LANES, TM_OUT))
    h1, m, rt, rw, cnt = _out_route(
        o.reshape(N, D), ma, sgb, x2, w_mla_o.astype(BF16), w_out.astype(BF16), g_moe[None], wr, br)

    counts = cnt[:N_EXPERTS, 0]
    padded = (counts + MOE_BLK - 1) // MOE_BLK * MOE_BLK
    pad_end = jnp.cumsum(padded)
    pad_start = pad_end - padded
    P = 2 * N + N_EXPERTS * MOE_BLK
    nblk = P // MOE_BLK
    blk_start = jnp.arange(nblk, dtype=jnp.int32) * MOE_BLK
    blk_e = jnp.minimum(
        jnp.sum((pad_end[None, :] <= blk_start[:, None]).astype(jnp.int32), axis=1),
        N_EXPERTS - 1).astype(jnp.int32)
    nused = (pad_end[-1:] // MOE_BLK).astype(jnp.int32)
    experts = jnp.arange(N_EXPERTS, dtype=jnp.int32)[:, None, None]
    start_of = jnp.sum(jnp.where(rt[None, 0:2] == experts, pad_start[:, None, None], 0), axis=0)
    dest8 = (start_of + rt[2:4]).astype(jnp.int32) * SUBLANES
    nt = N // TM_ROWS
    dest8_tiles = dest8.reshape(2, nt, TM_ROWS).transpose(1, 0, 2).reshape(nt, 1, 2 * TM_ROWS)

    buf = _dispatch(pad_end.astype(jnp.int32), (padded - counts).astype(jnp.int32), dest8_tiles, m, P)
    ys = _experts(blk_e, (padded // MOE_BLK).astype(jnp.int32), nused, buf, w_e_gate, w_e_up, w_e_down)
    out = _final(dest8_tiles.reshape(nt // 2, 1, 4 * TM_ROWS), h1, rw, p_l.reshape(N, PLE_DIM), g_ple[None],
                 w_ple_gate.astype(BF16), w_ple_proj.astype(BF16), g_out[None], ys)
    return out.reshape(B, S, D)


def kernel(x, p, g_mix, w_in, g_gv, w_spatial, b_spatial, w_gproj, g_cq, w_uq, g_ckv, w_ukv, w_mla_o,
           w_out, g_moe, w_router_g, b_router_g, w_router_e, b_router_e, w_e_gate, w_e_up, w_e_down,
           g_ple, w_ple_gate, w_ple_proj, g_final):
    depth = p.shape[0]
    assert depth == 1, "the final rmsnorm is fused into the single layer's last kernel"
    i = 0
    return _layer(x, p[i], g_mix[i], w_in[i], g_gv[i], w_spatial[i], b_spatial[i], w_gproj[i], g_cq[i],
                  w_uq[i], g_ckv[i], w_ukv[i], w_mla_o[i], w_out[i], g_moe[i], w_router_g[i],
                  b_router_g[i], w_router_e[i], b_router_e[i], w_e_gate[i], w_e_up[i], w_e_down[i],
                  g_ple[i], w_ple_gate[i], w_ple_proj[i], g_final)
```
